```python
import math
import jax, jax.numpy as jnp
from jax import lax
import numpy as np

D_MODEL = 1024
BATCH = 8
SEQ = 2048
DEPTH = 2
DEC_BATCH = 16
DEC_SEQ = 64
PAST_LEN = 2048

CHUNK = 64
HEAD_DIM = 64
A_HEADS = 8
A_WIDTH = A_HEADS * HEAD_DIM
BAND_CHUNKS = 8
BAND = BAND_CHUNKS * CHUNK
MAX_REL = 128
B_HEADS = 8
B_WIDTH = B_HEADS * HEAD_DIM
IDX_HEADS = 8
IDX_DIM = 64
TOPK_KEYS = 256
M_HEADS = 4
M_HEAD_DIM = 128
M_WIDTH = M_HEADS * M_HEAD_DIM
N_MEM = 256
N_BRANCH = 3
ROPE_THETA = 10000.0
N_EXPERTS = 256
EXPERT_TOPK = 8
N_GROUPS = 8
TOPK_GROUPS = 4
EXPERT_FF = 256
SHARED_FF = 256
ROUTE_SCALE = 2.5
EXPERT_BLOCK = 128
LN_EPS = 1e-5
ALPHA = (2 * DEPTH) ** 0.25
BETA = (8 * DEPTH) ** -0.25

IN_WIDTHS = (A_WIDTH, A_WIDTH, A_WIDTH, B_WIDTH, B_WIDTH, B_WIDTH,
             IDX_HEADS * IDX_DIM, IDX_DIM, IDX_HEADS, M_WIDTH, N_BRANCH * D_MODEL)
IN_SPLITS = tuple(sum(IN_WIDTHS[:i + 1]) for i in range(len(IN_WIDTHS) - 1))
IN_COLS = sum(IN_WIDTHS)

kernel_name = "hybrid_stream_band_dsa_moe_step"


def layer_norm(x, g, b):
    xf = x.astype(jnp.float32)
    mu = jnp.mean(xf, -1, keepdims=True)
    var = jnp.mean(jnp.square(xf - mu), -1, keepdims=True)
    return ((xf - mu) * lax.rsqrt(var + LN_EPS) * g + b).astype(x.dtype)


def rope(x, pos):
    half = x.shape[-1] // 2
    inv_freq = ROPE_THETA ** (-jnp.arange(half, dtype=jnp.float32) / half)
    ang = pos.astype(jnp.float32)[:, None] * inv_freq[None, :]
    cos = jnp.cos(ang)[None, :, None, :]
    sin = jnp.sin(ang)[None, :, None, :]
    xf = x.astype(jnp.float32)
    x1, x2 = xf[..., :half], xf[..., half:]
    return jnp.concatenate([x1 * cos - x2 * sin, x2 * cos + x1 * sin], -1).astype(x.dtype)


def in_projection(x, w_in, pos):
    bx, s, _ = x.shape
    qa, ka, va, qb, kb, vb, qi, ki, wi, qm, gates = jnp.split(x @ w_in, IN_SPLITS, axis=-1)
    hd = lambda t, h: t.reshape(bx, s, h, -1)
    qb = rope(hd(qb, B_HEADS), pos)
    kb = rope(hd(kb, B_HEADS), pos)
    qi = rope(hd(qi, IDX_HEADS), pos)
    ki = rope(ki[:, :, None, :], pos)[:, :, 0, :]
    return (hd(qa, A_HEADS), hd(ka, A_HEADS), hd(va, A_HEADS), qb, kb, hd(vb, B_HEADS),
            qi, ki, wi, hd(qm, M_HEADS), gates)


def rel_bias_scores(table, n_q, n_k, offset):
    rel = jnp.arange(n_q)[:, None] + offset - jnp.arange(n_k)[None, :]
    return table[:, jnp.clip(rel, -MAX_REL, MAX_REL) + MAX_REL].astype(jnp.float32)


def band_attn_prompt(q, k, v, table):
    bx, s, h, dh = q.shape
    nc = s // CHUNK
    padz = jnp.zeros((bx, BAND, h, dh), k.dtype)
    kc = jnp.concatenate([padz, k], 1).reshape(bx, nc + BAND_CHUNKS, CHUNK, h, dh)
    vc = jnp.concatenate([padz, v], 1).reshape(bx, nc + BAND_CHUNKS, CHUNK, h, dh)
    kband = jnp.concatenate([kc[:, j:j + nc] for j in range(BAND_CHUNKS + 1)], axis=2)
    vband = jnp.concatenate([vc[:, j:j + nc] for j in range(BAND_CHUNKS + 1)], axis=2)
    qc = q.reshape(bx, nc, CHUNK, h, dh)
    sc = jnp.einsum('bcqhd,bckhd->bchqk', qc, kband).astype(jnp.float32) * dh ** -0.5
    sc = sc + rel_bias_scores(table, CHUNK, BAND + CHUNK, BAND)[None, None]
    key_pos = jnp.arange(nc)[:, None] * CHUNK - BAND + jnp.arange(BAND + CHUNK)[None, :]
    p = jax.nn.softmax(jnp.where(key_pos[None, :, None, None, :] >= 0, sc, -jnp.inf), axis=-1)
    o = jnp.einsum('bchqk,bckhd->bcqhd', p.astype(v.dtype), vband)
    return o.reshape(bx, s, h * dh)


def band_attn_sample(q, k_new, v_new, k_cache, v_cache, table):
    bx, t, h, dh = q.shape
    ca = k_cache.shape[1]
    kk = jnp.concatenate([k_cache, k_new], 1)
    vv = jnp.concatenate([v_cache, v_new], 1)
    sc = jnp.einsum('bqhd,bkhd->bhqk', q, kk).astype(jnp.float32) * dh ** -0.5
    sc = sc + rel_bias_scores(table, t, ca + t, ca)[None]
    p = jax.nn.softmax(sc, axis=-1)
    o = jnp.einsum('bhqk,bkhd->bqhd', p.astype(vv.dtype), vv).reshape(bx, t, h * dh)
    return o, kk[:, -ca:], vv[:, -ca:]


def indexer_scores(qi, wi, ki):
    dots = jnp.einsum('bqhd,bsd->bqhs', qi, ki).astype(jnp.float32)
    return jnp.einsum('bqh,bqhs->bqs', wi.astype(jnp.float32), jax.nn.relu(dots))


def gather_rows(rows, sel):
    return jax.vmap(lambda r, i: r[i])(rows, sel)


def dsa_prompt(q, k, v, qi, ki, wi):
    bx, s, h, dh = q.shape
    nc = s // CHUNK
    n_sel = min(TOPK_KEYS, s // 4)
    key_pos = jnp.arange(s)
    to_blocks = lambda t: jnp.swapaxes(t.reshape((bx, nc, CHUNK) + t.shape[2:]), 0, 1)

    def one_block(args):
        c, qb, qib, wib = args
        limit = (c + 1) * CHUNK
        idx = jnp.where(key_pos < limit, indexer_scores(qib, wib, ki), -jnp.inf)
        _, sel = lax.top_k(idx, n_sel)
        kg = gather_rows(k, sel)
        vg = gather_rows(v, sel)
        sc = jnp.einsum('bqhd,bqkhd->bhqk', qb, kg).astype(jnp.float32) * dh ** -0.5
        valid = (sel < limit)[:, None]
        p = jax.nn.softmax(jnp.where(valid, sc, -jnp.inf), axis=-1)
        return jnp.einsum('bhqk,bqkhd->bqhd', p.astype(vg.dtype), vg)

    out = lax.map(one_block, (jnp.arange(nc), to_blocks(q), to_blocks(qi), to_blocks(wi)))
    return jnp.swapaxes(out, 0, 1).reshape(bx, s, h * dh)


def dsa_sample(q, k_new, v_new, qi, ki_new, wi, k_cache, v_cache, ki_cache):
    bx, t, h, dh = q.shape
    kk = jnp.concatenate([k_cache, k_new], 1)
    vv = jnp.concatenate([v_cache, v_new], 1)
    kik = jnp.concatenate([ki_cache, ki_new], 1)
    n_sel = min(TOPK_KEYS, kk.shape[1] // 4)
    _, sel = lax.top_k(indexer_scores(qi, wi, kik), n_sel)
    kg = gather_rows(kk, sel)
    vg = gather_rows(vv, sel)
    sc = jnp.einsum('bqhd,bqkhd->bhqk', q, kg).astype(jnp.float32) * dh ** -0.5
    p = jax.nn.softmax(sc, axis=-1)
    return jnp.einsum('bhqk,bqkhd->bqhd', p.astype(vg.dtype), vg).reshape(bx, t, h * dh)


def memory_kv(mem, w_mem_kv):
    k, v = jnp.split(mem @ w_mem_kv, 2, -1)
    shp = (mem.shape[0], mem.shape[1], M_HEADS, M_HEAD_DIM)
    return k.reshape(shp), v.reshape(shp)


def memory_attn(q, mk, mv):
    bx, s = q.shape[:2]
    sc = jnp.einsum('bqhd,bkhd->bhqk', q, mk).astype(jnp.float32) * M_HEAD_DIM ** -0.5
    p = jax.nn.softmax(sc, axis=-1)
    return jnp.einsum('bhqk,bkhd->bqhd', p.astype(mv.dtype), mv).reshape(bx, s, M_WIDTH)


def merge_branches(a, b, m, gates, w_branch, w_out):
    br = jnp.stack([a, b, m], axis=2)
    proj = jnp.einsum('bsnw,nwd->bsnd', br, w_branch)
    g = jax.nn.sigmoid(gates.reshape(proj.shape).astype(jnp.float32))
    return jnp.sum(g * proj, axis=2).astype(a.dtype) @ w_out


def swiglu(x, w_gu, w_dn):
    g, u = jnp.split(x @ w_gu, 2, -1)
    return (jax.nn.silu(g) * u) @ w_dn


def route(h, w_router, b_router):
    n = h.shape[0]
    s = jax.nn.sigmoid((h @ w_router).astype(jnp.float32))
    sb = s + b_router.astype(jnp.float32)
    gscore = jnp.sum(lax.top_k(sb.reshape(n, N_GROUPS, -1), 2)[0], -1)
    _, gsel = lax.top_k(gscore, TOPK_GROUPS)
    gmask = jnp.sum(jax.nn.one_hot(gsel, N_GROUPS, dtype=jnp.float32), 1) > 0
    sb = jnp.where(jnp.repeat(gmask, N_EXPERTS // N_GROUPS, axis=1), sb, -jnp.inf)
    _, eidx = lax.top_k(sb, EXPERT_TOPK)
    w = jnp.take_along_axis(s, eidx, 1)
    return eidx, w / jnp.sum(w, -1, keepdims=True) * ROUTE_SCALE


def moe_ffn(h, w_router, b_router, w_gu, w_dn, ws_gu, ws_dn):
    shp = h.shape
    hf = h.reshape(-1, shp[-1])
    n = hf.shape[0]
    eidx, gw = route(hf, w_router, b_router)
    n_assign = n * EXPERT_TOPK
    e_flat = eidx.reshape(-1)
    order = jnp.argsort(e_flat)
    e_sorted = e_flat[order]
    tok_sorted = order // EXPERT_TOPK
    w_sorted = gw.reshape(-1)[order]
    counts = jnp.bincount(e_flat, length=N_EXPERTS)
    padded = (counts + EXPERT_BLOCK - 1) // EXPERT_BLOCK * EXPERT_BLOCK
    pad_end = jnp.cumsum(padded)
    pad_start = pad_end - padded
    start = jnp.cumsum(counts) - counts
    dest = pad_start[e_sorted] + jnp.arange(n_assign) - start[e_sorted]
    n_blocks = -(-n_assign // EXPERT_BLOCK) + N_EXPERTS
    rows = n_blocks * EXPERT_BLOCK
    row_tok = jnp.full((rows,), n, jnp.int32).at[dest].set(tok_sorted.astype(jnp.int32))
    row_w = jnp.zeros((rows,), jnp.float32).at[dest].set(w_sorted)
    block_exp = jnp.minimum(jnp.searchsorted(pad_end // EXPERT_BLOCK, jnp.arange(n_blocks), side='right'),
                            N_EXPERTS - 1)
    xb = jnp.concatenate([hf, jnp.zeros((1, hf.shape[1]), hf.dtype)])[row_tok]
    xb = xb.reshape(n_blocks, EXPERT_BLOCK, -1)

    def expert_block(args):
        xblk, e = args
        return swiglu(xblk, w_gu[e], w_dn[e])

    yb = lax.map(expert_block, (xb, block_exp)).reshape(rows, -1)
    routed = jax.ops.segment_sum(yb.astype(jnp.float32) * row_w[:, None], row_tok, num_segments=n + 1)[:n]
    out = routed.astype(hf.dtype) + swiglu(hf, ws_gu, ws_dn)
    return out.reshape(shp)


def trunk_update(x, a, b, m, gates, w_branch, w_out, g1, b1, ffn_w, g2, b2):
    x = layer_norm(ALPHA * x + merge_branches(a, b, m, gates, w_branch, w_out), g1, b1)
    return layer_norm(ALPHA * x + moe_ffn(x, *ffn_w), g2, b2)


def setup_inputs(seed: int = 0) -> dict:
    key = jax.random.key(seed)
    ks = jax.random.split(key, 25)
    nrm = lambda k, shape, scale: jax.random.normal(k, shape, jnp.float32) * scale
    a_len = min(BAND, PAST_LEN)
    return {
        "x_prompt": nrm(ks[0], (BATCH, SEQ, D_MODEL), 1.0),
        "x_sample": nrm(ks[1], (DEC_BATCH, DEC_SEQ, D_MODEL), 1.0),
        "mem_prompt": nrm(ks[2], (BATCH, N_MEM, D_MODEL), 1.0),
        "cache_a_k": nrm(ks[3], (DEPTH, DEC_BATCH, a_len, A_HEADS, HEAD_DIM), 1.0),
        "cache_a_v": nrm(ks[4], (DEPTH, DEC_BATCH, a_len, A_HEADS, HEAD_DIM), 1.0),
        "cache_b_k": nrm(ks[5], (DEPTH, DEC_BATCH, PAST_LEN, B_HEADS, HEAD_DIM), 1.0),
        "cache_b_v": nrm(ks[6], (DEPTH, DEC_BATCH, PAST_LEN, B_HEADS, HEAD_DIM), 1.0),
        "cache_b_idx_k": nrm(ks[7], (DEPTH, DEC_BATCH, PAST_LEN, IDX_DIM), 1.0),
        "cache_mem_k": nrm(ks[8], (DEPTH, DEC_BATCH, N_MEM, M_HEADS, M_HEAD_DIM), 1.0),
        "cache_mem_v": nrm(ks[9], (DEPTH, DEC_BATCH, N_MEM, M_HEADS, M_HEAD_DIM), 1.0),
        "w_in": nrm(ks[10], (DEPTH, D_MODEL, IN_COLS), D_MODEL ** -0.5),
        "rel_bias": nrm(ks[11], (DEPTH, A_HEADS, 2 * MAX_REL + 1), 0.2),
        "w_mem_kv": nrm(ks[12], (DEPTH, D_MODEL, 2 * M_WIDTH), D_MODEL ** -0.5),
        "w_branch": nrm(ks[13], (DEPTH, N_BRANCH, A_WIDTH, D_MODEL), A_WIDTH ** -0.5 * BETA),
        "w_out": nrm(ks[14], (DEPTH, D_MODEL, D_MODEL), D_MODEL ** -0.5 * BETA),
        "ln_mix_g": 1.0 + nrm(ks[15], (DEPTH, D_MODEL), 0.02),
        "ln_mix_b": nrm(ks[16], (DEPTH, D_MODEL), 0.02),
        "w_router": nrm(ks[17], (DEPTH, D_MODEL, N_EXPERTS), D_MODEL ** -0.5),
        "b_router": nrm(ks[18], (DEPTH, N_EXPERTS), 0.01),
        "w_exp_gate_up": nrm(ks[19], (DEPTH, N_EXPERTS, D_MODEL, 2 * EXPERT_FF), D_MODEL ** -0.5),
        "w_exp_down": nrm(ks[20], (DEPTH, N_EXPERTS, EXPERT_FF, D_MODEL), EXPERT_FF ** -0.5 * BETA),
        "w_sh_gate_up": nrm(ks[21], (DEPTH, D_MODEL, 2 * SHARED_FF), D_MODEL ** -0.5),
        "w_sh_down": nrm(ks[22], (DEPTH, SHARED_FF, D_MODEL), SHARED_FF ** -0.5 * BETA),
        "ln_ffn_g": 1.0 + nrm(ks[23], (DEPTH, D_MODEL), 0.02),
        "ln_ffn_b": nrm(ks[24], (DEPTH, D_MODEL), 0.02),
    }


def reference(x_prompt, x_sample, mem_prompt, cache_a_k, cache_a_v, cache_b_k, cache_b_v,
              cache_b_idx_k, cache_mem_k, cache_mem_v, w_in, rel_bias, w_mem_kv, w_branch, w_out,
              ln_mix_g, ln_mix_b, w_router, b_router, w_exp_gate_up, w_exp_down, w_sh_gate_up,
              w_sh_down, ln_ffn_g, ln_ffn_b):
    seq = x_prompt.shape[1]
    past = cache_b_k.shape[2]
    pos_p = jnp.arange(seq)
    pos_s = past + jnp.arange(x_sample.shape[1])
    a_keep = min(BAND, seq)
    xp, xs = x_prompt, x_sample
    p_ak, p_av, p_bk, p_bv, p_ik, p_mk, p_mv = [], [], [], [], [], [], []
    s_ak, s_av, s_bk, s_bv, s_ik = [], [], [], [], []
    for l in range(DEPTH):
        ffn_w = (w_router[l], b_router[l], w_exp_gate_up[l], w_exp_down[l], w_sh_gate_up[l], w_sh_down[l])
        qa, ka, va, qb, kb, vb, qi, ki, wi, qm, gates = in_projection(xp, w_in[l], pos_p)
        mk, mv = memory_kv(mem_prompt, w_mem_kv[l])
        a = band_attn_prompt(qa, ka, va, rel_bias[l])
        b = dsa_prompt(qb, kb, vb, qi, ki, wi)
        m = memory_attn(qm, mk, mv)
        xp = trunk_update(xp, a, b, m, gates, w_branch[l], w_out[l], ln_mix_g[l], ln_mix_b[l],
                          ffn_w, ln_ffn_g[l], ln_ffn_b[l])
        p_ak.append(ka[:, -a_keep:])
        p_av.append(va[:, -a_keep:])
        p_bk.append(kb)
        p_bv.append(vb)
        p_ik.append(ki)
        p_mk.append(mk)
        p_mv.append(mv)
        qa, ka, va, qb, kb, vb, qi, ki, wi, qm, gates = in_projection(xs, w_in[l], pos_s)
        a, nak, nav = band_attn_sample(qa, ka, va, cache_a_k[l], cache_a_v[l], rel_bias[l])
        b = dsa_sample(qb, kb, vb, qi, ki, wi, cache_b_k[l], cache_b_v[l], cache_b_idx_k[l])
        m = memory_attn(qm, cache_mem_k[l], cache_mem_v[l])
        xs = trunk_update(xs, a, b, m, gates, w_branch[l], w_out[l], ln_mix_g[l], ln_mix_b[l],
                          ffn_w, ln_ffn_g[l], ln_ffn_b[l])
        s_ak.append(nak)
        s_av.append(nav)
        s_bk.append(kb)
        s_bv.append(vb)
        s_ik.append(ki)
    return (xp, xs,
            jnp.stack(p_ak), jnp.stack(p_av), jnp.stack(p_bk), jnp.stack(p_bv), jnp.stack(p_ik),
            jnp.stack(p_mk), jnp.stack(p_mv),
            jnp.stack(s_ak), jnp.stack(s_av), jnp.stack(s_bk), jnp.stack(s_bv), jnp.stack(s_ik))
```

```python
import functools

import jax
import jax.numpy as jnp
from jax import lax
from jax.experimental import pallas as pl
from jax.experimental.pallas import tpu as pltpu

F32 = jnp.float32
BF16 = jnp.bfloat16
I32 = jnp.int32

CHUNK = 64
HEAD_DIM = 64
N_HEADS = 8
WIDTH = N_HEADS * HEAD_DIM
BAND = 8 * CHUNK
MAX_REL = 128
TOPK_KEYS = 256
M_HEADS = 4
M_HEAD_DIM = 128
N_MEM = 256
ROPE_THETA = 10000.0
N_EXPERTS = 256
EXPERT_TOPK = 8
N_GROUPS = 8
TOPK_GROUPS = 4
EXPERT_FF = 256
ROUTE_SCALE = 2.5
EXPERT_BLOCK = 128
LN_EPS = 1e-5

LANES = 128
NEG_BIG = -1e30
INT_MIN = -(2 ** 31)
VMEM_LIMIT = 48 * 1024 * 1024

_NT_DIMS = (((1,), (1,)), ((), ()))


def _params(semantics):
    return pltpu.CompilerParams(dimension_semantics=semantics, vmem_limit_bytes=VMEM_LIMIT)


def _proj_body(*refs, widths, rope_flags):
    has_rope = any(rope_flags)
    x_ref, w_ref = refs[0], refs[1]
    if has_rope:
        cos_ref, sin_lo_ref, sin_hi_ref = refs[2:5]
        outs = refs[5:]
    else:
        outs = refs[2:]
    xb = x_ref[...].astype(BF16)
    off = 0
    for g, width in enumerate(widths):
        y = jnp.dot(xb, w_ref[:, off:off + width], preferred_element_type=F32)
        if rope_flags[g]:
            cos, sin_lo, sin_hi = cos_ref[...], sin_lo_ref[...], sin_hi_ref[...]
            for k in range(width // LANES):
                yk = y[:, k * LANES:(k + 1) * LANES]
                outs[g][:, k * LANES:(k + 1) * LANES] = (
                    yk * cos + pltpu.roll(yk, LANES - 32, 1) * sin_lo + pltpu.roll(yk, 32, 1) * sin_hi)
        else:
            outs[g][...] = y
        off += width


def _project(x, w, widths, rope_flags=None, tables=None, table_index=None, tm=512, name="proj"):
    n, d = x.shape
    rope_flags = rope_flags or (False,) * len(widths)
    assert n % tm == 0 and w.shape == (d, sum(widths))
    in_specs = [pl.BlockSpec((tm, d), lambda i: (i, 0)),
                pl.BlockSpec(w.shape, lambda i: (0, 0))]
    args = [x, w]
    if any(rope_flags):
        for t in tables:
            in_specs.append(pl.BlockSpec((tm, LANES), lambda i: (table_index(i), 0)))
            args.append(t)
    return pl.pallas_call(
        functools.partial(_proj_body, widths=tuple(widths), rope_flags=tuple(rope_flags)),
        out_shape=[jax.ShapeDtypeStruct((n, wd), F32) for wd in widths],
        grid=(n // tm,),
        in_specs=in_specs,
        out_specs=[pl.BlockSpec((tm, wd), lambda i: (i, 0)) for wd in widths],
        compiler_params=_params(("parallel",)),
        name=name,
    )(*args)


def _rope_tables(pos, rope_lanes):
    half = HEAD_DIM // 2
    inv_freq = ROPE_THETA ** (-jnp.arange(half, dtype=F32) / half)
    ang = pos.astype(F32)[:, None] * inv_freq[None, :]
    cos, sin = jnp.cos(ang), jnp.sin(ang)
    reps = LANES // HEAD_DIM
    zero = jnp.zeros_like(sin)
    cos_t = jnp.tile(jnp.concatenate([cos, cos], -1), (1, reps))
    sin_lo = jnp.tile(jnp.concatenate([-sin, zero], -1), (1, reps))
    sin_hi = jnp.tile(jnp.concatenate([zero, sin], -1), (1, reps))
    keep = (jnp.arange(LANES) < rope_lanes)[None, :]
    return (jnp.where(keep, cos_t, 1.0), jnp.where(keep, sin_lo, 0.0), jnp.where(keep, sin_hi, 0.0))


def _head_mask(rows, head_in_pair):
    lane = lax.broadcasted_iota(I32, (rows, LANES), 1)
    lo = head_in_pair * HEAD_DIM
    return (lane >= lo) & (lane < lo + HEAD_DIM)


def _pair_merge(o_even, o_odd):
    lane = lax.broadcasted_iota(I32, o_even.shape, 1)
    return jnp.where(lane < HEAD_DIM, o_even, o_odd)


def _band_body(*refs, sample):
    if sample:
        q_ref, kn_ref, vn_ref, kh_ref, vh_ref, bias_ref, _, o_ref, kpad, vpad = refs
        kpad[0:BAND, :] = kh_ref[...].astype(BF16)
        vpad[0:BAND, :] = vh_ref[...].astype(BF16)
        kpad[BAND:, :] = kn_ref[...].astype(BF16)
        vpad[BAND:, :] = vn_ref[...].astype(BF16)
        start = 0
        first_valid = 0
    else:
        q_ref, k_ref, v_ref, bias_ref, o_ref, kpad, vpad = refs
        c = pl.program_id(1)

        @pl.when(c == 0)
        def _():
            kpad[0:BAND, :] = jnp.zeros((BAND, WIDTH), BF16)
            vpad[0:BAND, :] = jnp.zeros((BAND, WIDTH), BF16)
            kpad[BAND:, :] = k_ref[...].astype(BF16)
            vpad[BAND:, :] = v_ref[...].astype(BF16)

        start = pl.multiple_of(c * CHUNK, CHUNK)
        first_valid = BAND - c * CHUNK

    win = BAND + CHUNK
    q = q_ref[...] * (HEAD_DIM ** -0.5)
    valid = lax.broadcasted_iota(I32, (CHUNK, win), 1) >= first_valid
    for hp in range(N_HEADS // 2):
        cols = slice(hp * LANES, (hp + 1) * LANES)
        q2 = q[:, cols]
        k2 = kpad[pl.ds(start, win), cols]
        v2 = vpad[pl.ds(start, win), cols]
        outs = []
        for hh in range(2):
            qm = jnp.where(_head_mask(CHUNK, hh), q2, 0.0).astype(BF16)
            s = lax.dot_general(qm, k2, _NT_DIMS, preferred_element_type=F32)
            s = s + bias_ref[2 * hp + hh]
            if not sample:
                s = jnp.where(valid, s, -jnp.inf)
            p = jnp.exp(s - jnp.max(s, axis=1, keepdims=True))
            denom = jnp.sum(p, axis=1, keepdims=True)
            o = jnp.dot(p.astype(BF16), v2, preferred_element_type=F32)
            outs.append(o / denom)
        o_ref[:, cols] = _pair_merge(outs[0], outs[1])


def _band_bias(table):
    rel = jnp.arange(CHUNK)[:, None] + BAND - jnp.arange(BAND + CHUNK)[None, :]
    return table[:, jnp.clip(rel, -MAX_REL, MAX_REL) + MAX_REL].astype(F32)


def _band_prompt(q, k, v, bias, n_batch, seq, n_total):
    nc = seq // CHUNK
    return pl.pallas_call(
        functools.partial(_band_body, sample=False),
        out_shape=jax.ShapeDtypeStruct((n_total, WIDTH), F32),
        grid=(n_batch, nc),
        in_specs=[pl.BlockSpec((CHUNK, WIDTH), lambda b, c: (b * nc + c, 0)),
                  pl.BlockSpec((seq, WIDTH), lambda b, c: (b, 0)),
                  pl.BlockSpec((seq, WIDTH), lambda b, c: (b, 0)),
                  pl.BlockSpec(bias.shape, lambda b, c: (0, 0, 0))],
        out_specs=pl.BlockSpec((CHUNK, WIDTH), lambda b, c: (b * nc + c, 0)),
        scratch_shapes=[pltpu.VMEM((BAND + seq, WIDTH), BF16), pltpu.VMEM((BAND + seq, WIDTH), BF16)],
        compiler_params=_params(("parallel", "arbitrary")),
        name="band_prompt",
    )(q, k, v, bias)


def _band_sample(q, k, v, k_hist, v_hist, layer, bias, out_buf, n_batch, row0):
    blk0 = row0 // CHUNK
    hist_spec = pl.BlockSpec((None, None, BAND, WIDTH), lambda b: (layer, b, 0, 0))
    new_spec = pl.BlockSpec((CHUNK, WIDTH), lambda b: (blk0 + b, 0))
    return pl.pallas_call(
        functools.partial(_band_body, sample=True),
        out_shape=jax.ShapeDtypeStruct(out_buf.shape, F32),
        grid=(n_batch,),
        in_specs=[new_spec, new_spec, new_spec, hist_spec, hist_spec,
                  pl.BlockSpec(bias.shape, lambda b: (0, 0, 0)),
                  pl.BlockSpec(memory_space=pl.ANY)],
        out_specs=new_spec,
        scratch_shapes=[pltpu.VMEM((BAND + CHUNK, WIDTH), BF16), pltpu.VMEM((BAND + CHUNK, WIDTH), BF16)],
        input_output_aliases={6: 0},
        compiler_params=_params(("parallel",)),
        name="band_sample",
    )(q, k, v, k_hist, v_hist, bias, out_buf)


def _mem_body(*refs, aliased):
    if aliased:
        q_ref, k_ref, v_ref, _, o_ref = refs
    else:
        q_ref, k_ref, v_ref, o_ref = refs
    scale = M_HEAD_DIM ** -0.5
    for h in range(M_HEADS):
        cols = slice(h * M_HEAD_DIM, (h + 1) * M_HEAD_DIM)
        s = lax.dot_general(q_ref[:, cols].astype(BF16), k_ref[:, cols].astype(BF16), _NT_DIMS,
                            preferred_element_type=F32) * scale
        p = jnp.exp(s - jnp.max(s, axis=1, keepdims=True))
        denom = jnp.sum(p, axis=1, keepdims=True)
        o = jnp.dot(p.astype(BF16), v_ref[:, cols].astype(BF16), preferred_element_type=F32)
        o_ref[:, cols] = o / denom


def _mem_attn(q, mk, mv, kv_index, n_batch, rows_per_batch, row0, tq, out_buf=None, n_total=None):
    width = M_HEADS * M_HEAD_DIM
    per = rows_per_batch // tq
    blk0 = row0 // tq
    q_spec = pl.BlockSpec((tq, width), lambda b, t: (blk0 + b * per + t, 0))
    kv_block = (None,) * (mk.ndim - 2) + (N_MEM, width)
    kv_spec = pl.BlockSpec(kv_block, lambda b, t: kv_index(b))
    in_specs = [q_spec, kv_spec, kv_spec]
    args = [q, mk, mv]
    aliases = {}
    if out_buf is not None:
        in_specs.append(pl.BlockSpec(memory_space=pl.ANY))
        args.append(out_buf)
        aliases = {3: 0}
        n_total = out_buf.shape[0]
    return pl.pallas_call(
        functools.partial(_mem_body, aliased=out_buf is not None),
        out_shape=jax.ShapeDtypeStruct((n_total, width), F32),
        grid=(n_batch, per),
        in_specs=in_specs,
        out_specs=q_spec,
        input_output_aliases=aliases,
        compiler_params=_params(("parallel", "parallel")),
        name="mem_attn",
    )(*args)


def _lane_tile_sum(x):
    acc = x[:, 0:LANES]
    for t in range(1, x.shape[1] // LANES):
        acc = acc + x[:, t * LANES:(t + 1) * LANES]
    return acc


def _dsa_body(*refs, sample, key_block, n_keys, n_sel):
    if sample:
        (q_ref, qi_ref, kiwq_ref, kn_ref, vn_ref, kc_ref, vc_ref, kic_ref, _,
         o_ref, ksc, vsc, kisc, keybuf, maskbuf) = refs
    else:
        (q_ref, qi_ref, kiwq_ref, k_ref, v_ref, kiw_ref,
         o_ref, ksc, vsc, kisc, keybuf, maskbuf) = refs
    kb = key_block
    n_pad = ksc.shape[0]

    def stage_keys():
        if sample:
            n_hist = kc_ref.shape[0]
            ksc[0:n_hist, :] = kc_ref[...].astype(BF16)
            vsc[0:n_hist, :] = vc_ref[...].astype(BF16)
            kic = kic_ref[...]
            kisc[0:n_hist, :] = jnp.concatenate([kic, kic], axis=1).astype(BF16)
            ksc[n_hist:n_keys, :] = kn_ref[...].astype(BF16)
            vsc[n_hist:n_keys, :] = vn_ref[...].astype(BF16)
            kiw = kiwq_ref[...]
            lane = lax.broadcasted_iota(I32, kiw.shape, 1)
            kisc[n_hist:n_keys, :] = jnp.where(lane < HEAD_DIM, kiw, pltpu.roll(kiw, HEAD_DIM, 1)).astype(BF16)
            if n_pad > n_keys:
                ksc[n_keys:, :] = jnp.zeros((n_pad - n_keys, WIDTH), BF16)
                vsc[n_keys:, :] = jnp.zeros((n_pad - n_keys, WIDTH), BF16)
                kisc[n_keys:, :] = jnp.zeros((n_pad - n_keys, LANES), BF16)
        else:
            ksc[...] = k_ref[...].astype(BF16)
            vsc[...] = v_ref[...].astype(BF16)
            kiw = kiw_ref[...]
            lane = lax.broadcasted_iota(I32, kiw.shape, 1)
            kisc[...] = jnp.where(lane < HEAD_DIM, kiw, pltpu.roll(kiw, HEAD_DIM, 1)).astype(BF16)

    if sample:
        stage_keys()
        limit = n_keys
        nb = n_pad // kb
    else:
        c = pl.program_id(1)
        pl.when(c == 0)(stage_keys)
        limit = (c + 1) * CHUNK
        nb = (limit + kb - 1) // kb

    def block_cols(j):
        return pl.ds(pl.multiple_of(j * kb, kb), kb)

    def key_pos(j):
        return j * kb + lax.broadcasted_iota(I32, (CHUNK, kb), 1)

    qi = qi_ref[...]
    kiwq = kiwq_ref[...]
    qi_heads = [jnp.where(_head_mask(CHUNK, h % 2), qi[:, (h // 2) * LANES:(h // 2 + 1) * LANES], 0.0).astype(BF16)
                for h in range(N_HEADS)]
    w_heads = [kiwq[:, HEAD_DIM + h:HEAD_DIM + h + 1] for h in range(N_HEADS)]

    def score_block(j, carry):
        kib = kisc[block_cols(j), :]
        acc = jnp.zeros((CHUNK, kb), F32)
        for h in range(N_HEADS):
            d = lax.dot_general(qi_heads[h], kib, _NT_DIMS, preferred_element_type=F32)
            acc = acc + w_heads[h] * jnp.maximum(d, 0.0)
        acc = acc + 0.0
        acc = jnp.where(key_pos(j) < limit, acc, -jnp.inf)
        bits = lax.bitcast_convert_type(acc, I32)
        keybuf[:, block_cols(j)] = bits ^ ((bits >> 31) & 0x7FFFFFFF)
        return carry

    lax.fori_loop(0, nb, score_block, 0)

    def count(pred_fn):
        def body(j, part):
            return part + _lane_tile_sum(jnp.where(pred_fn(keybuf[:, block_cols(j)], j), 1.0, 0.0))
        part = lax.fori_loop(0, nb, body, jnp.zeros((CHUNK, LANES), F32))
        return jnp.sum(part, axis=1, keepdims=True)

    def bit_step(i, prefix):
        cand = prefix | lax.shift_left(jnp.int32(1), 31 - i)
        thr = cand ^ INT_MIN
        cnt = count(lambda kblk, j: kblk >= thr)
        return jnp.where(cnt >= n_sel, cand, prefix)

    prefix = lax.fori_loop(0, 32, bit_step, jnp.zeros((CHUNK, 1), I32))
    thr = prefix ^ INT_MIN
    need = n_sel - count(lambda kblk, j: kblk > thr)

    pos_bits = max(1, (n_pad - 1).bit_length() + 1)

    def pos_step(i, bound):
        cand = bound | lax.shift_left(jnp.int32(1), pos_bits - 1 - i)
        cnt = count(lambda kblk, j: (kblk == thr) & (key_pos(j) < cand))
        return jnp.where(cnt <= need, cand, bound)

    bound = lax.fori_loop(0, pos_bits, pos_step, jnp.zeros((CHUNK, 1), I32))

    def mask_block(j, carry):
        kblk = keybuf[:, block_cols(j)]
        pos = key_pos(j)
        sel = ((kblk > thr) | ((kblk == thr) & (pos < bound))) & (pos < limit)
        maskbuf[:, block_cols(j)] = jnp.where(sel, 0.0, NEG_BIG)
        return carry

    lax.fori_loop(0, nb, mask_block, 0)

    q = q_ref[...] * (HEAD_DIM ** -0.5)
    for hp in range(N_HEADS // 2):
        cols = slice(hp * LANES, (hp + 1) * LANES)
        outs = []
        for hh in range(2):
            qm = jnp.where(_head_mask(CHUNK, hh), q[:, cols], 0.0).astype(BF16)

            def attend(j, carry, qm=qm, cols=cols):
                m_run, l_run, acc = carry
                s = lax.dot_general(qm, ksc[block_cols(j), cols], _NT_DIMS, preferred_element_type=F32)
                s = s + maskbuf[:, block_cols(j)]
                m_new = jnp.maximum(m_run, jnp.max(s, axis=1, keepdims=True))
                alpha = jnp.exp(m_run - m_new)
                p = jnp.exp(s - m_new)
                l_new = alpha * l_run + jnp.sum(p, axis=1, keepdims=True)
                acc = alpha * acc + jnp.dot(p.astype(BF16), vsc[block_cols(j), cols], preferred_element_type=F32)
                return m_new, l_new, acc

            init = (jnp.full((CHUNK, 1), NEG_BIG, F32), jnp.zeros((CHUNK, 1), F32),
                    jnp.zeros((CHUNK, LANES), F32))
            _, l_fin, acc = lax.fori_loop(0, nb, attend, init)
            outs.append(acc / l_fin)
        o_ref[:, cols] = _pair_merge(outs[0], outs[1])


def _dsa_scratch(n_pad):
    return [pltpu.VMEM((n_pad, WIDTH), BF16), pltpu.VMEM((n_pad, WIDTH), BF16),
            pltpu.VMEM((n_pad, LANES), BF16),
            pltpu.VMEM((CHUNK, n_pad), I32), pltpu.VMEM((CHUNK, n_pad), F32)]


def _dsa_prompt(q, k, v, qi, kiw, n_batch, seq, n_total, key_block=256):
    nc = seq // CHUNK
    key_block = min(key_block, seq)
    assert seq % key_block == 0
    n_sel = min(TOPK_KEYS, seq // 4)
    chunk_spec = lambda width: pl.BlockSpec((CHUNK, width), lambda b, c: (b * nc + c, 0))
    seq_spec = lambda width: pl.BlockSpec((seq, width), lambda b, c: (b, 0))
    return pl.pallas_call(
        functools.partial(_dsa_body, sample=False, key_block=key_block, n_keys=seq, n_sel=n_sel),
        out_shape=jax.ShapeDtypeStruct((n_total, WIDTH), F32),
        grid=(n_batch, nc),
        in_specs=[chunk_spec(WIDTH), chunk_spec(WIDTH), chunk_spec(LANES),
                  seq_spec(WIDTH), seq_spec(WIDTH), seq_spec(LANES)],
        out_specs=chunk_spec(WIDTH),
        scratch_shapes=_dsa_scratch(seq),
        compiler_params=_params(("parallel", "arbitrary")),
        name="dsa_prompt",
    )(q, qi, kiw, k, v, kiw)


def _dsa_sample(q, k, v, qi, kiw, k_cache, v_cache, ki_cache, layer, out_buf, n_batch, row0, key_block=256):
    past = k_cache.shape[2]
    n_keys = past + CHUNK
    n_pad = -(-n_keys // key_block) * key_block
    n_sel = min(TOPK_KEYS, n_keys // 4)
    blk0 = row0 // CHUNK
    new_spec = lambda width: pl.BlockSpec((CHUNK, width), lambda b: (blk0 + b, 0))
    cache_spec = lambda width: pl.BlockSpec((None, None, past, width), lambda b: (layer, b, 0, 0))
    return pl.pallas_call(
        functools.partial(_dsa_body, sample=True, key_block=key_block, n_keys=n_keys, n_sel=n_sel),
        out_shape=jax.ShapeDtypeStruct(out_buf.shape, F32),
        grid=(n_batch,),
        in_specs=[new_spec(WIDTH), new_spec(WIDTH), new_spec(LANES), new_spec(WIDTH), new_spec(WIDTH),
                  cache_spec(WIDTH), cache_spec(WIDTH), cache_spec(HEAD_DIM),
                  pl.BlockSpec(memory_space=pl.ANY)],
        out_specs=new_spec(WIDTH),
        scratch_shapes=_dsa_scratch(n_pad),
        input_output_aliases={8: 0},
        compiler_params=_params(("parallel",)),
        name="dsa_sample",
    )(q, qi, kiw, k, v, k_cache, v_cache, ki_cache, out_buf)


def _layer_norm(z, g, b):
    mu = jnp.mean(z, axis=-1, keepdims=True)
    zc = z - mu
    var = jnp.mean(zc * zc, axis=-1, keepdims=True)
    return zc * lax.rsqrt(var + LN_EPS) * g + b


def _merge_body(x_ref, a_ref, b_ref, m_ref, gates_ref, wbr_ref, wout_ref, g_ref, beta_ref, o_ref, *, alpha):
    d = x_ref.shape[1]
    mix = None
    for n, br_ref in enumerate((a_ref, b_ref, m_ref)):
        proj = jnp.dot(br_ref[...].astype(BF16), wbr_ref[n], preferred_element_type=F32)
        term = jax.nn.sigmoid(gates_ref[:, n * d:(n + 1) * d]) * proj
        mix = term if mix is None else mix + term
    y = jnp.dot(mix.astype(BF16), wout_ref[...], preferred_element_type=F32)
    o_ref[...] = _layer_norm(alpha * x_ref[...] + y, g_ref[...], beta_ref[...])


def _merge(x, a, b, m, gates, w_branch, w_out, g, beta, alpha, tm=256):
    n, d = x.shape
    row = lambda width: pl.BlockSpec((tm, width), lambda i: (i, 0))
    full = lambda arr: pl.BlockSpec(arr.shape, lambda i: (0,) * arr.ndim)
    return pl.pallas_call(
        functools.partial(_merge_body, alpha=alpha),
        out_shape=jax.ShapeDtypeStruct((n, d), F32),
        grid=(n // tm,),
        in_specs=[row(d), row(a.shape[1]), row(b.shape[1]), row(m.shape[1]), row(gates.shape[1]),
                  full(w_branch), full(w_out), full(g), full(beta)],
        out_specs=row(d),
        compiler_params=_params(("parallel",)),
        name="merge",
    )(x, a, b, m, gates, w_branch, w_out, g, beta)


def _first_max(vals, idx, n):
    top = jnp.max(vals, axis=0, keepdims=True)
    arg = jnp.min(jnp.where(vals == top, idx, n), axis=0, keepdims=True)
    return top, arg


def _router_body(x_ref, wt_ref, bias_ref, eidx_ref, gw_ref):
    tm = x_ref.shape[0]
    per = N_EXPERTS // N_GROUPS
    logits = lax.dot_general(wt_ref[...], x_ref[...].astype(BF16), _NT_DIMS, preferred_element_type=F32)
    s = jax.nn.sigmoid(logits)
    sb = s + bias_ref[...]
    in_group = lax.broadcasted_iota(I32, (per, tm), 0).astype(F32)
    group_scores = []
    for g in range(N_GROUPS):
        blk = sb[g * per:(g + 1) * per, :]
        top1, arg1 = _first_max(blk, in_group, per)
        top2 = jnp.max(jnp.where(in_group == arg1, -jnp.inf, blk), axis=0, keepdims=True)
        group_scores.append(top1 + top2)
    gs = jnp.concatenate(group_scores, axis=0)
    gidx = lax.broadcasted_iota(I32, (N_GROUPS, tm), 0).astype(F32)
    chosen = jnp.zeros((N_GROUPS, tm), F32)
    for _ in range(TOPK_GROUPS):
        _, arg = _first_max(gs, gidx, N_GROUPS)
        hit = gidx == arg
        chosen = jnp.where(hit, 1.0, chosen)
        gs = jnp.where(hit, -jnp.inf, gs)
    cand = jnp.concatenate(
        [jnp.where(chosen[g:g + 1, :] > 0.0, sb[g * per:(g + 1) * per, :], -jnp.inf) for g in range(N_GROUPS)],
        axis=0)
    eidx = lax.broadcasted_iota(I32, (N_EXPERTS, tm), 0).astype(F32)
    picks, weights = [], []
    for _ in range(EXPERT_TOPK):
        _, arg = _first_max(cand, eidx, N_EXPERTS)
        hit = eidx == arg
        weights.append(jnp.sum(jnp.where(hit, s, 0.0), axis=0, keepdims=True))
        picks.append(arg)
        cand = jnp.where(hit, -jnp.inf, cand)
    total = weights[0]
    for w in weights[1:]:
        total = total + w
    for r in range(EXPERT_TOPK):
        eidx_ref[r:r + 1, :] = picks[r].astype(I32)
        gw_ref[r:r + 1, :] = weights[r] / total * ROUTE_SCALE


def _router(x, w_router_t, b_router_col, tm=256):
    n, d = x.shape
    return pl.pallas_call(
        _router_body,
        out_shape=[jax.ShapeDtypeStruct((EXPERT_TOPK, n), I32), jax.ShapeDtypeStruct((EXPERT_TOPK, n), F32)],
        grid=(n // tm,),
        in_specs=[pl.BlockSpec((tm, d), lambda i: (i, 0)),
                  pl.BlockSpec(w_router_t.shape, lambda i: (0, 0)),
                  pl.BlockSpec(b_router_col.shape, lambda i: (0, 0))],
        out_specs=[pl.BlockSpec((EXPERT_TOPK, tm), lambda i: (0, i)),
                   pl.BlockSpec((EXPERT_TOPK, tm), lambda i: (0, i))],
        compiler_params=_params(("parallel",)),
        name="router",
    )(x, w_router_t, b_router_col)


def _experts_body(block_exp_ref, n_used_ref, x_ref, wgu_ref, wdn_ref, roww_ref, o_ref, wgu_bf, wdn_bf):
    i = pl.program_id(0)
    used = i < n_used_ref[0]

    @pl.when(used)
    def _():
        prev = block_exp_ref[jnp.maximum(i - 1, 0)]

        @pl.when((i == 0) | (block_exp_ref[i] != prev))
        def _():
            wgu_bf[...] = wgu_ref[...].astype(BF16)
            wdn_bf[...] = wdn_ref[...].astype(BF16)

        h = jnp.dot(x_ref[...], wgu_bf[...], preferred_element_type=F32)
        ff = h.shape[1] // 2
        act = jax.nn.silu(h[:, :ff]) * h[:, ff:]
        y = jnp.dot(act.astype(BF16), wdn_bf[...], preferred_element_type=F32)
        o_ref[...] = y * roww_ref[...]

    @pl.when(jnp.logical_not(used))
    def _():
        o_ref[...] = jnp.zeros(o_ref.shape, F32)


def _experts(xb, block_exp, n_used, w_gu, w_dn, row_w, layer):
    rows, d = xb.shape
    n_blocks = rows // EXPERT_BLOCK
    ff2 = w_gu.shape[-1]
    grid_spec = pltpu.PrefetchScalarGridSpec(
        num_scalar_prefetch=2,
        grid=(n_blocks,),
        in_specs=[pl.BlockSpec((EXPERT_BLOCK, d), lambda i, be, nu: (i, 0)),
                  pl.BlockSpec((None, None, d, ff2), lambda i, be, nu: (layer, be[i], 0, 0)),
                  pl.BlockSpec((None, None, ff2 // 2, d), lambda i, be, nu: (layer, be[i], 0, 0)),
                  pl.BlockSpec((EXPERT_BLOCK, 1), lambda i, be, nu: (i, 0))],
        out_specs=pl.BlockSpec((EXPERT_BLOCK, d), lambda i, be, nu: (i, 0)),
        scratch_shapes=[pltpu.VMEM((d, ff2), BF16), pltpu.VMEM((ff2 // 2, d), BF16)],
    )
    return pl.pallas_call(
        _experts_body,
        out_shape=jax.ShapeDtypeStruct((rows, d), F32),
        grid_spec=grid_spec,
        compiler_params=_params(("arbitrary",)),
        name="experts",
    )(block_exp, n_used, xb, w_gu, w_dn, row_w)


def _dispatch(eidx, gw, n):
    n_assign = n * EXPERT_TOPK
    e_flat = eidx.reshape(-1)
    order = jnp.argsort(e_flat)
    e_sorted = e_flat[order]
    tok_sorted = (order // EXPERT_TOPK).astype(I32)
    w_sorted = gw.reshape(-1)[order]
    counts = jnp.bincount(e_flat, length=N_EXPERTS)
    padded = (counts + EXPERT_BLOCK - 1) // EXPERT_BLOCK * EXPERT_BLOCK
    pad_end = jnp.cumsum(padded)
    pad_start = pad_end - padded
    start = jnp.cumsum(counts) - counts
    dest = pad_start[e_sorted] + jnp.arange(n_assign) - start[e_sorted]
    n_blocks = -(-n_assign // EXPERT_BLOCK) + N_EXPERTS
    rows = n_blocks * EXPERT_BLOCK
    row_tok = jnp.full((rows,), n, I32).at[dest].set(tok_sorted)
    row_w = jnp.zeros((rows,), F32).at[dest].set(w_sorted)
    block_exp = jnp.minimum(
        jnp.searchsorted(pad_end // EXPERT_BLOCK, jnp.arange(n_blocks), side="right"), N_EXPERTS - 1).astype(I32)
    n_used = (pad_end[-1] // EXPERT_BLOCK).astype(I32).reshape(1)
    return row_tok, row_w, block_exp, n_used


def _ffn_out_body(x_ref, routed_ref, wgu_ref, wdn_ref, g_ref, beta_ref, o_ref, *, alpha):
    x = x_ref[...]
    h = jnp.dot(x.astype(BF16), wgu_ref[...], preferred_element_type=F32)
    ff = h.shape[1] // 2
    act = jax.nn.silu(h[:, :ff]) * h[:, ff:]
    shared = jnp.dot(act.astype(BF16), wdn_ref[...], preferred_element_type=F32)
    o_ref[...] = _layer_norm(alpha * x + (routed_ref[...] + shared), g_ref[...], beta_ref[...])


def _ffn_out(x, routed, ws_gu, ws_dn, g, beta, alpha, tm=512):
    n, d = x.shape
    row = pl.BlockSpec((tm, d), lambda i: (i, 0))
    full = lambda arr: pl.BlockSpec(arr.shape, lambda i: (0,) * arr.ndim)
    return pl.pallas_call(
        functools.partial(_ffn_out_body, alpha=alpha),
        out_shape=jax.ShapeDtypeStruct((n, d), F32),
        grid=(n // tm,),
        in_specs=[row, row, full(ws_gu), full(ws_dn), full(g), full(beta)],
        out_specs=row,
        compiler_params=_params(("parallel",)),
        name="ffn_out",
    )(x, routed, ws_gu, ws_dn, g, beta)


def _moe(x, layer, w_router_t, b_router_col, w_exp_gate_up, w_exp_down, ws_gu, ws_dn, g, beta, alpha):
    n, d = x.shape
    eidx_t, gw_t = _router(x, w_router_t, b_router_col)
    row_tok, row_w, block_exp, n_used = _dispatch(eidx_t.T, gw_t.T, n)
    x_rows = jnp.concatenate([x.astype(BF16), jnp.zeros((1, d), BF16)])[row_tok]
    yb = _experts(x_rows, block_exp, n_used, w_exp_gate_up, w_exp_down, row_w[:, None], layer)
    routed = jax.ops.segment_sum(yb, row_tok, num_segments=n + 1)[:n]
    return _ffn_out(x, routed, ws_gu, ws_dn, g, beta, alpha)


def kernel(x_prompt, x_sample, mem_prompt, cache_a_k, cache_a_v, cache_b_k, cache_b_v, cache_b_idx_k,
           cache_mem_k, cache_mem_v, w_in, rel_bias, w_mem_kv, w_branch, w_out, ln_mix_g, ln_mix_b,
           w_router, b_router, w_exp_gate_up, w_exp_down, w_sh_gate_up, w_sh_down, ln_ffn_g, ln_ffn_b):
    n_batch, seq, d = x_prompt.shape
    s_batch, s_seq, _ = x_sample.shape
    depth = w_in.shape[0]
    past = cache_b_k.shape[2]
    assert s_seq == CHUNK and seq % CHUNK == 0 and cache_a_k.shape[2] == BAND and seq >= BAND
    alpha = (2 * depth) ** 0.25
    n_p, n_s = n_batch * seq, s_batch * s_seq
    n_tot = n_p + n_s
    tm = 512
    assert seq % tm == 0 and n_p % tm == 0 and n_s % tm == 0 and tm % CHUNK == 0

    x = jnp.concatenate([x_prompt.reshape(n_p, d), x_sample.reshape(n_s, d)])

    pos = jnp.concatenate([jnp.arange(seq), jnp.tile(past + jnp.arange(s_seq), tm // s_seq)])
    tab_heads = _rope_tables(pos, LANES)
    tab_kiw = _rope_tables(pos, HEAD_DIM)
    p_tiles, seq_tiles = n_p // tm, seq // tm
    table_index = lambda i: jnp.where(i < p_tiles, i % seq_tiles, seq_tiles)

    wd = WIDTH
    c_qi, c_ki, c_wi, c_qm, c_gates = 6 * wd, 7 * wd, 7 * wd + HEAD_DIM, 7 * wd + HEAD_DIM + N_HEADS, 8 * wd + HEAD_DIM + N_HEADS
    cache_a_k = cache_a_k.reshape(depth, s_batch, BAND, wd)
    cache_a_v = cache_a_v.reshape(depth, s_batch, BAND, wd)
    cache_b_k = cache_b_k.reshape(depth, s_batch, past, wd)
    cache_b_v = cache_b_v.reshape(depth, s_batch, past, wd)
    cache_mem_k = cache_mem_k.reshape(depth, s_batch, N_MEM, wd)
    cache_mem_v = cache_mem_v.reshape(depth, s_batch, N_MEM, wd)
    mem_rows = mem_prompt.reshape(n_batch * N_MEM, d)

    outs = {k: [] for k in ("p_ak", "p_av", "p_bk", "p_bv", "p_ik", "p_mk", "p_mv",
                            "s_ak", "s_av", "s_bk", "s_bv", "s_ik")}
    for l in range(depth):
        w = w_in[l]
        w_a = w[:, :3 * wd].astype(BF16)
        w_b = w[:, 3 * wd:7 * wd].astype(BF16)
        w_c = w[:, c_qm:].astype(BF16)
        w_d = jnp.concatenate([w[:, c_ki:c_qm], jnp.zeros((d, LANES - HEAD_DIM - N_HEADS), F32)], 1).astype(BF16)

        qa, ka, va = _project(x, w_a, (wd, wd, wd), tm=tm, name="proj_a")
        qb, kb, vb, qi = _project(x, w_b, (wd, wd, wd, wd), (True, True, False, True), tab_heads, table_index,
                                  tm=tm, name="proj_b")
        qm, gates = _project(x, w_c, (wd, 3 * d), tm=tm, name="proj_c")
        (kiw,) = _project(x, w_d, (LANES,), (True,), tab_kiw, table_index, tm=tm, name="proj_d")
        mk, mv = _project(mem_rows, w_mem_kv[l].astype(BF16), (wd, wd), tm=N_MEM, name="proj_mem")

        bias = _band_bias(rel_bias[l])
        a = _band_prompt(qa, ka, va, bias, n_batch, seq, n_tot)
        a = _band_sample(qa, ka, va, cache_a_k, cache_a_v, l, bias, a, s_batch, n_p)
        b = _dsa_prompt(qb, kb, vb, qi, kiw, n_batch, seq, n_tot)
        b = _dsa_sample(qb, kb, vb, qi, kiw, cache_b_k, cache_b_v, cache_b_idx_k, l, b, s_batch, n_p)
        m = _mem_attn(qm, mk, mv, lambda bb: (bb, 0), n_batch, seq, 0, tm, n_total=n_tot)
        m = _mem_attn(qm, cache_mem_k, cache_mem_v, lambda bb, l=l: (l, bb, 0, 0), s_batch, s_seq, n_p, s_seq,
                      out_buf=m)

        x1 = _merge(x, a, b, m, gates, w_branch[l].astype(BF16), w_out[l].astype(BF16),
                    ln_mix_g[l][None], ln_mix_b[l][None], alpha)
        x = _moe(x1, l, w_router[l].T.astype(BF16), b_router[l][:, None], w_exp_gate_up, w_exp_down,
                 w_sh_gate_up[l].astype(BF16), w_sh_down[l].astype(BF16),
                 ln_ffn_g[l][None], ln_ffn_b[l][None], alpha)

        ka_p = ka[:n_p].reshape(n_batch, seq, wd)
        va_p = va[:n_p].reshape(n_batch, seq, wd)
        outs["p_ak"].append(ka_p[:, -BAND:])
        outs["p_av"].append(va_p[:, -BAND:])
        outs["p_bk"].append(kb[:n_p].reshape(n_batch, seq, wd))
        outs["p_bv"].append(vb[:n_p].reshape(n_batch, seq, wd))
        outs["p_ik"].append(kiw[:n_p, :HEAD_DIM].reshape(n_batch, seq, HEAD_DIM))
        outs["p_mk"].append(mk.reshape(n_batch, N_MEM, wd))
        outs["p_mv"].append(mv.reshape(n_batch, N_MEM, wd))
        ka_s = ka[n_p:].reshape(s_batch, s_seq, wd)
        va_s = va[n_p:].reshape(s_batch, s_seq, wd)
        outs["s_ak"].append(jnp.concatenate([cache_a_k[l], ka_s], 1)[:, -BAND:])
        outs["s_av"].append(jnp.concatenate([cache_a_v[l], va_s], 1)[:, -BAND:])
        outs["s_bk"].append(kb[n_p:].reshape(s_batch, s_seq, wd))
        outs["s_bv"].append(vb[n_p:].reshape(s_batch, s_seq, wd))
        outs["s_ik"].append(kiw[n_p:, :HEAD_DIM].reshape(s_batch, s_seq, HEAD_DIM))

    heads = lambda t: t.reshape(t.shape[:-1] + (N_HEADS, HEAD_DIM))
    mheads = lambda t: t.reshape(t.shape[:-1] + (M_HEADS, M_HEAD_DIM))
    st = lambda key: jnp.stack(outs[key])
    return (x[:n_p].reshape(n_batch, seq, d), x[n_p:].reshape(s_batch, s_seq, d),
            heads(st("p_ak")), heads(st("p_av")), heads(st("p_bk")), heads(st("p_bv")), st("p_ik"),
            mheads(st("p_mk")), mheads(st("p_mv")),
            heads(st("s_ak")), heads(st("s_av")), heads(st("s_bk")), heads(st("s_bv")), st("s_ik"))
```

```python
import functools

import jax
import jax.numpy as jnp
from jax import lax
from jax.experimental import pallas as pl
from jax.experimental.pallas import tpu as pltpu

F32 = jnp.float32
BF16 = jnp.bfloat16
I32 = jnp.int32

CHUNK = 64
HEAD_DIM = 64
N_HEADS = 8
WIDTH = N_HEADS * HEAD_DIM
BAND = 8 * CHUNK
MAX_REL = 128
TOPK_KEYS = 256
M_HEADS = 4
M_HEAD_DIM = 128
N_MEM = 256
ROPE_THETA = 10000.0
N_EXPERTS = 256
EXPERT_TOPK = 8
N_GROUPS = 8
TOPK_GROUPS = 4
EXPERT_FF = 256
ROUTE_SCALE = 2.5
EXPERT_BLOCK = 128
LN_EPS = 1e-5

LANES = 128
NEG_BIG = -1e30
INT_MIN = -(2 ** 31)
VMEM_LIMIT = 48 * 1024 * 1024

_NT_DIMS = (((1,), (1,)), ((), ()))


def _params(semantics):
    return pltpu.CompilerParams(dimension_semantics=semantics, vmem_limit_bytes=VMEM_LIMIT)


def _proj_body(*refs, widths, rope_flags):
    has_rope = any(rope_flags)
    x_ref, w_ref = refs[0], refs[1]
    if has_rope:
        cos_ref, sin_lo_ref, sin_hi_ref = refs[2:5]
        outs = refs[5:]
    else:
        outs = refs[2:]
    xb = x_ref[...].astype(BF16)
    off = 0
    for g, width in enumerate(widths):
        y = jnp.dot(xb, w_ref[:, off:off + width], preferred_element_type=F32)
        if rope_flags[g]:
            cos, sin_lo, sin_hi = cos_ref[...], sin_lo_ref[...], sin_hi_ref[...]
            for k in range(width // LANES):
                yk = y[:, k * LANES:(k + 1) * LANES]
                outs[g][:, k * LANES:(k + 1) * LANES] = (
                    yk * cos + pltpu.roll(yk, LANES - 32, 1) * sin_lo + pltpu.roll(yk, 32, 1) * sin_hi)
        else:
            outs[g][...] = y
        off += width


def _project(x, w, widths, rope_flags=None, tables=None, table_index=None, tm=512, name="proj"):
    n, d = x.shape
    rope_flags = rope_flags or (False,) * len(widths)
    assert n % tm == 0 and w.shape == (d, sum(widths))
    in_specs = [pl.BlockSpec((tm, d), lambda i: (i, 0)),
                pl.BlockSpec(w.shape, lambda i: (0, 0))]
    args = [x, w]
    if any(rope_flags):
        for t in tables:
            in_specs.append(pl.BlockSpec((tm, LANES), lambda i: (table_index(i), 0)))
            args.append(t)
    return pl.pallas_call(
        functools.partial(_proj_body, widths=tuple(widths), rope_flags=tuple(rope_flags)),
        out_shape=[jax.ShapeDtypeStruct((n, wd), F32) for wd in widths],
        grid=(n // tm,),
        in_specs=in_specs,
        out_specs=[pl.BlockSpec((tm, wd), lambda i: (i, 0)) for wd in widths],
        compiler_params=_params(("parallel",)),
        name=name,
    )(*args)


def _rope_tables(pos, rope_lanes):
    half = HEAD_DIM // 2
    inv_freq = ROPE_THETA ** (-jnp.arange(half, dtype=F32) / half)
    ang = pos.astype(F32)[:, None] * inv_freq[None, :]
    cos, sin = jnp.cos(ang), jnp.sin(ang)
    reps = LANES // HEAD_DIM
    zero = jnp.zeros_like(sin)
    cos_t = jnp.tile(jnp.concatenate([cos, cos], -1), (1, reps))
    sin_lo = jnp.tile(jnp.concatenate([-sin, zero], -1), (1, reps))
    sin_hi = jnp.tile(jnp.concatenate([zero, sin], -1), (1, reps))
    keep = (jnp.arange(LANES) < rope_lanes)[None, :]
    return (jnp.where(keep, cos_t, 1.0), jnp.where(keep, sin_lo, 0.0), jnp.where(keep, sin_hi, 0.0))


def _head_mask(rows, head_in_pair):
    lane = lax.broadcasted_iota(I32, (rows, LANES), 1)
    lo = head_in_pair * HEAD_DIM
    return (lane >= lo) & (lane < lo + HEAD_DIM)


def _pair_merge(o_even, o_odd):
    lane = lax.broadcasted_iota(I32, o_even.shape, 1)
    return jnp.where(lane < HEAD_DIM, o_even, o_odd)


def _band_body(*refs, sample):
    if sample:
        q_ref, kn_ref, vn_ref, kh_ref, vh_ref, bias_ref, o_ref, kpad, vpad = refs
        kpad[0:BAND, :] = kh_ref[...].astype(BF16)
        vpad[0:BAND, :] = vh_ref[...].astype(BF16)
        kpad[BAND:, :] = kn_ref[...].astype(BF16)
        vpad[BAND:, :] = vn_ref[...].astype(BF16)
        start = 0
        first_valid = 0
    else:
        q_ref, k_ref, v_ref, bias_ref, o_ref, kpad, vpad = refs
        c = pl.program_id(1)

        @pl.when(c == 0)
        def _():
            kpad[0:BAND, :] = jnp.zeros((BAND, WIDTH), BF16)
            vpad[0:BAND, :] = jnp.zeros((BAND, WIDTH), BF16)
            kpad[BAND:, :] = k_ref[...].astype(BF16)
            vpad[BAND:, :] = v_ref[...].astype(BF16)

        start = pl.multiple_of(c * CHUNK, CHUNK)
        first_valid = BAND - c * CHUNK

    win = BAND + CHUNK
    q = q_ref[...] * (HEAD_DIM ** -0.5)
    valid = lax.broadcasted_iota(I32, (CHUNK, win), 1) >= first_valid
    for hp in range(N_HEADS // 2):
        cols = slice(hp * LANES, (hp + 1) * LANES)
        q2 = q[:, cols]
        k2 = kpad[pl.ds(start, win), cols]
        v2 = vpad[pl.ds(start, win), cols]
        outs = []
        for hh in range(2):
            qm = jnp.where(_head_mask(CHUNK, hh), q2, 0.0).astype(BF16)
            s = lax.dot_general(qm, k2, _NT_DIMS, preferred_element_type=F32)
            s = s + bias_ref[2 * hp + hh]
            if not sample:
                s = jnp.where(valid, s, -jnp.inf)
            p = jnp.exp(s - jnp.max(s, axis=1, keepdims=True))
            denom = jnp.sum(p, axis=1, keepdims=True)
            o = jnp.dot(p.astype(BF16), v2, preferred_element_type=F32)
            outs.append(o / denom)
        o_ref[:, cols] = _pair_merge(outs[0], outs[1])


def _band_bias(table):
    rel = jnp.arange(CHUNK)[:, None] + BAND - jnp.arange(BAND + CHUNK)[None, :]
    return table[:, jnp.clip(rel, -MAX_REL, MAX_REL) + MAX_REL].astype(F32)


def _band_prompt(q, k, v, bias, n_batch, seq):
    nc = seq // CHUNK
    return pl.pallas_call(
        functools.partial(_band_body, sample=False),
        out_shape=jax.ShapeDtypeStruct((n_batch * seq, WIDTH), F32),
        grid=(n_batch, nc),
        in_specs=[pl.BlockSpec((CHUNK, WIDTH), lambda b, c: (b * nc + c, 0)),
                  pl.BlockSpec((seq, WIDTH), lambda b, c: (b, 0)),
                  pl.BlockSpec((seq, WIDTH), lambda b, c: (b, 0)),
                  pl.BlockSpec(bias.shape, lambda b, c: (0, 0, 0))],
        out_specs=pl.BlockSpec((CHUNK, WIDTH), lambda b, c: (b * nc + c, 0)),
        scratch_shapes=[pltpu.VMEM((BAND + seq, WIDTH), BF16), pltpu.VMEM((BAND + seq, WIDTH), BF16)],
        compiler_params=_params(("parallel", "arbitrary")),
        name="band_prompt",
    )(q, k, v, bias)


def _band_sample(q, k, v, k_hist, v_hist, layer, bias, n_batch, row0):
    blk0 = row0 // CHUNK
    hist_spec = pl.BlockSpec((None, None, BAND, WIDTH), lambda b: (layer, b, 0, 0))
    new_spec = pl.BlockSpec((CHUNK, WIDTH), lambda b: (blk0 + b, 0))
    return pl.pallas_call(
        functools.partial(_band_body, sample=True),
        out_shape=jax.ShapeDtypeStruct((n_batch * CHUNK, WIDTH), F32),
        grid=(n_batch,),
        in_specs=[new_spec, new_spec, new_spec, hist_spec, hist_spec,
                  pl.BlockSpec(bias.shape, lambda b: (0, 0, 0))],
        out_specs=pl.BlockSpec((CHUNK, WIDTH), lambda b: (b, 0)),
        scratch_shapes=[pltpu.VMEM((BAND + CHUNK, WIDTH), BF16), pltpu.VMEM((BAND + CHUNK, WIDTH), BF16)],
        compiler_params=_params(("parallel",)),
        name="band_sample",
    )(q, k, v, k_hist, v_hist, bias)


def _mem_body(q_ref, k_ref, v_ref, o_ref):
    scale = M_HEAD_DIM ** -0.5
    for h in range(M_HEADS):
        cols = slice(h * M_HEAD_DIM, (h + 1) * M_HEAD_DIM)
        s = lax.dot_general(q_ref[:, cols].astype(BF16), k_ref[:, cols].astype(BF16), _NT_DIMS,
                            preferred_element_type=F32) * scale
        p = jnp.exp(s - jnp.max(s, axis=1, keepdims=True))
        denom = jnp.sum(p, axis=1, keepdims=True)
        o = jnp.dot(p.astype(BF16), v_ref[:, cols].astype(BF16), preferred_element_type=F32)
        o_ref[:, cols] = o / denom


def _mem_attn(q, mk, mv, kv_index, n_batch, rows_per_batch, row0, tq):
    width = M_HEADS * M_HEAD_DIM
    per = rows_per_batch // tq
    blk0 = row0 // tq
    kv_block = (None,) * (mk.ndim - 2) + (N_MEM, width)
    kv_spec = pl.BlockSpec(kv_block, lambda b, t: kv_index(b))
    return pl.pallas_call(
        _mem_body,
        out_shape=jax.ShapeDtypeStruct((n_batch * rows_per_batch, width), F32),
        grid=(n_batch, per),
        in_specs=[pl.BlockSpec((tq, width), lambda b, t: (blk0 + b * per + t, 0)), kv_spec, kv_spec],
        out_specs=pl.BlockSpec((tq, width), lambda b, t: (b * per + t, 0)),
        compiler_params=_params(("parallel", "parallel")),
        name="mem_attn",
    )(q, mk, mv)


def _lane_tile_sum(x):
    acc = x[:, 0:LANES]
    for t in range(1, x.shape[1] // LANES):
        acc = acc + x[:, t * LANES:(t + 1) * LANES]
    return acc


def _dsa_body(*refs, sample, key_block, n_keys, n_sel):
    if sample:
        (q_ref, qi_ref, kiwq_ref, kn_ref, vn_ref, kc_ref, vc_ref, kic_ref,
         o_ref, ksc, vsc, kisc, keybuf, maskbuf) = refs
    else:
        (q_ref, qi_ref, kiwq_ref, k_ref, v_ref, kiw_ref,
         o_ref, ksc, vsc, kisc, keybuf, maskbuf) = refs
    kb = key_block
    n_pad = ksc.shape[0]

    def stage_keys():
        if sample:
            n_hist = kc_ref.shape[0]
            ksc[0:n_hist, :] = kc_ref[...].astype(BF16)
            vsc[0:n_hist, :] = vc_ref[...].astype(BF16)
            kic = kic_ref[...]
            kisc[0:n_hist, :] = jnp.concatenate([kic, kic], axis=1).astype(BF16)
            ksc[n_hist:n_keys, :] = kn_ref[...].astype(BF16)
            vsc[n_hist:n_keys, :] = vn_ref[...].astype(BF16)
            kiw = kiwq_ref[...]
            lane = lax.broadcasted_iota(I32, kiw.shape, 1)
            kisc[n_hist:n_keys, :] = jnp.where(lane < HEAD_DIM, kiw, pltpu.roll(kiw, HEAD_DIM, 1)).astype(BF16)
            if n_pad > n_keys:
                ksc[n_keys:, :] = jnp.zeros((n_pad - n_keys, WIDTH), BF16)
                vsc[n_keys:, :] = jnp.zeros((n_pad - n_keys, WIDTH), BF16)
                kisc[n_keys:, :] = jnp.zeros((n_pad - n_keys, LANES), BF16)
        else:
            ksc[...] = k_ref[...].astype(BF16)
            vsc[...] = v_ref[...].astype(BF16)
            kiw = kiw_ref[...]
            lane = lax.broadcasted_iota(I32, kiw.shape, 1)
            kisc[...] = jnp.where(lane < HEAD_DIM, kiw, pltpu.roll(kiw, HEAD_DIM, 1)).astype(BF16)

    if sample:
        stage_keys()
        limit = n_keys
        nb = n_pad // kb
    else:
        c = pl.program_id(1)
        pl.when(c == 0)(stage_keys)
        limit = (c + 1) * CHUNK
        nb = (limit + kb - 1) // kb

    def block_cols(j):
        return pl.ds(pl.multiple_of(j * kb, kb), kb)

    def key_pos(j):
        return j * kb + lax.broadcasted_iota(I32, (CHUNK, kb), 1)

    qi = qi_ref[...]
    kiwq = kiwq_ref[...]
    qi_heads = [jnp.where(_head_mask(CHUNK, h % 2), qi[:, (h // 2) * LANES:(h // 2 + 1) * LANES], 0.0).astype(BF16)
                for h in range(N_HEADS)]
    w_heads = [kiwq[:, HEAD_DIM + h:HEAD_DIM + h + 1] for h in range(N_HEADS)]

    def score_block(j, carry):
        kib = kisc[block_cols(j), :]
        acc = jnp.zeros((CHUNK, kb), F32)
        for h in range(N_HEADS):
            d = lax.dot_general(qi_heads[h], kib, _NT_DIMS, preferred_element_type=F32)
            acc = acc + w_heads[h] * jnp.maximum(d, 0.0)
        acc = acc + 0.0
        acc = jnp.where(key_pos(j) < limit, acc, -jnp.inf)
        bits = lax.bitcast_convert_type(acc, I32)
        keybuf[:, block_cols(j)] = bits ^ ((bits >> 31) & 0x7FFFFFFF)
        return carry

    lax.fori_loop(0, nb, score_block, 0)

    def count(pred_fn):
        def body(j, part):
            return part + _lane_tile_sum(jnp.where(pred_fn(keybuf[:, block_cols(j)], j), 1.0, 0.0))
        part = lax.fori_loop(0, nb, body, jnp.zeros((CHUNK, LANES), F32))
        return jnp.sum(part, axis=1, keepdims=True)

    def bit_step(i, prefix):
        cand = prefix | lax.shift_left(jnp.int32(1), 31 - i)
        thr = cand ^ INT_MIN
        cnt = count(lambda kblk, j: kblk >= thr)
        return jnp.where(cnt >= n_sel, cand, prefix)

    prefix = lax.fori_loop(0, 32, bit_step, jnp.zeros((CHUNK, 1), I32))
    thr = prefix ^ INT_MIN
    need = n_sel - count(lambda kblk, j: kblk > thr)

    pos_bits = max(1, (n_pad - 1).bit_length() + 1)

    def pos_step(i, bound):
        cand = bound | lax.shift_left(jnp.int32(1), pos_bits - 1 - i)
        cnt = count(lambda kblk, j: (kblk == thr) & (key_pos(j) < cand))
        return jnp.where(cnt <= need, cand, bound)

    bound = lax.fori_loop(0, pos_bits, pos_step, jnp.zeros((CHUNK, 1), I32))

    def mask_block(j, carry):
        kblk = keybuf[:, block_cols(j)]
        pos = key_pos(j)
        sel = ((kblk > thr) | ((kblk == thr) & (pos < bound))) & (pos < limit)
        maskbuf[:, block_cols(j)] = jnp.where(sel, 0.0, NEG_BIG)
        return carry

    lax.fori_loop(0, nb, mask_block, 0)

    q = q_ref[...] * (HEAD_DIM ** -0.5)
    for hp in range(N_HEADS // 2):
        cols = slice(hp * LANES, (hp + 1) * LANES)
        outs = []
        for hh in range(2):
            qm = jnp.where(_head_mask(CHUNK, hh), q[:, cols], 0.0).astype(BF16)

            def attend(j, carry, qm=qm, cols=cols):
                m_run, l_run, acc = carry
                s = lax.dot_general(qm, ksc[block_cols(j), cols], _NT_DIMS, preferred_element_type=F32)
                s = s + maskbuf[:, block_cols(j)]
                m_new = jnp.maximum(m_run, jnp.max(s, axis=1, keepdims=True))
                alpha = jnp.exp(m_run - m_new)
                p = jnp.exp(s - m_new)
                l_new = alpha * l_run + jnp.sum(p, axis=1, keepdims=True)
                acc = alpha * acc + jnp.dot(p.astype(BF16), vsc[block_cols(j), cols], preferred_element_type=F32)
                return m_new, l_new, acc

            init = (jnp.full((CHUNK, 1), NEG_BIG, F32), jnp.zeros((CHUNK, 1), F32),
                    jnp.zeros((CHUNK, LANES), F32))
            _, l_fin, acc = lax.fori_loop(0, nb, attend, init)
            outs.append(acc / l_fin)
        o_ref[:, cols] = _pair_merge(outs[0], outs[1])


def _dsa_scratch(n_pad):
    return [pltpu.VMEM((n_pad, WIDTH), BF16), pltpu.VMEM((n_pad, WIDTH), BF16),
            pltpu.VMEM((n_pad, LANES), BF16),
            pltpu.VMEM((CHUNK, n_pad), I32), pltpu.VMEM((CHUNK, n_pad), F32)]


def _dsa_prompt(q, k, v, qi, kiw, n_batch, seq, key_block=256):
    nc = seq // CHUNK
    key_block = min(key_block, seq)
    assert seq % key_block == 0
    n_sel = min(TOPK_KEYS, seq // 4)
    chunk_spec = lambda width: pl.BlockSpec((CHUNK, width), lambda b, c: (b * nc + c, 0))
    seq_spec = lambda width: pl.BlockSpec((seq, width), lambda b, c: (b, 0))
    return pl.pallas_call(
        functools.partial(_dsa_body, sample=False, key_block=key_block, n_keys=seq, n_sel=n_sel),
        out_shape=jax.ShapeDtypeStruct((n_batch * seq, WIDTH), F32),
        grid=(n_batch, nc),
        in_specs=[chunk_spec(WIDTH), chunk_spec(WIDTH), chunk_spec(LANES),
                  seq_spec(WIDTH), seq_spec(WIDTH), seq_spec(LANES)],
        out_specs=chunk_spec(WIDTH),
        scratch_shapes=_dsa_scratch(seq),
        compiler_params=_params(("parallel", "arbitrary")),
        name="dsa_prompt",
    )(q, qi, kiw, k, v, kiw)


def _dsa_sample(q, k, v, qi, kiw, k_cache, v_cache, ki_cache, layer, n_batch, row0, key_block=256):
    past = k_cache.shape[2]
    n_keys = past + CHUNK
    n_pad = -(-n_keys // key_block) * key_block
    n_sel = min(TOPK_KEYS, n_keys // 4)
    blk0 = row0 // CHUNK
    new_spec = lambda width: pl.BlockSpec((CHUNK, width), lambda b: (blk0 + b, 0))
    cache_spec = lambda width: pl.BlockSpec((None, None, past, width), lambda b: (layer, b, 0, 0))
    return pl.pallas_call(
        functools.partial(_dsa_body, sample=True, key_block=key_block, n_keys=n_keys, n_sel=n_sel),
        out_shape=jax.ShapeDtypeStruct((n_batch * CHUNK, WIDTH), F32),
        grid=(n_batch,),
        in_specs=[new_spec(WIDTH), new_spec(WIDTH), new_spec(LANES), new_spec(WIDTH), new_spec(WIDTH),
                  cache_spec(WIDTH), cache_spec(WIDTH), cache_spec(HEAD_DIM)],
        out_specs=pl.BlockSpec((CHUNK, WIDTH), lambda b: (b, 0)),
        scratch_shapes=_dsa_scratch(n_pad),
        compiler_params=_params(("parallel",)),
        name="dsa_sample",
    )(q, qi, kiw, k, v, k_cache, v_cache, ki_cache)


def _layer_norm(z, g, b):
    mu = jnp.mean(z, axis=-1, keepdims=True)
    zc = z - mu
    var = jnp.mean(zc * zc, axis=-1, keepdims=True)
    return zc * lax.rsqrt(var + LN_EPS) * g + b


HI_HALF = -65536


def _pack_rows(x):
    half = x.shape[1] // 2
    lo = lax.bitcast_convert_type(x[:, :half].astype(BF16).astype(F32), I32)
    hi = lax.bitcast_convert_type(x[:, half:].astype(BF16).astype(F32), I32)
    return (hi & HI_HALF) | lax.shift_right_logical(lo, 16)


def _unpack_rows(w):
    lo = lax.bitcast_convert_type(lax.shift_left(w, 16), F32).astype(BF16)
    hi = lax.bitcast_convert_type(w & HI_HALF, F32).astype(BF16)
    return lo, hi


def _merge_body(x_ref, ap_ref, as_ref, bp_ref, bs_ref, mp_ref, ms_ref, gates_ref, wbr_ref, wout_ref, g_ref, beta_ref,
                o_ref, ow_ref, *, alpha, prompt_tiles):
    d = x_ref.shape[1]
    in_prompt = pl.program_id(0) < prompt_tiles
    mix = None
    for n, (p_ref, s_ref) in enumerate(((ap_ref, as_ref), (bp_ref, bs_ref), (mp_ref, ms_ref))):
        branch = jnp.where(in_prompt, p_ref[...], s_ref[...])
        proj = jnp.dot(branch.astype(BF16), wbr_ref[n], preferred_element_type=F32)
        term = jax.nn.sigmoid(gates_ref[:, n * d:(n + 1) * d]) * proj
        mix = term if mix is None else mix + term
    y = jnp.dot(mix.astype(BF16), wout_ref[...], preferred_element_type=F32)
    out = _layer_norm(alpha * x_ref[...] + y, g_ref[...], beta_ref[...])
    o_ref[...] = out
    ow_ref[...] = _pack_rows(out)


def _merge(x, branches, gates, w_branch, w_out, g, beta, alpha, tm=256):
    n, d = x.shape
    prompt_tiles = branches[0][0].shape[0] // tm
    assert all(p.shape[0] == prompt_tiles * tm and (n - p.shape[0]) == s.shape[0] and s.shape[0] % tm == 0
               for p, s in branches)
    packed = jax.eval_shape(_pack_rows, jax.ShapeDtypeStruct((tm, d), F32))
    row = lambda width: pl.BlockSpec((tm, width), lambda i: (i, 0))
    full = lambda arr: pl.BlockSpec(arr.shape, lambda i: (0,) * arr.ndim)
    prompt_row = lambda width: pl.BlockSpec((tm, width), lambda i: (jnp.minimum(i, prompt_tiles - 1), 0))
    sample_row = lambda width: pl.BlockSpec((tm, width), lambda i: (jnp.maximum(i - prompt_tiles, 0), 0))
    branch_specs, branch_args = [], []
    for p, s in branches:
        branch_specs += [prompt_row(p.shape[1]), sample_row(s.shape[1])]
        branch_args += [p, s]
    return pl.pallas_call(
        functools.partial(_merge_body, alpha=alpha, prompt_tiles=prompt_tiles),
        out_shape=[jax.ShapeDtypeStruct((n, d), F32), jax.ShapeDtypeStruct((n, packed.shape[1]), packed.dtype)],
        grid=(n // tm,),
        in_specs=[row(d)] + branch_specs + [row(gates.shape[1]), full(w_branch), full(w_out), full(g), full(beta)],
        out_specs=[row(d), row(packed.shape[1])],
        compiler_params=_params(("parallel",)),
        name="merge",
    )(x, *branch_args, gates, w_branch, w_out, g, beta)


def _first_max(vals, idx, n):
    top = jnp.max(vals, axis=0, keepdims=True)
    arg = jnp.min(jnp.where(vals == top, idx, n), axis=0, keepdims=True)
    return top, arg


def _router_body(x_ref, wt_ref, bias_ref, eidx_ref, gw_ref, rank_ref, counts_ref, running):
    tm = x_ref.shape[0]

    @pl.when(pl.program_id(0) == 0)
    def _():
        running[...] = jnp.zeros(running.shape, F32)

    per = N_EXPERTS // N_GROUPS
    logits = lax.dot_general(wt_ref[...], x_ref[...].astype(BF16), _NT_DIMS, preferred_element_type=F32)
    s = jax.nn.sigmoid(logits)
    sb = s + bias_ref[...]
    in_group = lax.broadcasted_iota(I32, (per, tm), 0).astype(F32)
    group_scores = []
    for g in range(N_GROUPS):
        blk = sb[g * per:(g + 1) * per, :]
        top1, arg1 = _first_max(blk, in_group, per)
        top2 = jnp.max(jnp.where(in_group == arg1, -jnp.inf, blk), axis=0, keepdims=True)
        group_scores.append(top1 + top2)
    gs = jnp.concatenate(group_scores, axis=0)
    gidx = lax.broadcasted_iota(I32, (N_GROUPS, tm), 0).astype(F32)
    chosen = jnp.zeros((N_GROUPS, tm), F32)
    for _ in range(TOPK_GROUPS):
        _, arg = _first_max(gs, gidx, N_GROUPS)
        hit = gidx == arg
        chosen = jnp.where(hit, 1.0, chosen)
        gs = jnp.where(hit, -jnp.inf, gs)
    cand = jnp.concatenate(
        [jnp.where(chosen[g:g + 1, :] > 0.0, sb[g * per:(g + 1) * per, :], -jnp.inf) for g in range(N_GROUPS)],
        axis=0)
    eidx = lax.broadcasted_iota(I32, (N_EXPERTS, tm), 0).astype(F32)
    picks, weights, hits = [], [], []
    for _ in range(EXPERT_TOPK):
        _, arg = _first_max(cand, eidx, N_EXPERTS)
        hit = eidx == arg
        weights.append(jnp.sum(jnp.where(hit, s, 0.0), axis=0, keepdims=True))
        picks.append(arg)
        hits.append(hit)
        cand = jnp.where(hit, -jnp.inf, cand)
    total = weights[0]
    for w in weights[1:]:
        total = total + w
    taken = jnp.zeros((N_EXPERTS, tm), F32)
    for hit in hits:
        taken = jnp.where(hit, 1.0, taken)
    earlier = (lax.broadcasted_iota(I32, (tm, tm), 0) < lax.broadcasted_iota(I32, (tm, tm), 1))
    before = jnp.dot(taken.astype(BF16), jnp.where(earlier, 1.0, 0.0).astype(BF16),
                     preferred_element_type=F32) + running[...]
    for r in range(EXPERT_TOPK):
        eidx_ref[r:r + 1, :] = picks[r].astype(I32)
        gw_ref[r:r + 1, :] = weights[r] / total * ROUTE_SCALE
        rank_ref[r:r + 1, :] = jnp.sum(jnp.where(hits[r], before, 0.0), axis=0, keepdims=True).astype(I32)
    running[...] = running[...] + jnp.sum(taken, axis=1, keepdims=True)
    counts_ref[...] = running[...]


def _router(x, w_router_t, b_router_col, tm=256):
    n, d = x.shape
    slot = pl.BlockSpec((EXPERT_TOPK, tm), lambda i: (0, i))
    return pl.pallas_call(
        _router_body,
        out_shape=[jax.ShapeDtypeStruct((EXPERT_TOPK, n), I32), jax.ShapeDtypeStruct((EXPERT_TOPK, n), F32),
                   jax.ShapeDtypeStruct((EXPERT_TOPK, n), I32), jax.ShapeDtypeStruct((N_EXPERTS, 1), F32)],
        grid=(n // tm,),
        in_specs=[pl.BlockSpec((tm, d), lambda i: (i, 0)),
                  pl.BlockSpec(w_router_t.shape, lambda i: (0, 0)),
                  pl.BlockSpec(b_router_col.shape, lambda i: (0, 0))],
        out_specs=[slot, slot, slot, pl.BlockSpec((N_EXPERTS, 1), lambda i: (0, 0))],
        scratch_shapes=[pltpu.VMEM((N_EXPERTS, 1), F32)],
        compiler_params=_params(("arbitrary",)),
        name="router",
    )(x, w_router_t, b_router_col)


def _dest_body(eidx_ref, rank_ref, start_ref, dest_ref):
    tm = eidx_ref.shape[1]
    experts = lax.broadcasted_iota(I32, (N_EXPERTS, tm), 0)
    for r in range(EXPERT_TOPK):
        base = jnp.sum(jnp.where(experts == eidx_ref[r:r + 1, :], start_ref[...], 0.0), axis=0, keepdims=True)
        dest_ref[r:r + 1, :] = base.astype(I32) + rank_ref[r:r + 1, :]


def _dest_rows(eidx_t, rank_t, start_col, tm=512):
    n = eidx_t.shape[1]
    slot = pl.BlockSpec((EXPERT_TOPK, tm), lambda i: (0, i))
    return pl.pallas_call(
        _dest_body,
        out_shape=jax.ShapeDtypeStruct((EXPERT_TOPK, n), I32),
        grid=(n // tm,),
        in_specs=[slot, slot, pl.BlockSpec(start_col.shape, lambda i: (0, 0))],
        out_specs=slot,
        compiler_params=_params(("parallel",)),
        name="dest_rows",
    )(eidx_t, rank_t, start_col)


def _scatter_body(dest_ref, x_ref, _, o_ref, sem):
    tm = x_ref.shape[0]

    def row_copy(t, k):
        return pltpu.make_async_copy(x_ref.at[pl.ds(t, 1)], o_ref.at[pl.ds(dest_ref[k, t], 1)], sem)

    def issue(t, carry):
        for k in range(EXPERT_TOPK):
            row_copy(t, k).start()
        return carry

    lax.fori_loop(0, tm, issue, 0)
    for k in range(EXPERT_TOPK):
        pltpu.make_async_copy(x_ref, o_ref.at[pl.ds(0, tm)], sem).wait()


def _scatter_rows(xw, dest_t, rows, tm=256):
    n, width = xw.shape
    return pl.pallas_call(
        _scatter_body,
        out_shape=jax.ShapeDtypeStruct((rows, width), xw.dtype),
        grid=(n // tm,),
        in_specs=[pl.BlockSpec((EXPERT_TOPK, tm), lambda i: (0, i), memory_space=pltpu.SMEM),
                  pl.BlockSpec((tm, width), lambda i: (i, 0)),
                  pl.BlockSpec(memory_space=pl.ANY)],
        out_specs=pl.BlockSpec(memory_space=pl.ANY),
        scratch_shapes=[pltpu.SemaphoreType.DMA],
        input_output_aliases={2: 0},
        compiler_params=_params(("arbitrary",)),
        name="scatter_rows",
    )(dest_t, xw, jnp.zeros((rows, width), xw.dtype))


def _experts_body(block_exp_ref, n_used_ref, x_ref, wgu_ref, wdn_ref, o_ref, wgu_bf, wdn_bf):
    i = pl.program_id(0)
    used = i < n_used_ref[0]

    @pl.when(used)
    def _():
        prev = block_exp_ref[jnp.maximum(i - 1, 0)]

        @pl.when((i == 0) | (block_exp_ref[i] != prev))
        def _():
            wgu_bf[...] = wgu_ref[...].astype(BF16)
            wdn_bf[...] = wdn_ref[...].astype(BF16)

        x_lo, x_hi = _unpack_rows(x_ref[...])
        half = x_lo.shape[1]
        h = (jnp.dot(x_lo, wgu_bf[0:half, :], preferred_element_type=F32)
             + jnp.dot(x_hi, wgu_bf[half:, :], preferred_element_type=F32))
        ff = h.shape[1] // 2
        act = jax.nn.silu(h[:, :ff]) * h[:, ff:]
        o_ref[...] = jnp.dot(act.astype(BF16), wdn_bf[...], preferred_element_type=F32)

    @pl.when(jnp.logical_not(used))
    def _():
        o_ref[...] = jnp.zeros(o_ref.shape, F32)


def _experts(x_rows, block_exp, n_used, w_gu, w_dn, layer):
    rows, width = x_rows.shape
    n_blocks = rows // EXPERT_BLOCK
    d, ff2 = w_gu.shape[-2:]
    grid_spec = pltpu.PrefetchScalarGridSpec(
        num_scalar_prefetch=2,
        grid=(n_blocks,),
        in_specs=[pl.BlockSpec((EXPERT_BLOCK, width), lambda i, be, nu: (i, 0)),
                  pl.BlockSpec((None, None, d, ff2), lambda i, be, nu: (layer, be[i], 0, 0)),
                  pl.BlockSpec((None, None, ff2 // 2, d), lambda i, be, nu: (layer, be[i], 0, 0))],
        out_specs=pl.BlockSpec((EXPERT_BLOCK, d), lambda i, be, nu: (i, 0)),
        scratch_shapes=[pltpu.VMEM((d, ff2), BF16), pltpu.VMEM((ff2 // 2, d), BF16)],
    )
    return pl.pallas_call(
        _experts_body,
        out_shape=jax.ShapeDtypeStruct((rows, d), F32),
        grid_spec=grid_spec,
        compiler_params=_params(("arbitrary",)),
        name="experts",
    )(block_exp, n_used, x_rows, w_gu, w_dn)


def _block_plan(counts, n_blocks):
    counts = counts.reshape(-1).astype(I32)
    padded = (counts + EXPERT_BLOCK - 1) // EXPERT_BLOCK * EXPERT_BLOCK
    pad_end = jnp.cumsum(padded)
    start_col = (pad_end - padded).astype(F32)[:, None]
    end_blocks = pad_end // EXPERT_BLOCK
    block_exp = jnp.minimum(jnp.sum(end_blocks[None, :] <= jnp.arange(n_blocks)[:, None], axis=1),
                            N_EXPERTS - 1).astype(I32)
    return start_col, block_exp, end_blocks[-1:].astype(I32)


def _ffn_out_body(dest_ref, dest_next_ref, x_ref, gw_ref, yb_ref, wgu_ref, wdn_ref, g_ref, beta_ref, o_ref,
                  ybuf, sems, *, alpha):
    i = pl.program_id(0)
    n_steps = pl.num_programs(0)
    tm = x_ref.shape[0]
    slot = i % 2

    def gather(dests, to_slot):
        def issue(t, carry):
            for k in range(EXPERT_TOPK):
                pltpu.make_async_copy(yb_ref.at[pl.ds(dests[k, t], 1)], ybuf.at[to_slot, k, pl.ds(t, 1)],
                                      sems.at[to_slot]).start()
            return carry
        lax.fori_loop(0, tm, issue, 0)

    @pl.when(i == 0)
    def _():
        gather(dest_ref, 0)

    @pl.when(i + 1 < n_steps)
    def _():
        gather(dest_next_ref, 1 - slot)

    x = x_ref[...]
    h = jnp.dot(x.astype(BF16), wgu_ref[...], preferred_element_type=F32)
    ff = h.shape[1] // 2
    act = jax.nn.silu(h[:, :ff]) * h[:, ff:]
    shared = jnp.dot(act.astype(BF16), wdn_ref[...], preferred_element_type=F32)

    for k in range(EXPERT_TOPK):
        pltpu.make_async_copy(yb_ref.at[pl.ds(0, tm)], ybuf.at[slot, k], sems.at[slot]).wait()
    routed = None
    for k in range(EXPERT_TOPK):
        term = ybuf[slot, k] * gw_ref[:, k:k + 1]
        routed = term if routed is None else routed + term
    o_ref[...] = _layer_norm(alpha * x + (routed + shared), g_ref[...], beta_ref[...])


def _ffn_out(x, yb, dest_t, gw, ws_gu, ws_dn, g, beta, alpha, tm=256):
    n, d = x.shape
    n_tiles = n // tm
    row = pl.BlockSpec((tm, d), lambda i: (i, 0))
    full = lambda arr: pl.BlockSpec(arr.shape, lambda i: (0,) * arr.ndim)
    dest_spec = lambda index: pl.BlockSpec((EXPERT_TOPK, tm), index, memory_space=pltpu.SMEM)
    return pl.pallas_call(
        functools.partial(_ffn_out_body, alpha=alpha),
        out_shape=jax.ShapeDtypeStruct((n, d), F32),
        grid=(n_tiles,),
        in_specs=[dest_spec(lambda i: (0, i)), dest_spec(lambda i: (0, jnp.minimum(i + 1, n_tiles - 1))),
                  row, pl.BlockSpec((tm, EXPERT_TOPK), lambda i: (i, 0)),
                  pl.BlockSpec(memory_space=pl.ANY),
                  full(ws_gu), full(ws_dn), full(g), full(beta)],
        out_specs=row,
        scratch_shapes=[pltpu.VMEM((2, EXPERT_TOPK, tm, d), F32), pltpu.SemaphoreType.DMA((2,))],
        compiler_params=_params(("arbitrary",)),
        name="ffn_out",
    )(dest_t, dest_t, x, gw, yb, ws_gu, ws_dn, g, beta)


def _moe(x, xw, layer, w_router_t, b_router_col, w_exp_gate_up, w_exp_down, ws_gu, ws_dn, g, beta, alpha):
    n = x.shape[0]
    n_blocks = -(-n * EXPERT_TOPK // EXPERT_BLOCK) + N_EXPERTS
    eidx_t, gw_t, rank_t, counts = _router(x, w_router_t, b_router_col)
    start_col, block_exp, n_used = _block_plan(counts, n_blocks)
    dest_t = _dest_rows(eidx_t, rank_t, start_col)
    x_rows = _scatter_rows(xw, dest_t, n_blocks * EXPERT_BLOCK)
    yb = _experts(x_rows, block_exp, n_used, w_exp_gate_up, w_exp_down, layer)
    return _ffn_out(x, yb, dest_t, gw_t.T, ws_gu, ws_dn, g, beta, alpha)


def kernel(x_prompt, x_sample, mem_prompt, cache_a_k, cache_a_v, cache_b_k, cache_b_v, cache_b_idx_k,
           cache_mem_k, cache_mem_v, w_in, rel_bias, w_mem_kv, w_branch, w_out, ln_mix_g, ln_mix_b,
           w_router, b_router, w_exp_gate_up, w_exp_down, w_sh_gate_up, w_sh_down, ln_ffn_g, ln_ffn_b):
    n_batch, seq, d = x_prompt.shape
    s_batch, s_seq, _ = x_sample.shape
    depth = w_in.shape[0]
    past = cache_b_k.shape[2]
    assert s_seq == CHUNK and seq % CHUNK == 0 and cache_a_k.shape[2] == BAND and seq >= BAND
    alpha = (2 * depth) ** 0.25
    n_p, n_s = n_batch * seq, s_batch * s_seq
    n_tot = n_p + n_s
    tm = 512
    assert seq % tm == 0 and n_p % tm == 0 and n_s % tm == 0 and tm % CHUNK == 0

    x = jnp.concatenate([x_prompt.reshape(n_p, d), x_sample.reshape(n_s, d)])

    pos = jnp.concatenate([jnp.arange(seq), jnp.tile(past + jnp.arange(s_seq), tm // s_seq)])
    tab_heads = _rope_tables(pos, LANES)
    tab_kiw = _rope_tables(pos, HEAD_DIM)
    p_tiles, seq_tiles = n_p // tm, seq // tm
    table_index = lambda i: jnp.where(i < p_tiles, i % seq_tiles, seq_tiles)

    wd = WIDTH
    c_qi, c_ki, c_wi, c_qm, c_gates = 6 * wd, 7 * wd, 7 * wd + HEAD_DIM, 7 * wd + HEAD_DIM + N_HEADS, 8 * wd + HEAD_DIM + N_HEADS
    cache_a_k = cache_a_k.reshape(depth, s_batch, BAND, wd)
    cache_a_v = cache_a_v.reshape(depth, s_batch, BAND, wd)
    cache_b_k = cache_b_k.reshape(depth, s_batch, past, wd)
    cache_b_v = cache_b_v.reshape(depth, s_batch, past, wd)
    cache_mem_k = cache_mem_k.reshape(depth, s_batch, N_MEM, wd)
    cache_mem_v = cache_mem_v.reshape(depth, s_batch, N_MEM, wd)
    mem_rows = mem_prompt.reshape(n_batch * N_MEM, d)

    outs = {k: [] for k in ("p_ak", "p_av", "p_bk", "p_bv", "p_ik", "p_mk", "p_mv",
                            "s_ak", "s_av", "s_bk", "s_bv", "s_ik")}
    for l in range(depth):
        w = w_in[l]
        w_a = w[:, :3 * wd].astype(BF16)
        w_b = w[:, 3 * wd:7 * wd].astype(BF16)
        w_c = w[:, c_qm:].astype(BF16)
        w_d = jnp.concatenate([w[:, c_ki:c_qm], jnp.zeros((d, LANES - HEAD_DIM - N_HEADS), F32)], 1).astype(BF16)

        qa, ka, va = _project(x, w_a, (wd, wd, wd), tm=tm, name="proj_a")
        qb, kb, vb, qi = _project(x, w_b, (wd, wd, wd, wd), (True, True, False, True), tab_heads, table_index,
                                  tm=tm, name="proj_b")
        qm, gates = _project(x, w_c, (wd, 3 * d), tm=tm, name="proj_c")
        (kiw,) = _project(x, w_d, (LANES,), (True,), tab_kiw, table_index, tm=tm, name="proj_d")
        mk, mv = _project(mem_rows, w_mem_kv[l].astype(BF16), (wd, wd), tm=N_MEM, name="proj_mem")

        bias = _band_bias(rel_bias[l])
        a = (_band_prompt(qa, ka, va, bias, n_batch, seq),
             _band_sample(qa, ka, va, cache_a_k, cache_a_v, l, bias, s_batch, n_p))
        b = (_dsa_prompt(qb, kb, vb, qi, kiw, n_batch, seq),
             _dsa_sample(qb, kb, vb, qi, kiw, cache_b_k, cache_b_v, cache_b_idx_k, l, s_batch, n_p))
        m = (_mem_attn(qm, mk, mv, lambda bb: (bb, 0), n_batch, seq, 0, tm),
             _mem_attn(qm, cache_mem_k, cache_mem_v, lambda bb, l=l: (l, bb, 0, 0), s_batch, s_seq, n_p, s_seq))

        x1, x1w = _merge(x, (a, b, m), gates, w_branch[l].astype(BF16), w_out[l].astype(BF16),
                         ln_mix_g[l][None], ln_mix_b[l][None], alpha)
        x = _moe(x1, x1w, l, w_router[l].T.astype(BF16), b_router[l][:, None], w_exp_gate_up, w_exp_down,
                 w_sh_gate_up[l].astype(BF16), w_sh_down[l].astype(BF16),
                 ln_ffn_g[l][None], ln_ffn_b[l][None], alpha)

        ka_p = ka[:n_p].reshape(n_batch, seq, wd)
        va_p = va[:n_p].reshape(n_batch, seq, wd)
        outs["p_ak"].append(ka_p[:, -BAND:])
        outs["p_av"].append(va_p[:, -BAND:])
        outs["p_bk"].append(kb[:n_p].reshape(n_batch, seq, wd))
        outs["p_bv"].append(vb[:n_p].reshape(n_batch, seq, wd))
        outs["p_ik"].append(kiw[:n_p, :HEAD_DIM].reshape(n_batch, seq, HEAD_DIM))
        outs["p_mk"].append(mk.reshape(n_batch, N_MEM, wd))
        outs["p_mv"].append(mv.reshape(n_batch, N_MEM, wd))
        ka_s = ka[n_p:].reshape(s_batch, s_seq, wd)
        va_s = va[n_p:].reshape(s_batch, s_seq, wd)
        outs["s_ak"].append(jnp.concatenate([cache_a_k[l], ka_s], 1)[:, -BAND:])
        outs["s_av"].append(jnp.concatenate([cache_a_v[l], va_s], 1)[:, -BAND:])
        outs["s_bk"].append(kb[n_p:].reshape(s_batch, s_seq, wd))
        outs["s_bv"].append(vb[n_p:].reshape(s_batch, s_seq, wd))
        outs["s_ik"].append(kiw[n_p:, :HEAD_DIM].reshape(s_batch, s_seq, HEAD_DIM))

    heads = lambda t: t.reshape(t.shape[:-1] + (N_HEADS, HEAD_DIM))
    mheads = lambda t: t.reshape(t.shape[:-1] + (M_HEADS, M_HEAD_DIM))
    st = lambda key: jnp.stack(outs[key])
    return (x[:n_p].reshape(n_batch, seq, d), x[n_p:].reshape(s_batch, s_seq, d),
            heads(st("p_ak")), heads(st("p_av")), heads(st("p_bk")), heads(st("p_bv")), st("p_ik"),
            mheads(st("p_mk")), mheads(st("p_mv")),
            heads(st("s_ak")), heads(st("s_av")), heads(st("s_bk")), heads(st("s_bv")), st("s_ik"))
```

```python
import functools

import jax
import jax.numpy as jnp
from jax import lax
from jax.experimental import pallas as pl
from jax.experimental.pallas import tpu as pltpu

F32 = jnp.float32
BF16 = jnp.bfloat16
I32 = jnp.int32

CHUNK = 64
HEAD_DIM = 64
N_HEADS = 8
WIDTH = N_HEADS * HEAD_DIM
BAND = 8 * CHUNK
MAX_REL = 128
TOPK_KEYS = 256
M_HEADS = 4
M_HEAD_DIM = 128
N_MEM = 256
ROPE_THETA = 10000.0
N_EXPERTS = 256
EXPERT_TOPK = 8
N_GROUPS = 8
TOPK_GROUPS = 4
EXPERT_FF = 256
ROUTE_SCALE = 2.5
EXPERT_BLOCK = 256
LN_EPS = 1e-5

LANES = 128
NEG_BIG = -1e30
INT_MIN = -(2 ** 31)
VMEM_LIMIT = 48 * 1024 * 1024

_NT_DIMS = (((1,), (1,)), ((), ()))


def _params(semantics):
    return pltpu.CompilerParams(dimension_semantics=semantics, vmem_limit_bytes=VMEM_LIMIT)


def _proj_body(*refs, widths, rope_flags):
    has_rope = any(rope_flags)
    x_ref, w_ref = refs[0], refs[1]
    if has_rope:
        cos_ref, sin_lo_ref, sin_hi_ref = refs[2:5]
        outs = refs[5:]
    else:
        outs = refs[2:]
    xb = x_ref[...].astype(BF16)
    off = 0
    for g, width in enumerate(widths):
        y = jnp.dot(xb, w_ref[:, off:off + width], preferred_element_type=F32)
        if rope_flags[g]:
            cos, sin_lo, sin_hi = cos_ref[...], sin_lo_ref[...], sin_hi_ref[...]
            for k in range(width // LANES):
                yk = y[:, k * LANES:(k + 1) * LANES]
                outs[g][:, k * LANES:(k + 1) * LANES] = (
                    yk * cos + pltpu.roll(yk, LANES - 32, 1) * sin_lo + pltpu.roll(yk, 32, 1) * sin_hi)
        else:
            outs[g][...] = y
        off += width


def _project(x, w, widths, rope_flags=None, tables=None, table_index=None, tm=512, name="proj"):
    n, d = x.shape
    rope_flags = rope_flags or (False,) * len(widths)
    assert n % tm == 0 and w.shape == (d, sum(widths))
    in_specs = [pl.BlockSpec((tm, d), lambda i: (i, 0)),
                pl.BlockSpec(w.shape, lambda i: (0, 0))]
    args = [x, w]
    if any(rope_flags):
        for t in tables:
            in_specs.append(pl.BlockSpec((tm, LANES), lambda i: (table_index(i), 0)))
            args.append(t)
    return pl.pallas_call(
        functools.partial(_proj_body, widths=tuple(widths), rope_flags=tuple(rope_flags)),
        out_shape=[jax.ShapeDtypeStruct((n, wd), F32) for wd in widths],
        grid=(n // tm,),
        in_specs=in_specs,
        out_specs=[pl.BlockSpec((tm, wd), lambda i: (i, 0)) for wd in widths],
        compiler_params=_params(("parallel",)),
        name=name,
    )(*args)


def _rope_tables(pos, rope_lanes):
    half = HEAD_DIM // 2
    inv_freq = ROPE_THETA ** (-jnp.arange(half, dtype=F32) / half)
    ang = pos.astype(F32)[:, None] * inv_freq[None, :]
    cos, sin = jnp.cos(ang), jnp.sin(ang)
    reps = LANES // HEAD_DIM
    zero = jnp.zeros_like(sin)
    cos_t = jnp.tile(jnp.concatenate([cos, cos], -1), (1, reps))
    sin_lo = jnp.tile(jnp.concatenate([-sin, zero], -1), (1, reps))
    sin_hi = jnp.tile(jnp.concatenate([zero, sin], -1), (1, reps))
    keep = (jnp.arange(LANES) < rope_lanes)[None, :]
    return (jnp.where(keep, cos_t, 1.0), jnp.where(keep, sin_lo, 0.0), jnp.where(keep, sin_hi, 0.0))


def _head_mask(rows, head_in_pair):
    lane = lax.broadcasted_iota(I32, (rows, LANES), 1)
    lo = head_in_pair * HEAD_DIM
    return (lane >= lo) & (lane < lo + HEAD_DIM)


def _pair_merge(o_even, o_odd):
    lane = lax.broadcasted_iota(I32, o_even.shape, 1)
    return jnp.where(lane < HEAD_DIM, o_even, o_odd)


def _band_body(*refs, sample):
    if sample:
        q_ref, kn_ref, vn_ref, kh_ref, vh_ref, bias_ref, o_ref, kpad, vpad = refs
        kpad[0:BAND, :] = kh_ref[...].astype(BF16)
        vpad[0:BAND, :] = vh_ref[...].astype(BF16)
        kpad[BAND:, :] = kn_ref[...].astype(BF16)
        vpad[BAND:, :] = vn_ref[...].astype(BF16)
        start = 0
        first_valid = 0
    else:
        q_ref, k_ref, v_ref, bias_ref, o_ref, kpad, vpad = refs
        c = pl.program_id(1)

        @pl.when(c == 0)
        def _():
            kpad[0:BAND, :] = jnp.zeros((BAND, WIDTH), BF16)
            vpad[0:BAND, :] = jnp.zeros((BAND, WIDTH), BF16)
            kpad[BAND:, :] = k_ref[...].astype(BF16)
            vpad[BAND:, :] = v_ref[...].astype(BF16)

        start = pl.multiple_of(c * CHUNK, CHUNK)
        first_valid = BAND - c * CHUNK

    win = BAND + CHUNK
    q = q_ref[...] * (HEAD_DIM ** -0.5)
    valid = lax.broadcasted_iota(I32, (CHUNK, win), 1) >= first_valid
    for hp in range(N_HEADS // 2):
        cols = slice(hp * LANES, (hp + 1) * LANES)
        q2 = q[:, cols]
        k2 = kpad[pl.ds(start, win), cols]
        v2 = vpad[pl.ds(start, win), cols]
        outs = []
        for hh in range(2):
            qm = jnp.where(_head_mask(CHUNK, hh), q2, 0.0).astype(BF16)
            s = lax.dot_general(qm, k2, _NT_DIMS, preferred_element_type=F32)
            s = s + bias_ref[2 * hp + hh]
            if not sample:
                s = jnp.where(valid, s, -jnp.inf)
            p = jnp.exp(s - jnp.max(s, axis=1, keepdims=True))
            denom = jnp.sum(p, axis=1, keepdims=True)
            o = jnp.dot(p.astype(BF16), v2, preferred_element_type=F32)
            outs.append(o / denom)
        o_ref[:, cols] = _pair_merge(outs[0], outs[1])


def _band_bias(table):
    rel = jnp.arange(CHUNK)[:, None] + BAND - jnp.arange(BAND + CHUNK)[None, :]
    return table[:, jnp.clip(rel, -MAX_REL, MAX_REL) + MAX_REL].astype(F32)


def _band_prompt(q, k, v, bias, n_batch, seq):
    nc = seq // CHUNK
    return pl.pallas_call(
        functools.partial(_band_body, sample=False),
        out_shape=jax.ShapeDtypeStruct((n_batch * seq, WIDTH), F32),
        grid=(n_batch, nc),
        in_specs=[pl.BlockSpec((CHUNK, WIDTH), lambda b, c: (b * nc + c, 0)),
                  pl.BlockSpec((seq, WIDTH), lambda b, c: (b, 0)),
                  pl.BlockSpec((seq, WIDTH), lambda b, c: (b, 0)),
                  pl.BlockSpec(bias.shape, lambda b, c: (0, 0, 0))],
        out_specs=pl.BlockSpec((CHUNK, WIDTH), lambda b, c: (b * nc + c, 0)),
        scratch_shapes=[pltpu.VMEM((BAND + seq, WIDTH), BF16), pltpu.VMEM((BAND + seq, WIDTH), BF16)],
        compiler_params=_params(("parallel", "arbitrary")),
        name="band_prompt",
    )(q, k, v, bias)


def _band_sample(q, k, v, k_hist, v_hist, layer, bias, n_batch, row0):
    blk0 = row0 // CHUNK
    hist_spec = pl.BlockSpec((None, None, BAND, WIDTH), lambda b: (layer, b, 0, 0))
    new_spec = pl.BlockSpec((CHUNK, WIDTH), lambda b: (blk0 + b, 0))
    return pl.pallas_call(
        functools.partial(_band_body, sample=True),
        out_shape=jax.ShapeDtypeStruct((n_batch * CHUNK, WIDTH), F32),
        grid=(n_batch,),
        in_specs=[new_spec, new_spec, new_spec, hist_spec, hist_spec,
                  pl.BlockSpec(bias.shape, lambda b: (0, 0, 0))],
        out_specs=pl.BlockSpec((CHUNK, WIDTH), lambda b: (b, 0)),
        scratch_shapes=[pltpu.VMEM((BAND + CHUNK, WIDTH), BF16), pltpu.VMEM((BAND + CHUNK, WIDTH), BF16)],
        compiler_params=_params(("parallel",)),
        name="band_sample",
    )(q, k, v, k_hist, v_hist, bias)


def _mem_body(q_ref, k_ref, v_ref, o_ref):
    scale = M_HEAD_DIM ** -0.5
    for h in range(M_HEADS):
        cols = slice(h * M_HEAD_DIM, (h + 1) * M_HEAD_DIM)
        s = lax.dot_general(q_ref[:, cols].astype(BF16), k_ref[:, cols].astype(BF16), _NT_DIMS,
                            preferred_element_type=F32) * scale
        p = jnp.exp(s - jnp.max(s, axis=1, keepdims=True))
        denom = jnp.sum(p, axis=1, keepdims=True)
        o = jnp.dot(p.astype(BF16), v_ref[:, cols].astype(BF16), preferred_element_type=F32)
        o_ref[:, cols] = o / denom


def _mem_attn(q, mk, mv, kv_index, n_batch, rows_per_batch, row0, tq):
    width = M_HEADS * M_HEAD_DIM
    per = rows_per_batch // tq
    blk0 = row0 // tq
    kv_block = (None,) * (mk.ndim - 2) + (N_MEM, width)
    kv_spec = pl.BlockSpec(kv_block, lambda b, t: kv_index(b))
    return pl.pallas_call(
        _mem_body,
        out_shape=jax.ShapeDtypeStruct((n_batch * rows_per_batch, width), F32),
        grid=(n_batch, per),
        in_specs=[pl.BlockSpec((tq, width), lambda b, t: (blk0 + b * per + t, 0)), kv_spec, kv_spec],
        out_specs=pl.BlockSpec((tq, width), lambda b, t: (b * per + t, 0)),
        compiler_params=_params(("parallel", "parallel")),
        name="mem_attn",
    )(q, mk, mv)


def _split_pair_lanes(x):
    lane = lax.broadcasted_iota(I32, x.shape, 1)
    even = (lane & (LANES - 1)) < HEAD_DIM
    return jnp.where(even, x, 0.0), jnp.where(even, 0.0, x)


def _dsa_body(*refs, sample, key_block, n_keys, n_sel):
    n_in = 8 if sample else 6
    if sample:
        q_ref, qi_ref, kiwq_ref, kn_ref, vn_ref, kc_ref, vc_ref, kic_ref = refs[:n_in]
    else:
        q_ref, qi_ref, kiwq_ref, k_ref, v_ref, kiw_ref = refs[:n_in]
    o_ref = refs[n_in]
    (k_even, k_odd, vsc, ki_lo, ki_hi, keybuf, maskbuf, qsc, qisc, wrep, m_sc, l_sc, acc_sc) = refs[n_in + 1:]
    kb = key_block
    n_pad = vsc.shape[0]
    rows = q_ref.shape[0]
    tiles = kb // LANES

    def stage_kv(dst, k_new, v_new):
        ke, ko = _split_pair_lanes(k_new)
        k_even[dst, :] = ke.astype(BF16)
        k_odd[dst, :] = ko.astype(BF16)
        vsc[dst, :] = v_new.astype(BF16)

    def stage_index_keys(dst, kiw):
        lane = lax.broadcasted_iota(I32, kiw.shape, 1)
        ki_lo[dst, :] = jnp.where(lane < HEAD_DIM, kiw, 0.0).astype(BF16)
        ki_hi[dst, :] = jnp.where(lane < HEAD_DIM, 0.0, pltpu.roll(kiw, HEAD_DIM, 1)).astype(BF16)

    def stage_keys():
        if sample:
            n_hist = kc_ref.shape[0]
            stage_kv(slice(0, n_hist), kc_ref[...], vc_ref[...])
            kic = kic_ref[...]
            zeros = jnp.zeros_like(kic)
            ki_lo[0:n_hist, :] = jnp.concatenate([kic, zeros], axis=1).astype(BF16)
            ki_hi[0:n_hist, :] = jnp.concatenate([zeros, kic], axis=1).astype(BF16)
            stage_kv(slice(n_hist, n_keys), kn_ref[...], vn_ref[...])
            stage_index_keys(slice(n_hist, n_keys), kiwq_ref[...])
            if n_pad > n_keys:
                for ref in (k_even, k_odd, vsc, ki_lo, ki_hi):
                    ref[n_keys:, :] = jnp.zeros((n_pad - n_keys, ref.shape[1]), BF16)
        else:
            stage_kv(slice(None), k_ref[...], v_ref[...])
            stage_index_keys(slice(None), kiw_ref[...])

    if sample:
        stage_keys()
        limit = n_keys
        nb = n_pad // kb
        search_bits = 32
    else:
        t = pl.program_id(1)
        pl.when(t == 0)(stage_keys)
        chunk = t * (rows // CHUNK) + lax.broadcasted_iota(I32, (rows, LANES), 0) // CHUNK
        limit = (chunk + 1) * CHUNK
        limit_max = (t + 1) * rows
        nb = (limit_max + kb - 1) // kb
        search_bits = jnp.where(limit_max <= n_sel, 0, 32)

    def block_cols(j):
        return pl.ds(pl.multiple_of(j * kb, kb), kb)

    def tile_cols(j, tl):
        return pl.ds(pl.multiple_of(j * kb + tl * LANES, LANES), LANES)

    def tile_pos(j, tl):
        return j * kb + tl * LANES + lax.broadcasted_iota(I32, (rows, LANES), 1)

    def lane_rep(col):
        return jnp.broadcast_to(col, (rows, LANES))

    qsc[...] = (q_ref[...] * (HEAD_DIM ** -0.5)).astype(BF16)
    qisc[...] = qi_ref[...].astype(BF16)
    kiwq = kiwq_ref[...]
    for h in range(N_HEADS):
        wrep[:, h * LANES:(h + 1) * LANES] = lane_rep(kiwq[:, HEAD_DIM + h:HEAD_DIM + h + 1])

    def score_block(j, carry):
        acc = [jnp.zeros((rows, LANES), F32) for _ in range(tiles)]
        for h in range(N_HEADS):
            pair = slice((h // 2) * LANES, (h // 2 + 1) * LANES)
            ki_ref = ki_lo if h % 2 == 0 else ki_hi
            d = lax.dot_general(qisc[:, pair], ki_ref[block_cols(j), :], _NT_DIMS, preferred_element_type=F32)
            w = wrep[:, h * LANES:(h + 1) * LANES]
            for tl in range(tiles):
                acc[tl] = acc[tl] + w * jnp.maximum(d[:, tl * LANES:(tl + 1) * LANES], 0.0)
        for tl in range(tiles):
            score = acc[tl] + 0.0
            score = jnp.where(tile_pos(j, tl) < limit, score, -jnp.inf)
            bits = lax.bitcast_convert_type(score, I32)
            keybuf[:, tile_cols(j, tl)] = bits ^ ((bits >> 31) & 0x7FFFFFFF)
        return carry

    lax.fori_loop(0, nb, score_block, 0)

    def count(pred_fn):
        def body(j, part):
            for tl in range(tiles):
                part = part + jnp.where(pred_fn(keybuf[:, tile_cols(j, tl)], j, tl), 1.0, 0.0)
            return part
        part = lax.fori_loop(0, nb, body, jnp.zeros((rows, LANES), F32))
        return jnp.sum(part, axis=1, keepdims=True)

    def bit_step(i, prefix):
        cand = prefix | lax.shift_left(jnp.int32(1), 31 - i)
        thr_rep = lane_rep(cand ^ INT_MIN)
        cnt = count(lambda kt, j, tl: kt >= thr_rep)
        return jnp.where(cnt >= n_sel, cand, prefix)

    prefix = lax.fori_loop(0, search_bits, bit_step, jnp.zeros((rows, 1), I32))
    thr = lane_rep(prefix ^ INT_MIN)
    above = count(lambda kt, j, tl: kt > thr)
    at_least = count(lambda kt, j, tl: kt >= thr)
    need = n_sel - above
    surplus = at_least > n_sel

    pos_bits = max(1, (n_pad - 1).bit_length() + 1)
    any_surplus = jnp.max(jnp.where(surplus, 1.0, 0.0)) > 0.0

    def pos_step(i, bound):
        cand = bound | lax.shift_left(jnp.int32(1), pos_bits - 1 - i)
        cand_rep = lane_rep(cand)
        cnt = count(lambda kt, j, tl: (kt == thr) & (tile_pos(j, tl) < cand_rep))
        return jnp.where(cnt <= need, cand, bound)

    bound = lax.fori_loop(0, jnp.where(any_surplus, pos_bits, 0), pos_step, jnp.zeros((rows, 1), I32))
    bound_rep = lane_rep(jnp.where(surplus, bound, 2 ** pos_bits - 1))

    def mask_block(j, carry):
        for tl in range(tiles):
            kt = keybuf[:, tile_cols(j, tl)]
            pos = tile_pos(j, tl)
            sel = ((kt > thr) | ((kt == thr) & (pos < bound_rep))) & (pos < limit)
            maskbuf[:, tile_cols(j, tl)] = jnp.where(sel, 0.0, NEG_BIG)
        return carry

    lax.fori_loop(0, nb, mask_block, 0)

    m_sc[...] = jnp.full(m_sc.shape, NEG_BIG, F32)
    l_sc[...] = jnp.zeros(l_sc.shape, F32)
    acc_sc[...] = jnp.zeros(acc_sc.shape, F32)

    def attend(j, carry):
        for h in range(N_HEADS):
            pair = slice((h // 2) * LANES, (h // 2 + 1) * LANES)
            head = slice(h * LANES, (h + 1) * LANES)
            k_ref_h = k_even if h % 2 == 0 else k_odd
            s = lax.dot_general(qsc[:, pair], k_ref_h[block_cols(j), pair], _NT_DIMS, preferred_element_type=F32)
            s_tiles = [s[:, tl * LANES:(tl + 1) * LANES] + maskbuf[:, tile_cols(j, tl)] for tl in range(tiles)]
            tile_max = s_tiles[0]
            for st in s_tiles[1:]:
                tile_max = jnp.maximum(tile_max, st)
            m_prev = m_sc[:, head]
            m_new = jnp.maximum(m_prev, jnp.max(tile_max, axis=1, keepdims=True))
            alpha = jnp.exp(m_prev - m_new)
            p_tiles = [jnp.exp(st - m_new) for st in s_tiles]
            p_sum = p_tiles[0]
            for pt in p_tiles[1:]:
                p_sum = p_sum + pt
            l_sc[:, head] = alpha * l_sc[:, head] + jnp.sum(p_sum, axis=1, keepdims=True)
            p = jnp.concatenate(p_tiles, axis=1).astype(BF16)
            acc_sc[:, head] = alpha * acc_sc[:, head] + jnp.dot(p, vsc[block_cols(j), pair],
                                                                 preferred_element_type=F32)
            m_sc[:, head] = m_new
        return carry

    lax.fori_loop(0, nb, attend, 0)
    for hp in range(N_HEADS // 2):
        even = slice(2 * hp * LANES, (2 * hp + 1) * LANES)
        odd = slice((2 * hp + 1) * LANES, (2 * hp + 2) * LANES)
        o_ref[:, hp * LANES:(hp + 1) * LANES] = _pair_merge(acc_sc[:, even] / l_sc[:, even],
                                                            acc_sc[:, odd] / l_sc[:, odd])


def _dsa_scratch(rows, n_pad):
    per_head = N_HEADS * LANES
    return [pltpu.VMEM((n_pad, WIDTH), BF16), pltpu.VMEM((n_pad, WIDTH), BF16), pltpu.VMEM((n_pad, WIDTH), BF16),
            pltpu.VMEM((n_pad, LANES), BF16), pltpu.VMEM((n_pad, LANES), BF16),
            pltpu.VMEM((rows, n_pad), I32), pltpu.VMEM((rows, n_pad), F32),
            pltpu.VMEM((rows, WIDTH), BF16), pltpu.VMEM((rows, WIDTH), BF16),
            pltpu.VMEM((rows, per_head), F32), pltpu.VMEM((rows, per_head), F32),
            pltpu.VMEM((rows, per_head), F32), pltpu.VMEM((rows, per_head), F32)]


def _dsa_prompt(q, k, v, qi, kiw, n_batch, seq, key_block=256, q_rows=256):
    nt = seq // q_rows
    key_block = min(key_block, seq)
    assert seq % key_block == 0 and seq % q_rows == 0 and q_rows % CHUNK == 0
    n_sel = min(TOPK_KEYS, seq // 4)
    tile_spec = lambda width: pl.BlockSpec((q_rows, width), lambda b, t: (b * nt + t, 0))
    seq_spec = lambda width: pl.BlockSpec((seq, width), lambda b, t: (b, 0))
    return pl.pallas_call(
        functools.partial(_dsa_body, sample=False, key_block=key_block, n_keys=seq, n_sel=n_sel),
        out_shape=jax.ShapeDtypeStruct((n_batch * seq, WIDTH), F32),
        grid=(n_batch, nt),
        in_specs=[tile_spec(WIDTH), tile_spec(WIDTH), tile_spec(LANES),
                  seq_spec(WIDTH), seq_spec(WIDTH), seq_spec(LANES)],
        out_specs=tile_spec(WIDTH),
        scratch_shapes=_dsa_scratch(q_rows, seq),
        compiler_params=_params(("parallel", "arbitrary")),
        name="dsa_prompt",
    )(q, qi, kiw, k, v, kiw)


def _dsa_sample(q, k, v, qi, kiw, k_cache, v_cache, ki_cache, layer, n_batch, row0, key_block=256):
    past = k_cache.shape[2]
    n_keys = past + CHUNK
    n_pad = -(-n_keys // key_block) * key_block
    n_sel = min(TOPK_KEYS, n_keys // 4)
    blk0 = row0 // CHUNK
    new_spec = lambda width: pl.BlockSpec((CHUNK, width), lambda b: (blk0 + b, 0))
    cache_spec = lambda width: pl.BlockSpec((None, None, past, width), lambda b: (layer, b, 0, 0))
    return pl.pallas_call(
        functools.partial(_dsa_body, sample=True, key_block=key_block, n_keys=n_keys, n_sel=n_sel),
        out_shape=jax.ShapeDtypeStruct((n_batch * CHUNK, WIDTH), F32),
        grid=(n_batch,),
        in_specs=[new_spec(WIDTH), new_spec(WIDTH), new_spec(LANES), new_spec(WIDTH), new_spec(WIDTH),
                  cache_spec(WIDTH), cache_spec(WIDTH), cache_spec(HEAD_DIM)],
        out_specs=pl.BlockSpec((CHUNK, WIDTH), lambda b: (b, 0)),
        scratch_shapes=_dsa_scratch(CHUNK, n_pad),
        compiler_params=_params(("parallel",)),
        name="dsa_sample",
    )(q, qi, kiw, k, v, k_cache, v_cache, ki_cache)


def _layer_norm(z, g, b):
    mu = jnp.mean(z, axis=-1, keepdims=True)
    zc = z - mu
    var = jnp.mean(zc * zc, axis=-1, keepdims=True)
    return zc * lax.rsqrt(var + LN_EPS) * g + b


HI_HALF = -65536


def _pack_rows(x):
    half = x.shape[1] // 2
    lo = lax.bitcast_convert_type(x[:, :half].astype(BF16).astype(F32), I32)
    hi = lax.bitcast_convert_type(x[:, half:].astype(BF16).astype(F32), I32)
    return (hi & HI_HALF) | lax.shift_right_logical(lo, 16)


def _unpack_rows(w):
    lo = lax.bitcast_convert_type(lax.shift_left(w, 16), F32).astype(BF16)
    hi = lax.bitcast_convert_type(w & HI_HALF, F32).astype(BF16)
    return lo, hi


def _merge_body(x_ref, ap_ref, as_ref, bp_ref, bs_ref, mp_ref, ms_ref, gates_ref, wbr_ref, wout_ref, g_ref, beta_ref,
                o_ref, ow_ref, *, alpha, prompt_tiles):
    d = x_ref.shape[1]
    in_prompt = pl.program_id(0) < prompt_tiles
    mix = None
    for n, (p_ref, s_ref) in enumerate(((ap_ref, as_ref), (bp_ref, bs_ref), (mp_ref, ms_ref))):
        branch = jnp.where(in_prompt, p_ref[...], s_ref[...])
        proj = jnp.dot(branch.astype(BF16), wbr_ref[n], preferred_element_type=F32)
        term = jax.nn.sigmoid(gates_ref[:, n * d:(n + 1) * d]) * proj
        mix = term if mix is None else mix + term
    y = jnp.dot(mix.astype(BF16), wout_ref[...], preferred_element_type=F32)
    out = _layer_norm(alpha * x_ref[...] + y, g_ref[...], beta_ref[...])
    o_ref[...] = out
    ow_ref[...] = _pack_rows(out)


def _merge(x, branches, gates, w_branch, w_out, g, beta, alpha, tm=256):
    n, d = x.shape
    prompt_tiles = branches[0][0].shape[0] // tm
    assert all(p.shape[0] == prompt_tiles * tm and (n - p.shape[0]) == s.shape[0] and s.shape[0] % tm == 0
               for p, s in branches)
    packed = jax.eval_shape(_pack_rows, jax.ShapeDtypeStruct((tm, d), F32))
    row = lambda width: pl.BlockSpec((tm, width), lambda i: (i, 0))
    full = lambda arr: pl.BlockSpec(arr.shape, lambda i: (0,) * arr.ndim)
    prompt_row = lambda width: pl.BlockSpec((tm, width), lambda i: (jnp.minimum(i, prompt_tiles - 1), 0))
    sample_row = lambda width: pl.BlockSpec((tm, width), lambda i: (jnp.maximum(i - prompt_tiles, 0), 0))
    branch_specs, branch_args = [], []
    for p, s in branches:
        branch_specs += [prompt_row(p.shape[1]), sample_row(s.shape[1])]
        branch_args += [p, s]
    return pl.pallas_call(
        functools.partial(_merge_body, alpha=alpha, prompt_tiles=prompt_tiles),
        out_shape=[jax.ShapeDtypeStruct((n, d), F32), jax.ShapeDtypeStruct((n, packed.shape[1]), packed.dtype)],
        grid=(n // tm,),
        in_specs=[row(d)] + branch_specs + [row(gates.shape[1]), full(w_branch), full(w_out), full(g), full(beta)],
        out_specs=[row(d), row(packed.shape[1])],
        compiler_params=_params(("parallel",)),
        name="merge",
    )(x, *branch_args, gates, w_branch, w_out, g, beta)


def _first_max(vals, idx, n):
    top = jnp.max(vals, axis=0, keepdims=True)
    arg = jnp.min(jnp.where(vals == top, idx, n), axis=0, keepdims=True)
    return top, arg


def _router_body(x_ref, wt_ref, bias_ref, eidx_ref, gw_ref, rank_ref, counts_ref, running):
    tm = x_ref.shape[0]

    @pl.when(pl.program_id(0) == 0)
    def _():
        running[...] = jnp.zeros(running.shape, F32)

    per = N_EXPERTS // N_GROUPS
    logits = lax.dot_general(wt_ref[...], x_ref[...].astype(BF16), _NT_DIMS, preferred_element_type=F32)
    s = jax.nn.sigmoid(logits)
    sb = s + bias_ref[...]
    in_group = lax.broadcasted_iota(I32, (per, tm), 0).astype(F32)
    group_scores = []
    for g in range(N_GROUPS):
        blk = sb[g * per:(g + 1) * per, :]
        top1, arg1 = _first_max(blk, in_group, per)
        top2 = jnp.max(jnp.where(in_group == arg1, -jnp.inf, blk), axis=0, keepdims=True)
        group_scores.append(top1 + top2)
    gs = jnp.concatenate(group_scores, axis=0)
    gidx = lax.broadcasted_iota(I32, (N_GROUPS, tm), 0).astype(F32)
    chosen = jnp.zeros((N_GROUPS, tm), F32)
    for _ in range(TOPK_GROUPS):
        _, arg = _first_max(gs, gidx, N_GROUPS)
        hit = gidx == arg
        chosen = jnp.where(hit, 1.0, chosen)
        gs = jnp.where(hit, -jnp.inf, gs)
    cand = jnp.concatenate(
        [jnp.where(chosen[g:g + 1, :] > 0.0, sb[g * per:(g + 1) * per, :], -jnp.inf) for g in range(N_GROUPS)],
        axis=0)
    eidx = lax.broadcasted_iota(I32, (N_EXPERTS, tm), 0).astype(F32)
    picks, weights, hits = [], [], []
    for _ in range(EXPERT_TOPK):
        _, arg = _first_max(cand, eidx, N_EXPERTS)
        hit = eidx == arg
        weights.append(jnp.sum(jnp.where(hit, s, 0.0), axis=0, keepdims=True))
        picks.append(arg)
        hits.append(hit)
        cand = jnp.where(hit, -jnp.inf, cand)
    total = weights[0]
    for w in weights[1:]:
        total = total + w
    taken = jnp.zeros((N_EXPERTS, tm), F32)
    for hit in hits:
        taken = jnp.where(hit, 1.0, taken)
    earlier = (lax.broadcasted_iota(I32, (tm, tm), 0) < lax.broadcasted_iota(I32, (tm, tm), 1))
    before = jnp.dot(taken.astype(BF16), jnp.where(earlier, 1.0, 0.0).astype(BF16),
                     preferred_element_type=F32) + running[...]
    for r in range(EXPERT_TOPK):
        eidx_ref[r:r + 1, :] = picks[r].astype(I32)
        gw_ref[r:r + 1, :] = weights[r] / total * ROUTE_SCALE
        rank_ref[r:r + 1, :] = jnp.sum(jnp.where(hits[r], before, 0.0), axis=0, keepdims=True).astype(I32)
    running[...] = running[...] + jnp.sum(taken, axis=1, keepdims=True)
    counts_ref[...] = running[...]


def _router(x, w_router_t, b_router_col, tm=256):
    n, d = x.shape
    slot = pl.BlockSpec((EXPERT_TOPK, tm), lambda i: (0, i))
    return pl.pallas_call(
        _router_body,
        out_shape=[jax.ShapeDtypeStruct((EXPERT_TOPK, n), I32), jax.ShapeDtypeStruct((EXPERT_TOPK, n), F32),
                   jax.ShapeDtypeStruct((EXPERT_TOPK, n), I32), jax.ShapeDtypeStruct((N_EXPERTS, 1), F32)],
        grid=(n // tm,),
        in_specs=[pl.BlockSpec((tm, d), lambda i: (i, 0)),
                  pl.BlockSpec(w_router_t.shape, lambda i: (0, 0)),
                  pl.BlockSpec(b_router_col.shape, lambda i: (0, 0))],
        out_specs=[slot, slot, slot, pl.BlockSpec((N_EXPERTS, 1), lambda i: (0, 0))],
        scratch_shapes=[pltpu.VMEM((N_EXPERTS, 1), F32)],
        compiler_params=_params(("arbitrary",)),
        name="router",
    )(x, w_router_t, b_router_col)


def _dest_body(eidx_ref, rank_ref, start_ref, dest_ref):
    tm = eidx_ref.shape[1]
    experts = lax.broadcasted_iota(I32, (N_EXPERTS, tm), 0)
    for r in range(EXPERT_TOPK):
        base = jnp.sum(jnp.where(experts == eidx_ref[r:r + 1, :], start_ref[...], 0.0), axis=0, keepdims=True)
        dest_ref[r:r + 1, :] = base.astype(I32) + rank_ref[r:r + 1, :]


def _dest_rows(eidx_t, rank_t, start_col, tm=512):
    n = eidx_t.shape[1]
    slot = pl.BlockSpec((EXPERT_TOPK, tm), lambda i: (0, i))
    return pl.pallas_call(
        _dest_body,
        out_shape=jax.ShapeDtypeStruct((EXPERT_TOPK, n), I32),
        grid=(n // tm,),
        in_specs=[slot, slot, pl.BlockSpec(start_col.shape, lambda i: (0, 0))],
        out_specs=slot,
        compiler_params=_params(("parallel",)),
        name="dest_rows",
    )(eidx_t, rank_t, start_col)


def _scatter_body(dest_ref, x_ref, _, o_ref, sem):
    tm = x_ref.shape[0]

    def row_copy(t, k):
        return pltpu.make_async_copy(x_ref.at[pl.ds(t, 1)], o_ref.at[pl.ds(dest_ref[k, t], 1)], sem)

    def issue(t, carry):
        for k in range(EXPERT_TOPK):
            row_copy(t, k).start()
        return carry

    lax.fori_loop(0, tm, issue, 0)
    for k in range(EXPERT_TOPK):
        pltpu.make_async_copy(x_ref, o_ref.at[pl.ds(0, tm)], sem).wait()


def _scatter_rows(xw, dest_t, rows, tm=256):
    n, width = xw.shape
    return pl.pallas_call(
        _scatter_body,
        out_shape=jax.ShapeDtypeStruct((rows, width), xw.dtype),
        grid=(n // tm,),
        in_specs=[pl.BlockSpec((EXPERT_TOPK, tm), lambda i: (0, i), memory_space=pltpu.SMEM),
                  pl.BlockSpec((tm, width), lambda i: (i, 0)),
                  pl.BlockSpec(memory_space=pl.ANY)],
        out_specs=pl.BlockSpec(memory_space=pl.ANY),
        scratch_shapes=[pltpu.SemaphoreType.DMA],
        input_output_aliases={2: 0},
        compiler_params=_params(("arbitrary",)),
        name="scatter_rows",
    )(dest_t, xw, jnp.zeros((rows, width), xw.dtype))


def _experts_body(block_exp_ref, n_used_ref, x_ref, wgu_ref, wdn_ref, o_ref, wgu_bf, wdn_bf):
    i = pl.program_id(0)
    used = i < n_used_ref[0]

    @pl.when(used)
    def _():
        prev = block_exp_ref[jnp.maximum(i - 1, 0)]

        @pl.when((i == 0) | (block_exp_ref[i] != prev))
        def _():
            wgu_bf[...] = wgu_ref[...].astype(BF16)
            wdn_bf[...] = wdn_ref[...].astype(BF16)

        x_lo, x_hi = _unpack_rows(x_ref[...])
        half = x_lo.shape[1]
        h = (jnp.dot(x_lo, wgu_bf[0:half, :], preferred_element_type=F32)
             + jnp.dot(x_hi, wgu_bf[half:, :], preferred_element_type=F32))
        ff = h.shape[1] // 2
        act = jax.nn.silu(h[:, :ff]) * h[:, ff:]
        o_ref[...] = jnp.dot(act.astype(BF16), wdn_bf[...], preferred_element_type=F32)

    @pl.when(jnp.logical_not(used))
    def _():
        o_ref[...] = jnp.zeros(o_ref.shape, F32)


def _experts(x_rows, block_exp, n_used, w_gu, w_dn, layer):
    rows, width = x_rows.shape
    n_blocks = rows // EXPERT_BLOCK
    d, ff2 = w_gu.shape[-2:]
    grid_spec = pltpu.PrefetchScalarGridSpec(
        num_scalar_prefetch=2,
        grid=(n_blocks,),
        in_specs=[pl.BlockSpec((EXPERT_BLOCK, width), lambda i, be, nu: (i, 0)),
                  pl.BlockSpec((None, None, d, ff2), lambda i, be, nu: (layer, be[i], 0, 0)),
                  pl.BlockSpec((None, None, ff2 // 2, d), lambda i, be, nu: (layer, be[i], 0, 0))],
        out_specs=pl.BlockSpec((EXPERT_BLOCK, d), lambda i, be, nu: (i, 0)),
        scratch_shapes=[pltpu.VMEM((d, ff2), BF16), pltpu.VMEM((ff2 // 2, d), BF16)],
    )
    return pl.pallas_call(
        _experts_body,
        out_shape=jax.ShapeDtypeStruct((rows, d), F32),
        grid_spec=grid_spec,
        compiler_params=_params(("arbitrary",)),
        name="experts",
    )(block_exp, n_used, x_rows, w_gu, w_dn)


def _block_plan(counts, n_blocks):
    counts = counts.reshape(-1).astype(I32)
    padded = (counts + EXPERT_BLOCK - 1) // EXPERT_BLOCK * EXPERT_BLOCK
    pad_end = jnp.cumsum(padded)
    start_col = (pad_end - padded).astype(F32)[:, None]
    end_blocks = pad_end // EXPERT_BLOCK
    block_exp = jnp.minimum(jnp.sum(end_blocks[None, :] <= jnp.arange(n_blocks)[:, None], axis=1),
                            N_EXPERTS - 1).astype(I32)
    return start_col, block_exp, end_blocks[-1:].astype(I32)


def _ffn_out_body(dest_ref, dest_next_ref, x_ref, gw_ref, yb_ref, wgu_ref, wdn_ref, g_ref, beta_ref, o_ref,
                  ybuf, sems, *, alpha):
    i = pl.program_id(0)
    n_steps = pl.num_programs(0)
    tm = x_ref.shape[0]
    slot = i % 2

    def gather(dests, to_slot):
        def issue(t, carry):
            for k in range(EXPERT_TOPK):
                pltpu.make_async_copy(yb_ref.at[pl.ds(dests[k, t], 1)], ybuf.at[to_slot, k, pl.ds(t, 1)],
                                      sems.at[to_slot]).start()
            return carry
        lax.fori_loop(0, tm, issue, 0)

    @pl.when(i == 0)
    def _():
        gather(dest_ref, 0)

    @pl.when(i + 1 < n_steps)
    def _():
        gather(dest_next_ref, 1 - slot)

    x = x_ref[...]
    h = jnp.dot(x.astype(BF16), wgu_ref[...], preferred_element_type=F32)
    ff = h.shape[1] // 2
    act = jax.nn.silu(h[:, :ff]) * h[:, ff:]
    shared = jnp.dot(act.astype(BF16), wdn_ref[...], preferred_element_type=F32)

    for k in range(EXPERT_TOPK):
        pltpu.make_async_copy(yb_ref.at[pl.ds(0, tm)], ybuf.at[slot, k], sems.at[slot]).wait()
    routed = None
    for k in range(EXPERT_TOPK):
        term = ybuf[slot, k] * gw_ref[:, k:k + 1]
        routed = term if routed is None else routed + term
    o_ref[...] = _layer_norm(alpha * x + (routed + shared), g_ref[...], beta_ref[...])


def _ffn_out(x, yb, dest_t, gw, ws_gu, ws_dn, g, beta, alpha, tm=256):
    n, d = x.shape
    n_tiles = n // tm
    row = pl.BlockSpec((tm, d), lambda i: (i, 0))
    full = lambda arr: pl.BlockSpec(arr.shape, lambda i: (0,) * arr.ndim)
    dest_spec = lambda index: pl.BlockSpec((EXPERT_TOPK, tm), index, memory_space=pltpu.SMEM)
    return pl.pallas_call(
        functools.partial(_ffn_out_body, alpha=alpha),
        out_shape=jax.ShapeDtypeStruct((n, d), F32),
        grid=(n_tiles,),
        in_specs=[dest_spec(lambda i: (0, i)), dest_spec(lambda i: (0, jnp.minimum(i + 1, n_tiles - 1))),
                  row, pl.BlockSpec((tm, EXPERT_TOPK), lambda i: (i, 0)),
                  pl.BlockSpec(memory_space=pl.ANY),
                  full(ws_gu), full(ws_dn), full(g), full(beta)],
        out_specs=row,
        scratch_shapes=[pltpu.VMEM((2, EXPERT_TOPK, tm, d), F32), pltpu.SemaphoreType.DMA((2,))],
        compiler_params=_params(("arbitrary",)),
        name="ffn_out",
    )(dest_t, dest_t, x, gw, yb, ws_gu, ws_dn, g, beta)


def _moe(x, xw, layer, w_router_t, b_router_col, w_exp_gate_up, w_exp_down, ws_gu, ws_dn, g, beta, alpha):
    n = x.shape[0]
    n_blocks = -(-n * EXPERT_TOPK // EXPERT_BLOCK) + N_EXPERTS
    eidx_t, gw_t, rank_t, counts = _router(x, w_router_t, b_router_col)
    start_col, block_exp, n_used = _block_plan(counts, n_blocks)
    dest_t = _dest_rows(eidx_t, rank_t, start_col)
    x_rows = _scatter_rows(xw, dest_t, n_blocks * EXPERT_BLOCK)
    yb = _experts(x_rows, block_exp, n_used, w_exp_gate_up, w_exp_down, layer)
    return _ffn_out(x, yb, dest_t, gw_t.T, ws_gu, ws_dn, g, beta, alpha)


def kernel(x_prompt, x_sample, mem_prompt, cache_a_k, cache_a_v, cache_b_k, cache_b_v, cache_b_idx_k,
           cache_mem_k, cache_mem_v, w_in, rel_bias, w_mem_kv, w_branch, w_out, ln_mix_g, ln_mix_b,
           w_router, b_router, w_exp_gate_up, w_exp_down, w_sh_gate_up, w_sh_down, ln_ffn_g, ln_ffn_b):
    n_batch, seq, d = x_prompt.shape
    s_batch, s_seq, _ = x_sample.shape
    depth = w_in.shape[0]
    past = cache_b_k.shape[2]
    assert s_seq == CHUNK and seq % CHUNK == 0 and cache_a_k.shape[2] == BAND and seq >= BAND
    alpha = (2 * depth) ** 0.25
    n_p, n_s = n_batch * seq, s_batch * s_seq
    n_tot = n_p + n_s
    tm = 512
    assert seq % tm == 0 and n_p % tm == 0 and n_s % tm == 0 and tm % CHUNK == 0

    x = jnp.concatenate([x_prompt.reshape(n_p, d), x_sample.reshape(n_s, d)])

    pos = jnp.concatenate([jnp.arange(seq), jnp.tile(past + jnp.arange(s_seq), tm // s_seq)])
    tab_heads = _rope_tables(pos, LANES)
    tab_kiw = _rope_tables(pos, HEAD_DIM)
    p_tiles, seq_tiles = n_p // tm, seq // tm
    table_index = lambda i: jnp.where(i < p_tiles, i % seq_tiles, seq_tiles)

    wd = WIDTH
    c_qi, c_ki, c_wi, c_qm, c_gates = 6 * wd, 7 * wd, 7 * wd + HEAD_DIM, 7 * wd + HEAD_DIM + N_HEADS, 8 * wd + HEAD_DIM + N_HEADS
    cache_a_k = cache_a_k.reshape(depth, s_batch, BAND, wd)
    cache_a_v = cache_a_v.reshape(depth, s_batch, BAND, wd)
    cache_b_k = cache_b_k.reshape(depth, s_batch, past, wd)
    cache_b_v = cache_b_v.reshape(depth, s_batch, past, wd)
    cache_mem_k = cache_mem_k.reshape(depth, s_batch, N_MEM, wd)
    cache_mem_v = cache_mem_v.reshape(depth, s_batch, N_MEM, wd)
    mem_rows = mem_prompt.reshape(n_batch * N_MEM, d)

    outs = {k: [] for k in ("p_ak", "p_av", "p_bk", "p_bv", "p_ik", "p_mk", "p_mv",
                            "s_ak", "s_av", "s_bk", "s_bv", "s_ik")}
    for l in range(depth):
        w = w_in[l]
        w_a = w[:, :3 * wd].astype(BF16)
        w_b = w[:, 3 * wd:7 * wd].astype(BF16)
        w_c = w[:, c_qm:].astype(BF16)
        w_d = jnp.concatenate([w[:, c_ki:c_qm], jnp.zeros((d, LANES - HEAD_DIM - N_HEADS), F32)], 1).astype(BF16)

        qa, ka, va = _project(x, w_a, (wd, wd, wd), tm=tm, name="proj_a")
        qb, kb, vb, qi = _project(x, w_b, (wd, wd, wd, wd), (True, True, False, True), tab_heads, table_index,
                                  tm=tm, name="proj_b")
        qm, gates = _project(x, w_c, (wd, 3 * d), tm=tm, name="proj_c")
        (kiw,) = _project(x, w_d, (LANES,), (True,), tab_kiw, table_index, tm=tm, name="proj_d")
        mk, mv = _project(mem_rows, w_mem_kv[l].astype(BF16), (wd, wd), tm=N_MEM, name="proj_mem")

        bias = _band_bias(rel_bias[l])
        a = (_band_prompt(qa, ka, va, bias, n_batch, seq),
             _band_sample(qa, ka, va, cache_a_k, cache_a_v, l, bias, s_batch, n_p))
        b = (_dsa_prompt(qb, kb, vb, qi, kiw, n_batch, seq),
             _dsa_sample(qb, kb, vb, qi, kiw, cache_b_k, cache_b_v, cache_b_idx_k, l, s_batch, n_p))
        m = (_mem_attn(qm, mk, mv, lambda bb: (bb, 0), n_batch, seq, 0, tm),
             _mem_attn(qm, cache_mem_k, cache_mem_v, lambda bb, l=l: (l, bb, 0, 0), s_batch, s_seq, n_p, s_seq))

        x1, x1w = _merge(x, (a, b, m), gates, w_branch[l].astype(BF16), w_out[l].astype(BF16),
                         ln_mix_g[l][None], ln_mix_b[l][None], alpha)
        x = _moe(x1, x1w, l, w_router[l].T.astype(BF16), b_router[l][:, None], w_exp_gate_up, w_exp_down,
                 w_sh_gate_up[l].astype(BF16), w_sh_down[l].astype(BF16),
                 ln_ffn_g[l][None], ln_ffn_b[l][None], alpha)

        ka_p = ka[:n_p].reshape(n_batch, seq, wd)
        va_p = va[:n_p].reshape(n_batch, seq, wd)
        outs["p_ak"].append(ka_p[:, -BAND:])
        outs["p_av"].append(va_p[:, -BAND:])
        outs["p_bk"].append(kb[:n_p].reshape(n_batch, seq, wd))
        outs["p_bv"].append(vb[:n_p].reshape(n_batch, seq, wd))
        outs["p_ik"].append(kiw[:n_p, :HEAD_DIM].reshape(n_batch, seq, HEAD_DIM))
        outs["p_mk"].append(mk.reshape(n_batch, N_MEM, wd))
        outs["p_mv"].append(mv.reshape(n_batch, N_MEM, wd))
        ka_s = ka[n_p:].reshape(s_batch, s_seq, wd)
        va_s = va[n_p:].reshape(s_batch, s_seq, wd)
        outs["s_ak"].append(jnp.concatenate([cache_a_k[l], ka_s], 1)[:, -BAND:])
        outs["s_av"].append(jnp.concatenate([cache_a_v[l], va_s], 1)[:, -BAND:])
        outs["s_bk"].append(kb[n_p:].reshape(s_batch, s_seq, wd))
        outs["s_bv"].append(vb[n_p:].reshape(s_batch, s_seq, wd))
        outs["s_ik"].append(kiw[n_p:, :HEAD_DIM].reshape(s_batch, s_seq, HEAD_DIM))

    heads = lambda t: t.reshape(t.shape[:-1] + (N_HEADS, HEAD_DIM))
    mheads = lambda t: t.reshape(t.shape[:-1] + (M_HEADS, M_HEAD_DIM))
    st = lambda key: jnp.stack(outs[key])
    return (x[:n_p].reshape(n_batch, seq, d), x[n_p:].reshape(s_batch, s_seq, d),
            heads(st("p_ak")), heads(st("p_av")), heads(st("p_bk")), heads(st("p_bv")), st("p_ik"),
            mheads(st("p_mk")), mheads(st("p_mv")),
            heads(st("s_ak")), heads(st("s_av")), heads(st("s_bk")), heads(st("s_bv")), st("s_ik"))
```

```python
import functools

import jax
import jax.numpy as jnp
from jax import lax
from jax.experimental import pallas as pl
from jax.experimental.pallas import tpu as pltpu

F32 = jnp.float32
BF16 = jnp.bfloat16
I32 = jnp.int32

CHUNK = 64
HEAD_DIM = 64
N_HEADS = 8
WIDTH = N_HEADS * HEAD_DIM
BAND = 8 * CHUNK
MAX_REL = 128
TOPK_KEYS = 256
M_HEADS = 4
M_HEAD_DIM = 128
N_MEM = 256
ROPE_THETA = 10000.0
N_EXPERTS = 256
EXPERT_TOPK = 8
N_GROUPS = 8
TOPK_GROUPS = 4
EXPERT_FF = 256
ROUTE_SCALE = 2.5
EXPERT_BLOCK = 256
LN_EPS = 1e-5

LANES = 128
NEG_BIG = -1e30
INT_MIN = -(2 ** 31)
VMEM_LIMIT = 48 * 1024 * 1024

_NT_DIMS = (((1,), (1,)), ((), ()))


def _params(semantics):
    return pltpu.CompilerParams(dimension_semantics=semantics, vmem_limit_bytes=VMEM_LIMIT)


def _proj_body(*refs, widths, rope_flags):
    has_rope = any(rope_flags)
    x_ref, w_ref = refs[0], refs[1]
    if has_rope:
        cos_ref, sin_lo_ref, sin_hi_ref = refs[2:5]
        outs = refs[5:]
    else:
        outs = refs[2:]
    xb = x_ref[...].astype(BF16)
    off = 0
    for g, width in enumerate(widths):
        y = jnp.dot(xb, w_ref[:, off:off + width], preferred_element_type=F32)
        if rope_flags[g]:
            cos, sin_lo, sin_hi = cos_ref[...], sin_lo_ref[...], sin_hi_ref[...]
            for k in range(width // LANES):
                yk = y[:, k * LANES:(k + 1) * LANES]
                outs[g][:, k * LANES:(k + 1) * LANES] = (
                    yk * cos + pltpu.roll(yk, LANES - 32, 1) * sin_lo + pltpu.roll(yk, 32, 1) * sin_hi)
        else:
            outs[g][...] = y
        off += width


def _project(x, w, widths, rope_flags=None, tables=None, table_index=None, tm=512, name="proj"):
    n, d = x.shape
    rope_flags = rope_flags or (False,) * len(widths)
    assert n % tm == 0 and w.shape == (d, sum(widths))
    in_specs = [pl.BlockSpec((tm, d), lambda i: (i, 0)),
                pl.BlockSpec(w.shape, lambda i: (0, 0))]
    args = [x, w]
    if any(rope_flags):
        for t in tables:
            in_specs.append(pl.BlockSpec((tm, LANES), lambda i: (table_index(i), 0)))
            args.append(t)
    return pl.pallas_call(
        functools.partial(_proj_body, widths=tuple(widths), rope_flags=tuple(rope_flags)),
        out_shape=[jax.ShapeDtypeStruct((n, wd), F32) for wd in widths],
        grid=(n // tm,),
        in_specs=in_specs,
        out_specs=[pl.BlockSpec((tm, wd), lambda i: (i, 0)) for wd in widths],
        compiler_params=_params(("parallel",)),
        name=name,
    )(*args)


def _rope_tables(pos, rope_lanes):
    half = HEAD_DIM // 2
    inv_freq = ROPE_THETA ** (-jnp.arange(half, dtype=F32) / half)
    ang = pos.astype(F32)[:, None] * inv_freq[None, :]
    cos, sin = jnp.cos(ang), jnp.sin(ang)
    reps = LANES // HEAD_DIM
    zero = jnp.zeros_like(sin)
    cos_t = jnp.tile(jnp.concatenate([cos, cos], -1), (1, reps))
    sin_lo = jnp.tile(jnp.concatenate([-sin, zero], -1), (1, reps))
    sin_hi = jnp.tile(jnp.concatenate([zero, sin], -1), (1, reps))
    keep = (jnp.arange(LANES) < rope_lanes)[None, :]
    return (jnp.where(keep, cos_t, 1.0), jnp.where(keep, sin_lo, 0.0), jnp.where(keep, sin_hi, 0.0))


def _head_mask(rows, head_in_pair):
    lane = lax.broadcasted_iota(I32, (rows, LANES), 1)
    lo = head_in_pair * HEAD_DIM
    return (lane >= lo) & (lane < lo + HEAD_DIM)


def _pair_merge(o_even, o_odd):
    lane = lax.broadcasted_iota(I32, o_even.shape, 1)
    return jnp.where(lane < HEAD_DIM, o_even, o_odd)


def _band_body(*refs, sample):
    if sample:
        q_ref, kn_ref, vn_ref, kh_ref, vh_ref, bias_ref, o_ref, kpad, vpad = refs
        kpad[0:BAND, :] = kh_ref[...].astype(BF16)
        vpad[0:BAND, :] = vh_ref[...].astype(BF16)
        kpad[BAND:, :] = kn_ref[...].astype(BF16)
        vpad[BAND:, :] = vn_ref[...].astype(BF16)
        first_chunk = 0
    else:
        q_ref, k_ref, v_ref, bias_ref, o_ref, kpad, vpad = refs
        t = pl.program_id(1)

        @pl.when(t == 0)
        def _():
            kpad[0:BAND, :] = jnp.zeros((BAND, WIDTH), BF16)
            vpad[0:BAND, :] = jnp.zeros((BAND, WIDTH), BF16)
            kpad[BAND:, :] = k_ref[...].astype(BF16)
            vpad[BAND:, :] = v_ref[...].astype(BF16)

        first_chunk = t * (q_ref.shape[0] // CHUNK)

    win = BAND + CHUNK
    for cc in range(q_ref.shape[0] // CHUNK):
        rows = slice(cc * CHUNK, (cc + 1) * CHUNK)
        c = first_chunk + cc
        start = c * CHUNK if sample else pl.multiple_of(c * CHUNK, CHUNK)
        valid = lax.broadcasted_iota(I32, (CHUNK, win), 1) >= BAND - c * CHUNK
        q = q_ref[rows, :] * (HEAD_DIM ** -0.5)
        for hp in range(N_HEADS // 2):
            cols = slice(hp * LANES, (hp + 1) * LANES)
            q2 = q[:, cols]
            k2 = kpad[pl.ds(start, win), cols]
            v2 = vpad[pl.ds(start, win), cols]
            outs = []
            for hh in range(2):
                qm = jnp.where(_head_mask(CHUNK, hh), q2, 0.0).astype(BF16)
                s = lax.dot_general(qm, k2, _NT_DIMS, preferred_element_type=F32)
                s = s + bias_ref[2 * hp + hh]
                if not sample:
                    s = jnp.where(valid, s, -jnp.inf)
                p = jnp.exp(s - jnp.max(s, axis=1, keepdims=True))
                denom = jnp.sum(p, axis=1, keepdims=True)
                o = jnp.dot(p.astype(BF16), v2, preferred_element_type=F32)
                outs.append(o / denom)
            o_ref[rows, cols] = _pair_merge(outs[0], outs[1])


def _band_bias(table):
    win = BAND + CHUNK
    assert CHUNK <= MAX_REL <= BAND
    n_heads = table.shape[0]
    g = jnp.concatenate([jnp.broadcast_to(table[:, 2 * MAX_REL:], (n_heads, win - MAX_REL)),
                         table[:, MAX_REL - CHUNK:2 * MAX_REL][:, ::-1],
                         jnp.zeros((n_heads, 1), table.dtype)], axis=1).astype(F32)
    period = g.shape[1]
    skew = jnp.tile(g, (1, CHUNK))[:, :CHUNK * (period - 1)].reshape(n_heads, CHUNK, period - 1)
    return skew[:, :, CHUNK - 1:CHUNK - 1 + win]


def _band_prompt(q, k, v, bias, n_batch, seq, q_rows=256):
    nt = seq // q_rows
    assert seq % q_rows == 0 and q_rows % CHUNK == 0
    return pl.pallas_call(
        functools.partial(_band_body, sample=False),
        out_shape=jax.ShapeDtypeStruct((n_batch * seq, WIDTH), F32),
        grid=(n_batch, nt),
        in_specs=[pl.BlockSpec((q_rows, WIDTH), lambda b, t: (b * nt + t, 0)),
                  pl.BlockSpec((seq, WIDTH), lambda b, t: (b, 0)),
                  pl.BlockSpec((seq, WIDTH), lambda b, t: (b, 0)),
                  pl.BlockSpec(bias.shape, lambda b, t: (0, 0, 0))],
        out_specs=pl.BlockSpec((q_rows, WIDTH), lambda b, t: (b * nt + t, 0)),
        scratch_shapes=[pltpu.VMEM((BAND + seq, WIDTH), BF16), pltpu.VMEM((BAND + seq, WIDTH), BF16)],
        compiler_params=_params(("parallel", "arbitrary")),
        name="band_prompt",
    )(q, k, v, bias)


def _band_sample(q, k, v, k_hist, v_hist, layer, bias, n_batch, row0):
    blk0 = row0 // CHUNK
    hist_spec = pl.BlockSpec((None, None, BAND, WIDTH), lambda b: (layer, b, 0, 0))
    new_spec = pl.BlockSpec((CHUNK, WIDTH), lambda b: (blk0 + b, 0))
    return pl.pallas_call(
        functools.partial(_band_body, sample=True),
        out_shape=jax.ShapeDtypeStruct((n_batch * CHUNK, WIDTH), F32),
        grid=(n_batch,),
        in_specs=[new_spec, new_spec, new_spec, hist_spec, hist_spec,
                  pl.BlockSpec(bias.shape, lambda b: (0, 0, 0))],
        out_specs=pl.BlockSpec((CHUNK, WIDTH), lambda b: (b, 0)),
        scratch_shapes=[pltpu.VMEM((BAND + CHUNK, WIDTH), BF16), pltpu.VMEM((BAND + CHUNK, WIDTH), BF16)],
        compiler_params=_params(("parallel",)),
        name="band_sample",
    )(q, k, v, k_hist, v_hist, bias)


def _mem_body(q_ref, k_ref, v_ref, o_ref):
    scale = M_HEAD_DIM ** -0.5
    for h in range(M_HEADS):
        cols = slice(h * M_HEAD_DIM, (h + 1) * M_HEAD_DIM)
        s = lax.dot_general(q_ref[:, cols].astype(BF16), k_ref[:, cols].astype(BF16), _NT_DIMS,
                            preferred_element_type=F32) * scale
        p = jnp.exp(s - jnp.max(s, axis=1, keepdims=True))
        denom = jnp.sum(p, axis=1, keepdims=True)
        o = jnp.dot(p.astype(BF16), v_ref[:, cols].astype(BF16), preferred_element_type=F32)
        o_ref[:, cols] = o / denom


def _mem_attn(q, mk, mv, kv_index, n_batch, rows_per_batch, row0, tq):
    width = M_HEADS * M_HEAD_DIM
    per = rows_per_batch // tq
    blk0 = row0 // tq
    kv_block = (None,) * (mk.ndim - 2) + (N_MEM, width)
    kv_spec = pl.BlockSpec(kv_block, lambda b, t: kv_index(b))
    return pl.pallas_call(
        _mem_body,
        out_shape=jax.ShapeDtypeStruct((n_batch * rows_per_batch, width), F32),
        grid=(n_batch, per),
        in_specs=[pl.BlockSpec((tq, width), lambda b, t: (blk0 + b * per + t, 0)), kv_spec, kv_spec],
        out_specs=pl.BlockSpec((tq, width), lambda b, t: (b * per + t, 0)),
        compiler_params=_params(("parallel", "parallel")),
        name="mem_attn",
    )(q, mk, mv)


def _split_pair_lanes(x):
    lane = lax.broadcasted_iota(I32, x.shape, 1)
    even = (lane & (LANES - 1)) < HEAD_DIM
    return jnp.where(even, x, 0.0), jnp.where(even, 0.0, x)


def _dsa_body(*refs, sample, key_block, n_keys, n_sel):
    n_in = 8 if sample else 6
    if sample:
        q_ref, qi_ref, kiwq_ref, kn_ref, vn_ref, kc_ref, vc_ref, kic_ref = refs[:n_in]
    else:
        q_ref, qi_ref, kiwq_ref, k_ref, v_ref, kiw_ref = refs[:n_in]
    o_ref = refs[n_in]
    (k_even, k_odd, v_t, ki_lo, ki_hi, keybuf, maskbuf, q_t, qi_t, kiwq_t, m_sc, l_sc, a_sc, acc_t,
     s_sc, p_sc) = refs[n_in + 1:]
    kb = key_block
    n_pad = k_even.shape[0]
    nq = q_ref.shape[0]
    stage_rows = 256

    def stage_kv(row0, k_new, v_new):
        n_new = k_new.shape[0]
        ke, ko = _split_pair_lanes(k_new)
        k_even[row0:row0 + n_new, :] = ke.astype(BF16)
        k_odd[row0:row0 + n_new, :] = ko.astype(BF16)
        v_t[:, row0:row0 + n_new] = v_new.T.astype(BF16)

    def stage_index_keys(row0, kiw):
        lane = lax.broadcasted_iota(I32, kiw.shape, 1)
        ki_lo[row0:row0 + kiw.shape[0], :] = jnp.where(lane < HEAD_DIM, kiw, 0.0).astype(BF16)
        ki_hi[row0:row0 + kiw.shape[0], :] = jnp.where(lane < HEAD_DIM, 0.0, pltpu.roll(kiw, HEAD_DIM, 1)).astype(BF16)

    def stage_keys():
        if sample:
            n_hist = kc_ref.shape[0]
            for r in range(0, n_hist, stage_rows):
                stage_kv(r, kc_ref[r:r + stage_rows, :], vc_ref[r:r + stage_rows, :])
            kic = kic_ref[...]
            zeros = jnp.zeros_like(kic)
            ki_lo[0:n_hist, :] = jnp.concatenate([kic, zeros], axis=1).astype(BF16)
            ki_hi[0:n_hist, :] = jnp.concatenate([zeros, kic], axis=1).astype(BF16)
            stage_kv(n_hist, kn_ref[...], vn_ref[...])
            stage_index_keys(n_hist, kiwq_ref[...])
            if n_pad > n_keys:
                for ref in (k_even, k_odd, ki_lo, ki_hi):
                    ref[n_keys:, :] = jnp.zeros((n_pad - n_keys, ref.shape[1]), BF16)
                v_t[:, n_keys:] = jnp.zeros((v_t.shape[0], n_pad - n_keys), BF16)
        else:
            for r in range(0, n_keys, stage_rows):
                stage_kv(r, k_ref[r:r + stage_rows, :], v_ref[r:r + stage_rows, :])
            stage_index_keys(0, kiw_ref[...])

    if sample:
        stage_keys()
        limit = n_keys
        nb = n_pad // kb
        search_bits = 32
    else:
        t = pl.program_id(1)
        pl.when(t == 0)(stage_keys)
        chunk = t * (nq // CHUNK) + lax.broadcasted_iota(I32, (1, nq), 1) // CHUNK
        limit = (chunk + 1) * CHUNK
        limit_max = (t + 1) * nq
        nb = (limit_max + kb - 1) // kb
        search_bits = jnp.where(limit_max <= n_sel, 0, 32)

    def block_rows(j):
        return pl.ds(pl.multiple_of(j * kb, kb), kb)

    def key_pos(j):
        return j * kb + lax.broadcasted_iota(I32, (kb, nq), 0)

    q_t[...] = (q_ref[...] * (HEAD_DIM ** -0.5)).T.astype(BF16)
    qi_t[...] = qi_ref[...].T.astype(BF16)
    kiwq_t[...] = kiwq_ref[...].T

    def score_block(j, carry):
        acc = jnp.zeros((kb, nq), F32)
        for h in range(N_HEADS):
            pair = slice((h // 2) * LANES, (h // 2 + 1) * LANES)
            ki_ref = ki_lo if h % 2 == 0 else ki_hi
            d = jnp.dot(ki_ref[block_rows(j), :], qi_t[pair, :], preferred_element_type=F32)
            acc = acc + kiwq_t[HEAD_DIM + h:HEAD_DIM + h + 1, :] * jnp.maximum(d, 0.0)
        acc = acc + 0.0
        acc = jnp.where(key_pos(j) < limit, acc, -jnp.inf)
        bits = lax.bitcast_convert_type(acc, I32)
        keybuf[block_rows(j), :] = bits ^ ((bits >> 31) & 0x7FFFFFFF)
        return carry

    lax.fori_loop(0, nb, score_block, 0)

    def count(pred_fn):
        def body(j, part):
            hit = jnp.where(pred_fn(keybuf[block_rows(j), :], j), 1.0, 0.0)
            return part + jnp.sum(hit.reshape(kb // 8, 8, nq), axis=0)
        part = lax.fori_loop(0, nb, body, jnp.zeros((8, nq), F32))
        return jnp.sum(part, axis=0, keepdims=True)

    def bit_step(i, prefix):
        cand = prefix | lax.shift_left(jnp.int32(1), 31 - i)
        thr_i = cand ^ INT_MIN
        cnt = count(lambda kblk, j: kblk >= thr_i)
        return jnp.where(cnt >= n_sel, cand, prefix)

    prefix = lax.fori_loop(0, search_bits, bit_step, jnp.zeros((1, nq), I32))
    thr = prefix ^ INT_MIN
    above = count(lambda kblk, j: kblk > thr)
    at_least = count(lambda kblk, j: kblk >= thr)
    need = n_sel - above
    surplus = at_least > n_sel

    pos_bits = max(1, (n_pad - 1).bit_length() + 1)
    any_surplus = jnp.max(jnp.where(surplus, 1.0, 0.0)) > 0.0

    def pos_step(i, bound):
        cand = bound | lax.shift_left(jnp.int32(1), pos_bits - 1 - i)
        cnt = count(lambda kblk, j: (kblk == thr) & (key_pos(j) < cand))
        return jnp.where(cnt <= need, cand, bound)

    bound = lax.fori_loop(0, jnp.where(any_surplus, pos_bits, 0), pos_step, jnp.zeros((1, nq), I32))
    bound = jnp.where(surplus, bound, 2 ** pos_bits - 1)

    def mask_block(j, carry):
        kblk = keybuf[block_rows(j), :]
        pos = key_pos(j)
        sel = ((kblk > thr) | ((kblk == thr) & (pos < bound))) & (pos < limit)
        maskbuf[block_rows(j), :] = jnp.where(sel, 0.0, NEG_BIG)
        return carry

    lax.fori_loop(0, nb, mask_block, 0)

    m_sc[...] = jnp.full(m_sc.shape, NEG_BIG, F32)
    l_sc[...] = jnp.zeros(l_sc.shape, F32)
    acc_t[...] = jnp.zeros(acc_t.shape, F32)

    def attend(j, carry):
        mask = maskbuf[block_rows(j), :]
        for h in range(N_HEADS):
            pair = slice((h // 2) * LANES, (h // 2 + 1) * LANES)
            k_ref_h = k_even if h % 2 == 0 else k_odd
            s_sc[h * kb:(h + 1) * kb, :] = jnp.dot(k_ref_h[block_rows(j), pair], q_t[pair, :],
                                                   preferred_element_type=F32) + mask
        for h in range(N_HEADS):
            stat = slice(h * 8, (h + 1) * 8)
            s = s_sc[h * kb:(h + 1) * kb, :]
            m_prev = m_sc[stat, :]
            m_new = jnp.maximum(m_prev, jnp.max(s, axis=0, keepdims=True))
            alpha = jnp.exp(m_prev - m_new)
            p = jnp.exp(s - m_new[0:1, :])
            l_sc[stat, :] = alpha * l_sc[stat, :] + jnp.sum(p, axis=0, keepdims=True)
            m_sc[stat, :] = m_new
            a_sc[stat, :] = alpha
            p_sc[h * kb:(h + 1) * kb, :] = p.astype(BF16)
        for h in range(N_HEADS):
            dims = slice(h * HEAD_DIM, (h + 1) * HEAD_DIM)
            pv = jnp.dot(v_t[dims, block_rows(j)], p_sc[h * kb:(h + 1) * kb, :], preferred_element_type=F32)
            acc_t[dims, :] = a_sc[h * 8:h * 8 + 1, :] * acc_t[dims, :] + pv
        return carry

    lax.fori_loop(0, nb, attend, 0)
    for h in range(N_HEADS):
        dims = slice(h * HEAD_DIM, (h + 1) * HEAD_DIM)
        acc_t[dims, :] = acc_t[dims, :] / l_sc[h * 8:h * 8 + 1, :]
    o_ref[...] = acc_t[...].T


def _dsa_scratch(nq, n_pad, key_block):
    return [pltpu.VMEM((n_pad, WIDTH), BF16), pltpu.VMEM((n_pad, WIDTH), BF16),
            pltpu.VMEM((WIDTH, n_pad), BF16),
            pltpu.VMEM((n_pad, LANES), BF16), pltpu.VMEM((n_pad, LANES), BF16),
            pltpu.VMEM((n_pad, nq), I32), pltpu.VMEM((n_pad, nq), F32),
            pltpu.VMEM((WIDTH, nq), BF16), pltpu.VMEM((WIDTH, nq), BF16),
            pltpu.VMEM((LANES, nq), F32),
            pltpu.VMEM((N_HEADS * 8, nq), F32), pltpu.VMEM((N_HEADS * 8, nq), F32),
            pltpu.VMEM((N_HEADS * 8, nq), F32),
            pltpu.VMEM((WIDTH, nq), F32),
            pltpu.VMEM((N_HEADS * key_block, nq), F32),
            pltpu.VMEM((N_HEADS * key_block, nq), BF16)]


def _dsa_prompt(q, k, v, qi, kiw, n_batch, seq, key_block=256, q_rows=256):
    nt = seq // q_rows
    key_block = min(key_block, seq)
    assert seq % key_block == 0 and seq % q_rows == 0 and q_rows % CHUNK == 0
    n_sel = min(TOPK_KEYS, seq // 4)
    tile_spec = lambda width: pl.BlockSpec((q_rows, width), lambda b, t: (b * nt + t, 0))
    seq_spec = lambda width: pl.BlockSpec((seq, width), lambda b, t: (b, 0))
    return pl.pallas_call(
        functools.partial(_dsa_body, sample=False, key_block=key_block, n_keys=seq, n_sel=n_sel),
        out_shape=jax.ShapeDtypeStruct((n_batch * seq, WIDTH), F32),
        grid=(n_batch, nt),
        in_specs=[tile_spec(WIDTH), tile_spec(WIDTH), tile_spec(LANES),
                  seq_spec(WIDTH), seq_spec(WIDTH), seq_spec(LANES)],
        out_specs=tile_spec(WIDTH),
        scratch_shapes=_dsa_scratch(q_rows, seq, key_block),
        compiler_params=_params(("parallel", "arbitrary")),
        name="dsa_prompt",
    )(q, qi, kiw, k, v, kiw)


def _dsa_sample(q, k, v, qi, kiw, k_cache, v_cache, ki_cache, layer, n_batch, row0, key_block=256):
    past = k_cache.shape[2]
    n_keys = past + CHUNK
    n_pad = -(-n_keys // key_block) * key_block
    n_sel = min(TOPK_KEYS, n_keys // 4)
    blk0 = row0 // CHUNK
    new_spec = lambda width: pl.BlockSpec((CHUNK, width), lambda b: (blk0 + b, 0))
    cache_spec = lambda width: pl.BlockSpec((None, None, past, width), lambda b: (layer, b, 0, 0))
    return pl.pallas_call(
        functools.partial(_dsa_body, sample=True, key_block=key_block, n_keys=n_keys, n_sel=n_sel),
        out_shape=jax.ShapeDtypeStruct((n_batch * CHUNK, WIDTH), F32),
        grid=(n_batch,),
        in_specs=[new_spec(WIDTH), new_spec(WIDTH), new_spec(LANES), new_spec(WIDTH), new_spec(WIDTH),
                  cache_spec(WIDTH), cache_spec(WIDTH), cache_spec(HEAD_DIM)],
        out_specs=pl.BlockSpec((CHUNK, WIDTH), lambda b: (b, 0)),
        scratch_shapes=_dsa_scratch(CHUNK, n_pad, key_block),
        compiler_params=_params(("parallel",)),
        name="dsa_sample",
    )(q, qi, kiw, k, v, k_cache, v_cache, ki_cache)


def _layer_norm(z, g, b):
    mu = jnp.mean(z, axis=-1, keepdims=True)
    zc = z - mu
    var = jnp.mean(zc * zc, axis=-1, keepdims=True)
    return zc * lax.rsqrt(var + LN_EPS) * g + b


HI_HALF = -65536


def _pack_rows(x):
    half = x.shape[1] // 2
    lo = lax.bitcast_convert_type(x[:, :half].astype(BF16).astype(F32), I32)
    hi = lax.bitcast_convert_type(x[:, half:].astype(BF16).astype(F32), I32)
    return (hi & HI_HALF) | lax.shift_right_logical(lo, 16)


def _unpack_rows(w):
    lo = lax.bitcast_convert_type(lax.shift_left(w, 16), F32).astype(BF16)
    hi = lax.bitcast_convert_type(w & HI_HALF, F32).astype(BF16)
    return lo, hi


def _merge_body(x_ref, ap_ref, as_ref, bp_ref, bs_ref, mp_ref, ms_ref, gates_ref, wbr_ref, wout_ref, g_ref, beta_ref,
                o_ref, ow_ref, *, alpha, prompt_tiles):
    d = x_ref.shape[1]
    in_prompt = pl.program_id(0) < prompt_tiles
    mix = None
    for n, (p_ref, s_ref) in enumerate(((ap_ref, as_ref), (bp_ref, bs_ref), (mp_ref, ms_ref))):
        branch = jnp.where(in_prompt, p_ref[...], s_ref[...])
        proj = jnp.dot(branch.astype(BF16), wbr_ref[n], preferred_element_type=F32)
        term = jax.nn.sigmoid(gates_ref[:, n * d:(n + 1) * d]) * proj
        mix = term if mix is None else mix + term
    y = jnp.dot(mix.astype(BF16), wout_ref[...], preferred_element_type=F32)
    out = _layer_norm(alpha * x_ref[...] + y, g_ref[...], beta_ref[...])
    o_ref[...] = out
    ow_ref[...] = _pack_rows(out)


def _merge(x, branches, gates, w_branch, w_out, g, beta, alpha, tm=256):
    n, d = x.shape
    prompt_tiles = branches[0][0].shape[0] // tm
    assert all(p.shape[0] == prompt_tiles * tm and (n - p.shape[0]) == s.shape[0] and s.shape[0] % tm == 0
               for p, s in branches)
    packed = jax.eval_shape(_pack_rows, jax.ShapeDtypeStruct((tm, d), F32))
    row = lambda width: pl.BlockSpec((tm, width), lambda i: (i, 0))
    full = lambda arr: pl.BlockSpec(arr.shape, lambda i: (0,) * arr.ndim)
    prompt_row = lambda width: pl.BlockSpec((tm, width), lambda i: (jnp.minimum(i, prompt_tiles - 1), 0))
    sample_row = lambda width: pl.BlockSpec((tm, width), lambda i: (jnp.maximum(i - prompt_tiles, 0), 0))
    branch_specs, branch_args = [], []
    for p, s in branches:
        branch_specs += [prompt_row(p.shape[1]), sample_row(s.shape[1])]
        branch_args += [p, s]
    return pl.pallas_call(
        functools.partial(_merge_body, alpha=alpha, prompt_tiles=prompt_tiles),
        out_shape=[jax.ShapeDtypeStruct((n, d), F32), jax.ShapeDtypeStruct((n, packed.shape[1]), packed.dtype)],
        grid=(n // tm,),
        in_specs=[row(d)] + branch_specs + [row(gates.shape[1]), full(w_branch), full(w_out), full(g), full(beta)],
        out_specs=[row(d), row(packed.shape[1])],
        compiler_params=_params(("parallel",)),
        name="merge",
    )(x, *branch_args, gates, w_branch, w_out, g, beta)


def _first_max(vals, idx, n):
    top = jnp.max(vals, axis=0, keepdims=True)
    arg = jnp.min(jnp.where(vals == top, idx, n), axis=0, keepdims=True)
    return top, arg


def _router_body(x_ref, wt_ref, bias_ref, eidx_ref, gw_ref, rank_ref, counts_ref, running):
    tm = x_ref.shape[0]

    @pl.when(pl.program_id(0) == 0)
    def _():
        running[...] = jnp.zeros(running.shape, F32)

    per = N_EXPERTS // N_GROUPS
    logits = lax.dot_general(wt_ref[...], x_ref[...].astype(BF16), _NT_DIMS, preferred_element_type=F32)
    s = jax.nn.sigmoid(logits)
    sb = s + bias_ref[...]
    in_group = lax.broadcasted_iota(I32, (per, tm), 0).astype(F32)
    group_scores = []
    for g in range(N_GROUPS):
        blk = sb[g * per:(g + 1) * per, :]
        top1, arg1 = _first_max(blk, in_group, per)
        top2 = jnp.max(jnp.where(in_group == arg1, -jnp.inf, blk), axis=0, keepdims=True)
        group_scores.append(top1 + top2)
    gs = jnp.concatenate(group_scores, axis=0)
    gidx = lax.broadcasted_iota(I32, (N_GROUPS, tm), 0).astype(F32)
    chosen = jnp.zeros((N_GROUPS, tm), F32)
    for _ in range(TOPK_GROUPS):
        _, arg = _first_max(gs, gidx, N_GROUPS)
        hit = gidx == arg
        chosen = jnp.where(hit, 1.0, chosen)
        gs = jnp.where(hit, -jnp.inf, gs)
    cand = jnp.concatenate(
        [jnp.where(chosen[g:g + 1, :] > 0.0, sb[g * per:(g + 1) * per, :], -jnp.inf) for g in range(N_GROUPS)],
        axis=0)
    eidx = lax.broadcasted_iota(I32, (N_EXPERTS, tm), 0).astype(F32)
    picks, weights, hits = [], [], []
    for _ in range(EXPERT_TOPK):
        _, arg = _first_max(cand, eidx, N_EXPERTS)
        hit = eidx == arg
        weights.append(jnp.sum(jnp.where(hit, s, 0.0), axis=0, keepdims=True))
        picks.append(arg)
        hits.append(hit)
        cand = jnp.where(hit, -jnp.inf, cand)
    total = weights[0]
    for w in weights[1:]:
        total = total + w
    taken = jnp.zeros((N_EXPERTS, tm), F32)
    for hit in hits:
        taken = jnp.where(hit, 1.0, taken)
    earlier = (lax.broadcasted_iota(I32, (tm, tm), 0) < lax.broadcasted_iota(I32, (tm, tm), 1))
    before = jnp.dot(taken.astype(BF16), jnp.where(earlier, 1.0, 0.0).astype(BF16),
                     preferred_element_type=F32) + running[...]
    for r in range(EXPERT_TOPK):
        eidx_ref[r:r + 1, :] = picks[r].astype(I32)
        gw_ref[r:r + 1, :] = weights[r] / total * ROUTE_SCALE
        rank_ref[r:r + 1, :] = jnp.sum(jnp.where(hits[r], before, 0.0), axis=0, keepdims=True).astype(I32)
    running[...] = running[...] + jnp.sum(taken, axis=1, keepdims=True)
    counts_ref[...] = running[...]


def _router(x, w_router_t, b_router_col, tm=256):
    n, d = x.shape
    slot = pl.BlockSpec((EXPERT_TOPK, tm), lambda i: (0, i))
    return pl.pallas_call(
        _router_body,
        out_shape=[jax.ShapeDtypeStruct((EXPERT_TOPK, n), I32), jax.ShapeDtypeStruct((EXPERT_TOPK, n), F32),
                   jax.ShapeDtypeStruct((EXPERT_TOPK, n), I32), jax.ShapeDtypeStruct((N_EXPERTS, 1), F32)],
        grid=(n // tm,),
        in_specs=[pl.BlockSpec((tm, d), lambda i: (i, 0)),
                  pl.BlockSpec(w_router_t.shape, lambda i: (0, 0)),
                  pl.BlockSpec(b_router_col.shape, lambda i: (0, 0))],
        out_specs=[slot, slot, slot, pl.BlockSpec((N_EXPERTS, 1), lambda i: (0, 0))],
        scratch_shapes=[pltpu.VMEM((N_EXPERTS, 1), F32)],
        compiler_params=_params(("arbitrary",)),
        name="router",
    )(x, w_router_t, b_router_col)


def _dest_body(eidx_ref, rank_ref, start_ref, dest_ref):
    tm = eidx_ref.shape[1]
    experts = lax.broadcasted_iota(I32, (N_EXPERTS, tm), 0)
    for r in range(EXPERT_TOPK):
        base = jnp.sum(jnp.where(experts == eidx_ref[r:r + 1, :], start_ref[...], 0.0), axis=0, keepdims=True)
        dest_ref[r:r + 1, :] = base.astype(I32) + rank_ref[r:r + 1, :]


def _dest_rows(eidx_t, rank_t, start_col, tm=512):
    n = eidx_t.shape[1]
    slot = pl.BlockSpec((EXPERT_TOPK, tm), lambda i: (0, i))
    return pl.pallas_call(
        _dest_body,
        out_shape=jax.ShapeDtypeStruct((EXPERT_TOPK, n), I32),
        grid=(n // tm,),
        in_specs=[slot, slot, pl.BlockSpec(start_col.shape, lambda i: (0, 0))],
        out_specs=slot,
        compiler_params=_params(("parallel",)),
        name="dest_rows",
    )(eidx_t, rank_t, start_col)


def _scatter_body(dest_ref, x_ref, _, o_ref, sem):
    tm = x_ref.shape[0]

    def row_copy(t, k):
        return pltpu.make_async_copy(x_ref.at[pl.ds(t, 1)], o_ref.at[pl.ds(dest_ref[k, t], 1)], sem)

    def issue(t, carry):
        for k in range(EXPERT_TOPK):
            row_copy(t, k).start()
        return carry

    lax.fori_loop(0, tm, issue, 0)
    for k in range(EXPERT_TOPK):
        pltpu.make_async_copy(x_ref, o_ref.at[pl.ds(0, tm)], sem).wait()


def _scatter_rows(xw, dest_t, rows, tm=256):
    n, width = xw.shape
    return pl.pallas_call(
        _scatter_body,
        out_shape=jax.ShapeDtypeStruct((rows, width), xw.dtype),
        grid=(n // tm,),
        in_specs=[pl.BlockSpec((EXPERT_TOPK, tm), lambda i: (0, i), memory_space=pltpu.SMEM),
                  pl.BlockSpec((tm, width), lambda i: (i, 0)),
                  pl.BlockSpec(memory_space=pl.ANY)],
        out_specs=pl.BlockSpec(memory_space=pl.ANY),
        scratch_shapes=[pltpu.SemaphoreType.DMA],
        input_output_aliases={2: 0},
        compiler_params=_params(("arbitrary",)),
        name="scatter_rows",
    )(dest_t, xw, jnp.zeros((rows, width), xw.dtype))


def _experts_body(block_exp_ref, n_used_ref, x_ref, wgu_ref, wdn_ref, o_ref, wgu_bf, wdn_bf):
    i = pl.program_id(0)
    used = i < n_used_ref[0]

    @pl.when(used)
    def _():
        prev = block_exp_ref[jnp.maximum(i - 1, 0)]

        @pl.when((i == 0) | (block_exp_ref[i] != prev))
        def _():
            wgu_bf[...] = wgu_ref[...].astype(BF16)
            wdn_bf[...] = wdn_ref[...].astype(BF16)

        x_lo, x_hi = _unpack_rows(x_ref[...])
        half = x_lo.shape[1]
        h = (jnp.dot(x_lo, wgu_bf[0:half, :], preferred_element_type=F32)
             + jnp.dot(x_hi, wgu_bf[half:, :], preferred_element_type=F32))
        ff = h.shape[1] // 2
        act = jax.nn.silu(h[:, :ff]) * h[:, ff:]
        o_ref[...] = jnp.dot(act.astype(BF16), wdn_bf[...], preferred_element_type=F32)

    @pl.when(jnp.logical_not(used))
    def _():
        o_ref[...] = jnp.zeros(o_ref.shape, F32)


def _experts(x_rows, block_exp, n_used, w_gu, w_dn, layer):
    rows, width = x_rows.shape
    n_blocks = rows // EXPERT_BLOCK
    d, ff2 = w_gu.shape[-2:]
    grid_spec = pltpu.PrefetchScalarGridSpec(
        num_scalar_prefetch=2,
        grid=(n_blocks,),
        in_specs=[pl.BlockSpec((EXPERT_BLOCK, width), lambda i, be, nu: (i, 0)),
                  pl.BlockSpec((None, None, d, ff2), lambda i, be, nu: (layer, be[i], 0, 0)),
                  pl.BlockSpec((None, None, ff2 // 2, d), lambda i, be, nu: (layer, be[i], 0, 0))],
        out_specs=pl.BlockSpec((EXPERT_BLOCK, d), lambda i, be, nu: (i, 0)),
        scratch_shapes=[pltpu.VMEM((d, ff2), BF16), pltpu.VMEM((ff2 // 2, d), BF16)],
    )
    return pl.pallas_call(
        _experts_body,
        out_shape=jax.ShapeDtypeStruct((rows, d), F32),
        grid_spec=grid_spec,
        compiler_params=_params(("arbitrary",)),
        name="experts",
    )(block_exp, n_used, x_rows, w_gu, w_dn)


def _block_plan(counts, n_blocks):
    counts = counts.reshape(-1).astype(I32)
    padded = (counts + EXPERT_BLOCK - 1) // EXPERT_BLOCK * EXPERT_BLOCK
    pad_end = jnp.cumsum(padded)
    start_col = (pad_end - padded).astype(F32)[:, None]
    end_blocks = pad_end // EXPERT_BLOCK
    block_exp = jnp.minimum(jnp.sum(end_blocks[None, :] <= jnp.arange(n_blocks)[:, None], axis=1),
                            N_EXPERTS - 1).astype(I32)
    return start_col, block_exp, end_blocks[-1:].astype(I32)


def _ffn_out_body(dest_ref, dest_next_ref, x_ref, gw_ref, yb_ref, wgu_ref, wdn_ref, g_ref, beta_ref, o_ref,
                  ybuf, sems, *, alpha):
    i = pl.program_id(0)
    n_steps = pl.num_programs(0)
    tm = x_ref.shape[0]
    slot = i % 2

    def gather(dests, to_slot):
        def issue(t, carry):
            for k in range(EXPERT_TOPK):
                pltpu.make_async_copy(yb_ref.at[pl.ds(dests[k, t], 1)], ybuf.at[to_slot, k, pl.ds(t, 1)],
                                      sems.at[to_slot]).start()
            return carry
        lax.fori_loop(0, tm, issue, 0)

    @pl.when(i == 0)
    def _():
        gather(dest_ref, 0)

    @pl.when(i + 1 < n_steps)
    def _():
        gather(dest_next_ref, 1 - slot)

    x = x_ref[...]
    h = jnp.dot(x.astype(BF16), wgu_ref[...], preferred_element_type=F32)
    ff = h.shape[1] // 2
    act = jax.nn.silu(h[:, :ff]) * h[:, ff:]
    shared = jnp.dot(act.astype(BF16), wdn_ref[...], preferred_element_type=F32)

    for k in range(EXPERT_TOPK):
        pltpu.make_async_copy(yb_ref.at[pl.ds(0, tm)], ybuf.at[slot, k], sems.at[slot]).wait()
    routed = None
    for k in range(EXPERT_TOPK):
        term = ybuf[slot, k] * gw_ref[:, k:k + 1]
        routed = term if routed is None else routed + term
    o_ref[...] = _layer_norm(alpha * x + (routed + shared), g_ref[...], beta_ref[...])


def _ffn_out(x, yb, dest_t, gw, ws_gu, ws_dn, g, beta, alpha, tm=256):
    n, d = x.shape
    n_tiles = n // tm
    row = pl.BlockSpec((tm, d), lambda i: (i, 0))
    full = lambda arr: pl.BlockSpec(arr.shape, lambda i: (0,) * arr.ndim)
    dest_spec = lambda index: pl.BlockSpec((EXPERT_TOPK, tm), index, memory_space=pltpu.SMEM)
    return pl.pallas_call(
        functools.partial(_ffn_out_body, alpha=alpha),
        out_shape=jax.ShapeDtypeStruct((n, d), F32),
        grid=(n_tiles,),
        in_specs=[dest_spec(lambda i: (0, i)), dest_spec(lambda i: (0, jnp.minimum(i + 1, n_tiles - 1))),
                  row, pl.BlockSpec((tm, EXPERT_TOPK), lambda i: (i, 0)),
                  pl.BlockSpec(memory_space=pl.ANY),
                  full(ws_gu), full(ws_dn), full(g), full(beta)],
        out_specs=row,
        scratch_shapes=[pltpu.VMEM((2, EXPERT_TOPK, tm, d), F32), pltpu.SemaphoreType.DMA((2,))],
        compiler_params=_params(("arbitrary",)),
        name="ffn_out",
    )(dest_t, dest_t, x, gw, yb, ws_gu, ws_dn, g, beta)


def _moe(x, xw, layer, w_router_t, b_router_col, w_exp_gate_up, w_exp_down, ws_gu, ws_dn, g, beta, alpha):
    n = x.shape[0]
    n_blocks = -(-n * EXPERT_TOPK // EXPERT_BLOCK) + N_EXPERTS
    eidx_t, gw_t, rank_t, counts = _router(x, w_router_t, b_router_col)
    start_col, block_exp, n_used = _block_plan(counts, n_blocks)
    dest_t = _dest_rows(eidx_t, rank_t, start_col)
    x_rows = _scatter_rows(xw, dest_t, n_blocks * EXPERT_BLOCK)
    yb = _experts(x_rows, block_exp, n_used, w_exp_gate_up, w_exp_down, layer)
    return _ffn_out(x, yb, dest_t, gw_t.T, ws_gu, ws_dn, g, beta, alpha)


def kernel(x_prompt, x_sample, mem_prompt, cache_a_k, cache_a_v, cache_b_k, cache_b_v, cache_b_idx_k,
           cache_mem_k, cache_mem_v, w_in, rel_bias, w_mem_kv, w_branch, w_out, ln_mix_g, ln_mix_b,
           w_router, b_router, w_exp_gate_up, w_exp_down, w_sh_gate_up, w_sh_down, ln_ffn_g, ln_ffn_b):
    n_batch, seq, d = x_prompt.shape
    s_batch, s_seq, _ = x_sample.shape
    depth = w_in.shape[0]
    past = cache_b_k.shape[2]
    assert s_seq == CHUNK and seq % CHUNK == 0 and cache_a_k.shape[2] == BAND and seq >= BAND
    alpha = (2 * depth) ** 0.25
    n_p, n_s = n_batch * seq, s_batch * s_seq
    n_tot = n_p + n_s
    tm = 512
    assert seq % tm == 0 and n_p % tm == 0 and n_s % tm == 0 and tm % CHUNK == 0

    x = jnp.concatenate([x_prompt.reshape(n_p, d), x_sample.reshape(n_s, d)])

    pos = jnp.concatenate([jnp.arange(seq), jnp.tile(past + jnp.arange(s_seq), tm // s_seq)])
    tab_heads = _rope_tables(pos, LANES)
    tab_kiw = _rope_tables(pos, HEAD_DIM)
    p_tiles, seq_tiles = n_p // tm, seq // tm
    table_index = lambda i: jnp.where(i < p_tiles, i % seq_tiles, seq_tiles)

    wd = WIDTH
    c_qi, c_ki, c_wi, c_qm, c_gates = 6 * wd, 7 * wd, 7 * wd + HEAD_DIM, 7 * wd + HEAD_DIM + N_HEADS, 8 * wd + HEAD_DIM + N_HEADS
    cache_a_k = cache_a_k.reshape(depth, s_batch, BAND, wd)
    cache_a_v = cache_a_v.reshape(depth, s_batch, BAND, wd)
    cache_b_k = cache_b_k.reshape(depth, s_batch, past, wd)
    cache_b_v = cache_b_v.reshape(depth, s_batch, past, wd)
    cache_mem_k = cache_mem_k.reshape(depth, s_batch, N_MEM, wd)
    cache_mem_v = cache_mem_v.reshape(depth, s_batch, N_MEM, wd)
    mem_rows = mem_prompt.reshape(n_batch * N_MEM, d)

    outs = {k: [] for k in ("p_ak", "p_av", "p_bk", "p_bv", "p_ik", "p_mk", "p_mv",
                            "s_ak", "s_av", "s_bk", "s_bv", "s_ik")}
    for l in range(depth):
        w = w_in[l]
        w_a = w[:, :3 * wd].astype(BF16)
        w_b = w[:, 3 * wd:7 * wd].astype(BF16)
        w_c = w[:, c_qm:].astype(BF16)
        w_d = jnp.concatenate([w[:, c_ki:c_qm], jnp.zeros((d, LANES - HEAD_DIM - N_HEADS), F32)], 1).astype(BF16)

        qa, ka, va = _project(x, w_a, (wd, wd, wd), tm=tm, name="proj_a")
        qb, kb, vb, qi = _project(x, w_b, (wd, wd, wd, wd), (True, True, False, True), tab_heads, table_index,
                                  tm=tm, name="proj_b")
        qm, gates = _project(x, w_c, (wd, 3 * d), tm=tm, name="proj_c")
        (kiw,) = _project(x, w_d, (LANES,), (True,), tab_kiw, table_index, tm=tm, name="proj_d")
        mk, mv = _project(mem_rows, w_mem_kv[l].astype(BF16), (wd, wd), tm=N_MEM, name="proj_mem")

        bias = _band_bias(rel_bias[l])
        a = (_band_prompt(qa, ka, va, bias, n_batch, seq),
             _band_sample(qa, ka, va, cache_a_k, cache_a_v, l, bias, s_batch, n_p))
        b = (_dsa_prompt(qb, kb, vb, qi, kiw, n_batch, seq),
             _dsa_sample(qb, kb, vb, qi, kiw, cache_b_k, cache_b_v, cache_b_idx_k, l, s_batch, n_p))
        m = (_mem_attn(qm, mk, mv, lambda bb: (bb, 0), n_batch, seq, 0, tm),
             _mem_attn(qm, cache_mem_k, cache_mem_v, lambda bb, l=l: (l, bb, 0, 0), s_batch, s_seq, n_p, s_seq))

        x1, x1w = _merge(x, (a, b, m), gates, w_branch[l].astype(BF16), w_out[l].astype(BF16),
                         ln_mix_g[l][None], ln_mix_b[l][None], alpha)
        x = _moe(x1, x1w, l, w_router[l].T.astype(BF16), b_router[l][:, None], w_exp_gate_up, w_exp_down,
                 w_sh_gate_up[l].astype(BF16), w_sh_down[l].astype(BF16),
                 ln_ffn_g[l][None], ln_ffn_b[l][None], alpha)

        ka_p = ka[:n_p].reshape(n_batch, seq, wd)
        va_p = va[:n_p].reshape(n_batch, seq, wd)
        outs["p_ak"].append(ka_p[:, -BAND:])
        outs["p_av"].append(va_p[:, -BAND:])
        outs["p_bk"].append(kb[:n_p].reshape(n_batch, seq, wd))
        outs["p_bv"].append(vb[:n_p].reshape(n_batch, seq, wd))
        outs["p_ik"].append(kiw[:n_p, :HEAD_DIM].reshape(n_batch, seq, HEAD_DIM))
        outs["p_mk"].append(mk.reshape(n_batch, N_MEM, wd))
        outs["p_mv"].append(mv.reshape(n_batch, N_MEM, wd))
        ka_s = ka[n_p:].reshape(s_batch, s_seq, wd)
        va_s = va[n_p:].reshape(s_batch, s_seq, wd)
        outs["s_ak"].append(jnp.concatenate([cache_a_k[l], ka_s], 1)[:, -BAND:])
        outs["s_av"].append(jnp.concatenate([cache_a_v[l], va_s], 1)[:, -BAND:])
        outs["s_bk"].append(kb[n_p:].reshape(s_batch, s_seq, wd))
        outs["s_bv"].append(vb[n_p:].reshape(s_batch, s_seq, wd))
        outs["s_ik"].append(kiw[n_p:, :HEAD_DIM].reshape(s_batch, s_seq, HEAD_DIM))

    heads = lambda t: t.reshape(t.shape[:-1] + (N_HEADS, HEAD_DIM))
    mheads = lambda t: t.reshape(t.shape[:-1] + (M_HEADS, M_HEAD_DIM))
    st = lambda key: jnp.stack(outs[key])
    return (x[:n_p].reshape(n_batch, seq, d), x[n_p:].reshape(s_batch, s_seq, d),
            heads(st("p_ak")), heads(st("p_av")), heads(st("p_bk")), heads(st("p_bv")), st("p_ik"),
            mheads(st("p_mk")), mheads(st("p_mv")),
            heads(st("s_ak")), heads(st("s_av")), heads(st("s_bk")), heads(st("s_bv")), st("s_ik"))
```

```python
import functools

import jax
import jax.numpy as jnp
from jax import lax
from jax.experimental import pallas as pl
from jax.experimental.pallas import tpu as pltpu

F32 = jnp.float32
BF16 = jnp.bfloat16
I32 = jnp.int32

CHUNK = 64
HEAD_DIM = 64
N_HEADS = 8
WIDTH = N_HEADS * HEAD_DIM
BAND = 8 * CHUNK
MAX_REL = 128
TOPK_KEYS = 256
M_HEADS = 4
M_HEAD_DIM = 128
N_MEM = 256
ROPE_THETA = 10000.0
N_EXPERTS = 256
EXPERT_TOPK = 8
N_GROUPS = 8
TOPK_GROUPS = 4
EXPERT_FF = 256
ROUTE_SCALE = 2.5
EXPERT_BLOCK = 256
LN_EPS = 1e-5

LANES = 128
NEG_BIG = -1e30
INT_MIN = -(2 ** 31)
VMEM_LIMIT = 48 * 1024 * 1024

_NT_DIMS = (((1,), (1,)), ((), ()))


def _params(semantics):
    return pltpu.CompilerParams(dimension_semantics=semantics, vmem_limit_bytes=VMEM_LIMIT)


def _proj_body(*refs, widths, rope_flags):
    has_rope = any(rope_flags)
    x_ref, w_ref = refs[0], refs[1]
    if has_rope:
        cos_ref, sin_lo_ref, sin_hi_ref = refs[2:5]
        outs = refs[5:]
    else:
        outs = refs[2:]
    xb = x_ref[...].astype(BF16)
    off = 0
    for g, width in enumerate(widths):
        y = jnp.dot(xb, w_ref[:, off:off + width], preferred_element_type=F32)
        if rope_flags[g]:
            cos, sin_lo, sin_hi = cos_ref[...], sin_lo_ref[...], sin_hi_ref[...]
            for k in range(width // LANES):
                yk = y[:, k * LANES:(k + 1) * LANES]
                outs[g][:, k * LANES:(k + 1) * LANES] = (
                    yk * cos + pltpu.roll(yk, LANES - 32, 1) * sin_lo + pltpu.roll(yk, 32, 1) * sin_hi)
        else:
            outs[g][...] = y
        off += width


def _project(x, w, widths, rope_flags=None, tables=None, table_index=None, tm=512, name="proj"):
    n, d = x.shape
    rope_flags = rope_flags or (False,) * len(widths)
    assert n % tm == 0 and w.shape == (d, sum(widths))
    in_specs = [pl.BlockSpec((tm, d), lambda i: (i, 0)),
                pl.BlockSpec(w.shape, lambda i: (0, 0))]
    args = [x, w]
    if any(rope_flags):
        for t in tables:
            in_specs.append(pl.BlockSpec((tm, LANES), lambda i: (table_index(i), 0)))
            args.append(t)
    return pl.pallas_call(
        functools.partial(_proj_body, widths=tuple(widths), rope_flags=tuple(rope_flags)),
        out_shape=[jax.ShapeDtypeStruct((n, wd), F32) for wd in widths],
        grid=(n // tm,),
        in_specs=in_specs,
        out_specs=[pl.BlockSpec((tm, wd), lambda i: (i, 0)) for wd in widths],
        compiler_params=_params(("parallel",)),
        name=name,
    )(*args)


def _rope_tables(pos, rope_lanes):
    half = HEAD_DIM // 2
    inv_freq = ROPE_THETA ** (-jnp.arange(half, dtype=F32) / half)
    ang = pos.astype(F32)[:, None] * inv_freq[None, :]
    cos, sin = jnp.cos(ang), jnp.sin(ang)
    reps = LANES // HEAD_DIM
    zero = jnp.zeros_like(sin)
    cos_t = jnp.tile(jnp.concatenate([cos, cos], -1), (1, reps))
    sin_lo = jnp.tile(jnp.concatenate([-sin, zero], -1), (1, reps))
    sin_hi = jnp.tile(jnp.concatenate([zero, sin], -1), (1, reps))
    keep = (jnp.arange(LANES) < rope_lanes)[None, :]
    return (jnp.where(keep, cos_t, 1.0), jnp.where(keep, sin_lo, 0.0), jnp.where(keep, sin_hi, 0.0))


def _head_mask(rows, head_in_pair):
    lane = lax.broadcasted_iota(I32, (rows, LANES), 1)
    lo = head_in_pair * HEAD_DIM
    return (lane >= lo) & (lane < lo + HEAD_DIM)


def _pair_merge(o_even, o_odd):
    lane = lax.broadcasted_iota(I32, o_even.shape, 1)
    return jnp.where(lane < HEAD_DIM, o_even, o_odd)


def _band_sample_body(q_ref, kn_ref, vn_ref, kh_ref, vh_ref, bias_ref, o_ref, kwin, vwin):
    kwin[0:BAND, :] = kh_ref[...].astype(BF16)
    vwin[0:BAND, :] = vh_ref[...].astype(BF16)
    kwin[BAND:, :] = kn_ref[...].astype(BF16)
    vwin[BAND:, :] = vn_ref[...].astype(BF16)
    q = q_ref[...] * (HEAD_DIM ** -0.5)
    for hp in range(N_HEADS // 2):
        cols = slice(hp * LANES, (hp + 1) * LANES)
        outs = []
        for hh in range(2):
            qm = jnp.where(_head_mask(CHUNK, hh), q[:, cols], 0.0).astype(BF16)
            s = lax.dot_general(qm, kwin[:, cols], _NT_DIMS, preferred_element_type=F32) + bias_ref[2 * hp + hh]
            p = jnp.exp(s - jnp.max(s, axis=1, keepdims=True))
            denom = jnp.sum(p, axis=1, keepdims=True)
            outs.append(jnp.dot(p.astype(BF16), vwin[:, cols], preferred_element_type=F32) / denom)
        o_ref[:, cols] = _pair_merge(outs[0], outs[1])


def _band_prompt_body(q_ref, k_ref, v_ref, bias_ref, o_ref, k_even, k_odd, v_t, q_t, s_sc, p_sc, o_t):
    t = pl.program_id(1)
    nq = q_ref.shape[0]
    win = BAND + nq
    seq = k_ref.shape[0]
    stage_rows = 256

    @pl.when(t == 0)
    def _():
        k_even[0:BAND, :] = jnp.zeros((BAND, WIDTH), BF16)
        k_odd[0:BAND, :] = jnp.zeros((BAND, WIDTH), BF16)
        v_t[:, 0:BAND] = jnp.zeros((WIDTH, BAND), BF16)
        for r in range(0, seq, stage_rows):
            ke, ko = _split_pair_lanes(k_ref[r:r + stage_rows, :])
            k_even[BAND + r:BAND + r + stage_rows, :] = ke.astype(BF16)
            k_odd[BAND + r:BAND + r + stage_rows, :] = ko.astype(BF16)
            v_t[:, BAND + r:BAND + r + stage_rows] = v_ref[r:r + stage_rows, :].T.astype(BF16)

    start = pl.multiple_of(t * nq, nq)
    window = pl.ds(start, win)
    q_t[...] = (q_ref[...] * (HEAD_DIM ** -0.5)).T.astype(BF16)
    before_start = jnp.where(lax.broadcasted_iota(I32, (win, nq), 0) >= BAND - t * nq, 0.0, NEG_BIG)
    for h in range(N_HEADS):
        pair = slice((h // 2) * LANES, (h // 2 + 1) * LANES)
        k_ref_h = k_even if h % 2 == 0 else k_odd
        s_sc[h * win:(h + 1) * win, :] = (jnp.dot(k_ref_h[window, pair], q_t[pair, :], preferred_element_type=F32)
                                          + bias_ref[h] + before_start)
    for h in range(N_HEADS):
        s = s_sc[h * win:(h + 1) * win, :]
        p = jnp.exp(s - jnp.max(s, axis=0, keepdims=True))
        p_sc[h * win:(h + 1) * win, :] = (p / jnp.sum(p, axis=0, keepdims=True)).astype(BF16)
    for h in range(N_HEADS):
        dims = slice(h * HEAD_DIM, (h + 1) * HEAD_DIM)
        o_t[dims, :] = jnp.dot(v_t[dims, window], p_sc[h * win:(h + 1) * win, :], preferred_element_type=F32)
    o_ref[...] = o_t[...].T


def _band_bias(table, n_rows):
    win = BAND + n_rows
    assert n_rows <= MAX_REL <= BAND
    n_heads = table.shape[0]
    g = jnp.concatenate([jnp.broadcast_to(table[:, 2 * MAX_REL:], (n_heads, win - MAX_REL)),
                         table[:, MAX_REL - n_rows:2 * MAX_REL][:, ::-1],
                         jnp.zeros((n_heads, 1), table.dtype)], axis=1).astype(F32)
    period = g.shape[1]
    skew = jnp.tile(g, (1, n_rows))[:, :n_rows * (period - 1)].reshape(n_heads, n_rows, period - 1)
    return skew[:, :, n_rows - 1:n_rows - 1 + win]


def _band_prompt(q, k, v, table, n_batch, seq, q_rows=128):
    nt = seq // q_rows
    win = BAND + q_rows
    assert seq % q_rows == 0 and q_rows % CHUNK == 0
    key_chunk = jnp.arange(win)[:, None] // CHUNK - BAND // CHUNK
    query_chunk = jnp.arange(q_rows)[None, :] // CHUNK
    visible = (key_chunk <= query_chunk) & (key_chunk >= query_chunk - BAND // CHUNK)
    bias = jnp.where(visible[None], jnp.swapaxes(_band_bias(table, q_rows), 1, 2), NEG_BIG)
    tile_spec = pl.BlockSpec((q_rows, WIDTH), lambda b, t: (b * nt + t, 0))
    seq_spec = pl.BlockSpec((seq, WIDTH), lambda b, t: (b, 0))
    return pl.pallas_call(
        _band_prompt_body,
        out_shape=jax.ShapeDtypeStruct((n_batch * seq, WIDTH), F32),
        grid=(n_batch, nt),
        in_specs=[tile_spec, seq_spec, seq_spec, pl.BlockSpec(bias.shape, lambda b, t: (0, 0, 0))],
        out_specs=tile_spec,
        scratch_shapes=[pltpu.VMEM((BAND + seq, WIDTH), BF16), pltpu.VMEM((BAND + seq, WIDTH), BF16),
                        pltpu.VMEM((WIDTH, BAND + seq), BF16), pltpu.VMEM((WIDTH, q_rows), BF16),
                        pltpu.VMEM((N_HEADS * win, q_rows), F32), pltpu.VMEM((N_HEADS * win, q_rows), BF16),
                        pltpu.VMEM((WIDTH, q_rows), F32)],
        compiler_params=_params(("parallel", "arbitrary")),
        name="band_prompt",
    )(q, k, v, bias)


def _band_sample(q, k, v, k_hist, v_hist, layer, bias, n_batch, row0):
    blk0 = row0 // CHUNK
    hist_spec = pl.BlockSpec((None, None, BAND, WIDTH), lambda b: (layer, b, 0, 0))
    new_spec = pl.BlockSpec((CHUNK, WIDTH), lambda b: (blk0 + b, 0))
    return pl.pallas_call(
        _band_sample_body,
        out_shape=jax.ShapeDtypeStruct((n_batch * CHUNK, WIDTH), F32),
        grid=(n_batch,),
        in_specs=[new_spec, new_spec, new_spec, hist_spec, hist_spec,
                  pl.BlockSpec(bias.shape, lambda b: (0, 0, 0))],
        out_specs=pl.BlockSpec((CHUNK, WIDTH), lambda b: (b, 0)),
        scratch_shapes=[pltpu.VMEM((BAND + CHUNK, WIDTH), BF16), pltpu.VMEM((BAND + CHUNK, WIDTH), BF16)],
        compiler_params=_params(("parallel",)),
        name="band_sample",
    )(q, k, v, k_hist, v_hist, bias)


def _mem_body(q_ref, k_ref, v_ref, o_ref):
    scale = M_HEAD_DIM ** -0.5
    for h in range(M_HEADS):
        cols = slice(h * M_HEAD_DIM, (h + 1) * M_HEAD_DIM)
        s = lax.dot_general(q_ref[:, cols].astype(BF16), k_ref[:, cols].astype(BF16), _NT_DIMS,
                            preferred_element_type=F32) * scale
        p = jnp.exp(s - jnp.max(s, axis=1, keepdims=True))
        denom = jnp.sum(p, axis=1, keepdims=True)
        o = jnp.dot(p.astype(BF16), v_ref[:, cols].astype(BF16), preferred_element_type=F32)
        o_ref[:, cols] = o / denom


def _mem_attn(q, mk, mv, kv_index, n_batch, rows_per_batch, row0, tq):
    width = M_HEADS * M_HEAD_DIM
    per = rows_per_batch // tq
    blk0 = row0 // tq
    kv_block = (None,) * (mk.ndim - 2) + (N_MEM, width)
    kv_spec = pl.BlockSpec(kv_block, lambda b, t: kv_index(b))
    return pl.pallas_call(
        _mem_body,
        out_shape=jax.ShapeDtypeStruct((n_batch * rows_per_batch, width), F32),
        grid=(n_batch, per),
        in_specs=[pl.BlockSpec((tq, width), lambda b, t: (blk0 + b * per + t, 0)), kv_spec, kv_spec],
        out_specs=pl.BlockSpec((tq, width), lambda b, t: (b * per + t, 0)),
        compiler_params=_params(("parallel", "parallel")),
        name="mem_attn",
    )(q, mk, mv)


def _split_pair_lanes(x):
    lane = lax.broadcasted_iota(I32, x.shape, 1)
    even = (lane & (LANES - 1)) < HEAD_DIM
    return jnp.where(even, x, 0.0), jnp.where(even, 0.0, x)


def _dsa_body(*refs, sample, key_block, n_keys, n_sel):
    n_in = 8 if sample else 6
    if sample:
        q_ref, qi_ref, kiwq_ref, kn_ref, vn_ref, kc_ref, vc_ref, kic_ref = refs[:n_in]
    else:
        q_ref, qi_ref, kiwq_ref, k_ref, v_ref, kiw_ref = refs[:n_in]
    o_ref = refs[n_in]
    (k_even, k_odd, v_t, ki_lo, ki_hi, keybuf, maskbuf, q_t, qi_t, kiwq_t, m_sc, l_sc, a_sc, acc_t,
     s_sc, p_sc) = refs[n_in + 1:]
    kb = key_block
    n_pad = k_even.shape[0]
    nq = q_ref.shape[0]
    stage_rows = 256

    def stage_kv(row0, k_new, v_new):
        n_new = k_new.shape[0]
        ke, ko = _split_pair_lanes(k_new)
        k_even[row0:row0 + n_new, :] = ke.astype(BF16)
        k_odd[row0:row0 + n_new, :] = ko.astype(BF16)
        v_t[:, row0:row0 + n_new] = v_new.T.astype(BF16)

    def stage_index_keys(row0, kiw):
        lane = lax.broadcasted_iota(I32, kiw.shape, 1)
        ki_lo[row0:row0 + kiw.shape[0], :] = jnp.where(lane < HEAD_DIM, kiw, 0.0).astype(BF16)
        ki_hi[row0:row0 + kiw.shape[0], :] = jnp.where(lane < HEAD_DIM, 0.0, pltpu.roll(kiw, HEAD_DIM, 1)).astype(BF16)

    def stage_keys():
        if sample:
            n_hist = kc_ref.shape[0]
            for r in range(0, n_hist, stage_rows):
                stage_kv(r, kc_ref[r:r + stage_rows, :], vc_ref[r:r + stage_rows, :])
            kic = kic_ref[...]
            zeros = jnp.zeros_like(kic)
            ki_lo[0:n_hist, :] = jnp.concatenate([kic, zeros], axis=1).astype(BF16)
            ki_hi[0:n_hist, :] = jnp.concatenate([zeros, kic], axis=1).astype(BF16)
            stage_kv(n_hist, kn_ref[...], vn_ref[...])
            stage_index_keys(n_hist, kiwq_ref[...])
            if n_pad > n_keys:
                for ref in (k_even, k_odd, ki_lo, ki_hi):
                    ref[n_keys:, :] = jnp.zeros((n_pad - n_keys, ref.shape[1]), BF16)
                v_t[:, n_keys:] = jnp.zeros((v_t.shape[0], n_pad - n_keys), BF16)
        else:
            for r in range(0, n_keys, stage_rows):
                stage_kv(r, k_ref[r:r + stage_rows, :], v_ref[r:r + stage_rows, :])
            stage_index_keys(0, kiw_ref[...])

    if sample:
        stage_keys()
        limit = n_keys
        nb = n_pad // kb
        search_bits = 32
    else:
        t = pl.program_id(1)
        pl.when(t == 0)(stage_keys)
        chunk = t * (nq // CHUNK) + lax.broadcasted_iota(I32, (1, nq), 1) // CHUNK
        limit = (chunk + 1) * CHUNK
        limit_max = (t + 1) * nq
        nb = (limit_max + kb - 1) // kb
        search_bits = jnp.where(limit_max <= n_sel, 0, 32)

    def block_rows(j):
        return pl.ds(pl.multiple_of(j * kb, kb), kb)

    def key_pos(j):
        return j * kb + lax.broadcasted_iota(I32, (kb, nq), 0)

    q_t[...] = (q_ref[...] * (HEAD_DIM ** -0.5)).T.astype(BF16)
    qi_t[...] = qi_ref[...].T.astype(BF16)
    kiwq_t[...] = kiwq_ref[...].T

    def score_block(j, carry):
        acc = jnp.zeros((kb, nq), F32)
        for h in range(N_HEADS):
            pair = slice((h // 2) * LANES, (h // 2 + 1) * LANES)
            ki_ref = ki_lo if h % 2 == 0 else ki_hi
            d = jnp.dot(ki_ref[block_rows(j), :], qi_t[pair, :], preferred_element_type=F32)
            acc = acc + kiwq_t[HEAD_DIM + h:HEAD_DIM + h + 1, :] * jnp.maximum(d, 0.0)
        acc = acc + 0.0
        acc = jnp.where(key_pos(j) < limit, acc, -jnp.inf)
        bits = lax.bitcast_convert_type(acc, I32)
        keybuf[block_rows(j), :] = bits ^ ((bits >> 31) & 0x7FFFFFFF)
        return carry

    lax.fori_loop(0, nb, score_block, 0)

    def count(pred_fn):
        def body(j, part):
            hit = jnp.where(pred_fn(keybuf[block_rows(j), :], j), 1.0, 0.0)
            return part + jnp.sum(hit.reshape(kb // 8, 8, nq), axis=0)
        part = lax.fori_loop(0, nb, body, jnp.zeros((8, nq), F32))
        return jnp.sum(part, axis=0, keepdims=True)

    def bit_step(i, prefix):
        cand = prefix | lax.shift_left(jnp.int32(1), 31 - i)
        thr_i = cand ^ INT_MIN
        cnt = count(lambda kblk, j: kblk >= thr_i)
        return jnp.where(cnt >= n_sel, cand, prefix)

    prefix = lax.fori_loop(0, search_bits, bit_step, jnp.zeros((1, nq), I32))
    thr = prefix ^ INT_MIN
    above = count(lambda kblk, j: kblk > thr)
    at_least = count(lambda kblk, j: kblk >= thr)
    need = n_sel - above
    surplus = at_least > n_sel

    pos_bits = max(1, (n_pad - 1).bit_length() + 1)
    any_surplus = jnp.max(jnp.where(surplus, 1.0, 0.0)) > 0.0

    def pos_step(i, bound):
        cand = bound | lax.shift_left(jnp.int32(1), pos_bits - 1 - i)
        cnt = count(lambda kblk, j: (kblk == thr) & (key_pos(j) < cand))
        return jnp.where(cnt <= need, cand, bound)

    bound = lax.fori_loop(0, jnp.where(any_surplus, pos_bits, 0), pos_step, jnp.zeros((1, nq), I32))
    bound = jnp.where(surplus, bound, 2 ** pos_bits - 1)

    def mask_block(j, carry):
        kblk = keybuf[block_rows(j), :]
        pos = key_pos(j)
        sel = ((kblk > thr) | ((kblk == thr) & (pos < bound))) & (pos < limit)
        maskbuf[block_rows(j), :] = jnp.where(sel, 0.0, NEG_BIG)
        return carry

    lax.fori_loop(0, nb, mask_block, 0)

    m_sc[...] = jnp.full(m_sc.shape, NEG_BIG, F32)
    l_sc[...] = jnp.zeros(l_sc.shape, F32)
    acc_t[...] = jnp.zeros(acc_t.shape, F32)

    def attend(j, carry):
        mask = maskbuf[block_rows(j), :]
        for h in range(N_HEADS):
            pair = slice((h // 2) * LANES, (h // 2 + 1) * LANES)
            k_ref_h = k_even if h % 2 == 0 else k_odd
            s_sc[h * kb:(h + 1) * kb, :] = jnp.dot(k_ref_h[block_rows(j), pair], q_t[pair, :],
                                                   preferred_element_type=F32) + mask
        for h in range(N_HEADS):
            stat = slice(h * 8, (h + 1) * 8)
            s = s_sc[h * kb:(h + 1) * kb, :]
            m_prev = m_sc[stat, :]
            m_new = jnp.maximum(m_prev, jnp.max(s, axis=0, keepdims=True))
            alpha = jnp.exp(m_prev - m_new)
            p = jnp.exp(s - m_new[0:1, :])
            l_sc[stat, :] = alpha * l_sc[stat, :] + jnp.sum(p, axis=0, keepdims=True)
            m_sc[stat, :] = m_new
            a_sc[stat, :] = alpha
            p_sc[h * kb:(h + 1) * kb, :] = p.astype(BF16)
        for h in range(N_HEADS):
            dims = slice(h * HEAD_DIM, (h + 1) * HEAD_DIM)
            pv = jnp.dot(v_t[dims, block_rows(j)], p_sc[h * kb:(h + 1) * kb, :], preferred_element_type=F32)
            acc_t[dims, :] = a_sc[h * 8:h * 8 + 1, :] * acc_t[dims, :] + pv
        return carry

    lax.fori_loop(0, nb, attend, 0)
    for h in range(N_HEADS):
        dims = slice(h * HEAD_DIM, (h + 1) * HEAD_DIM)
        acc_t[dims, :] = acc_t[dims, :] / l_sc[h * 8:h * 8 + 1, :]
    o_ref[...] = acc_t[...].T


def _dsa_scratch(nq, n_pad, key_block):
    return [pltpu.VMEM((n_pad, WIDTH), BF16), pltpu.VMEM((n_pad, WIDTH), BF16),
            pltpu.VMEM((WIDTH, n_pad), BF16),
            pltpu.VMEM((n_pad, LANES), BF16), pltpu.VMEM((n_pad, LANES), BF16),
            pltpu.VMEM((n_pad, nq), I32), pltpu.VMEM((n_pad, nq), F32),
            pltpu.VMEM((WIDTH, nq), BF16), pltpu.VMEM((WIDTH, nq), BF16),
            pltpu.VMEM((LANES, nq), F32),
            pltpu.VMEM((N_HEADS * 8, nq), F32), pltpu.VMEM((N_HEADS * 8, nq), F32),
            pltpu.VMEM((N_HEADS * 8, nq), F32),
            pltpu.VMEM((WIDTH, nq), F32),
            pltpu.VMEM((N_HEADS * key_block, nq), F32),
            pltpu.VMEM((N_HEADS * key_block, nq), BF16)]


def _dsa_prompt(q, k, v, qi, kiw, n_batch, seq, key_block=256, q_rows=256):
    nt = seq // q_rows
    key_block = min(key_block, seq)
    assert seq % key_block == 0 and seq % q_rows == 0 and q_rows % CHUNK == 0
    n_sel = min(TOPK_KEYS, seq // 4)
    tile_spec = lambda width: pl.BlockSpec((q_rows, width), lambda b, t: (b * nt + t, 0))
    seq_spec = lambda width: pl.BlockSpec((seq, width), lambda b, t: (b, 0))
    return pl.pallas_call(
        functools.partial(_dsa_body, sample=False, key_block=key_block, n_keys=seq, n_sel=n_sel),
        out_shape=jax.ShapeDtypeStruct((n_batch * seq, WIDTH), F32),
        grid=(n_batch, nt),
        in_specs=[tile_spec(WIDTH), tile_spec(WIDTH), tile_spec(LANES),
                  seq_spec(WIDTH), seq_spec(WIDTH), seq_spec(LANES)],
        out_specs=tile_spec(WIDTH),
        scratch_shapes=_dsa_scratch(q_rows, seq, key_block),
        compiler_params=_params(("parallel", "arbitrary")),
        name="dsa_prompt",
    )(q, qi, kiw, k, v, kiw)


def _dsa_sample(q, k, v, qi, kiw, k_cache, v_cache, ki_cache, layer, n_batch, row0, key_block=256):
    past = k_cache.shape[2]
    n_keys = past + CHUNK
    n_pad = -(-n_keys // key_block) * key_block
    n_sel = min(TOPK_KEYS, n_keys // 4)
    blk0 = row0 // CHUNK
    new_spec = lambda width: pl.BlockSpec((CHUNK, width), lambda b: (blk0 + b, 0))
    cache_spec = lambda width: pl.BlockSpec((None, None, past, width), lambda b: (layer, b, 0, 0))
    return pl.pallas_call(
        functools.partial(_dsa_body, sample=True, key_block=key_block, n_keys=n_keys, n_sel=n_sel),
        out_shape=jax.ShapeDtypeStruct((n_batch * CHUNK, WIDTH), F32),
        grid=(n_batch,),
        in_specs=[new_spec(WIDTH), new_spec(WIDTH), new_spec(LANES), new_spec(WIDTH), new_spec(WIDTH),
                  cache_spec(WIDTH), cache_spec(WIDTH), cache_spec(HEAD_DIM)],
        out_specs=pl.BlockSpec((CHUNK, WIDTH), lambda b: (b, 0)),
        scratch_shapes=_dsa_scratch(CHUNK, n_pad, key_block),
        compiler_params=_params(("parallel",)),
        name="dsa_sample",
    )(q, qi, kiw, k, v, k_cache, v_cache, ki_cache)


def _layer_norm(z, g, b):
    mu = jnp.mean(z, axis=-1, keepdims=True)
    zc = z - mu
    var = jnp.mean(zc * zc, axis=-1, keepdims=True)
    return zc * lax.rsqrt(var + LN_EPS) * g + b


HI_HALF = -65536


def _pack_rows(x):
    half = x.shape[1] // 2
    lo = lax.bitcast_convert_type(x[:, :half].astype(BF16).astype(F32), I32)
    hi = lax.bitcast_convert_type(x[:, half:].astype(BF16).astype(F32), I32)
    return (hi & HI_HALF) | lax.shift_right_logical(lo, 16)


def _unpack_rows(w):
    lo = lax.bitcast_convert_type(lax.shift_left(w, 16), F32).astype(BF16)
    hi = lax.bitcast_convert_type(w & HI_HALF, F32).astype(BF16)
    return lo, hi


def _merge_body(x_ref, ap_ref, as_ref, bp_ref, bs_ref, mp_ref, ms_ref, gates_ref, wbr_ref, wout_ref, g_ref, beta_ref,
                o_ref, ow_ref, *, alpha, prompt_tiles):
    d = x_ref.shape[1]
    in_prompt = pl.program_id(0) < prompt_tiles
    mix = None
    for n, (p_ref, s_ref) in enumerate(((ap_ref, as_ref), (bp_ref, bs_ref), (mp_ref, ms_ref))):
        branch = jnp.where(in_prompt, p_ref[...], s_ref[...])
        proj = jnp.dot(branch.astype(BF16), wbr_ref[n], preferred_element_type=F32)
        term = jax.nn.sigmoid(gates_ref[:, n * d:(n + 1) * d]) * proj
        mix = term if mix is None else mix + term
    y = jnp.dot(mix.astype(BF16), wout_ref[...], preferred_element_type=F32)
    out = _layer_norm(alpha * x_ref[...] + y, g_ref[...], beta_ref[...])
    o_ref[...] = out
    ow_ref[...] = _pack_rows(out)


def _merge(x, branches, gates, w_branch, w_out, g, beta, alpha, tm=256):
    n, d = x.shape
    prompt_tiles = branches[0][0].shape[0] // tm
    assert all(p.shape[0] == prompt_tiles * tm and (n - p.shape[0]) == s.shape[0] and s.shape[0] % tm == 0
               for p, s in branches)
    packed = jax.eval_shape(_pack_rows, jax.ShapeDtypeStruct((tm, d), F32))
    row = lambda width: pl.BlockSpec((tm, width), lambda i: (i, 0))
    full = lambda arr: pl.BlockSpec(arr.shape, lambda i: (0,) * arr.ndim)
    prompt_row = lambda width: pl.BlockSpec((tm, width), lambda i: (jnp.minimum(i, prompt_tiles - 1), 0))
    sample_row = lambda width: pl.BlockSpec((tm, width), lambda i: (jnp.maximum(i - prompt_tiles, 0), 0))
    branch_specs, branch_args = [], []
    for p, s in branches:
        branch_specs += [prompt_row(p.shape[1]), sample_row(s.shape[1])]
        branch_args += [p, s]
    return pl.pallas_call(
        functools.partial(_merge_body, alpha=alpha, prompt_tiles=prompt_tiles),
        out_shape=[jax.ShapeDtypeStruct((n, d), F32), jax.ShapeDtypeStruct((n, packed.shape[1]), packed.dtype)],
        grid=(n // tm,),
        in_specs=[row(d)] + branch_specs + [row(gates.shape[1]), full(w_branch), full(w_out), full(g), full(beta)],
        out_specs=[row(d), row(packed.shape[1])],
        compiler_params=_params(("parallel",)),
        name="merge",
    )(x, *branch_args, gates, w_branch, w_out, g, beta)


def _first_max(vals, idx, n):
    top = jnp.max(vals, axis=0, keepdims=True)
    arg = jnp.min(jnp.where(vals == top, idx, n), axis=0, keepdims=True)
    return top, arg


def _router_body(x_ref, wt_ref, bias_ref, eidx_ref, gw_ref, rank_ref, counts_ref, running):
    tm = x_ref.shape[0]

    @pl.when(pl.program_id(0) == 0)
    def _():
        running[...] = jnp.zeros(running.shape, F32)

    per = N_EXPERTS // N_GROUPS
    logits = lax.dot_general(wt_ref[...], x_ref[...].astype(BF16), _NT_DIMS, preferred_element_type=F32)
    s = jax.nn.sigmoid(logits)
    sb = s + bias_ref[...]
    in_group = lax.broadcasted_iota(I32, (per, tm), 0).astype(F32)
    group_scores = []
    for g in range(N_GROUPS):
        blk = sb[g * per:(g + 1) * per, :]
        top1, arg1 = _first_max(blk, in_group, per)
        top2 = jnp.max(jnp.where(in_group == arg1, -jnp.inf, blk), axis=0, keepdims=True)
        group_scores.append(top1 + top2)
    gs = jnp.concatenate(group_scores, axis=0)
    gidx = lax.broadcasted_iota(I32, (N_GROUPS, tm), 0).astype(F32)
    chosen = jnp.zeros((N_GROUPS, tm), F32)
    for _ in range(TOPK_GROUPS):
        _, arg = _first_max(gs, gidx, N_GROUPS)
        hit = gidx == arg
        chosen = jnp.where(hit, 1.0, chosen)
        gs = jnp.where(hit, -jnp.inf, gs)
    cand = jnp.concatenate(
        [jnp.where(chosen[g:g + 1, :] > 0.0, sb[g * per:(g + 1) * per, :], -jnp.inf) for g in range(N_GROUPS)],
        axis=0)
    eidx = lax.broadcasted_iota(I32, (N_EXPERTS, tm), 0).astype(F32)
    picks, weights, hits = [], [], []
    for _ in range(EXPERT_TOPK):
        _, arg = _first_max(cand, eidx, N_EXPERTS)
        hit = eidx == arg
        weights.append(jnp.sum(jnp.where(hit, s, 0.0), axis=0, keepdims=True))
        picks.append(arg)
        hits.append(hit)
        cand = jnp.where(hit, -jnp.inf, cand)
    total = weights[0]
    for w in weights[1:]:
        total = total + w
    taken = jnp.zeros((N_EXPERTS, tm), F32)
    for hit in hits:
        taken = jnp.where(hit, 1.0, taken)
    earlier = (lax.broadcasted_iota(I32, (tm, tm), 0) < lax.broadcasted_iota(I32, (tm, tm), 1))
    before = jnp.dot(taken.astype(BF16), jnp.where(earlier, 1.0, 0.0).astype(BF16),
                     preferred_element_type=F32) + running[...]
    for r in range(EXPERT_TOPK):
        eidx_ref[r:r + 1, :] = picks[r].astype(I32)
        gw_ref[r:r + 1, :] = weights[r] / total * ROUTE_SCALE
        rank_ref[r:r + 1, :] = jnp.sum(jnp.where(hits[r], before, 0.0), axis=0, keepdims=True).astype(I32)
    running[...] = running[...] + jnp.sum(taken, axis=1, keepdims=True)
    counts_ref[...] = running[...]


def _router(x, w_router_t, b_router_col, tm=256):
    n, d = x.shape
    slot = pl.BlockSpec((EXPERT_TOPK, tm), lambda i: (0, i))
    return pl.pallas_call(
        _router_body,
        out_shape=[jax.ShapeDtypeStruct((EXPERT_TOPK, n), I32), jax.ShapeDtypeStruct((EXPERT_TOPK, n), F32),
                   jax.ShapeDtypeStruct((EXPERT_TOPK, n), I32), jax.ShapeDtypeStruct((N_EXPERTS, 1), F32)],
        grid=(n // tm,),
        in_specs=[pl.BlockSpec((tm, d), lambda i: (i, 0)),
                  pl.BlockSpec(w_router_t.shape, lambda i: (0, 0)),
                  pl.BlockSpec(b_router_col.shape, lambda i: (0, 0))],
        out_specs=[slot, slot, slot, pl.BlockSpec((N_EXPERTS, 1), lambda i: (0, 0))],
        scratch_shapes=[pltpu.VMEM((N_EXPERTS, 1), F32)],
        compiler_params=_params(("arbitrary",)),
        name="router",
    )(x, w_router_t, b_router_col)


def _dest_body(eidx_ref, rank_ref, start_ref, dest_ref):
    tm = eidx_ref.shape[1]
    experts = lax.broadcasted_iota(I32, (N_EXPERTS, tm), 0)
    for r in range(EXPERT_TOPK):
        base = jnp.sum(jnp.where(experts == eidx_ref[r:r + 1, :], start_ref[...], 0.0), axis=0, keepdims=True)
        dest_ref[r:r + 1, :] = base.astype(I32) + rank_ref[r:r + 1, :]


def _dest_rows(eidx_t, rank_t, start_col, tm=512):
    n = eidx_t.shape[1]
    slot = pl.BlockSpec((EXPERT_TOPK, tm), lambda i: (0, i))
    return pl.pallas_call(
        _dest_body,
        out_shape=jax.ShapeDtypeStruct((EXPERT_TOPK, n), I32),
        grid=(n // tm,),
        in_specs=[slot, slot, pl.BlockSpec(start_col.shape, lambda i: (0, 0))],
        out_specs=slot,
        compiler_params=_params(("parallel",)),
        name="dest_rows",
    )(eidx_t, rank_t, start_col)


def _scatter_body(dest_ref, x_ref, _, o_ref, sem):
    tm = x_ref.shape[0]

    def row_copy(t, k):
        return pltpu.make_async_copy(x_ref.at[pl.ds(t, 1)], o_ref.at[pl.ds(dest_ref[k, t], 1)], sem)

    def issue(t, carry):
        for k in range(EXPERT_TOPK):
            row_copy(t, k).start(priority=k % 2)
        return carry

    lax.fori_loop(0, tm, issue, 0)
    for k in range(EXPERT_TOPK):
        pltpu.make_async_copy(x_ref, o_ref.at[pl.ds(0, tm)], sem).wait()


def _scatter_rows(xw, dest_t, rows, tm=256):
    n, width = xw.shape
    return pl.pallas_call(
        _scatter_body,
        out_shape=jax.ShapeDtypeStruct((rows, width), xw.dtype),
        grid=(n // tm,),
        in_specs=[pl.BlockSpec((EXPERT_TOPK, tm), lambda i: (0, i), memory_space=pltpu.SMEM),
                  pl.BlockSpec((tm, width), lambda i: (i, 0)),
                  pl.BlockSpec(memory_space=pl.ANY)],
        out_specs=pl.BlockSpec(memory_space=pl.ANY),
        scratch_shapes=[pltpu.SemaphoreType.DMA],
        input_output_aliases={2: 0},
        compiler_params=_params(("arbitrary",)),
        name="scatter_rows",
    )(dest_t, xw, jnp.zeros((rows, width), xw.dtype))


def _experts_body(block_exp_ref, n_used_ref, x_ref, wgu_ref, wdn_ref, o_ref, wgu_bf, wdn_bf):
    i = pl.program_id(0)
    used = i < n_used_ref[0]

    @pl.when(used)
    def _():
        prev = block_exp_ref[jnp.maximum(i - 1, 0)]

        @pl.when((i == 0) | (block_exp_ref[i] != prev))
        def _():
            wgu_bf[...] = wgu_ref[...].astype(BF16)
            wdn_bf[...] = wdn_ref[...].astype(BF16)

        x_lo, x_hi = _unpack_rows(x_ref[...])
        half = x_lo.shape[1]
        h = (jnp.dot(x_lo, wgu_bf[0:half, :], preferred_element_type=F32)
             + jnp.dot(x_hi, wgu_bf[half:, :], preferred_element_type=F32))
        ff = h.shape[1] // 2
        act = jax.nn.silu(h[:, :ff]) * h[:, ff:]
        o_ref[...] = jnp.dot(act.astype(BF16), wdn_bf[...], preferred_element_type=F32)

    @pl.when(jnp.logical_not(used))
    def _():
        o_ref[...] = jnp.zeros(o_ref.shape, F32)


def _experts(x_rows, block_exp, n_used, w_gu, w_dn, layer):
    rows, width = x_rows.shape
    n_blocks = rows // EXPERT_BLOCK
    d, ff2 = w_gu.shape[-2:]
    grid_spec = pltpu.PrefetchScalarGridSpec(
        num_scalar_prefetch=2,
        grid=(n_blocks,),
        in_specs=[pl.BlockSpec((EXPERT_BLOCK, width), lambda i, be, nu: (i, 0)),
                  pl.BlockSpec((None, None, d, ff2), lambda i, be, nu: (layer, be[i], 0, 0)),
                  pl.BlockSpec((None, None, ff2 // 2, d), lambda i, be, nu: (layer, be[i], 0, 0))],
        out_specs=pl.BlockSpec((EXPERT_BLOCK, d), lambda i, be, nu: (i, 0)),
        scratch_shapes=[pltpu.VMEM((d, ff2), BF16), pltpu.VMEM((ff2 // 2, d), BF16)],
    )
    return pl.pallas_call(
        _experts_body,
        out_shape=jax.ShapeDtypeStruct((rows, d), F32),
        grid_spec=grid_spec,
        compiler_params=_params(("arbitrary",)),
        name="experts",
    )(block_exp, n_used, x_rows, w_gu, w_dn)


def _block_plan(counts, n_blocks):
    counts = counts.reshape(-1).astype(I32)
    padded = (counts + EXPERT_BLOCK - 1) // EXPERT_BLOCK * EXPERT_BLOCK
    pad_end = jnp.cumsum(padded)
    start_col = (pad_end - padded).astype(F32)[:, None]
    end_blocks = pad_end // EXPERT_BLOCK
    block_exp = jnp.minimum(jnp.sum(end_blocks[None, :] <= jnp.arange(n_blocks)[:, None], axis=1),
                            N_EXPERTS - 1).astype(I32)
    return start_col, block_exp, end_blocks[-1:].astype(I32)


def _ffn_out_body(dest_ref, dest_next_ref, x_ref, gw_ref, yb_ref, wgu_ref, wdn_ref, g_ref, beta_ref, o_ref,
                  ybuf, sems, *, alpha):
    i = pl.program_id(0)
    n_steps = pl.num_programs(0)
    tm = x_ref.shape[0]
    slot = i % 2

    def gather(dests, to_slot):
        def issue(t, carry):
            for k in range(EXPERT_TOPK):
                pltpu.make_async_copy(yb_ref.at[pl.ds(dests[k, t], 1)], ybuf.at[to_slot, k, pl.ds(t, 1)],
                                      sems.at[to_slot]).start(priority=k % 2)
            return carry
        lax.fori_loop(0, tm, issue, 0)

    @pl.when(i == 0)
    def _():
        gather(dest_ref, 0)

    @pl.when(i + 1 < n_steps)
    def _():
        gather(dest_next_ref, 1 - slot)

    x = x_ref[...]
    h = jnp.dot(x.astype(BF16), wgu_ref[...], preferred_element_type=F32)
    ff = h.shape[1] // 2
    act = jax.nn.silu(h[:, :ff]) * h[:, ff:]
    shared = jnp.dot(act.astype(BF16), wdn_ref[...], preferred_element_type=F32)

    for k in range(EXPERT_TOPK):
        pltpu.make_async_copy(yb_ref.at[pl.ds(0, tm)], ybuf.at[slot, k], sems.at[slot]).wait()
    routed = None
    for k in range(EXPERT_TOPK):
        term = ybuf[slot, k] * gw_ref[:, k:k + 1]
        routed = term if routed is None else routed + term
    o_ref[...] = _layer_norm(alpha * x + (routed + shared), g_ref[...], beta_ref[...])


def _ffn_out(x, yb, dest_t, gw, ws_gu, ws_dn, g, beta, alpha, tm=256):
    n, d = x.shape
    n_tiles = n // tm
    row = pl.BlockSpec((tm, d), lambda i: (i, 0))
    full = lambda arr: pl.BlockSpec(arr.shape, lambda i: (0,) * arr.ndim)
    dest_spec = lambda index: pl.BlockSpec((EXPERT_TOPK, tm), index, memory_space=pltpu.SMEM)
    return pl.pallas_call(
        functools.partial(_ffn_out_body, alpha=alpha),
        out_shape=jax.ShapeDtypeStruct((n, d), F32),
        grid=(n_tiles,),
        in_specs=[dest_spec(lambda i: (0, i)), dest_spec(lambda i: (0, jnp.minimum(i + 1, n_tiles - 1))),
                  row, pl.BlockSpec((tm, EXPERT_TOPK), lambda i: (i, 0)),
                  pl.BlockSpec(memory_space=pl.ANY),
                  full(ws_gu), full(ws_dn), full(g), full(beta)],
        out_specs=row,
        scratch_shapes=[pltpu.VMEM((2, EXPERT_TOPK, tm, d), F32), pltpu.SemaphoreType.DMA((2,))],
        compiler_params=_params(("arbitrary",)),
        name="ffn_out",
    )(dest_t, dest_t, x, gw, yb, ws_gu, ws_dn, g, beta)


def _moe(x, xw, layer, w_router_t, b_router_col, w_exp_gate_up, w_exp_down, ws_gu, ws_dn, g, beta, alpha):
    n = x.shape[0]
    n_blocks = -(-n * EXPERT_TOPK // EXPERT_BLOCK) + N_EXPERTS
    eidx_t, gw_t, rank_t, counts = _router(x, w_router_t, b_router_col)
    start_col, block_exp, n_used = _block_plan(counts, n_blocks)
    dest_t = _dest_rows(eidx_t, rank_t, start_col)
    x_rows = _scatter_rows(xw, dest_t, n_blocks * EXPERT_BLOCK)
    yb = _experts(x_rows, block_exp, n_used, w_exp_gate_up, w_exp_down, layer)
    return _ffn_out(x, yb, dest_t, gw_t.T, ws_gu, ws_dn, g, beta, alpha)


def kernel(x_prompt, x_sample, mem_prompt, cache_a_k, cache_a_v, cache_b_k, cache_b_v, cache_b_idx_k,
           cache_mem_k, cache_mem_v, w_in, rel_bias, w_mem_kv, w_branch, w_out, ln_mix_g, ln_mix_b,
           w_router, b_router, w_exp_gate_up, w_exp_down, w_sh_gate_up, w_sh_down, ln_ffn_g, ln_ffn_b):
    n_batch, seq, d = x_prompt.shape
    s_batch, s_seq, _ = x_sample.shape
    depth = w_in.shape[0]
    past = cache_b_k.shape[2]
    assert s_seq == CHUNK and seq % CHUNK == 0 and cache_a_k.shape[2] == BAND and seq >= BAND
    alpha = (2 * depth) ** 0.25
    n_p, n_s = n_batch * seq, s_batch * s_seq
    n_tot = n_p + n_s
    tm = 512
    assert seq % tm == 0 and n_p % tm == 0 and n_s % tm == 0 and tm % CHUNK == 0

    x = jnp.concatenate([x_prompt.reshape(n_p, d), x_sample.reshape(n_s, d)])

    pos = jnp.concatenate([jnp.arange(seq), jnp.tile(past + jnp.arange(s_seq), tm // s_seq)])
    tab_heads = _rope_tables(pos, LANES)
    tab_kiw = _rope_tables(pos, HEAD_DIM)
    p_tiles, seq_tiles = n_p // tm, seq // tm
    table_index = lambda i: jnp.where(i < p_tiles, i % seq_tiles, seq_tiles)

    wd = WIDTH
    c_qi, c_ki, c_wi, c_qm, c_gates = 6 * wd, 7 * wd, 7 * wd + HEAD_DIM, 7 * wd + HEAD_DIM + N_HEADS, 8 * wd + HEAD_DIM + N_HEADS
    cache_a_k = cache_a_k.reshape(depth, s_batch, BAND, wd)
    cache_a_v = cache_a_v.reshape(depth, s_batch, BAND, wd)
    cache_b_k = cache_b_k.reshape(depth, s_batch, past, wd)
    cache_b_v = cache_b_v.reshape(depth, s_batch, past, wd)
    cache_mem_k = cache_mem_k.reshape(depth, s_batch, N_MEM, wd)
    cache_mem_v = cache_mem_v.reshape(depth, s_batch, N_MEM, wd)
    mem_rows = mem_prompt.reshape(n_batch * N_MEM, d)

    outs = {k: [] for k in ("p_ak", "p_av", "p_bk", "p_bv", "p_ik", "p_mk", "p_mv",
                            "s_ak", "s_av", "s_bk", "s_bv", "s_ik")}
    for l in range(depth):
        w = w_in[l]
        w_a = w[:, :3 * wd].astype(BF16)
        w_b = w[:, 3 * wd:7 * wd].astype(BF16)
        w_c = w[:, c_qm:].astype(BF16)
        w_d = jnp.concatenate([w[:, c_ki:c_qm], jnp.zeros((d, LANES - HEAD_DIM - N_HEADS), F32)], 1).astype(BF16)

        qa, ka, va = _project(x, w_a, (wd, wd, wd), tm=tm, name="proj_a")
        qb, kb, vb, qi = _project(x, w_b, (wd, wd, wd, wd), (True, True, False, True), tab_heads, table_index,
                                  tm=tm, name="proj_b")
        qm, gates = _project(x, w_c, (wd, 3 * d), tm=tm, name="proj_c")
        (kiw,) = _project(x, w_d, (LANES,), (True,), tab_kiw, table_index, tm=tm, name="proj_d")
        mk, mv = _project(mem_rows, w_mem_kv[l].astype(BF16), (wd, wd), tm=N_MEM, name="proj_mem")

        a = (_band_prompt(qa, ka, va, rel_bias[l], n_batch, seq),
             _band_sample(qa, ka, va, cache_a_k, cache_a_v, l, _band_bias(rel_bias[l], CHUNK), s_batch, n_p))
        b = (_dsa_prompt(qb, kb, vb, qi, kiw, n_batch, seq),
             _dsa_sample(qb, kb, vb, qi, kiw, cache_b_k, cache_b_v, cache_b_idx_k, l, s_batch, n_p))
        m = (_mem_attn(qm, mk, mv, lambda bb: (bb, 0), n_batch, seq, 0, tm),
             _mem_attn(qm, cache_mem_k, cache_mem_v, lambda bb, l=l: (l, bb, 0, 0), s_batch, s_seq, n_p, s_seq))

        x1, x1w = _merge(x, (a, b, m), gates, w_branch[l].astype(BF16), w_out[l].astype(BF16),
                         ln_mix_g[l][None], ln_mix_b[l][None], alpha)
        x = _moe(x1, x1w, l, w_router[l].T.astype(BF16), b_router[l][:, None], w_exp_gate_up, w_exp_down,
                 w_sh_gate_up[l].astype(BF16), w_sh_down[l].astype(BF16),
                 ln_ffn_g[l][None], ln_ffn_b[l][None], alpha)

        ka_p = ka[:n_p].reshape(n_batch, seq, wd)
        va_p = va[:n_p].reshape(n_batch, seq, wd)
        outs["p_ak"].append(ka_p[:, -BAND:])
        outs["p_av"].append(va_p[:, -BAND:])
        outs["p_bk"].append(kb[:n_p].reshape(n_batch, seq, wd))
        outs["p_bv"].append(vb[:n_p].reshape(n_batch, seq, wd))
        outs["p_ik"].append(kiw[:n_p, :HEAD_DIM].reshape(n_batch, seq, HEAD_DIM))
        outs["p_mk"].append(mk.reshape(n_batch, N_MEM, wd))
        outs["p_mv"].append(mv.reshape(n_batch, N_MEM, wd))
        ka_s = ka[n_p:].reshape(s_batch, s_seq, wd)
        va_s = va[n_p:].reshape(s_batch, s_seq, wd)
        outs["s_ak"].append(jnp.concatenate([cache_a_k[l], ka_s], 1)[:, -BAND:])
        outs["s_av"].append(jnp.concatenate([cache_a_v[l], va_s], 1)[:, -BAND:])
        outs["s_bk"].append(kb[n_p:].reshape(s_batch, s_seq, wd))
        outs["s_bv"].append(vb[n_p:].reshape(s_batch, s_seq, wd))
        outs["s_ik"].append(kiw[n_p:, :HEAD_DIM].reshape(s_batch, s_seq, HEAD_DIM))

    heads = lambda t: t.reshape(t.shape[:-1] + (N_HEADS, HEAD_DIM))
    mheads = lambda t: t.reshape(t.shape[:-1] + (M_HEADS, M_HEAD_DIM))
    st = lambda key: jnp.stack(outs[key])
    return (x[:n_p].reshape(n_batch, seq, d), x[n_p:].reshape(s_batch, s_seq, d),
            heads(st("p_ak")), heads(st("p_av")), heads(st("p_bk")), heads(st("p_bv")), st("p_ik"),
            mheads(st("p_mk")), mheads(st("p_mv")),
            heads(st("s_ak")), heads(st("s_av")), heads(st("s_bk")), heads(st("s_bv")), st("s_ik"))
```

```python
import functools

import jax
import jax.numpy as jnp
from jax import lax
from jax.experimental import pallas as pl
from jax.experimental.pallas import tpu as pltpu

F32 = jnp.float32
BF16 = jnp.bfloat16
I32 = jnp.int32

CHUNK = 64
HEAD_DIM = 64
N_HEADS = 8
WIDTH = N_HEADS * HEAD_DIM
BAND = 8 * CHUNK
MAX_REL = 128
TOPK_KEYS = 256
M_HEADS = 4
M_HEAD_DIM = 128
N_MEM = 256
ROPE_THETA = 10000.0
N_EXPERTS = 256
EXPERT_TOPK = 8
N_GROUPS = 8
TOPK_GROUPS = 4
EXPERT_FF = 256
ROUTE_SCALE = 2.5
EXPERT_BLOCK = 256
LN_EPS = 1e-5

LANES = 128
NEG_BIG = -1e30
INT_MIN = -(2 ** 31)
VMEM_LIMIT = 48 * 1024 * 1024

_NT_DIMS = (((1,), (1,)), ((), ()))


def _params(semantics):
    return pltpu.CompilerParams(dimension_semantics=semantics, vmem_limit_bytes=VMEM_LIMIT)


def _proj_body(*refs, widths, rope_flags, split_flags, prompt_tiles):
    has_rope = any(rope_flags)
    x_ref, w_ref = refs[0], refs[1]
    if has_rope:
        cos_ref, sin_lo_ref, sin_hi_ref = refs[2:5]
        outs = list(refs[5:])
    else:
        outs = list(refs[2:])
    in_prompt = pl.program_id(0) < prompt_tiles
    xb = x_ref[...].astype(BF16)
    off = 0
    for g, width in enumerate(widths):
        y = jnp.dot(xb, w_ref[:, off:off + width], preferred_element_type=F32)
        if rope_flags[g]:
            cos, sin_lo, sin_hi = cos_ref[...], sin_lo_ref[...], sin_hi_ref[...]
            chunks = [yk * cos + pltpu.roll(yk, LANES - 32, 1) * sin_lo + pltpu.roll(yk, 32, 1) * sin_hi
                      for yk in (y[:, k * LANES:(k + 1) * LANES] for k in range(width // LANES))]
        else:
            chunks = [y]

        def store(ref, chunks=chunks):
            step = chunks[0].shape[1]
            for k, chunk in enumerate(chunks):
                ref[:, k * step:(k + 1) * step] = chunk

        if split_flags[g]:
            prompt_ref, sample_ref = outs.pop(0), outs.pop(0)
            pl.when(in_prompt)(functools.partial(store, prompt_ref))
            pl.when(jnp.logical_not(in_prompt))(functools.partial(store, sample_ref))
        else:
            store(outs.pop(0))
        off += width


def _project(x, w, widths, rope_flags=None, tables=None, table_index=None, split_flags=None, n_prompt=None,
             tm=512, name="proj"):
    n, d = x.shape
    rope_flags = rope_flags or (False,) * len(widths)
    split_flags = split_flags or (False,) * len(widths)
    assert n % tm == 0 and w.shape == (d, sum(widths))
    prompt_tiles = n // tm
    if any(split_flags):
        assert n_prompt % tm == 0 and 0 < n_prompt < n
        prompt_tiles = n_prompt // tm
    in_specs = [pl.BlockSpec((tm, d), lambda i: (i, 0)),
                pl.BlockSpec(w.shape, lambda i: (0, 0))]
    args = [x, w]
    if any(rope_flags):
        for t in tables:
            in_specs.append(pl.BlockSpec((tm, LANES), lambda i: (table_index(i), 0)))
            args.append(t)
    out_shape, out_specs = [], []
    for wd, split in zip(widths, split_flags):
        if split:
            out_shape += [jax.ShapeDtypeStruct((n_prompt, wd), F32), jax.ShapeDtypeStruct((n - n_prompt, wd), F32)]
            out_specs += [pl.BlockSpec((tm, wd), lambda i: (jnp.minimum(i, prompt_tiles - 1), 0)),
                          pl.BlockSpec((tm, wd), lambda i: (jnp.maximum(i - prompt_tiles, 0), 0))]
        else:
            out_shape.append(jax.ShapeDtypeStruct((n, wd), F32))
            out_specs.append(pl.BlockSpec((tm, wd), lambda i: (i, 0)))
    flat = pl.pallas_call(
        functools.partial(_proj_body, widths=tuple(widths), rope_flags=tuple(rope_flags),
                          split_flags=tuple(split_flags), prompt_tiles=prompt_tiles),
        out_shape=out_shape,
        grid=(n // tm,),
        in_specs=in_specs,
        out_specs=out_specs,
        compiler_params=_params(("arbitrary",)),
        name=name,
    )(*args)
    flat = list(flat)
    return [(flat.pop(0), flat.pop(0)) if split else flat.pop(0) for split in split_flags]


def _rope_tables(pos, rope_lanes):
    half = HEAD_DIM // 2
    inv_freq = ROPE_THETA ** (-jnp.arange(half, dtype=F32) / half)
    ang = pos.astype(F32)[:, None] * inv_freq[None, :]
    cos, sin = jnp.cos(ang), jnp.sin(ang)
    reps = LANES // HEAD_DIM
    zero = jnp.zeros_like(sin)
    cos_t = jnp.tile(jnp.concatenate([cos, cos], -1), (1, reps))
    sin_lo = jnp.tile(jnp.concatenate([-sin, zero], -1), (1, reps))
    sin_hi = jnp.tile(jnp.concatenate([zero, sin], -1), (1, reps))
    keep = (jnp.arange(LANES) < rope_lanes)[None, :]
    return (jnp.where(keep, cos_t, 1.0), jnp.where(keep, sin_lo, 0.0), jnp.where(keep, sin_hi, 0.0))


def _head_mask(rows, head_in_pair):
    lane = lax.broadcasted_iota(I32, (rows, LANES), 1)
    lo = head_in_pair * HEAD_DIM
    return (lane >= lo) & (lane < lo + HEAD_DIM)


def _pair_merge(o_even, o_odd):
    lane = lax.broadcasted_iota(I32, o_even.shape, 1)
    return jnp.where(lane < HEAD_DIM, o_even, o_odd)


def _band_sample_body(q_ref, kn_ref, vn_ref, kh_ref, vh_ref, bias_ref, o_ref, kwin, vwin):
    kwin[0:BAND, :] = kh_ref[...].astype(BF16)
    vwin[0:BAND, :] = vh_ref[...].astype(BF16)
    kwin[BAND:, :] = kn_ref[...].astype(BF16)
    vwin[BAND:, :] = vn_ref[...].astype(BF16)
    q = q_ref[...] * (HEAD_DIM ** -0.5)
    for hp in range(N_HEADS // 2):
        cols = slice(hp * LANES, (hp + 1) * LANES)
        outs = []
        for hh in range(2):
            qm = jnp.where(_head_mask(CHUNK, hh), q[:, cols], 0.0).astype(BF16)
            s = lax.dot_general(qm, kwin[:, cols], _NT_DIMS, preferred_element_type=F32) + bias_ref[2 * hp + hh]
            p = jnp.exp(s - jnp.max(s, axis=1, keepdims=True))
            denom = jnp.sum(p, axis=1, keepdims=True)
            outs.append(jnp.dot(p.astype(BF16), vwin[:, cols], preferred_element_type=F32) / denom)
        o_ref[:, cols] = _pair_merge(outs[0], outs[1])


def _band_prompt_body(q_ref, k_ref, v_ref, bias_ref, o_ref, k_even, k_odd, v_t, q_t, s_sc, p_sc, o_t):
    t = pl.program_id(1)
    nq = q_ref.shape[0]
    win = BAND + nq
    seq = k_ref.shape[0]
    stage_rows = 256

    @pl.when(t == 0)
    def _():
        k_even[0:BAND, :] = jnp.zeros((BAND, WIDTH), BF16)
        k_odd[0:BAND, :] = jnp.zeros((BAND, WIDTH), BF16)
        v_t[:, 0:BAND] = jnp.zeros((WIDTH, BAND), BF16)
        for r in range(0, seq, stage_rows):
            ke, ko = _split_pair_lanes(k_ref[r:r + stage_rows, :])
            k_even[BAND + r:BAND + r + stage_rows, :] = ke.astype(BF16)
            k_odd[BAND + r:BAND + r + stage_rows, :] = ko.astype(BF16)
            v_t[:, BAND + r:BAND + r + stage_rows] = v_ref[r:r + stage_rows, :].T.astype(BF16)

    start = pl.multiple_of(t * nq, nq)
    window = pl.ds(start, win)
    q_t[...] = (q_ref[...] * (HEAD_DIM ** -0.5)).T.astype(BF16)
    before_start = jnp.where(lax.broadcasted_iota(I32, (win, nq), 0) >= BAND - t * nq, 0.0, NEG_BIG)
    for h in range(N_HEADS):
        pair = slice((h // 2) * LANES, (h // 2 + 1) * LANES)
        k_ref_h = k_even if h % 2 == 0 else k_odd
        s_sc[h * win:(h + 1) * win, :] = (jnp.dot(k_ref_h[window, pair], q_t[pair, :], preferred_element_type=F32)
                                          + bias_ref[h] + before_start)
    for h in range(N_HEADS):
        s = s_sc[h * win:(h + 1) * win, :]
        p = jnp.exp(s - jnp.max(s, axis=0, keepdims=True))
        p_sc[h * win:(h + 1) * win, :] = (p / jnp.sum(p, axis=0, keepdims=True)).astype(BF16)
    for h in range(N_HEADS):
        dims = slice(h * HEAD_DIM, (h + 1) * HEAD_DIM)
        o_t[dims, :] = jnp.dot(v_t[dims, window], p_sc[h * win:(h + 1) * win, :], preferred_element_type=F32)
    o_ref[...] = o_t[...].T


def _band_bias(table, n_rows):
    win = BAND + n_rows
    assert n_rows <= MAX_REL <= BAND
    n_heads = table.shape[0]
    g = jnp.concatenate([jnp.broadcast_to(table[:, 2 * MAX_REL:], (n_heads, win - MAX_REL)),
                         table[:, MAX_REL - n_rows:2 * MAX_REL][:, ::-1],
                         jnp.zeros((n_heads, 1), table.dtype)], axis=1).astype(F32)
    period = g.shape[1]
    skew = jnp.tile(g, (1, n_rows))[:, :n_rows * (period - 1)].reshape(n_heads, n_rows, period - 1)
    return skew[:, :, n_rows - 1:n_rows - 1 + win]


def _band_prompt(q, k, v, table, n_batch, seq, q_rows=128):
    nt = seq // q_rows
    win = BAND + q_rows
    assert seq % q_rows == 0 and q_rows % CHUNK == 0
    key_chunk = jnp.arange(win)[:, None] // CHUNK - BAND // CHUNK
    query_chunk = jnp.arange(q_rows)[None, :] // CHUNK
    visible = (key_chunk <= query_chunk) & (key_chunk >= query_chunk - BAND // CHUNK)
    bias = jnp.where(visible[None], jnp.swapaxes(_band_bias(table, q_rows), 1, 2), NEG_BIG)
    tile_spec = pl.BlockSpec((q_rows, WIDTH), lambda b, t: (b * nt + t, 0))
    seq_spec = pl.BlockSpec((seq, WIDTH), lambda b, t: (b, 0))
    return pl.pallas_call(
        _band_prompt_body,
        out_shape=jax.ShapeDtypeStruct((n_batch * seq, WIDTH), F32),
        grid=(n_batch, nt),
        in_specs=[tile_spec, seq_spec, seq_spec, pl.BlockSpec(bias.shape, lambda b, t: (0, 0, 0))],
        out_specs=tile_spec,
        scratch_shapes=[pltpu.VMEM((BAND + seq, WIDTH), BF16), pltpu.VMEM((BAND + seq, WIDTH), BF16),
                        pltpu.VMEM((WIDTH, BAND + seq), BF16), pltpu.VMEM((WIDTH, q_rows), BF16),
                        pltpu.VMEM((N_HEADS * win, q_rows), F32), pltpu.VMEM((N_HEADS * win, q_rows), BF16),
                        pltpu.VMEM((WIDTH, q_rows), F32)],
        compiler_params=_params(("parallel", "arbitrary")),
        name="band_prompt",
    )(q, k, v, bias)


def _band_sample(q, k, v, k_hist, v_hist, layer, bias, n_batch, row0):
    blk0 = row0 // CHUNK
    hist_spec = pl.BlockSpec((None, None, BAND, WIDTH), lambda b: (layer, b, 0, 0))
    new_spec = pl.BlockSpec((CHUNK, WIDTH), lambda b: (b, 0))
    return pl.pallas_call(
        _band_sample_body,
        out_shape=jax.ShapeDtypeStruct((n_batch * CHUNK, WIDTH), F32),
        grid=(n_batch,),
        in_specs=[pl.BlockSpec((CHUNK, WIDTH), lambda b: (blk0 + b, 0)), new_spec, new_spec, hist_spec, hist_spec,
                  pl.BlockSpec(bias.shape, lambda b: (0, 0, 0))],
        out_specs=new_spec,
        scratch_shapes=[pltpu.VMEM((BAND + CHUNK, WIDTH), BF16), pltpu.VMEM((BAND + CHUNK, WIDTH), BF16)],
        compiler_params=_params(("parallel",)),
        name="band_sample",
    )(q, k, v, k_hist, v_hist, bias)


def _mem_body(q_ref, k_ref, v_ref, o_ref):
    scale = M_HEAD_DIM ** -0.5
    for h in range(M_HEADS):
        cols = slice(h * M_HEAD_DIM, (h + 1) * M_HEAD_DIM)
        s = lax.dot_general(q_ref[:, cols].astype(BF16), k_ref[:, cols].astype(BF16), _NT_DIMS,
                            preferred_element_type=F32) * scale
        p = jnp.exp(s - jnp.max(s, axis=1, keepdims=True))
        denom = jnp.sum(p, axis=1, keepdims=True)
        o = jnp.dot(p.astype(BF16), v_ref[:, cols].astype(BF16), preferred_element_type=F32)
        o_ref[:, cols] = o / denom


def _mem_attn(q, mk, mv, kv_index, n_batch, rows_per_batch, row0, tq):
    width = M_HEADS * M_HEAD_DIM
    per = rows_per_batch // tq
    blk0 = row0 // tq
    kv_block = (None,) * (mk.ndim - 2) + (N_MEM, width)
    kv_spec = pl.BlockSpec(kv_block, lambda b, t: kv_index(b))
    return pl.pallas_call(
        _mem_body,
        out_shape=jax.ShapeDtypeStruct((n_batch * rows_per_batch, width), F32),
        grid=(n_batch, per),
        in_specs=[pl.BlockSpec((tq, width), lambda b, t: (blk0 + b * per + t, 0)), kv_spec, kv_spec],
        out_specs=pl.BlockSpec((tq, width), lambda b, t: (b * per + t, 0)),
        compiler_params=_params(("parallel", "parallel")),
        name="mem_attn",
    )(q, mk, mv)


def _split_pair_lanes(x):
    lane = lax.broadcasted_iota(I32, x.shape, 1)
    even = (lane & (LANES - 1)) < HEAD_DIM
    return jnp.where(even, x, 0.0), jnp.where(even, 0.0, x)


def _dsa_body(*refs, sample, key_block, n_keys, n_sel):
    n_in = 8 if sample else 6
    if sample:
        q_ref, qi_ref, kiwq_ref, kn_ref, vn_ref, kc_ref, vc_ref, kic_ref = refs[:n_in]
    else:
        q_ref, qi_ref, kiwq_ref, k_ref, v_ref, kiw_ref = refs[:n_in]
    o_ref = refs[n_in]
    (k_even, k_odd, v_t, ki_lo, ki_hi, keybuf, maskbuf, q_t, qi_t, kiwq_t, m_sc, l_sc, a_sc, acc_t,
     s_sc, p_sc) = refs[n_in + 1:]
    kb = key_block
    n_pad = k_even.shape[0]
    nq = q_ref.shape[0]
    stage_rows = 256

    def stage_kv(row0, k_new, v_new):
        n_new = k_new.shape[0]
        ke, ko = _split_pair_lanes(k_new)
        k_even[row0:row0 + n_new, :] = ke.astype(BF16)
        k_odd[row0:row0 + n_new, :] = ko.astype(BF16)
        v_t[:, row0:row0 + n_new] = v_new.T.astype(BF16)

    def stage_index_keys(row0, kiw):
        lane = lax.broadcasted_iota(I32, kiw.shape, 1)
        ki_lo[row0:row0 + kiw.shape[0], :] = jnp.where(lane < HEAD_DIM, kiw, 0.0).astype(BF16)
        ki_hi[row0:row0 + kiw.shape[0], :] = jnp.where(lane < HEAD_DIM, 0.0, pltpu.roll(kiw, HEAD_DIM, 1)).astype(BF16)

    def stage_keys():
        if sample:
            n_hist = kc_ref.shape[0]
            for r in range(0, n_hist, stage_rows):
                stage_kv(r, kc_ref[r:r + stage_rows, :], vc_ref[r:r + stage_rows, :])
            kic = kic_ref[...]
            zeros = jnp.zeros_like(kic)
            ki_lo[0:n_hist, :] = jnp.concatenate([kic, zeros], axis=1).astype(BF16)
            ki_hi[0:n_hist, :] = jnp.concatenate([zeros, kic], axis=1).astype(BF16)
            stage_kv(n_hist, kn_ref[...], vn_ref[...])
            stage_index_keys(n_hist, kiwq_ref[...])
            if n_pad > n_keys:
                for ref in (k_even, k_odd, ki_lo, ki_hi):
                    ref[n_keys:, :] = jnp.zeros((n_pad - n_keys, ref.shape[1]), BF16)
                v_t[:, n_keys:] = jnp.zeros((v_t.shape[0], n_pad - n_keys), BF16)
        else:
            for r in range(0, n_keys, stage_rows):
                stage_kv(r, k_ref[r:r + stage_rows, :], v_ref[r:r + stage_rows, :])
            stage_index_keys(0, kiw_ref[...])

    if sample:
        stage_keys()
        limit = n_keys
        nb = n_pad // kb
        search_bits = 32
    else:
        t = pl.program_id(1)
        pl.when(t == 0)(stage_keys)
        chunk = t * (nq // CHUNK) + lax.broadcasted_iota(I32, (1, nq), 1) // CHUNK
        limit = (chunk + 1) * CHUNK
        limit_max = (t + 1) * nq
        nb = (limit_max + kb - 1) // kb
        search_bits = jnp.where(limit_max <= n_sel, 0, 32)

    def block_rows(j):
        return pl.ds(pl.multiple_of(j * kb, kb), kb)

    def key_pos(j):
        return j * kb + lax.broadcasted_iota(I32, (kb, nq), 0)

    q_t[...] = (q_ref[...] * (HEAD_DIM ** -0.5)).T.astype(BF16)
    qi_t[...] = qi_ref[...].T.astype(BF16)
    kiwq_t[...] = kiwq_ref[...].T

    def score_block(j, carry):
        acc = jnp.zeros((kb, nq), F32)
        for h in range(N_HEADS):
            pair = slice((h // 2) * LANES, (h // 2 + 1) * LANES)
            ki_ref = ki_lo if h % 2 == 0 else ki_hi
            d = jnp.dot(ki_ref[block_rows(j), :], qi_t[pair, :], preferred_element_type=F32)
            acc = acc + kiwq_t[HEAD_DIM + h:HEAD_DIM + h + 1, :] * jnp.maximum(d, 0.0)
        acc = acc + 0.0
        acc = jnp.where(key_pos(j) < limit, acc, -jnp.inf)
        bits = lax.bitcast_convert_type(acc, I32)
        keybuf[block_rows(j), :] = bits ^ ((bits >> 31) & 0x7FFFFFFF)
        return carry

    lax.fori_loop(0, nb, score_block, 0)

    def count(pred_fn):
        def body(j, part):
            hit = jnp.where(pred_fn(keybuf[block_rows(j), :], j), 1.0, 0.0)
            return part + jnp.sum(hit.reshape(kb // 8, 8, nq), axis=0)
        part = lax.fori_loop(0, nb, body, jnp.zeros((8, nq), F32))
        return jnp.sum(part, axis=0, keepdims=True)

    def bit_step(i, prefix):
        cand = prefix | lax.shift_left(jnp.int32(1), 31 - i)
        thr_i = cand ^ INT_MIN
        cnt = count(lambda kblk, j: kblk >= thr_i)
        return jnp.where(cnt >= n_sel, cand, prefix)

    prefix = lax.fori_loop(0, search_bits, bit_step, jnp.zeros((1, nq), I32))
    thr = prefix ^ INT_MIN
    above = count(lambda kblk, j: kblk > thr)
    at_least = count(lambda kblk, j: kblk >= thr)
    need = n_sel - above
    surplus = at_least > n_sel

    pos_bits = max(1, (n_pad - 1).bit_length() + 1)
    any_surplus = jnp.max(jnp.where(surplus, 1.0, 0.0)) > 0.0

    def pos_step(i, bound):
        cand = bound | lax.shift_left(jnp.int32(1), pos_bits - 1 - i)
        cnt = count(lambda kblk, j: (kblk == thr) & (key_pos(j) < cand))
        return jnp.where(cnt <= need, cand, bound)

    bound = lax.fori_loop(0, jnp.where(any_surplus, pos_bits, 0), pos_step, jnp.zeros((1, nq), I32))
    bound = jnp.where(surplus, bound, 2 ** pos_bits - 1)

    def mask_block(j, carry):
        kblk = keybuf[block_rows(j), :]
        pos = key_pos(j)
        sel = ((kblk > thr) | ((kblk == thr) & (pos < bound))) & (pos < limit)
        maskbuf[block_rows(j), :] = jnp.where(sel, 0.0, NEG_BIG)
        return carry

    lax.fori_loop(0, nb, mask_block, 0)

    m_sc[...] = jnp.full(m_sc.shape, NEG_BIG, F32)
    l_sc[...] = jnp.zeros(l_sc.shape, F32)
    acc_t[...] = jnp.zeros(acc_t.shape, F32)

    def attend(j, carry):
        mask = maskbuf[block_rows(j), :]
        for h in range(N_HEADS):
            pair = slice((h // 2) * LANES, (h // 2 + 1) * LANES)
            k_ref_h = k_even if h % 2 == 0 else k_odd
            s_sc[h * kb:(h + 1) * kb, :] = jnp.dot(k_ref_h[block_rows(j), pair], q_t[pair, :],
                                                   preferred_element_type=F32) + mask
        for h in range(N_HEADS):
            stat = slice(h * 8, (h + 1) * 8)
            s = s_sc[h * kb:(h + 1) * kb, :]
            m_prev = m_sc[stat, :]
            m_new = jnp.maximum(m_prev, jnp.max(s, axis=0, keepdims=True))
            alpha = jnp.exp(m_prev - m_new)
            p = jnp.exp(s - m_new[0:1, :])
            l_sc[stat, :] = alpha * l_sc[stat, :] + jnp.sum(p, axis=0, keepdims=True)
            m_sc[stat, :] = m_new
            a_sc[stat, :] = alpha
            p_sc[h * kb:(h + 1) * kb, :] = p.astype(BF16)
        for h in range(N_HEADS):
            dims = slice(h * HEAD_DIM, (h + 1) * HEAD_DIM)
            pv = jnp.dot(v_t[dims, block_rows(j)], p_sc[h * kb:(h + 1) * kb, :], preferred_element_type=F32)
            acc_t[dims, :] = a_sc[h * 8:h * 8 + 1, :] * acc_t[dims, :] + pv
        return carry

    lax.fori_loop(0, nb, attend, 0)
    for h in range(N_HEADS):
        dims = slice(h * HEAD_DIM, (h + 1) * HEAD_DIM)
        acc_t[dims, :] = acc_t[dims, :] / l_sc[h * 8:h * 8 + 1, :]
    o_ref[...] = acc_t[...].T


def _dsa_scratch(nq, n_pad, key_block):
    return [pltpu.VMEM((n_pad, WIDTH), BF16), pltpu.VMEM((n_pad, WIDTH), BF16),
            pltpu.VMEM((WIDTH, n_pad), BF16),
            pltpu.VMEM((n_pad, LANES), BF16), pltpu.VMEM((n_pad, LANES), BF16),
            pltpu.VMEM((n_pad, nq), I32), pltpu.VMEM((n_pad, nq), F32),
            pltpu.VMEM((WIDTH, nq), BF16), pltpu.VMEM((WIDTH, nq), BF16),
            pltpu.VMEM((LANES, nq), F32),
            pltpu.VMEM((N_HEADS * 8, nq), F32), pltpu.VMEM((N_HEADS * 8, nq), F32),
            pltpu.VMEM((N_HEADS * 8, nq), F32),
            pltpu.VMEM((WIDTH, nq), F32),
            pltpu.VMEM((N_HEADS * key_block, nq), F32),
            pltpu.VMEM((N_HEADS * key_block, nq), BF16)]


def _dsa_prompt(q, k, v, qi, kiw, n_batch, seq, key_block=256, q_rows=256):
    nt = seq // q_rows
    key_block = min(key_block, seq)
    assert seq % key_block == 0 and seq % q_rows == 0 and q_rows % CHUNK == 0
    n_sel = min(TOPK_KEYS, seq // 4)
    tile_spec = lambda width: pl.BlockSpec((q_rows, width), lambda b, t: (b * nt + t, 0))
    seq_spec = lambda width: pl.BlockSpec((seq, width), lambda b, t: (b, 0))
    return pl.pallas_call(
        functools.partial(_dsa_body, sample=False, key_block=key_block, n_keys=seq, n_sel=n_sel),
        out_shape=jax.ShapeDtypeStruct((n_batch * seq, WIDTH), F32),
        grid=(n_batch, nt),
        in_specs=[tile_spec(WIDTH), tile_spec(WIDTH), tile_spec(LANES),
                  seq_spec(WIDTH), seq_spec(WIDTH), seq_spec(LANES)],
        out_specs=tile_spec(WIDTH),
        scratch_shapes=_dsa_scratch(q_rows, seq, key_block),
        compiler_params=_params(("parallel", "arbitrary")),
        name="dsa_prompt",
    )(q, qi, kiw, k, v, kiw)


def _dsa_sample(q, k, v, qi, kiw, k_cache, v_cache, ki_cache, layer, n_batch, row0, key_block=768):
    past = k_cache.shape[2]
    n_keys = past + CHUNK
    key_block = min(key_block, -(-n_keys // 256) * 256)
    n_pad = -(-n_keys // key_block) * key_block
    n_sel = min(TOPK_KEYS, n_keys // 4)
    blk0 = row0 // CHUNK
    query_spec = pl.BlockSpec((CHUNK, WIDTH), lambda b: (blk0 + b, 0))
    new_spec = lambda width: pl.BlockSpec((CHUNK, width), lambda b: (b, 0))
    cache_spec = lambda width: pl.BlockSpec((None, None, past, width), lambda b: (layer, b, 0, 0))
    return pl.pallas_call(
        functools.partial(_dsa_body, sample=True, key_block=key_block, n_keys=n_keys, n_sel=n_sel),
        out_shape=jax.ShapeDtypeStruct((n_batch * CHUNK, WIDTH), F32),
        grid=(n_batch,),
        in_specs=[query_spec, query_spec, new_spec(LANES), new_spec(WIDTH), new_spec(WIDTH),
                  cache_spec(WIDTH), cache_spec(WIDTH), cache_spec(HEAD_DIM)],
        out_specs=new_spec(WIDTH),
        scratch_shapes=_dsa_scratch(CHUNK, n_pad, key_block),
        compiler_params=_params(("parallel",)),
        name="dsa_sample",
    )(q, qi, kiw, k, v, k_cache, v_cache, ki_cache)


def _layer_norm(z, g, b):
    mu = jnp.mean(z, axis=-1, keepdims=True)
    zc = z - mu
    var = jnp.mean(zc * zc, axis=-1, keepdims=True)
    return zc * lax.rsqrt(var + LN_EPS) * g + b


HI_HALF = -65536


def _pack_rows(x):
    half = x.shape[1] // 2
    lo = lax.bitcast_convert_type(x[:, :half].astype(BF16).astype(F32), I32)
    hi = lax.bitcast_convert_type(x[:, half:].astype(BF16).astype(F32), I32)
    return (hi & HI_HALF) | lax.shift_right_logical(lo, 16)


def _unpack_rows(w):
    lo = lax.bitcast_convert_type(lax.shift_left(w, 16), F32).astype(BF16)
    hi = lax.bitcast_convert_type(w & HI_HALF, F32).astype(BF16)
    return lo, hi


def _merge_body(x_ref, ap_ref, as_ref, bp_ref, bs_ref, mp_ref, ms_ref, gates_ref, wbr_ref, wout_ref, g_ref, beta_ref,
                o_ref, ow_ref, *, alpha, prompt_tiles):
    d = x_ref.shape[1]
    in_prompt = pl.program_id(0) < prompt_tiles
    mix = None
    for n, (p_ref, s_ref) in enumerate(((ap_ref, as_ref), (bp_ref, bs_ref), (mp_ref, ms_ref))):
        branch = jnp.where(in_prompt, p_ref[...], s_ref[...])
        proj = jnp.dot(branch.astype(BF16), wbr_ref[n], preferred_element_type=F32)
        term = jax.nn.sigmoid(gates_ref[:, n * d:(n + 1) * d]) * proj
        mix = term if mix is None else mix + term
    y = jnp.dot(mix.astype(BF16), wout_ref[...], preferred_element_type=F32)
    out = _layer_norm(alpha * x_ref[...] + y, g_ref[...], beta_ref[...])
    o_ref[...] = out
    ow_ref[...] = _pack_rows(out)


def _merge(x, branches, gates, w_branch, w_out, g, beta, alpha, tm=256):
    n, d = x.shape
    prompt_tiles = branches[0][0].shape[0] // tm
    assert all(p.shape[0] == prompt_tiles * tm and (n - p.shape[0]) == s.shape[0] and s.shape[0] % tm == 0
               for p, s in branches)
    packed = jax.eval_shape(_pack_rows, jax.ShapeDtypeStruct((tm, d), F32))
    row = lambda width: pl.BlockSpec((tm, width), lambda i: (i, 0))
    full = lambda arr: pl.BlockSpec(arr.shape, lambda i: (0,) * arr.ndim)
    prompt_row = lambda width: pl.BlockSpec((tm, width), lambda i: (jnp.minimum(i, prompt_tiles - 1), 0))
    sample_row = lambda width: pl.BlockSpec((tm, width), lambda i: (jnp.maximum(i - prompt_tiles, 0), 0))
    branch_specs, branch_args = [], []
    for p, s in branches:
        branch_specs += [prompt_row(p.shape[1]), sample_row(s.shape[1])]
        branch_args += [p, s]
    return pl.pallas_call(
        functools.partial(_merge_body, alpha=alpha, prompt_tiles=prompt_tiles),
        out_shape=[jax.ShapeDtypeStruct((n, d), F32), jax.ShapeDtypeStruct((n, packed.shape[1]), packed.dtype)],
        grid=(n // tm,),
        in_specs=[row(d)] + branch_specs + [row(gates.shape[1]), full(w_branch), full(w_out), full(g), full(beta)],
        out_specs=[row(d), row(packed.shape[1])],
        compiler_params=_params(("parallel",)),
        name="merge",
    )(x, *branch_args, gates, w_branch, w_out, g, beta)


def _first_max(vals, idx, n):
    top = jnp.max(vals, axis=0, keepdims=True)
    arg = jnp.min(jnp.where(vals == top, idx, n), axis=0, keepdims=True)
    return top, arg


def _router_body(x_ref, wt_ref, bias_ref, eidx_ref, gw_ref, rank_ref, counts_ref, running):
    tm = x_ref.shape[0]

    @pl.when(pl.program_id(0) == 0)
    def _():
        running[...] = jnp.zeros(running.shape, F32)

    per = N_EXPERTS // N_GROUPS
    logits = lax.dot_general(wt_ref[...], x_ref[...].astype(BF16), _NT_DIMS, preferred_element_type=F32)
    s = jax.nn.sigmoid(logits)
    sb = s + bias_ref[...]
    in_group = lax.broadcasted_iota(I32, (per, tm), 0).astype(F32)
    group_scores = []
    for g in range(N_GROUPS):
        blk = sb[g * per:(g + 1) * per, :]
        top1, arg1 = _first_max(blk, in_group, per)
        top2 = jnp.max(jnp.where(in_group == arg1, -jnp.inf, blk), axis=0, keepdims=True)
        group_scores.append(top1 + top2)
    gs = jnp.concatenate(group_scores, axis=0)
    gidx = lax.broadcasted_iota(I32, (N_GROUPS, tm), 0).astype(F32)
    chosen = jnp.zeros((N_GROUPS, tm), F32)
    for _ in range(TOPK_GROUPS):
        _, arg = _first_max(gs, gidx, N_GROUPS)
        hit = gidx == arg
        chosen = jnp.where(hit, 1.0, chosen)
        gs = jnp.where(hit, -jnp.inf, gs)
    cand = jnp.concatenate(
        [jnp.where(chosen[g:g + 1, :] > 0.0, sb[g * per:(g + 1) * per, :], -jnp.inf) for g in range(N_GROUPS)],
        axis=0)
    eidx = lax.broadcasted_iota(I32, (N_EXPERTS, tm), 0).astype(F32)
    picks, weights, hits = [], [], []
    for _ in range(EXPERT_TOPK):
        _, arg = _first_max(cand, eidx, N_EXPERTS)
        hit = eidx == arg
        weights.append(jnp.sum(jnp.where(hit, s, 0.0), axis=0, keepdims=True))
        picks.append(arg)
        hits.append(hit)
        cand = jnp.where(hit, -jnp.inf, cand)
    total = weights[0]
    for w in weights[1:]:
        total = total + w
    taken = jnp.zeros((N_EXPERTS, tm), F32)
    for hit in hits:
        taken = jnp.where(hit, 1.0, taken)
    earlier = (lax.broadcasted_iota(I32, (tm, tm), 0) < lax.broadcasted_iota(I32, (tm, tm), 1))
    before = jnp.dot(taken.astype(BF16), jnp.where(earlier, 1.0, 0.0).astype(BF16),
                     preferred_element_type=F32) + running[...]
    for r in range(EXPERT_TOPK):
        eidx_ref[r:r + 1, :] = picks[r].astype(I32)
        gw_ref[r:r + 1, :] = weights[r] / total * ROUTE_SCALE
        rank_ref[r:r + 1, :] = jnp.sum(jnp.where(hits[r], before, 0.0), axis=0, keepdims=True).astype(I32)
    running[...] = running[...] + jnp.sum(taken, axis=1, keepdims=True)
    counts_ref[...] = running[...]


def _router(x, w_router_t, b_router_col, tm=256):
    n, d = x.shape
    slot = pl.BlockSpec((EXPERT_TOPK, tm), lambda i: (0, i))
    return pl.pallas_call(
        _router_body,
        out_shape=[jax.ShapeDtypeStruct((EXPERT_TOPK, n), I32), jax.ShapeDtypeStruct((EXPERT_TOPK, n), F32),
                   jax.ShapeDtypeStruct((EXPERT_TOPK, n), I32), jax.ShapeDtypeStruct((N_EXPERTS, 1), F32)],
        grid=(n // tm,),
        in_specs=[pl.BlockSpec((tm, d), lambda i: (i, 0)),
                  pl.BlockSpec(w_router_t.shape, lambda i: (0, 0)),
                  pl.BlockSpec(b_router_col.shape, lambda i: (0, 0))],
        out_specs=[slot, slot, slot, pl.BlockSpec((N_EXPERTS, 1), lambda i: (0, 0))],
        scratch_shapes=[pltpu.VMEM((N_EXPERTS, 1), F32)],
        compiler_params=_params(("arbitrary",)),
        name="router",
    )(x, w_router_t, b_router_col)


def _dest_body(eidx_ref, rank_ref, start_ref, dest_ref):
    tm = eidx_ref.shape[1]
    experts = lax.broadcasted_iota(I32, (N_EXPERTS, tm), 0)
    for r in range(EXPERT_TOPK):
        base = jnp.sum(jnp.where(experts == eidx_ref[r:r + 1, :], start_ref[...], 0.0), axis=0, keepdims=True)
        dest_ref[r:r + 1, :] = base.astype(I32) + rank_ref[r:r + 1, :]


def _dest_rows(eidx_t, rank_t, start_col, tm=512):
    n = eidx_t.shape[1]
    slot = pl.BlockSpec((EXPERT_TOPK, tm), lambda i: (0, i))
    return pl.pallas_call(
        _dest_body,
        out_shape=jax.ShapeDtypeStruct((EXPERT_TOPK, n), I32),
        grid=(n // tm,),
        in_specs=[slot, slot, pl.BlockSpec(start_col.shape, lambda i: (0, 0))],
        out_specs=slot,
        compiler_params=_params(("parallel",)),
        name="dest_rows",
    )(eidx_t, rank_t, start_col)


def _scatter_body(dest_ref, x_ref, _, o_ref, sem):
    tm = x_ref.shape[0]

    def row_copy(t, k):
        return pltpu.make_async_copy(x_ref.at[pl.ds(t, 1)], o_ref.at[pl.ds(dest_ref[k, t], 1)], sem)

    def issue(t, carry):
        for k in range(EXPERT_TOPK):
            row_copy(t, k).start(priority=k % 2)
        return carry

    lax.fori_loop(0, tm, issue, 0)
    for k in range(EXPERT_TOPK):
        pltpu.make_async_copy(x_ref, o_ref.at[pl.ds(0, tm)], sem).wait()


def _scatter_rows(xw, dest_t, rows, tm=256):
    n, width = xw.shape
    return pl.pallas_call(
        _scatter_body,
        out_shape=jax.ShapeDtypeStruct((rows, width), xw.dtype),
        grid=(n // tm,),
        in_specs=[pl.BlockSpec((EXPERT_TOPK, tm), lambda i: (0, i), memory_space=pltpu.SMEM),
                  pl.BlockSpec((tm, width), lambda i: (i, 0)),
                  pl.BlockSpec(memory_space=pl.ANY)],
        out_specs=pl.BlockSpec(memory_space=pl.ANY),
        scratch_shapes=[pltpu.SemaphoreType.DMA],
        input_output_aliases={2: 0},
        compiler_params=_params(("arbitrary",)),
        name="scatter_rows",
    )(dest_t, xw, jnp.zeros((rows, width), xw.dtype))


def _experts_body(block_exp_ref, n_used_ref, x_ref, wgu_ref, wdn_ref, o_ref, wgu_bf, wdn_bf):
    i = pl.program_id(0)
    used = i < n_used_ref[0]

    @pl.when(used)
    def _():
        prev = block_exp_ref[jnp.maximum(i - 1, 0)]

        @pl.when((i == 0) | (block_exp_ref[i] != prev))
        def _():
            wgu_bf[...] = wgu_ref[...].astype(BF16)
            wdn_bf[...] = wdn_ref[...].astype(BF16)

        x_lo, x_hi = _unpack_rows(x_ref[...])
        half = x_lo.shape[1]
        h = (jnp.dot(x_lo, wgu_bf[0:half, :], preferred_element_type=F32)
             + jnp.dot(x_hi, wgu_bf[half:, :], preferred_element_type=F32))
        ff = h.shape[1] // 2
        act = jax.nn.silu(h[:, :ff]) * h[:, ff:]
        o_ref[...] = jnp.dot(act.astype(BF16), wdn_bf[...], preferred_element_type=F32)

    @pl.when(jnp.logical_not(used))
    def _():
        o_ref[...] = jnp.zeros(o_ref.shape, F32)


def _experts(x_rows, block_exp, n_used, w_gu, w_dn, layer):
    rows, width = x_rows.shape
    n_blocks = rows // EXPERT_BLOCK
    d, ff2 = w_gu.shape[-2:]
    grid_spec = pltpu.PrefetchScalarGridSpec(
        num_scalar_prefetch=2,
        grid=(n_blocks,),
        in_specs=[pl.BlockSpec((EXPERT_BLOCK, width), lambda i, be, nu: (i, 0)),
                  pl.BlockSpec((None, None, d, ff2), lambda i, be, nu: (layer, be[i], 0, 0)),
                  pl.BlockSpec((None, None, ff2 // 2, d), lambda i, be, nu: (layer, be[i], 0, 0))],
        out_specs=pl.BlockSpec((EXPERT_BLOCK, d), lambda i, be, nu: (i, 0)),
        scratch_shapes=[pltpu.VMEM((d, ff2), BF16), pltpu.VMEM((ff2 // 2, d), BF16)],
    )
    return pl.pallas_call(
        _experts_body,
        out_shape=jax.ShapeDtypeStruct((rows, d), F32),
        grid_spec=grid_spec,
        compiler_params=_params(("arbitrary",)),
        name="experts",
    )(block_exp, n_used, x_rows, w_gu, w_dn)


def _block_plan(counts, n_blocks):
    counts = counts.reshape(-1).astype(I32)
    padded = (counts + EXPERT_BLOCK - 1) // EXPERT_BLOCK * EXPERT_BLOCK
    pad_end = jnp.cumsum(padded)
    start_col = (pad_end - padded).astype(F32)[:, None]
    end_blocks = pad_end // EXPERT_BLOCK
    block_exp = jnp.minimum(jnp.sum(end_blocks[None, :] <= jnp.arange(n_blocks)[:, None], axis=1),
                            N_EXPERTS - 1).astype(I32)
    return start_col, block_exp, end_blocks[-1:].astype(I32)


def _ffn_out_body(dest_ref, dest_next_ref, x_ref, gw_ref, yb_ref, wgu_ref, wdn_ref, g_ref, beta_ref, o_ref,
                  ybuf, sems, *, alpha):
    i = pl.program_id(0)
    n_steps = pl.num_programs(0)
    tm = x_ref.shape[0]
    slot = i % 2

    def gather(dests, to_slot):
        def issue(t, carry):
            for k in range(EXPERT_TOPK):
                pltpu.make_async_copy(yb_ref.at[pl.ds(dests[k, t], 1)], ybuf.at[to_slot, k, pl.ds(t, 1)],
                                      sems.at[to_slot]).start(priority=k % 2)
            return carry
        lax.fori_loop(0, tm, issue, 0)

    @pl.when(i == 0)
    def _():
        gather(dest_ref, 0)

    @pl.when(i + 1 < n_steps)
    def _():
        gather(dest_next_ref, 1 - slot)

    x = x_ref[...]
    h = jnp.dot(x.astype(BF16), wgu_ref[...], preferred_element_type=F32)
    ff = h.shape[1] // 2
    act = jax.nn.silu(h[:, :ff]) * h[:, ff:]
    shared = jnp.dot(act.astype(BF16), wdn_ref[...], preferred_element_type=F32)

    for k in range(EXPERT_TOPK):
        pltpu.make_async_copy(yb_ref.at[pl.ds(0, tm)], ybuf.at[slot, k], sems.at[slot]).wait()
    routed = None
    for k in range(EXPERT_TOPK):
        term = ybuf[slot, k] * gw_ref[:, k:k + 1]
        routed = term if routed is None else routed + term
    o_ref[...] = _layer_norm(alpha * x + (routed + shared), g_ref[...], beta_ref[...])


def _ffn_out(x, yb, dest_t, gw, ws_gu, ws_dn, g, beta, alpha, tm=256):
    n, d = x.shape
    n_tiles = n // tm
    row = pl.BlockSpec((tm, d), lambda i: (i, 0))
    full = lambda arr: pl.BlockSpec(arr.shape, lambda i: (0,) * arr.ndim)
    dest_spec = lambda index: pl.BlockSpec((EXPERT_TOPK, tm), index, memory_space=pltpu.SMEM)
    return pl.pallas_call(
        functools.partial(_ffn_out_body, alpha=alpha),
        out_shape=jax.ShapeDtypeStruct((n, d), F32),
        grid=(n_tiles,),
        in_specs=[dest_spec(lambda i: (0, i)), dest_spec(lambda i: (0, jnp.minimum(i + 1, n_tiles - 1))),
                  row, pl.BlockSpec((tm, EXPERT_TOPK), lambda i: (i, 0)),
                  pl.BlockSpec(memory_space=pl.ANY),
                  full(ws_gu), full(ws_dn), full(g), full(beta)],
        out_specs=row,
        scratch_shapes=[pltpu.VMEM((2, EXPERT_TOPK, tm, d), F32), pltpu.SemaphoreType.DMA((2,))],
        compiler_params=_params(("arbitrary",)),
        name="ffn_out",
    )(dest_t, dest_t, x, gw, yb, ws_gu, ws_dn, g, beta)


def _moe(x, xw, layer, w_router_t, b_router_col, w_exp_gate_up, w_exp_down, ws_gu, ws_dn, g, beta, alpha):
    n = x.shape[0]
    n_blocks = -(-n * EXPERT_TOPK // EXPERT_BLOCK) + N_EXPERTS
    eidx_t, gw_t, rank_t, counts = _router(x, w_router_t, b_router_col)
    start_col, block_exp, n_used = _block_plan(counts, n_blocks)
    dest_t = _dest_rows(eidx_t, rank_t, start_col)
    x_rows = _scatter_rows(xw, dest_t, n_blocks * EXPERT_BLOCK)
    yb = _experts(x_rows, block_exp, n_used, w_exp_gate_up, w_exp_down, layer)
    return _ffn_out(x, yb, dest_t, gw_t.T, ws_gu, ws_dn, g, beta, alpha)


def kernel(x_prompt, x_sample, mem_prompt, cache_a_k, cache_a_v, cache_b_k, cache_b_v, cache_b_idx_k,
           cache_mem_k, cache_mem_v, w_in, rel_bias, w_mem_kv, w_branch, w_out, ln_mix_g, ln_mix_b,
           w_router, b_router, w_exp_gate_up, w_exp_down, w_sh_gate_up, w_sh_down, ln_ffn_g, ln_ffn_b):
    n_batch, seq, d = x_prompt.shape
    s_batch, s_seq, _ = x_sample.shape
    depth = w_in.shape[0]
    past = cache_b_k.shape[2]
    assert s_seq == CHUNK and seq % CHUNK == 0 and cache_a_k.shape[2] == BAND and seq >= BAND
    alpha = (2 * depth) ** 0.25
    n_p, n_s = n_batch * seq, s_batch * s_seq
    n_tot = n_p + n_s
    tm = 512
    assert seq % tm == 0 and n_p % tm == 0 and n_s % tm == 0 and tm % CHUNK == 0

    x = jnp.concatenate([x_prompt.reshape(n_p, d), x_sample.reshape(n_s, d)])

    pos = jnp.concatenate([jnp.arange(seq), jnp.tile(past + jnp.arange(s_seq), tm // s_seq)])
    tab_heads = _rope_tables(pos, LANES)
    tab_kiw = _rope_tables(pos, HEAD_DIM)
    p_tiles, seq_tiles = n_p // tm, seq // tm
    table_index = lambda i: jnp.where(i < p_tiles, i % seq_tiles, seq_tiles)

    wd = WIDTH
    c_qi, c_ki, c_wi, c_qm, c_gates = 6 * wd, 7 * wd, 7 * wd + HEAD_DIM, 7 * wd + HEAD_DIM + N_HEADS, 8 * wd + HEAD_DIM + N_HEADS
    cache_a_k = cache_a_k.reshape(depth, s_batch, BAND, wd)
    cache_a_v = cache_a_v.reshape(depth, s_batch, BAND, wd)
    cache_b_k = cache_b_k.reshape(depth, s_batch, past, wd)
    cache_b_v = cache_b_v.reshape(depth, s_batch, past, wd)
    cache_mem_k = cache_mem_k.reshape(depth, s_batch, N_MEM, wd)
    cache_mem_v = cache_mem_v.reshape(depth, s_batch, N_MEM, wd)
    mem_rows = mem_prompt.reshape(n_batch * N_MEM, d)

    outs = {k: [] for k in ("p_ak", "p_av", "p_bk", "p_bv", "p_ik", "p_mk", "p_mv",
                            "s_ak", "s_av", "s_bk", "s_bv", "s_ik")}
    for l in range(depth):
        w = w_in[l]
        w_a = w[:, :3 * wd].astype(BF16)
        w_b = w[:, 3 * wd:7 * wd].astype(BF16)
        w_c = w[:, c_qm:].astype(BF16)
        w_d = jnp.concatenate([w[:, c_ki:c_qm], jnp.zeros((d, LANES - HEAD_DIM - N_HEADS), F32)], 1).astype(BF16)

        qa, ka, va = _project(x, w_a, (wd, wd, wd), split_flags=(False, True, True), n_prompt=n_p,
                              tm=tm, name="proj_a")
        qb, kb, vb, qi = _project(x, w_b, (wd, wd, wd, wd), (True, True, False, True), tab_heads, table_index,
                                  split_flags=(False, True, True, False), n_prompt=n_p, tm=tm, name="proj_b")
        qm, gates = _project(x, w_c, (wd, 3 * d), tm=tm, name="proj_c")
        (kiw,) = _project(x, w_d, (LANES,), (True,), tab_kiw, table_index, split_flags=(True,), n_prompt=n_p,
                          tm=tm, name="proj_d")
        mk, mv = _project(mem_rows, w_mem_kv[l].astype(BF16), (wd, wd), tm=N_MEM, name="proj_mem")

        a = (_band_prompt(qa, ka[0], va[0], rel_bias[l], n_batch, seq),
             _band_sample(qa, ka[1], va[1], cache_a_k, cache_a_v, l, _band_bias(rel_bias[l], CHUNK), s_batch, n_p))
        b = (_dsa_prompt(qb, kb[0], vb[0], qi, kiw[0], n_batch, seq),
             _dsa_sample(qb, kb[1], vb[1], qi, kiw[1], cache_b_k, cache_b_v, cache_b_idx_k, l, s_batch, n_p))
        m = (_mem_attn(qm, mk, mv, lambda bb: (bb, 0), n_batch, seq, 0, tm),
             _mem_attn(qm, cache_mem_k, cache_mem_v, lambda bb, l=l: (l, bb, 0, 0), s_batch, s_seq, n_p, s_seq))

        x1, x1w = _merge(x, (a, b, m), gates, w_branch[l].astype(BF16), w_out[l].astype(BF16),
                         ln_mix_g[l][None], ln_mix_b[l][None], alpha)
        x = _moe(x1, x1w, l, w_router[l].T.astype(BF16), b_router[l][:, None], w_exp_gate_up, w_exp_down,
                 w_sh_gate_up[l].astype(BF16), w_sh_down[l].astype(BF16),
                 ln_ffn_g[l][None], ln_ffn_b[l][None], alpha)

        outs["p_ak"].append(ka[0].reshape(n_batch, seq, wd)[:, -BAND:])
        outs["p_av"].append(va[0].reshape(n_batch, seq, wd)[:, -BAND:])
        outs["p_bk"].append(kb[0].reshape(n_batch, seq, wd))
        outs["p_bv"].append(vb[0].reshape(n_batch, seq, wd))
        outs["p_ik"].append(kiw[0][:, :HEAD_DIM].reshape(n_batch, seq, HEAD_DIM))
        outs["p_mk"].append(mk.reshape(n_batch, N_MEM, wd))
        outs["p_mv"].append(mv.reshape(n_batch, N_MEM, wd))
        ka_s = ka[1].reshape(s_batch, s_seq, wd)
        va_s = va[1].reshape(s_batch, s_seq, wd)
        outs["s_ak"].append(jnp.concatenate([cache_a_k[l], ka_s], 1)[:, -BAND:])
        outs["s_av"].append(jnp.concatenate([cache_a_v[l], va_s], 1)[:, -BAND:])
        outs["s_bk"].append(kb[1].reshape(s_batch, s_seq, wd))
        outs["s_bv"].append(vb[1].reshape(s_batch, s_seq, wd))
        outs["s_ik"].append(kiw[1][:, :HEAD_DIM].reshape(s_batch, s_seq, HEAD_DIM))

    heads = lambda t: t.reshape(t.shape[:-1] + (N_HEADS, HEAD_DIM))
    mheads = lambda t: t.reshape(t.shape[:-1] + (M_HEADS, M_HEAD_DIM))
    st = lambda key: jnp.stack(outs[key])
    return (x[:n_p].reshape(n_batch, seq, d), x[n_p:].reshape(s_batch, s_seq, d),
            heads(st("p_ak")), heads(st("p_av")), heads(st("p_bk")), heads(st("p_bv")), st("p_ik"),
            mheads(st("p_mk")), mheads(st("p_mv")),
            heads(st("s_ak")), heads(st("s_av")), heads(st("s_bk")), heads(st("s_bv")), st("s_ik"))
```

```python
import functools

import jax
import jax.numpy as jnp
from jax import lax
from jax.experimental import pallas as pl
from jax.experimental.pallas import tpu as pltpu

F32 = jnp.float32
BF16 = jnp.bfloat16
I32 = jnp.int32

CHUNK = 64
HEAD_DIM = 64
N_HEADS = 8
WIDTH = N_HEADS * HEAD_DIM
BAND = 8 * CHUNK
MAX_REL = 128
TOPK_KEYS = 256
M_HEADS = 4
M_HEAD_DIM = 128
N_MEM = 256
ROPE_THETA = 10000.0
N_EXPERTS = 256
EXPERT_TOPK = 8
N_GROUPS = 8
TOPK_GROUPS = 4
EXPERT_FF = 256
ROUTE_SCALE = 2.5
EXPERT_BLOCK = 256
LN_EPS = 1e-5

LANES = 128
NEG_BIG = -1e30
INT_MIN = -(2 ** 31)
VMEM_LIMIT = 48 * 1024 * 1024

_NT_DIMS = (((1,), (1,)), ((), ()))


def _params(semantics):
    return pltpu.CompilerParams(dimension_semantics=semantics, vmem_limit_bytes=VMEM_LIMIT)


def _proj_body(*refs, widths, rope_flags, split_flags, prompt_tiles):
    has_rope = any(rope_flags)
    x_ref, w_ref = refs[0], refs[1]
    if has_rope:
        cos_ref, sin_lo_ref, sin_hi_ref = refs[2:5]
        outs = list(refs[5:])
    else:
        outs = list(refs[2:])
    in_prompt = pl.program_id(0) < prompt_tiles
    xb = x_ref[...].astype(BF16)
    off = 0
    for g, width in enumerate(widths):
        y = jnp.dot(xb, w_ref[:, off:off + width], preferred_element_type=F32)
        if rope_flags[g]:
            cos, sin_lo, sin_hi = cos_ref[...], sin_lo_ref[...], sin_hi_ref[...]
            chunks = [yk * cos + pltpu.roll(yk, LANES - 32, 1) * sin_lo + pltpu.roll(yk, 32, 1) * sin_hi
                      for yk in (y[:, k * LANES:(k + 1) * LANES] for k in range(width // LANES))]
        else:
            chunks = [y]

        def store(ref, chunks=chunks):
            step = chunks[0].shape[1]
            for k, chunk in enumerate(chunks):
                ref[:, k * step:(k + 1) * step] = chunk

        if split_flags[g]:
            prompt_ref, sample_ref = outs.pop(0), outs.pop(0)
            pl.when(in_prompt)(functools.partial(store, prompt_ref))
            pl.when(jnp.logical_not(in_prompt))(functools.partial(store, sample_ref))
        else:
            store(outs.pop(0))
        off += width


def _project(x, w, widths, rope_flags=None, tables=None, table_index=None, split_flags=None, n_prompt=None,
             tm=512, name="proj"):
    n, d = x.shape
    rope_flags = rope_flags or (False,) * len(widths)
    split_flags = split_flags or (False,) * len(widths)
    assert n % tm == 0 and w.shape == (d, sum(widths))
    prompt_tiles = n // tm
    if any(split_flags):
        assert n_prompt % tm == 0 and 0 < n_prompt < n
        prompt_tiles = n_prompt // tm
    in_specs = [pl.BlockSpec((tm, d), lambda i: (i, 0)),
                pl.BlockSpec(w.shape, lambda i: (0, 0))]
    args = [x, w]
    if any(rope_flags):
        for t in tables:
            in_specs.append(pl.BlockSpec((tm, LANES), lambda i: (table_index(i), 0)))
            args.append(t)
    out_shape, out_specs = [], []
    for wd, split in zip(widths, split_flags):
        if split:
            out_shape += [jax.ShapeDtypeStruct((n_prompt, wd), F32), jax.ShapeDtypeStruct((n - n_prompt, wd), F32)]
            out_specs += [pl.BlockSpec((tm, wd), lambda i: (jnp.minimum(i, prompt_tiles - 1), 0)),
                          pl.BlockSpec((tm, wd), lambda i: (jnp.maximum(i - prompt_tiles, 0), 0))]
        else:
            out_shape.append(jax.ShapeDtypeStruct((n, wd), F32))
            out_specs.append(pl.BlockSpec((tm, wd), lambda i: (i, 0)))
    flat = pl.pallas_call(
        functools.partial(_proj_body, widths=tuple(widths), rope_flags=tuple(rope_flags),
                          split_flags=tuple(split_flags), prompt_tiles=prompt_tiles),
        out_shape=out_shape,
        grid=(n // tm,),
        in_specs=in_specs,
        out_specs=out_specs,
        compiler_params=_params(("arbitrary",)),
        name=name,
    )(*args)
    flat = list(flat)
    return [(flat.pop(0), flat.pop(0)) if split else flat.pop(0) for split in split_flags]


def _rope_tables(pos, rope_lanes):
    half = HEAD_DIM // 2
    inv_freq = ROPE_THETA ** (-jnp.arange(half, dtype=F32) / half)
    ang = pos.astype(F32)[:, None] * inv_freq[None, :]
    cos, sin = jnp.cos(ang), jnp.sin(ang)
    reps = LANES // HEAD_DIM
    zero = jnp.zeros_like(sin)
    cos_t = jnp.tile(jnp.concatenate([cos, cos], -1), (1, reps))
    sin_lo = jnp.tile(jnp.concatenate([-sin, zero], -1), (1, reps))
    sin_hi = jnp.tile(jnp.concatenate([zero, sin], -1), (1, reps))
    keep = (jnp.arange(LANES) < rope_lanes)[None, :]
    return (jnp.where(keep, cos_t, 1.0), jnp.where(keep, sin_lo, 0.0), jnp.where(keep, sin_hi, 0.0))


def _head_mask(rows, head_in_pair):
    lane = lax.broadcasted_iota(I32, (rows, LANES), 1)
    lo = head_in_pair * HEAD_DIM
    return (lane >= lo) & (lane < lo + HEAD_DIM)


def _pair_merge(o_even, o_odd):
    lane = lax.broadcasted_iota(I32, o_even.shape, 1)
    return jnp.where(lane < HEAD_DIM, o_even, o_odd)


def _band_sample_body(q_ref, kn_ref, vn_ref, kh_ref, vh_ref, bias_ref, o_ref, kwin, vwin):
    kwin[0:BAND, :] = kh_ref[...].astype(BF16)
    vwin[0:BAND, :] = vh_ref[...].astype(BF16)
    kwin[BAND:, :] = kn_ref[...].astype(BF16)
    vwin[BAND:, :] = vn_ref[...].astype(BF16)
    q = q_ref[...] * (HEAD_DIM ** -0.5)
    for hp in range(N_HEADS // 2):
        cols = slice(hp * LANES, (hp + 1) * LANES)
        outs = []
        for hh in range(2):
            qm = jnp.where(_head_mask(CHUNK, hh), q[:, cols], 0.0).astype(BF16)
            s = lax.dot_general(qm, kwin[:, cols], _NT_DIMS, preferred_element_type=F32) + bias_ref[2 * hp + hh]
            p = jnp.exp(s - jnp.max(s, axis=1, keepdims=True))
            denom = jnp.sum(p, axis=1, keepdims=True)
            outs.append(jnp.dot(p.astype(BF16), vwin[:, cols], preferred_element_type=F32) / denom)
        o_ref[:, cols] = _pair_merge(outs[0], outs[1])


def _band_prompt_body(q_ref, k_ref, v_ref, bias_ref, o_ref, k_even, k_odd, v_t, q_t, s_sc, p_sc, o_t):
    t = pl.program_id(1)
    nq = q_ref.shape[0]
    win = BAND + nq
    seq = k_ref.shape[0]
    stage_rows = 256

    @pl.when(t == 0)
    def _():
        k_even[0:BAND, :] = jnp.zeros((BAND, WIDTH), BF16)
        k_odd[0:BAND, :] = jnp.zeros((BAND, WIDTH), BF16)
        v_t[:, 0:BAND] = jnp.zeros((WIDTH, BAND), BF16)
        for r in range(0, seq, stage_rows):
            ke, ko = _split_pair_lanes(k_ref[r:r + stage_rows, :])
            k_even[BAND + r:BAND + r + stage_rows, :] = ke.astype(BF16)
            k_odd[BAND + r:BAND + r + stage_rows, :] = ko.astype(BF16)
            v_t[:, BAND + r:BAND + r + stage_rows] = v_ref[r:r + stage_rows, :].T.astype(BF16)

    start = pl.multiple_of(t * nq, nq)
    window = pl.ds(start, win)
    q_t[...] = (q_ref[...] * (HEAD_DIM ** -0.5)).T.astype(BF16)
    before_start = jnp.where(lax.broadcasted_iota(I32, (win, nq), 0) >= BAND - t * nq, 0.0, NEG_BIG)
    for h in range(N_HEADS):
        pair = slice((h // 2) * LANES, (h // 2 + 1) * LANES)
        k_ref_h = k_even if h % 2 == 0 else k_odd
        s_sc[h * win:(h + 1) * win, :] = (jnp.dot(k_ref_h[window, pair], q_t[pair, :], preferred_element_type=F32)
                                          + bias_ref[h] + before_start)
    for h in range(N_HEADS):
        s = s_sc[h * win:(h + 1) * win, :]
        p = jnp.exp(s - jnp.max(s, axis=0, keepdims=True))
        p_sc[h * win:(h + 1) * win, :] = (p / jnp.sum(p, axis=0, keepdims=True)).astype(BF16)
    for h in range(N_HEADS):
        dims = slice(h * HEAD_DIM, (h + 1) * HEAD_DIM)
        o_t[dims, :] = jnp.dot(v_t[dims, window], p_sc[h * win:(h + 1) * win, :], preferred_element_type=F32)
    o_ref[...] = o_t[...].T


def _band_bias(table, n_rows):
    win = BAND + n_rows
    assert n_rows <= MAX_REL <= BAND
    n_heads = table.shape[0]
    g = jnp.concatenate([jnp.broadcast_to(table[:, 2 * MAX_REL:], (n_heads, win - MAX_REL)),
                         table[:, MAX_REL - n_rows:2 * MAX_REL][:, ::-1],
                         jnp.zeros((n_heads, 1), table.dtype)], axis=1).astype(F32)
    period = g.shape[1]
    skew = jnp.tile(g, (1, n_rows))[:, :n_rows * (period - 1)].reshape(n_heads, n_rows, period - 1)
    return skew[:, :, n_rows - 1:n_rows - 1 + win]


def _band_prompt(q, k, v, table, n_batch, seq, q_rows=128):
    nt = seq // q_rows
    win = BAND + q_rows
    assert seq % q_rows == 0 and q_rows % CHUNK == 0
    key_chunk = jnp.arange(win)[:, None] // CHUNK - BAND // CHUNK
    query_chunk = jnp.arange(q_rows)[None, :] // CHUNK
    visible = (key_chunk <= query_chunk) & (key_chunk >= query_chunk - BAND // CHUNK)
    bias = jnp.where(visible[None], jnp.swapaxes(_band_bias(table, q_rows), 1, 2), NEG_BIG)
    tile_spec = pl.BlockSpec((q_rows, WIDTH), lambda b, t: (b * nt + t, 0))
    seq_spec = pl.BlockSpec((seq, WIDTH), lambda b, t: (b, 0))
    return pl.pallas_call(
        _band_prompt_body,
        out_shape=jax.ShapeDtypeStruct((n_batch * seq, WIDTH), F32),
        grid=(n_batch, nt),
        in_specs=[tile_spec, seq_spec, seq_spec, pl.BlockSpec(bias.shape, lambda b, t: (0, 0, 0))],
        out_specs=tile_spec,
        scratch_shapes=[pltpu.VMEM((BAND + seq, WIDTH), BF16), pltpu.VMEM((BAND + seq, WIDTH), BF16),
                        pltpu.VMEM((WIDTH, BAND + seq), BF16), pltpu.VMEM((WIDTH, q_rows), BF16),
                        pltpu.VMEM((N_HEADS * win, q_rows), F32), pltpu.VMEM((N_HEADS * win, q_rows), BF16),
                        pltpu.VMEM((WIDTH, q_rows), F32)],
        compiler_params=_params(("parallel", "arbitrary")),
        name="band_prompt",
    )(q, k, v, bias)


def _band_sample(q, k, v, k_hist, v_hist, layer, bias, n_batch, row0):
    blk0 = row0 // CHUNK
    hist_spec = pl.BlockSpec((None, None, BAND, WIDTH), lambda b: (layer, b, 0, 0))
    new_spec = pl.BlockSpec((CHUNK, WIDTH), lambda b: (b, 0))
    return pl.pallas_call(
        _band_sample_body,
        out_shape=jax.ShapeDtypeStruct((n_batch * CHUNK, WIDTH), F32),
        grid=(n_batch,),
        in_specs=[pl.BlockSpec((CHUNK, WIDTH), lambda b: (blk0 + b, 0)), new_spec, new_spec, hist_spec, hist_spec,
                  pl.BlockSpec(bias.shape, lambda b: (0, 0, 0))],
        out_specs=new_spec,
        scratch_shapes=[pltpu.VMEM((BAND + CHUNK, WIDTH), BF16), pltpu.VMEM((BAND + CHUNK, WIDTH), BF16)],
        compiler_params=_params(("parallel",)),
        name="band_sample",
    )(q, k, v, k_hist, v_hist, bias)


def _mem_body(q_ref, k_ref, v_ref, o_ref):
    scale = M_HEAD_DIM ** -0.5
    for h in range(M_HEADS):
        cols = slice(h * M_HEAD_DIM, (h + 1) * M_HEAD_DIM)
        s = lax.dot_general(q_ref[:, cols].astype(BF16), k_ref[:, cols].astype(BF16), _NT_DIMS,
                            preferred_element_type=F32) * scale
        p = jnp.exp(s - jnp.max(s, axis=1, keepdims=True))
        denom = jnp.sum(p, axis=1, keepdims=True)
        o = jnp.dot(p.astype(BF16), v_ref[:, cols].astype(BF16), preferred_element_type=F32)
        o_ref[:, cols] = o / denom


def _mem_attn(q, mk, mv, kv_index, n_batch, rows_per_batch, row0, tq):
    width = M_HEADS * M_HEAD_DIM
    per = rows_per_batch // tq
    blk0 = row0 // tq
    kv_block = (None,) * (mk.ndim - 2) + (N_MEM, width)
    kv_spec = pl.BlockSpec(kv_block, lambda b, t: kv_index(b))
    return pl.pallas_call(
        _mem_body,
        out_shape=jax.ShapeDtypeStruct((n_batch * rows_per_batch, width), F32),
        grid=(n_batch, per),
        in_specs=[pl.BlockSpec((tq, width), lambda b, t: (blk0 + b * per + t, 0)), kv_spec, kv_spec],
        out_specs=pl.BlockSpec((tq, width), lambda b, t: (b * per + t, 0)),
        compiler_params=_params(("parallel", "parallel")),
        name="mem_attn",
    )(q, mk, mv)


def _split_pair_lanes(x):
    lane = lax.broadcasted_iota(I32, x.shape, 1)
    even = (lane & (LANES - 1)) < HEAD_DIM
    return jnp.where(even, x, 0.0), jnp.where(even, 0.0, x)


def _dsa_body(*refs, sample, key_block, n_keys, n_sel):
    n_in = 8 if sample else 6
    if sample:
        q_ref, qi_ref, kiwq_ref, kn_ref, vn_ref, kc_ref, vc_ref, kic_ref = refs[:n_in]
    else:
        q_ref, qi_ref, kiwq_ref, k_ref, v_ref, kiw_ref = refs[:n_in]
    o_ref = refs[n_in]
    (k_even, k_odd, v_t, ki_lo, ki_hi, keybuf, maskbuf, q_t, qi_t, kiwq_t, m_sc, l_sc, a_sc, acc_t,
     s_sc, p_sc) = refs[n_in + 1:]
    kb = key_block
    n_pad = k_even.shape[0]
    nq = q_ref.shape[0]
    stage_rows = 256

    def stage_kv(row0, k_new, v_new):
        n_new = k_new.shape[0]
        ke, ko = _split_pair_lanes(k_new)
        k_even[row0:row0 + n_new, :] = ke.astype(BF16)
        k_odd[row0:row0 + n_new, :] = ko.astype(BF16)
        v_t[:, row0:row0 + n_new] = v_new.T.astype(BF16)

    def stage_index_keys(row0, kiw):
        lane = lax.broadcasted_iota(I32, kiw.shape, 1)
        ki_lo[row0:row0 + kiw.shape[0], :] = jnp.where(lane < HEAD_DIM, kiw, 0.0).astype(BF16)
        ki_hi[row0:row0 + kiw.shape[0], :] = jnp.where(lane < HEAD_DIM, 0.0, pltpu.roll(kiw, HEAD_DIM, 1)).astype(BF16)

    def stage_keys():
        if sample:
            n_hist = kc_ref.shape[0]
            for r in range(0, n_hist, stage_rows):
                stage_kv(r, kc_ref[r:r + stage_rows, :], vc_ref[r:r + stage_rows, :])
            kic = kic_ref[...]
            zeros = jnp.zeros_like(kic)
            ki_lo[0:n_hist, :] = jnp.concatenate([kic, zeros], axis=1).astype(BF16)
            ki_hi[0:n_hist, :] = jnp.concatenate([zeros, kic], axis=1).astype(BF16)
            stage_kv(n_hist, kn_ref[...], vn_ref[...])
            stage_index_keys(n_hist, kiwq_ref[...])
            if n_pad > n_keys:
                for ref in (k_even, k_odd, ki_lo, ki_hi):
                    ref[n_keys:, :] = jnp.zeros((n_pad - n_keys, ref.shape[1]), BF16)
                v_t[:, n_keys:] = jnp.zeros((v_t.shape[0], n_pad - n_keys), BF16)
        else:
            for r in range(0, n_keys, stage_rows):
                stage_kv(r, k_ref[r:r + stage_rows, :], v_ref[r:r + stage_rows, :])
            stage_index_keys(0, kiw_ref[...])

    if sample:
        stage_keys()
        limit = n_keys
        nb = n_pad // kb
        search_bits = 32
    else:
        t = pl.program_id(1)
        pl.when(t == 0)(stage_keys)
        chunk = t * (nq // CHUNK) + lax.broadcasted_iota(I32, (1, nq), 1) // CHUNK
        limit = (chunk + 1) * CHUNK
        limit_max = (t + 1) * nq
        nb = (limit_max + kb - 1) // kb
        search_bits = jnp.where(limit_max <= n_sel, 0, 32)

    def block_rows(j):
        return pl.ds(pl.multiple_of(j * kb, kb), kb)

    def key_pos(j):
        return j * kb + lax.broadcasted_iota(I32, (kb, nq), 0)

    q_t[...] = (q_ref[...] * (HEAD_DIM ** -0.5)).T.astype(BF16)
    qi_t[...] = qi_ref[...].T.astype(BF16)
    kiwq_t[...] = kiwq_ref[...].T

    def score_block(j, carry):
        acc = jnp.zeros((kb, nq), F32)
        for h in range(N_HEADS):
            pair = slice((h // 2) * LANES, (h // 2 + 1) * LANES)
            ki_ref = ki_lo if h % 2 == 0 else ki_hi
            d = jnp.dot(ki_ref[block_rows(j), :], qi_t[pair, :], preferred_element_type=F32)
            acc = acc + kiwq_t[HEAD_DIM + h:HEAD_DIM + h + 1, :] * jnp.maximum(d, 0.0)
        acc = acc + 0.0
        acc = jnp.where(key_pos(j) < limit, acc, -jnp.inf)
        bits = lax.bitcast_convert_type(acc, I32)
        keybuf[block_rows(j), :] = bits ^ ((bits >> 31) & 0x7FFFFFFF)
        return carry

    lax.fori_loop(0, nb, score_block, 0)

    def count(pred_fn):
        def body(j, part):
            hit = jnp.where(pred_fn(keybuf[block_rows(j), :], j), 1.0, 0.0)
            return part + jnp.sum(hit.reshape(kb // 8, 8, nq), axis=0)
        part = lax.fori_loop(0, nb, body, jnp.zeros((8, nq), F32))
        return jnp.sum(part, axis=0, keepdims=True)

    def bit_step(i, prefix):
        cand = prefix | lax.shift_left(jnp.int32(1), 31 - i)
        thr_i = cand ^ INT_MIN
        cnt = count(lambda kblk, j: kblk >= thr_i)
        return jnp.where(cnt >= n_sel, cand, prefix)

    prefix = lax.fori_loop(0, search_bits, bit_step, jnp.zeros((1, nq), I32))
    thr = prefix ^ INT_MIN
    above = count(lambda kblk, j: kblk > thr)
    at_least = count(lambda kblk, j: kblk >= thr)
    need = n_sel - above
    surplus = at_least > n_sel

    pos_bits = max(1, (n_pad - 1).bit_length() + 1)
    any_surplus = jnp.max(jnp.where(surplus, 1.0, 0.0)) > 0.0

    def pos_step(i, bound):
        cand = bound | lax.shift_left(jnp.int32(1), pos_bits - 1 - i)
        cnt = count(lambda kblk, j: (kblk == thr) & (key_pos(j) < cand))
        return jnp.where(cnt <= need, cand, bound)

    bound = lax.fori_loop(0, jnp.where(any_surplus, pos_bits, 0), pos_step, jnp.zeros((1, nq), I32))
    bound = jnp.where(surplus, bound, 2 ** pos_bits - 1)

    def mask_block(j, carry):
        kblk = keybuf[block_rows(j), :]
        pos = key_pos(j)
        sel = ((kblk > thr) | ((kblk == thr) & (pos < bound))) & (pos < limit)
        maskbuf[block_rows(j), :] = jnp.where(sel, 0.0, NEG_BIG)
        return carry

    lax.fori_loop(0, nb, mask_block, 0)

    m_sc[...] = jnp.full(m_sc.shape, NEG_BIG, F32)
    l_sc[...] = jnp.zeros(l_sc.shape, F32)
    acc_t[...] = jnp.zeros(acc_t.shape, F32)

    def attend(j, carry):
        mask = maskbuf[block_rows(j), :]
        for h in range(N_HEADS):
            pair = slice((h // 2) * LANES, (h // 2 + 1) * LANES)
            k_ref_h = k_even if h % 2 == 0 else k_odd
            s_sc[h * kb:(h + 1) * kb, :] = jnp.dot(k_ref_h[block_rows(j), pair], q_t[pair, :],
                                                   preferred_element_type=F32) + mask
        for h in range(N_HEADS):
            stat = slice(h * 8, (h + 1) * 8)
            s = s_sc[h * kb:(h + 1) * kb, :]
            m_prev = m_sc[stat, :]
            m_new = jnp.maximum(m_prev, jnp.max(s, axis=0, keepdims=True))
            alpha = jnp.exp(m_prev - m_new)
            p = jnp.exp(s - m_new[0:1, :])
            l_sc[stat, :] = alpha * l_sc[stat, :] + jnp.sum(p, axis=0, keepdims=True)
            m_sc[stat, :] = m_new
            a_sc[stat, :] = alpha
            p_sc[h * kb:(h + 1) * kb, :] = p.astype(BF16)
        for h in range(N_HEADS):
            dims = slice(h * HEAD_DIM, (h + 1) * HEAD_DIM)
            pv = jnp.dot(v_t[dims, block_rows(j)], p_sc[h * kb:(h + 1) * kb, :], preferred_element_type=F32)
            acc_t[dims, :] = a_sc[h * 8:h * 8 + 1, :] * acc_t[dims, :] + pv
        return carry

    lax.fori_loop(0, nb, attend, 0)
    for h in range(N_HEADS):
        dims = slice(h * HEAD_DIM, (h + 1) * HEAD_DIM)
        acc_t[dims, :] = acc_t[dims, :] / l_sc[h * 8:h * 8 + 1, :]
    o_ref[...] = acc_t[...].T


def _dsa_scratch(nq, n_pad, key_block):
    return [pltpu.VMEM((n_pad, WIDTH), BF16), pltpu.VMEM((n_pad, WIDTH), BF16),
            pltpu.VMEM((WIDTH, n_pad), BF16),
            pltpu.VMEM((n_pad, LANES), BF16), pltpu.VMEM((n_pad, LANES), BF16),
            pltpu.VMEM((n_pad, nq), I32), pltpu.VMEM((n_pad, nq), F32),
            pltpu.VMEM((WIDTH, nq), BF16), pltpu.VMEM((WIDTH, nq), BF16),
            pltpu.VMEM((LANES, nq), F32),
            pltpu.VMEM((N_HEADS * 8, nq), F32), pltpu.VMEM((N_HEADS * 8, nq), F32),
            pltpu.VMEM((N_HEADS * 8, nq), F32),
            pltpu.VMEM((WIDTH, nq), F32),
            pltpu.VMEM((N_HEADS * key_block, nq), F32),
            pltpu.VMEM((N_HEADS * key_block, nq), BF16)]


def _dsa_prompt(q, k, v, qi, kiw, n_batch, seq, key_block=256, q_rows=256):
    nt = seq // q_rows
    key_block = min(key_block, seq)
    assert seq % key_block == 0 and seq % q_rows == 0 and q_rows % CHUNK == 0
    n_sel = min(TOPK_KEYS, seq // 4)
    tile_spec = lambda width: pl.BlockSpec((q_rows, width), lambda b, t: (b * nt + t, 0))
    seq_spec = lambda width: pl.BlockSpec((seq, width), lambda b, t: (b, 0))
    return pl.pallas_call(
        functools.partial(_dsa_body, sample=False, key_block=key_block, n_keys=seq, n_sel=n_sel),
        out_shape=jax.ShapeDtypeStruct((n_batch * seq, WIDTH), F32),
        grid=(n_batch, nt),
        in_specs=[tile_spec(WIDTH), tile_spec(WIDTH), tile_spec(LANES),
                  seq_spec(WIDTH), seq_spec(WIDTH), seq_spec(LANES)],
        out_specs=tile_spec(WIDTH),
        scratch_shapes=_dsa_scratch(q_rows, seq, key_block),
        compiler_params=_params(("parallel", "arbitrary")),
        name="dsa_prompt",
    )(q, qi, kiw, k, v, kiw)


def _dsa_sample(q, k, v, qi, kiw, k_cache, v_cache, ki_cache, layer, n_batch, row0, key_block=768):
    past = k_cache.shape[2]
    n_keys = past + CHUNK
    key_block = min(key_block, -(-n_keys // 256) * 256)
    n_pad = -(-n_keys // key_block) * key_block
    n_sel = min(TOPK_KEYS, n_keys // 4)
    blk0 = row0 // CHUNK
    query_spec = pl.BlockSpec((CHUNK, WIDTH), lambda b: (blk0 + b, 0))
    new_spec = lambda width: pl.BlockSpec((CHUNK, width), lambda b: (b, 0))
    cache_spec = lambda width: pl.BlockSpec((None, None, past, width), lambda b: (layer, b, 0, 0))
    return pl.pallas_call(
        functools.partial(_dsa_body, sample=True, key_block=key_block, n_keys=n_keys, n_sel=n_sel),
        out_shape=jax.ShapeDtypeStruct((n_batch * CHUNK, WIDTH), F32),
        grid=(n_batch,),
        in_specs=[query_spec, query_spec, new_spec(LANES), new_spec(WIDTH), new_spec(WIDTH),
                  cache_spec(WIDTH), cache_spec(WIDTH), cache_spec(HEAD_DIM)],
        out_specs=new_spec(WIDTH),
        scratch_shapes=_dsa_scratch(CHUNK, n_pad, key_block),
        compiler_params=_params(("parallel",)),
        name="dsa_sample",
    )(q, qi, kiw, k, v, k_cache, v_cache, ki_cache)


def _layer_norm(z, g, b):
    mu = jnp.mean(z, axis=-1, keepdims=True)
    zc = z - mu
    var = jnp.mean(zc * zc, axis=-1, keepdims=True)
    return zc * lax.rsqrt(var + LN_EPS) * g + b


HI_HALF = -65536


def _pack_rows(x):
    half = x.shape[1] // 2
    lo = lax.bitcast_convert_type(x[:, :half].astype(BF16).astype(F32), I32)
    hi = lax.bitcast_convert_type(x[:, half:].astype(BF16).astype(F32), I32)
    return (hi & HI_HALF) | lax.shift_right_logical(lo, 16)


def _unpack_rows(w):
    lo = lax.bitcast_convert_type(lax.shift_left(w, 16), F32).astype(BF16)
    hi = lax.bitcast_convert_type(w & HI_HALF, F32).astype(BF16)
    return lo, hi


def _merge_body(x_ref, ap_ref, as_ref, bp_ref, bs_ref, mp_ref, ms_ref, gates_ref, wbr_ref, wout_ref, g_ref, beta_ref,
                o_ref, ow_ref, *, alpha, prompt_tiles):
    d = x_ref.shape[1]
    in_prompt = pl.program_id(0) < prompt_tiles
    mix = None
    for n, (p_ref, s_ref) in enumerate(((ap_ref, as_ref), (bp_ref, bs_ref), (mp_ref, ms_ref))):
        branch = jnp.where(in_prompt, p_ref[...], s_ref[...])
        proj = jnp.dot(branch.astype(BF16), wbr_ref[n], preferred_element_type=F32)
        term = jax.nn.sigmoid(gates_ref[:, n * d:(n + 1) * d]) * proj
        mix = term if mix is None else mix + term
    y = jnp.dot(mix.astype(BF16), wout_ref[...], preferred_element_type=F32)
    out = _layer_norm(alpha * x_ref[...] + y, g_ref[...], beta_ref[...])
    o_ref[...] = out
    ow_ref[...] = _pack_rows(out)


def _merge(x, branches, gates, w_branch, w_out, g, beta, alpha, tm=256):
    n, d = x.shape
    prompt_tiles = branches[0][0].shape[0] // tm
    assert all(p.shape[0] == prompt_tiles * tm and (n - p.shape[0]) == s.shape[0] and s.shape[0] % tm == 0
               for p, s in branches)
    packed = jax.eval_shape(_pack_rows, jax.ShapeDtypeStruct((tm, d), F32))
    row = lambda width: pl.BlockSpec((tm, width), lambda i: (i, 0))
    full = lambda arr: pl.BlockSpec(arr.shape, lambda i: (0,) * arr.ndim)
    prompt_row = lambda width: pl.BlockSpec((tm, width), lambda i: (jnp.minimum(i, prompt_tiles - 1), 0))
    sample_row = lambda width: pl.BlockSpec((tm, width), lambda i: (jnp.maximum(i - prompt_tiles, 0), 0))
    branch_specs, branch_args = [], []
    for p, s in branches:
        branch_specs += [prompt_row(p.shape[1]), sample_row(s.shape[1])]
        branch_args += [p, s]
    return pl.pallas_call(
        functools.partial(_merge_body, alpha=alpha, prompt_tiles=prompt_tiles),
        out_shape=[jax.ShapeDtypeStruct((n, d), F32), jax.ShapeDtypeStruct((n, packed.shape[1]), packed.dtype)],
        grid=(n // tm,),
        in_specs=[row(d)] + branch_specs + [row(gates.shape[1]), full(w_branch), full(w_out), full(g), full(beta)],
        out_specs=[row(d), row(packed.shape[1])],
        compiler_params=_params(("parallel",)),
        name="merge",
    )(x, *branch_args, gates, w_branch, w_out, g, beta)


def _first_max(vals, idx, n):
    top = jnp.max(vals, axis=0, keepdims=True)
    arg = jnp.min(jnp.where(vals == top, idx, n), axis=0, keepdims=True)
    return top, arg


def _router_body(x_ref, wt_ref, bias_ref, eidx_ref, gw_ref, rank_ref, counts_ref, running):
    tm = x_ref.shape[0]

    @pl.when(pl.program_id(0) == 0)
    def _():
        running[...] = jnp.zeros(running.shape, F32)

    per = N_EXPERTS // N_GROUPS
    logits = lax.dot_general(wt_ref[...], x_ref[...].astype(BF16), _NT_DIMS, preferred_element_type=F32)
    s = jax.nn.sigmoid(logits)
    sb = s + bias_ref[...]
    in_group = lax.broadcasted_iota(I32, (per, tm), 0).astype(F32)
    group_scores = []
    for g in range(N_GROUPS):
        blk = sb[g * per:(g + 1) * per, :]
        top1, arg1 = _first_max(blk, in_group, per)
        top2 = jnp.max(jnp.where(in_group == arg1, -jnp.inf, blk), axis=0, keepdims=True)
        group_scores.append(top1 + top2)
    gs = jnp.concatenate(group_scores, axis=0)
    gidx = lax.broadcasted_iota(I32, (N_GROUPS, tm), 0).astype(F32)
    chosen = jnp.zeros((N_GROUPS, tm), F32)
    for _ in range(TOPK_GROUPS):
        _, arg = _first_max(gs, gidx, N_GROUPS)
        hit = gidx == arg
        chosen = jnp.where(hit, 1.0, chosen)
        gs = jnp.where(hit, -jnp.inf, gs)
    cand = jnp.concatenate(
        [jnp.where(chosen[g:g + 1, :] > 0.0, sb[g * per:(g + 1) * per, :], -jnp.inf) for g in range(N_GROUPS)],
        axis=0)
    eidx = lax.broadcasted_iota(I32, (N_EXPERTS, tm), 0).astype(F32)
    picks, weights, hits = [], [], []
    for _ in range(EXPERT_TOPK):
        _, arg = _first_max(cand, eidx, N_EXPERTS)
        hit = eidx == arg
        weights.append(jnp.sum(jnp.where(hit, s, 0.0), axis=0, keepdims=True))
        picks.append(arg)
        hits.append(hit)
        cand = jnp.where(hit, -jnp.inf, cand)
    total = weights[0]
    for w in weights[1:]:
        total = total + w
    taken = jnp.zeros((N_EXPERTS, tm), F32)
    for hit in hits:
        taken = jnp.where(hit, 1.0, taken)
    earlier = (lax.broadcasted_iota(I32, (tm, tm), 0) < lax.broadcasted_iota(I32, (tm, tm), 1))
    before = jnp.dot(taken.astype(BF16), jnp.where(earlier, 1.0, 0.0).astype(BF16),
                     preferred_element_type=F32) + running[...]
    for r in range(EXPERT_TOPK):
        eidx_ref[r:r + 1, :] = picks[r].astype(I32)
        gw_ref[r:r + 1, :] = weights[r] / total * ROUTE_SCALE
        rank_ref[r:r + 1, :] = jnp.sum(jnp.where(hits[r], before, 0.0), axis=0, keepdims=True).astype(I32)
    running[...] = running[...] + jnp.sum(taken, axis=1, keepdims=True)
    counts_ref[...] = running[...]


def _router(x, w_router_t, b_router_col, tm=256):
    n, d = x.shape
    slot = pl.BlockSpec((EXPERT_TOPK, tm), lambda i: (0, i))
    return pl.pallas_call(
        _router_body,
        out_shape=[jax.ShapeDtypeStruct((EXPERT_TOPK, n), I32), jax.ShapeDtypeStruct((EXPERT_TOPK, n), F32),
                   jax.ShapeDtypeStruct((EXPERT_TOPK, n), I32), jax.ShapeDtypeStruct((N_EXPERTS, 1), F32)],
        grid=(n // tm,),
        in_specs=[pl.BlockSpec((tm, d), lambda i: (i, 0)),
                  pl.BlockSpec(w_router_t.shape, lambda i: (0, 0)),
                  pl.BlockSpec(b_router_col.shape, lambda i: (0, 0))],
        out_specs=[slot, slot, slot, pl.BlockSpec((N_EXPERTS, 1), lambda i: (0, 0))],
        scratch_shapes=[pltpu.VMEM((N_EXPERTS, 1), F32)],
        compiler_params=_params(("arbitrary",)),
        name="router",
    )(x, w_router_t, b_router_col)


def _dest_body(eidx_ref, rank_ref, start_ref, dest_ref):
    tm = eidx_ref.shape[1]
    experts = lax.broadcasted_iota(I32, (N_EXPERTS, tm), 0)
    for r in range(EXPERT_TOPK):
        base = jnp.sum(jnp.where(experts == eidx_ref[r:r + 1, :], start_ref[...], 0.0), axis=0, keepdims=True)
        dest_ref[r:r + 1, :] = base.astype(I32) + rank_ref[r:r + 1, :]


def _dest_rows(eidx_t, rank_t, start_col, tm=512):
    n = eidx_t.shape[1]
    slot = pl.BlockSpec((EXPERT_TOPK, tm), lambda i: (0, i))
    return pl.pallas_call(
        _dest_body,
        out_shape=jax.ShapeDtypeStruct((EXPERT_TOPK, n), I32),
        grid=(n // tm,),
        in_specs=[slot, slot, pl.BlockSpec(start_col.shape, lambda i: (0, 0))],
        out_specs=slot,
        compiler_params=_params(("parallel",)),
        name="dest_rows",
    )(eidx_t, rank_t, start_col)


def _scatter_body(dest_ref, x_ref, _, o_ref, sem):
    tm = x_ref.shape[0]

    def row_copy(t, k):
        return pltpu.make_async_copy(x_ref.at[pl.ds(t, 1)], o_ref.at[pl.ds(dest_ref[k, t], 1)], sem)

    def issue(t, carry):
        for k in range(EXPERT_TOPK):
            row_copy(t, k).start(priority=k % 2)
        return carry

    lax.fori_loop(0, tm, issue, 0)
    for k in range(EXPERT_TOPK):
        pltpu.make_async_copy(x_ref, o_ref.at[pl.ds(0, tm)], sem).wait()


def _scatter_rows(xw, dest_t, rows, tm=256):
    n, width = xw.shape
    return pl.pallas_call(
        _scatter_body,
        out_shape=jax.ShapeDtypeStruct((rows, width), xw.dtype),
        grid=(n // tm,),
        in_specs=[pl.BlockSpec((EXPERT_TOPK, tm), lambda i: (0, i), memory_space=pltpu.SMEM),
                  pl.BlockSpec((tm, width), lambda i: (i, 0)),
                  pl.BlockSpec(memory_space=pl.ANY)],
        out_specs=pl.BlockSpec(memory_space=pl.ANY),
        scratch_shapes=[pltpu.SemaphoreType.DMA],
        input_output_aliases={2: 0},
        compiler_params=_params(("arbitrary",)),
        name="scatter_rows",
    )(dest_t, xw, jnp.zeros((rows, width), xw.dtype))


def _experts_body(first_block_ref, x_hbm, wgu_ref, wdn_ref, o_hbm, wgu_bf, wdn_bf, xbuf, obuf, in_sems, out_sems):
    e = pl.program_id(0)
    blk0 = first_block_ref[e]
    nb = first_block_ref[e + 1] - blk0
    n_blocks = o_hbm.shape[0] // EXPERT_BLOCK

    def block_rows(j):
        return pl.ds(pl.multiple_of(j * EXPERT_BLOCK, EXPERT_BLOCK), EXPERT_BLOCK)

    def in_copy(j, slot):
        return pltpu.make_async_copy(x_hbm.at[block_rows(j)], xbuf.at[slot], in_sems.at[slot])

    def out_copy(j, slot):
        return pltpu.make_async_copy(obuf.at[slot], o_hbm.at[block_rows(j)], out_sems.at[slot])

    @pl.when(nb > 0)
    def _():
        wgu_bf[...] = wgu_ref[...].astype(BF16)
        wdn_bf[...] = wdn_ref[...].astype(BF16)
        in_copy(blk0, 0).start()

        def step(j, carry):
            slot = j % 2

            @pl.when(j + 1 < nb)
            def _():
                in_copy(blk0 + j + 1, 1 - slot).start()

            in_copy(blk0 + j, slot).wait()

            @pl.when(j >= 2)
            def _():
                out_copy(blk0 + j - 2, slot).wait()

            x_lo, x_hi = _unpack_rows(xbuf[slot])
            half = x_lo.shape[1]
            h = (jnp.dot(x_lo, wgu_bf[0:half, :], preferred_element_type=F32)
                 + jnp.dot(x_hi, wgu_bf[half:, :], preferred_element_type=F32))
            ff = h.shape[1] // 2
            act = jax.nn.silu(h[:, :ff]) * h[:, ff:]
            obuf[slot] = jnp.dot(act.astype(BF16), wdn_bf[...], preferred_element_type=F32)
            out_copy(blk0 + j, slot).start()
            return carry

        lax.fori_loop(0, nb, step, 0)

        @pl.when(nb >= 2)
        def _():
            out_copy(blk0 + nb - 2, nb % 2).wait()

        out_copy(blk0 + nb - 1, (nb - 1) % 2).wait()

    @pl.when(e == pl.num_programs(0) - 1)
    def _():
        n_used = first_block_ref[e + 1]
        obuf[0] = jnp.zeros(obuf.shape[1:], F32)

        def fill(j, carry):
            out_copy(j, 0).start()
            return carry

        def drain(j, carry):
            out_copy(j, 0).wait()
            return carry

        lax.fori_loop(n_used, n_blocks, fill, 0)
        lax.fori_loop(n_used, n_blocks, drain, 0)


def _experts(x_rows, first_block, w_gu, w_dn, layer):
    rows, width = x_rows.shape
    n_experts, d, ff2 = w_gu.shape[-3:]
    grid_spec = pltpu.PrefetchScalarGridSpec(
        num_scalar_prefetch=1,
        grid=(n_experts,),
        in_specs=[pl.BlockSpec(memory_space=pl.ANY),
                  pl.BlockSpec((None, None, d, ff2), lambda e, fb: (layer, e, 0, 0)),
                  pl.BlockSpec((None, None, ff2 // 2, d), lambda e, fb: (layer, e, 0, 0))],
        out_specs=pl.BlockSpec(memory_space=pl.ANY),
        scratch_shapes=[pltpu.VMEM((d, ff2), BF16), pltpu.VMEM((ff2 // 2, d), BF16),
                        pltpu.VMEM((2, EXPERT_BLOCK, width), x_rows.dtype), pltpu.VMEM((2, EXPERT_BLOCK, d), F32),
                        pltpu.SemaphoreType.DMA((2,)), pltpu.SemaphoreType.DMA((2,))],
    )
    return pl.pallas_call(
        _experts_body,
        out_shape=jax.ShapeDtypeStruct((rows, d), F32),
        grid_spec=grid_spec,
        compiler_params=_params(("arbitrary",)),
        name="experts",
    )(first_block, x_rows, w_gu, w_dn)


def _block_plan(counts):
    counts = counts.reshape(-1).astype(I32)
    blocks = (counts + EXPERT_BLOCK - 1) // EXPERT_BLOCK
    first_block = jnp.concatenate([jnp.zeros((1,), I32), jnp.cumsum(blocks).astype(I32)])
    start_col = (first_block[:-1] * EXPERT_BLOCK).astype(F32)[:, None]
    return start_col, first_block


def _ffn_out_body(dest_ref, dest_next_ref, x_ref, gw_ref, yb_ref, wgu_ref, wdn_ref, g_ref, beta_ref, o_ref,
                  ybuf, sems, *, alpha):
    i = pl.program_id(0)
    n_steps = pl.num_programs(0)
    tm = x_ref.shape[0]
    slot = i % 2

    def gather(dests, to_slot):
        def issue(t, carry):
            for k in range(EXPERT_TOPK):
                pltpu.make_async_copy(yb_ref.at[pl.ds(dests[k, t], 1)], ybuf.at[to_slot, k, pl.ds(t, 1)],
                                      sems.at[to_slot]).start(priority=k % 2)
            return carry
        lax.fori_loop(0, tm, issue, 0)

    @pl.when(i == 0)
    def _():
        gather(dest_ref, 0)

    @pl.when(i + 1 < n_steps)
    def _():
        gather(dest_next_ref, 1 - slot)

    x = x_ref[...]
    h = jnp.dot(x.astype(BF16), wgu_ref[...], preferred_element_type=F32)
    ff = h.shape[1] // 2
    act = jax.nn.silu(h[:, :ff]) * h[:, ff:]
    shared = jnp.dot(act.astype(BF16), wdn_ref[...], preferred_element_type=F32)

    for k in range(EXPERT_TOPK):
        pltpu.make_async_copy(yb_ref.at[pl.ds(0, tm)], ybuf.at[slot, k], sems.at[slot]).wait()
    routed = None
    for k in range(EXPERT_TOPK):
        term = ybuf[slot, k] * gw_ref[:, k:k + 1]
        routed = term if routed is None else routed + term
    o_ref[...] = _layer_norm(alpha * x + (routed + shared), g_ref[...], beta_ref[...])


def _ffn_out(x, yb, dest_t, gw, ws_gu, ws_dn, g, beta, alpha, tm=256):
    n, d = x.shape
    n_tiles = n // tm
    row = pl.BlockSpec((tm, d), lambda i: (i, 0))
    full = lambda arr: pl.BlockSpec(arr.shape, lambda i: (0,) * arr.ndim)
    dest_spec = lambda index: pl.BlockSpec((EXPERT_TOPK, tm), index, memory_space=pltpu.SMEM)
    return pl.pallas_call(
        functools.partial(_ffn_out_body, alpha=alpha),
        out_shape=jax.ShapeDtypeStruct((n, d), F32),
        grid=(n_tiles,),
        in_specs=[dest_spec(lambda i: (0, i)), dest_spec(lambda i: (0, jnp.minimum(i + 1, n_tiles - 1))),
                  row, pl.BlockSpec((tm, EXPERT_TOPK), lambda i: (i, 0)),
                  pl.BlockSpec(memory_space=pl.ANY),
                  full(ws_gu), full(ws_dn), full(g), full(beta)],
        out_specs=row,
        scratch_shapes=[pltpu.VMEM((2, EXPERT_TOPK, tm, d), F32), pltpu.SemaphoreType.DMA((2,))],
        compiler_params=_params(("arbitrary",)),
        name="ffn_out",
    )(dest_t, dest_t, x, gw, yb, ws_gu, ws_dn, g, beta)


def _moe(x, xw, layer, w_router_t, b_router_col, w_exp_gate_up, w_exp_down, ws_gu, ws_dn, g, beta, alpha):
    n = x.shape[0]
    n_blocks = -(-n * EXPERT_TOPK // EXPERT_BLOCK) + N_EXPERTS
    eidx_t, gw_t, rank_t, counts = _router(x, w_router_t, b_router_col)
    start_col, first_block = _block_plan(counts)
    dest_t = _dest_rows(eidx_t, rank_t, start_col)
    x_rows = _scatter_rows(xw, dest_t, n_blocks * EXPERT_BLOCK)
    yb = _experts(x_rows, first_block, w_exp_gate_up, w_exp_down, layer)
    return _ffn_out(x, yb, dest_t, gw_t.T, ws_gu, ws_dn, g, beta, alpha)


def kernel(x_prompt, x_sample, mem_prompt, cache_a_k, cache_a_v, cache_b_k, cache_b_v, cache_b_idx_k,
           cache_mem_k, cache_mem_v, w_in, rel_bias, w_mem_kv, w_branch, w_out, ln_mix_g, ln_mix_b,
           w_router, b_router, w_exp_gate_up, w_exp_down, w_sh_gate_up, w_sh_down, ln_ffn_g, ln_ffn_b):
    n_batch, seq, d = x_prompt.shape
    s_batch, s_seq, _ = x_sample.shape
    depth = w_in.shape[0]
    past = cache_b_k.shape[2]
    assert s_seq == CHUNK and seq % CHUNK == 0 and cache_a_k.shape[2] == BAND and seq >= BAND
    alpha = (2 * depth) ** 0.25
    n_p, n_s = n_batch * seq, s_batch * s_seq
    n_tot = n_p + n_s
    tm = 512
    assert seq % tm == 0 and n_p % tm == 0 and n_s % tm == 0 and tm % CHUNK == 0

    x = jnp.concatenate([x_prompt.reshape(n_p, d), x_sample.reshape(n_s, d)])

    pos = jnp.concatenate([jnp.arange(seq), jnp.tile(past + jnp.arange(s_seq), tm // s_seq)])
    tab_heads = _rope_tables(pos, LANES)
    tab_kiw = _rope_tables(pos, HEAD_DIM)
    p_tiles, seq_tiles = n_p // tm, seq // tm
    table_index = lambda i: jnp.where(i < p_tiles, i % seq_tiles, seq_tiles)

    wd = WIDTH
    c_qi, c_ki, c_wi, c_qm, c_gates = 6 * wd, 7 * wd, 7 * wd + HEAD_DIM, 7 * wd + HEAD_DIM + N_HEADS, 8 * wd + HEAD_DIM + N_HEADS
    cache_a_k = cache_a_k.reshape(depth, s_batch, BAND, wd)
    cache_a_v = cache_a_v.reshape(depth, s_batch, BAND, wd)
    cache_b_k = cache_b_k.reshape(depth, s_batch, past, wd)
    cache_b_v = cache_b_v.reshape(depth, s_batch, past, wd)
    cache_mem_k = cache_mem_k.reshape(depth, s_batch, N_MEM, wd)
    cache_mem_v = cache_mem_v.reshape(depth, s_batch, N_MEM, wd)
    mem_rows = mem_prompt.reshape(n_batch * N_MEM, d)

    outs = {k: [] for k in ("p_ak", "p_av", "p_bk", "p_bv", "p_ik", "p_mk", "p_mv",
                            "s_ak", "s_av", "s_bk", "s_bv", "s_ik")}
    for l in range(depth):
        w = w_in[l]
        w_a = w[:, :3 * wd].astype(BF16)
        w_b = w[:, 3 * wd:7 * wd].astype(BF16)
        w_c = w[:, c_qm:].astype(BF16)
        w_d = jnp.concatenate([w[:, c_ki:c_qm], jnp.zeros((d, LANES - HEAD_DIM - N_HEADS), F32)], 1).astype(BF16)

        qa, ka, va = _project(x, w_a, (wd, wd, wd), split_flags=(False, True, True), n_prompt=n_p,
                              tm=tm, name="proj_a")
        qb, kb, vb, qi = _project(x, w_b, (wd, wd, wd, wd), (True, True, False, True), tab_heads, table_index,
                                  split_flags=(False, True, True, False), n_prompt=n_p, tm=tm, name="proj_b")
        qm, gates = _project(x, w_c, (wd, 3 * d), tm=tm, name="proj_c")
        (kiw,) = _project(x, w_d, (LANES,), (True,), tab_kiw, table_index, split_flags=(True,), n_prompt=n_p,
                          tm=tm, name="proj_d")
        mk, mv = _project(mem_rows, w_mem_kv[l].astype(BF16), (wd, wd), tm=N_MEM, name="proj_mem")

        a = (_band_prompt(qa, ka[0], va[0], rel_bias[l], n_batch, seq),
             _band_sample(qa, ka[1], va[1], cache_a_k, cache_a_v, l, _band_bias(rel_bias[l], CHUNK), s_batch, n_p))
        b = (_dsa_prompt(qb, kb[0], vb[0], qi, kiw[0], n_batch, seq),
             _dsa_sample(qb, kb[1], vb[1], qi, kiw[1], cache_b_k, cache_b_v, cache_b_idx_k, l, s_batch, n_p))
        m = (_mem_attn(qm, mk, mv, lambda bb: (bb, 0), n_batch, seq, 0, tm),
             _mem_attn(qm, cache_mem_k, cache_mem_v, lambda bb, l=l: (l, bb, 0, 0), s_batch, s_seq, n_p, s_seq))

        x1, x1w = _merge(x, (a, b, m), gates, w_branch[l].astype(BF16), w_out[l].astype(BF16),
                         ln_mix_g[l][None], ln_mix_b[l][None], alpha)
        x = _moe(x1, x1w, l, w_router[l].T.astype(BF16), b_router[l][:, None], w_exp_gate_up, w_exp_down,
                 w_sh_gate_up[l].astype(BF16), w_sh_down[l].astype(BF16),
                 ln_ffn_g[l][None], ln_ffn_b[l][None], alpha)

        outs["p_ak"].append(ka[0].reshape(n_batch, seq, wd)[:, -BAND:])
        outs["p_av"].append(va[0].reshape(n_batch, seq, wd)[:, -BAND:])
        outs["p_bk"].append(kb[0].reshape(n_batch, seq, wd))
        outs["p_bv"].append(vb[0].reshape(n_batch, seq, wd))
        outs["p_ik"].append(kiw[0][:, :HEAD_DIM].reshape(n_batch, seq, HEAD_DIM))
        outs["p_mk"].append(mk.reshape(n_batch, N_MEM, wd))
        outs["p_mv"].append(mv.reshape(n_batch, N_MEM, wd))
        ka_s = ka[1].reshape(s_batch, s_seq, wd)
        va_s = va[1].reshape(s_batch, s_seq, wd)
        outs["s_ak"].append(jnp.concatenate([cache_a_k[l], ka_s], 1)[:, -BAND:])
        outs["s_av"].append(jnp.concatenate([cache_a_v[l], va_s], 1)[:, -BAND:])
        outs["s_bk"].append(kb[1].reshape(s_batch, s_seq, wd))
        outs["s_bv"].append(vb[1].reshape(s_batch, s_seq, wd))
        outs["s_ik"].append(kiw[1][:, :HEAD_DIM].reshape(s_batch, s_seq, HEAD_DIM))

    heads = lambda t: t.reshape(t.shape[:-1] + (N_HEADS, HEAD_DIM))
    mheads = lambda t: t.reshape(t.shape[:-1] + (M_HEADS, M_HEAD_DIM))
    st = lambda key: jnp.stack(outs[key])
    return (x[:n_p].reshape(n_batch, seq, d), x[n_p:].reshape(s_batch, s_seq, d),
            heads(st("p_ak")), heads(st("p_av")), heads(st("p_bk")), heads(st("p_bv")), st("p_ik"),
            mheads(st("p_mk")), mheads(st("p_mv")),
            heads(st("s_ak")), heads(st("s_av")), heads(st("s_bk")), heads(st("s_bv")), st("s_ik"))
```

```python
import functools

import jax
import jax.numpy as jnp
from jax import lax
from jax.experimental import pallas as pl
from jax.experimental.pallas import tpu as pltpu

F32 = jnp.float32
BF16 = jnp.bfloat16
I32 = jnp.int32

CHUNK = 64
HEAD_DIM = 64
N_HEADS = 8
WIDTH = N_HEADS * HEAD_DIM
BAND = 8 * CHUNK
MAX_REL = 128
TOPK_KEYS = 256
M_HEADS = 4
M_HEAD_DIM = 128
N_MEM = 256
ROPE_THETA = 10000.0
N_EXPERTS = 256
EXPERT_TOPK = 8
N_GROUPS = 8
TOPK_GROUPS = 4
EXPERT_FF = 256
ROUTE_SCALE = 2.5
EXPERT_BLOCK = 256
LN_EPS = 1e-5

LANES = 128
NEG_BIG = -1e30
INT_MIN = -(2 ** 31)
VMEM_LIMIT = 48 * 1024 * 1024

_NT_DIMS = (((1,), (1,)), ((), ()))


def _params(semantics):
    return pltpu.CompilerParams(dimension_semantics=semantics, vmem_limit_bytes=VMEM_LIMIT)


def _proj_body(*refs, widths, rope_flags, split_flags, prompt_tiles):
    has_rope = any(rope_flags)
    x_ref, w_ref = refs[0], refs[1]
    if has_rope:
        cos_ref, sin_lo_ref, sin_hi_ref = refs[2:5]
        outs = list(refs[5:])
    else:
        outs = list(refs[2:])
    in_prompt = pl.program_id(0) < prompt_tiles
    xb = x_ref[...].astype(BF16)
    off = 0
    for g, width in enumerate(widths):
        y = jnp.dot(xb, w_ref[:, off:off + width], preferred_element_type=F32)
        if rope_flags[g]:
            cos, sin_lo, sin_hi = cos_ref[...], sin_lo_ref[...], sin_hi_ref[...]
            chunks = [yk * cos + pltpu.roll(yk, LANES - 32, 1) * sin_lo + pltpu.roll(yk, 32, 1) * sin_hi
                      for yk in (y[:, k * LANES:(k + 1) * LANES] for k in range(width // LANES))]
        else:
            chunks = [y]

        def store(ref, chunks=chunks):
            step = chunks[0].shape[1]
            for k, chunk in enumerate(chunks):
                ref[:, k * step:(k + 1) * step] = chunk

        if split_flags[g]:
            prompt_ref, sample_ref = outs.pop(0), outs.pop(0)
            pl.when(in_prompt)(functools.partial(store, prompt_ref))
            pl.when(jnp.logical_not(in_prompt))(functools.partial(store, sample_ref))
        else:
            store(outs.pop(0))
        off += width


def _project(x, w, widths, rope_flags=None, tables=None, table_index=None, split_flags=None, n_prompt=None,
             tm=512, name="proj"):
    n, d = x.shape
    rope_flags = rope_flags or (False,) * len(widths)
    split_flags = split_flags or (False,) * len(widths)
    assert n % tm == 0 and w.shape == (d, sum(widths))
    prompt_tiles = n // tm
    if any(split_flags):
        assert n_prompt % tm == 0 and 0 < n_prompt < n
        prompt_tiles = n_prompt // tm
    in_specs = [pl.BlockSpec((tm, d), lambda i: (i, 0)),
                pl.BlockSpec(w.shape, lambda i: (0, 0))]
    args = [x, w]
    if any(rope_flags):
        for t in tables:
            in_specs.append(pl.BlockSpec((tm, LANES), lambda i: (table_index(i), 0)))
            args.append(t)
    out_shape, out_specs = [], []
    for wd, split in zip(widths, split_flags):
        if split:
            out_shape += [jax.ShapeDtypeStruct((n_prompt, wd), F32), jax.ShapeDtypeStruct((n - n_prompt, wd), F32)]
            out_specs += [pl.BlockSpec((tm, wd), lambda i: (jnp.minimum(i, prompt_tiles - 1), 0)),
                          pl.BlockSpec((tm, wd), lambda i: (jnp.maximum(i - prompt_tiles, 0), 0))]
        else:
            out_shape.append(jax.ShapeDtypeStruct((n, wd), F32))
            out_specs.append(pl.BlockSpec((tm, wd), lambda i: (i, 0)))
    flat = pl.pallas_call(
        functools.partial(_proj_body, widths=tuple(widths), rope_flags=tuple(rope_flags),
                          split_flags=tuple(split_flags), prompt_tiles=prompt_tiles),
        out_shape=out_shape,
        grid=(n // tm,),
        in_specs=in_specs,
        out_specs=out_specs,
        compiler_params=_params(("arbitrary",)),
        name=name,
    )(*args)
    flat = list(flat)
    return [(flat.pop(0), flat.pop(0)) if split else flat.pop(0) for split in split_flags]


def _rope_tables(pos, rope_lanes):
    half = HEAD_DIM // 2
    inv_freq = ROPE_THETA ** (-jnp.arange(half, dtype=F32) / half)
    ang = pos.astype(F32)[:, None] * inv_freq[None, :]
    cos, sin = jnp.cos(ang), jnp.sin(ang)
    reps = LANES // HEAD_DIM
    zero = jnp.zeros_like(sin)
    cos_t = jnp.tile(jnp.concatenate([cos, cos], -1), (1, reps))
    sin_lo = jnp.tile(jnp.concatenate([-sin, zero], -1), (1, reps))
    sin_hi = jnp.tile(jnp.concatenate([zero, sin], -1), (1, reps))
    keep = (jnp.arange(LANES) < rope_lanes)[None, :]
    return (jnp.where(keep, cos_t, 1.0), jnp.where(keep, sin_lo, 0.0), jnp.where(keep, sin_hi, 0.0))


def _reduce_rows(x, op, keep_groups=False):
    rows, n = x.shape
    groups = next(g for g in (8, 4, 2, 1) if rows % (8 * g) == 0 and rows // (8 * g) >= 4 or g == 1)
    parts = op(x.reshape(groups, rows // (8 * groups), 8, n), axis=1)
    if keep_groups:
        return parts
    return op(op(parts, axis=0), axis=0, keepdims=True)


def _head_mask(rows, head_in_pair):
    lane = lax.broadcasted_iota(I32, (rows, LANES), 1)
    lo = head_in_pair * HEAD_DIM
    return (lane >= lo) & (lane < lo + HEAD_DIM)


def _pair_merge(o_even, o_odd):
    lane = lax.broadcasted_iota(I32, o_even.shape, 1)
    return jnp.where(lane < HEAD_DIM, o_even, o_odd)


def _band_sample_body(q_ref, kn_ref, vn_ref, kh_ref, vh_ref, bias_ref, o_ref, kwin, vwin):
    kwin[0:BAND, :] = kh_ref[...].astype(BF16)
    vwin[0:BAND, :] = vh_ref[...].astype(BF16)
    kwin[BAND:, :] = kn_ref[...].astype(BF16)
    vwin[BAND:, :] = vn_ref[...].astype(BF16)
    q = q_ref[...] * (HEAD_DIM ** -0.5)
    for hp in range(N_HEADS // 2):
        cols = slice(hp * LANES, (hp + 1) * LANES)
        outs = []
        for hh in range(2):
            qm = jnp.where(_head_mask(CHUNK, hh), q[:, cols], 0.0).astype(BF16)
            s = lax.dot_general(qm, kwin[:, cols], _NT_DIMS, preferred_element_type=F32) + bias_ref[2 * hp + hh]
            p = jnp.exp(s - jnp.max(s, axis=1, keepdims=True))
            denom = jnp.sum(p, axis=1, keepdims=True)
            outs.append(jnp.dot(p.astype(BF16), vwin[:, cols], preferred_element_type=F32) / denom)
        o_ref[:, cols] = _pair_merge(outs[0], outs[1])


def _band_prompt_body(q_ref, k_ref, v_ref, bias_ref, o_ref, k_even, k_odd, v_t, q_t, s_sc, p_sc, o_t):
    t = pl.program_id(1)
    nq = q_ref.shape[0]
    win = BAND + nq
    seq = k_ref.shape[0]
    stage_rows = 256

    @pl.when(t == 0)
    def _():
        k_even[0:BAND, :] = jnp.zeros((BAND, WIDTH), BF16)
        k_odd[0:BAND, :] = jnp.zeros((BAND, WIDTH), BF16)
        v_t[:, 0:BAND] = jnp.zeros((WIDTH, BAND), BF16)
        for r in range(0, seq, stage_rows):
            ke, ko = _split_pair_lanes(k_ref[r:r + stage_rows, :])
            k_even[BAND + r:BAND + r + stage_rows, :] = ke.astype(BF16)
            k_odd[BAND + r:BAND + r + stage_rows, :] = ko.astype(BF16)
            v_t[:, BAND + r:BAND + r + stage_rows] = v_ref[r:r + stage_rows, :].T.astype(BF16)

    start = pl.multiple_of(t * nq, nq)
    window = pl.ds(start, win)
    q_t[...] = (q_ref[...] * (HEAD_DIM ** -0.5)).T.astype(BF16)
    before_start = jnp.where(lax.broadcasted_iota(I32, (win, nq), 0) >= BAND - t * nq, 0.0, NEG_BIG)
    for h in range(N_HEADS):
        pair = slice((h // 2) * LANES, (h // 2 + 1) * LANES)
        k_ref_h = k_even if h % 2 == 0 else k_odd
        s_sc[h * win:(h + 1) * win, :] = (jnp.dot(k_ref_h[window, pair], q_t[pair, :], preferred_element_type=F32)
                                          + bias_ref[h] + before_start)
    for h in range(N_HEADS):
        s = s_sc[h * win:(h + 1) * win, :]
        p = jnp.exp(s - _reduce_rows(s, jnp.max))
        p_sc[h * win:(h + 1) * win, :] = (p / _reduce_rows(p, jnp.sum)).astype(BF16)
    for h in range(N_HEADS):
        dims = slice(h * HEAD_DIM, (h + 1) * HEAD_DIM)
        o_t[dims, :] = jnp.dot(v_t[dims, window], p_sc[h * win:(h + 1) * win, :], preferred_element_type=F32)
    o_ref[...] = o_t[...].T


def _band_bias(table, n_rows):
    win = BAND + n_rows
    assert n_rows <= MAX_REL <= BAND
    n_heads = table.shape[0]
    g = jnp.concatenate([jnp.broadcast_to(table[:, 2 * MAX_REL:], (n_heads, win - MAX_REL)),
                         table[:, MAX_REL - n_rows:2 * MAX_REL][:, ::-1],
                         jnp.zeros((n_heads, 1), table.dtype)], axis=1).astype(F32)
    period = g.shape[1]
    skew = jnp.tile(g, (1, n_rows))[:, :n_rows * (period - 1)].reshape(n_heads, n_rows, period - 1)
    return skew[:, :, n_rows - 1:n_rows - 1 + win]


def _band_prompt(q, k, v, table, n_batch, seq, q_rows=128):
    nt = seq // q_rows
    win = BAND + q_rows
    assert seq % q_rows == 0 and q_rows % CHUNK == 0
    key_chunk = jnp.arange(win)[:, None] // CHUNK - BAND // CHUNK
    query_chunk = jnp.arange(q_rows)[None, :] // CHUNK
    visible = (key_chunk <= query_chunk) & (key_chunk >= query_chunk - BAND // CHUNK)
    bias = jnp.where(visible[None], jnp.swapaxes(_band_bias(table, q_rows), 1, 2), NEG_BIG)
    tile_spec = pl.BlockSpec((q_rows, WIDTH), lambda b, t: (b * nt + t, 0))
    seq_spec = pl.BlockSpec((seq, WIDTH), lambda b, t: (b, 0))
    return pl.pallas_call(
        _band_prompt_body,
        out_shape=jax.ShapeDtypeStruct((n_batch * seq, WIDTH), F32),
        grid=(n_batch, nt),
        in_specs=[tile_spec, seq_spec, seq_spec, pl.BlockSpec(bias.shape, lambda b, t: (0, 0, 0))],
        out_specs=tile_spec,
        scratch_shapes=[pltpu.VMEM((BAND + seq, WIDTH), BF16), pltpu.VMEM((BAND + seq, WIDTH), BF16),
                        pltpu.VMEM((WIDTH, BAND + seq), BF16), pltpu.VMEM((WIDTH, q_rows), BF16),
                        pltpu.VMEM((N_HEADS * win, q_rows), F32), pltpu.VMEM((N_HEADS * win, q_rows), BF16),
                        pltpu.VMEM((WIDTH, q_rows), F32)],
        compiler_params=_params(("parallel", "arbitrary")),
        name="band_prompt",
    )(q, k, v, bias)


def _band_sample(q, k, v, k_hist, v_hist, layer, bias, n_batch, row0):
    blk0 = row0 // CHUNK
    hist_spec = pl.BlockSpec((None, None, BAND, WIDTH), lambda b: (layer, b, 0, 0))
    new_spec = pl.BlockSpec((CHUNK, WIDTH), lambda b: (b, 0))
    return pl.pallas_call(
        _band_sample_body,
        out_shape=jax.ShapeDtypeStruct((n_batch * CHUNK, WIDTH), F32),
        grid=(n_batch,),
        in_specs=[pl.BlockSpec((CHUNK, WIDTH), lambda b: (blk0 + b, 0)), new_spec, new_spec, hist_spec, hist_spec,
                  pl.BlockSpec(bias.shape, lambda b: (0, 0, 0))],
        out_specs=new_spec,
        scratch_shapes=[pltpu.VMEM((BAND + CHUNK, WIDTH), BF16), pltpu.VMEM((BAND + CHUNK, WIDTH), BF16)],
        compiler_params=_params(("parallel",)),
        name="band_sample",
    )(q, k, v, k_hist, v_hist, bias)


def _mem_body(q_ref, k_ref, v_ref, o_ref):
    scale = M_HEAD_DIM ** -0.5
    for h in range(M_HEADS):
        cols = slice(h * M_HEAD_DIM, (h + 1) * M_HEAD_DIM)
        s = lax.dot_general(q_ref[:, cols].astype(BF16), k_ref[:, cols].astype(BF16), _NT_DIMS,
                            preferred_element_type=F32) * scale
        p = jnp.exp(s - jnp.max(s, axis=1, keepdims=True))
        denom = jnp.sum(p, axis=1, keepdims=True)
        o = jnp.dot(p.astype(BF16), v_ref[:, cols].astype(BF16), preferred_element_type=F32)
        o_ref[:, cols] = o / denom


def _mem_attn(q, mk, mv, kv_index, n_batch, rows_per_batch, row0, tq):
    width = M_HEADS * M_HEAD_DIM
    per = rows_per_batch // tq
    blk0 = row0 // tq
    kv_block = (None,) * (mk.ndim - 2) + (N_MEM, width)
    kv_spec = pl.BlockSpec(kv_block, lambda b, t: kv_index(b))
    return pl.pallas_call(
        _mem_body,
        out_shape=jax.ShapeDtypeStruct((n_batch * rows_per_batch, width), F32),
        grid=(n_batch, per),
        in_specs=[pl.BlockSpec((tq, width), lambda b, t: (blk0 + b * per + t, 0)), kv_spec, kv_spec],
        out_specs=pl.BlockSpec((tq, width), lambda b, t: (b * per + t, 0)),
        compiler_params=_params(("parallel", "parallel")),
        name="mem_attn",
    )(q, mk, mv)


def _split_pair_lanes(x):
    lane = lax.broadcasted_iota(I32, x.shape, 1)
    even = (lane & (LANES - 1)) < HEAD_DIM
    return jnp.where(even, x, 0.0), jnp.where(even, 0.0, x)


def _dsa_body(*refs, sample, key_block, n_keys, n_sel):
    n_in = 8 if sample else 6
    if sample:
        q_ref, qi_ref, kiwq_ref, kn_ref, vn_ref, kc_ref, vc_ref, kic_ref = refs[:n_in]
    else:
        q_ref, qi_ref, kiwq_ref, k_ref, v_ref, kiw_ref = refs[:n_in]
    o_ref = refs[n_in]
    (k_even, k_odd, v_t, ki_lo, ki_hi, keybuf, maskbuf, q_t, qi_t, kiwq_t, m_sc, l_sc, a_sc, acc_t,
     s_sc, p_sc) = refs[n_in + 1:]
    kb = key_block
    n_pad = k_even.shape[0]
    nq = q_t.shape[1]
    widen = lambda rows: jnp.concatenate([rows] * (nq // rows.shape[0]), axis=0)
    stage_rows = 256

    def stage_kv(row0, k_new, v_new):
        n_new = k_new.shape[0]
        ke, ko = _split_pair_lanes(k_new)
        k_even[row0:row0 + n_new, :] = ke.astype(BF16)
        k_odd[row0:row0 + n_new, :] = ko.astype(BF16)
        v_t[:, row0:row0 + n_new] = v_new.T.astype(BF16)

    def stage_index_keys(row0, kiw):
        lane = lax.broadcasted_iota(I32, kiw.shape, 1)
        ki_lo[row0:row0 + kiw.shape[0], :] = jnp.where(lane < HEAD_DIM, kiw, 0.0).astype(BF16)
        ki_hi[row0:row0 + kiw.shape[0], :] = jnp.where(lane < HEAD_DIM, 0.0, pltpu.roll(kiw, HEAD_DIM, 1)).astype(BF16)

    def stage_keys():
        if sample:
            n_hist = kc_ref.shape[0]
            for r in range(0, n_hist, stage_rows):
                stage_kv(r, kc_ref[r:r + stage_rows, :], vc_ref[r:r + stage_rows, :])
            kic = kic_ref[...]
            zeros = jnp.zeros_like(kic)
            ki_lo[0:n_hist, :] = jnp.concatenate([kic, zeros], axis=1).astype(BF16)
            ki_hi[0:n_hist, :] = jnp.concatenate([zeros, kic], axis=1).astype(BF16)
            stage_kv(n_hist, kn_ref[...], vn_ref[...])
            stage_index_keys(n_hist, kiwq_ref[...])
            if n_pad > n_keys:
                for ref in (k_even, k_odd, ki_lo, ki_hi):
                    ref[n_keys:, :] = jnp.zeros((n_pad - n_keys, ref.shape[1]), BF16)
                v_t[:, n_keys:] = jnp.zeros((v_t.shape[0], n_pad - n_keys), BF16)
        else:
            for r in range(0, n_keys, stage_rows):
                stage_kv(r, k_ref[r:r + stage_rows, :], v_ref[r:r + stage_rows, :])
            stage_index_keys(0, kiw_ref[...])

    if sample:
        stage_keys()
        limit = n_keys
        nb = n_pad // kb
        search_bits = 32
    else:
        t = pl.program_id(1)
        pl.when(t == 0)(stage_keys)
        chunk = t * (nq // CHUNK) + lax.broadcasted_iota(I32, (1, nq), 1) // CHUNK
        limit = (chunk + 1) * CHUNK
        limit_max = (t + 1) * nq
        nb = (limit_max + kb - 1) // kb
        search_bits = jnp.where(limit_max <= n_sel, 0, 32)

    def block_rows(j):
        return pl.ds(pl.multiple_of(j * kb, kb), kb)

    def key_pos(j):
        return j * kb + lax.broadcasted_iota(I32, (kb, nq), 0)

    q_t[...] = widen(q_ref[...] * (HEAD_DIM ** -0.5)).T.astype(BF16)
    qi_t[...] = widen(qi_ref[...]).T.astype(BF16)
    kiwq_t[...] = widen(kiwq_ref[...]).T

    def score_block(j, carry):
        acc = jnp.zeros((kb, nq), F32)
        for h in range(N_HEADS):
            pair = slice((h // 2) * LANES, (h // 2 + 1) * LANES)
            ki_ref = ki_lo if h % 2 == 0 else ki_hi
            d = jnp.dot(ki_ref[block_rows(j), :], qi_t[pair, :], preferred_element_type=F32)
            acc = acc + kiwq_t[HEAD_DIM + h:HEAD_DIM + h + 1, :] * jnp.maximum(d, 0.0)
        acc = acc + 0.0
        acc = jnp.where(key_pos(j) < limit, acc, -jnp.inf)
        bits = lax.bitcast_convert_type(acc, I32)
        keybuf[block_rows(j), :] = bits ^ ((bits >> 31) & 0x7FFFFFFF)
        return carry

    lax.fori_loop(0, nb, score_block, 0)

    def count(pred_fn):
        def body(j, part):
            hit = jnp.where(pred_fn(keybuf[block_rows(j), :], j), 1.0, 0.0)
            return part + _reduce_rows(hit, jnp.sum, keep_groups=True)
        groups = jax.eval_shape(functools.partial(_reduce_rows, op=jnp.sum, keep_groups=True),
                                jax.ShapeDtypeStruct((kb, nq), F32)).shape[0]
        part = lax.fori_loop(0, nb, body, jnp.zeros((groups, 8, nq), F32))
        return jnp.sum(jnp.sum(part, axis=0), axis=0, keepdims=True)

    def bit_step(i, prefix):
        cand = prefix | lax.shift_left(jnp.int32(1), 31 - i)
        thr_i = cand ^ INT_MIN
        cnt = count(lambda kblk, j: kblk >= thr_i)
        return jnp.where(cnt >= n_sel, cand, prefix)

    prefix = lax.fori_loop(0, search_bits, bit_step, jnp.zeros((1, nq), I32))
    thr = prefix ^ INT_MIN
    above = count(lambda kblk, j: kblk > thr)
    at_least = count(lambda kblk, j: kblk >= thr)
    need = n_sel - above
    surplus = at_least > n_sel

    pos_bits = max(1, (n_pad - 1).bit_length() + 1)
    any_surplus = jnp.max(jnp.where(surplus, 1.0, 0.0)) > 0.0

    def pos_step(i, bound):
        cand = bound | lax.shift_left(jnp.int32(1), pos_bits - 1 - i)
        cnt = count(lambda kblk, j: (kblk == thr) & (key_pos(j) < cand))
        return jnp.where(cnt <= need, cand, bound)

    bound = lax.fori_loop(0, jnp.where(any_surplus, pos_bits, 0), pos_step, jnp.zeros((1, nq), I32))
    bound = jnp.where(surplus, bound, 2 ** pos_bits - 1)

    def mask_block(j, carry):
        kblk = keybuf[block_rows(j), :]
        pos = key_pos(j)
        sel = ((kblk > thr) | ((kblk == thr) & (pos < bound))) & (pos < limit)
        maskbuf[block_rows(j), :] = jnp.where(sel, 0.0, NEG_BIG)
        return carry

    lax.fori_loop(0, nb, mask_block, 0)

    m_sc[...] = jnp.full(m_sc.shape, NEG_BIG, F32)
    l_sc[...] = jnp.zeros(l_sc.shape, F32)
    acc_t[...] = jnp.zeros(acc_t.shape, F32)

    def attend(j, carry):
        mask = maskbuf[block_rows(j), :]
        for h in range(N_HEADS):
            pair = slice((h // 2) * LANES, (h // 2 + 1) * LANES)
            k_ref_h = k_even if h % 2 == 0 else k_odd
            s_sc[h * kb:(h + 1) * kb, :] = jnp.dot(k_ref_h[block_rows(j), pair], q_t[pair, :],
                                                   preferred_element_type=F32) + mask
        for h in range(N_HEADS):
            stat = slice(h * 8, (h + 1) * 8)
            s = s_sc[h * kb:(h + 1) * kb, :]
            m_prev = m_sc[stat, :]
            m_new = jnp.maximum(m_prev, _reduce_rows(s, jnp.max))
            alpha = jnp.exp(m_prev - m_new)
            p = jnp.exp(s - m_new[0:1, :])
            l_sc[stat, :] = alpha * l_sc[stat, :] + _reduce_rows(p, jnp.sum)
            m_sc[stat, :] = m_new
            a_sc[stat, :] = alpha
            p_sc[h * kb:(h + 1) * kb, :] = p.astype(BF16)
        for h in range(N_HEADS):
            dims = slice(h * HEAD_DIM, (h + 1) * HEAD_DIM)
            pv = jnp.dot(v_t[dims, block_rows(j)], p_sc[h * kb:(h + 1) * kb, :], preferred_element_type=F32)
            acc_t[dims, :] = a_sc[h * 8:h * 8 + 1, :] * acc_t[dims, :] + pv
        return carry

    lax.fori_loop(0, nb, attend, 0)
    for h in range(N_HEADS):
        dims = slice(h * HEAD_DIM, (h + 1) * HEAD_DIM)
        acc_t[dims, :] = acc_t[dims, :] / l_sc[h * 8:h * 8 + 1, :]
    o_ref[...] = acc_t[...].T[0:o_ref.shape[0], :]


def _dsa_scratch(nq, n_pad, key_block):
    nq = max(nq, LANES)
    return [pltpu.VMEM((n_pad, WIDTH), BF16), pltpu.VMEM((n_pad, WIDTH), BF16),
            pltpu.VMEM((WIDTH, n_pad), BF16),
            pltpu.VMEM((n_pad, LANES), BF16), pltpu.VMEM((n_pad, LANES), BF16),
            pltpu.VMEM((n_pad, nq), I32), pltpu.VMEM((n_pad, nq), F32),
            pltpu.VMEM((WIDTH, nq), BF16), pltpu.VMEM((WIDTH, nq), BF16),
            pltpu.VMEM((LANES, nq), F32),
            pltpu.VMEM((N_HEADS * 8, nq), F32), pltpu.VMEM((N_HEADS * 8, nq), F32),
            pltpu.VMEM((N_HEADS * 8, nq), F32),
            pltpu.VMEM((WIDTH, nq), F32),
            pltpu.VMEM((N_HEADS * key_block, nq), F32),
            pltpu.VMEM((N_HEADS * key_block, nq), BF16)]


def _dsa_prompt(q, k, v, qi, kiw, n_batch, seq, key_block=256, q_rows=256):
    nt = seq // q_rows
    key_block = min(key_block, seq)
    assert seq % key_block == 0 and seq % q_rows == 0 and q_rows % CHUNK == 0
    n_sel = min(TOPK_KEYS, seq // 4)
    tile_spec = lambda width: pl.BlockSpec((q_rows, width), lambda b, t: (b * nt + t, 0))
    seq_spec = lambda width: pl.BlockSpec((seq, width), lambda b, t: (b, 0))
    return pl.pallas_call(
        functools.partial(_dsa_body, sample=False, key_block=key_block, n_keys=seq, n_sel=n_sel),
        out_shape=jax.ShapeDtypeStruct((n_batch * seq, WIDTH), F32),
        grid=(n_batch, nt),
        in_specs=[tile_spec(WIDTH), tile_spec(WIDTH), tile_spec(LANES),
                  seq_spec(WIDTH), seq_spec(WIDTH), seq_spec(LANES)],
        out_specs=tile_spec(WIDTH),
        scratch_shapes=_dsa_scratch(q_rows, seq, key_block),
        compiler_params=_params(("parallel", "arbitrary")),
        name="dsa_prompt",
    )(q, qi, kiw, k, v, kiw)


def _dsa_sample(q, k, v, qi, kiw, k_cache, v_cache, ki_cache, layer, n_batch, row0, key_block=768):
    past = k_cache.shape[2]
    n_keys = past + CHUNK
    key_block = min(key_block, -(-n_keys // 256) * 256)
    n_pad = -(-n_keys // key_block) * key_block
    n_sel = min(TOPK_KEYS, n_keys // 4)
    blk0 = row0 // CHUNK
    query_spec = pl.BlockSpec((CHUNK, WIDTH), lambda b: (blk0 + b, 0))
    new_spec = lambda width: pl.BlockSpec((CHUNK, width), lambda b: (b, 0))
    cache_spec = lambda width: pl.BlockSpec((None, None, past, width), lambda b: (layer, b, 0, 0))
    return pl.pallas_call(
        functools.partial(_dsa_body, sample=True, key_block=key_block, n_keys=n_keys, n_sel=n_sel),
        out_shape=jax.ShapeDtypeStruct((n_batch * CHUNK, WIDTH), F32),
        grid=(n_batch,),
        in_specs=[query_spec, query_spec, new_spec(LANES), new_spec(WIDTH), new_spec(WIDTH),
                  cache_spec(WIDTH), cache_spec(WIDTH), cache_spec(HEAD_DIM)],
        out_specs=new_spec(WIDTH),
        scratch_shapes=_dsa_scratch(CHUNK, n_pad, key_block),
        compiler_params=_params(("parallel",)),
        name="dsa_sample",
    )(q, qi, kiw, k, v, k_cache, v_cache, ki_cache)


def _layer_norm(z, g, b):
    mu = jnp.mean(z, axis=-1, keepdims=True)
    zc = z - mu
    var = jnp.mean(zc * zc, axis=-1, keepdims=True)
    return zc * lax.rsqrt(var + LN_EPS) * g + b


HI_HALF = -65536


def _pack_rows(x):
    half = x.shape[1] // 2
    lo = lax.bitcast_convert_type(x[:, :half].astype(BF16).astype(F32), I32)
    hi = lax.bitcast_convert_type(x[:, half:].astype(BF16).astype(F32), I32)
    return (hi & HI_HALF) | lax.shift_right_logical(lo, 16)


def _unpack_rows(w):
    lo = lax.bitcast_convert_type(lax.shift_left(w, 16), F32).astype(BF16)
    hi = lax.bitcast_convert_type(w & HI_HALF, F32).astype(BF16)
    return lo, hi


def _merge_body(x_ref, ap_ref, as_ref, bp_ref, bs_ref, mp_ref, ms_ref, gates_ref, wbr_ref, wout_ref, g_ref, beta_ref,
                o_ref, ow_ref, *, alpha, prompt_tiles):
    d = x_ref.shape[1]
    in_prompt = pl.program_id(0) < prompt_tiles
    mix = None
    for n, (p_ref, s_ref) in enumerate(((ap_ref, as_ref), (bp_ref, bs_ref), (mp_ref, ms_ref))):
        branch = jnp.where(in_prompt, p_ref[...], s_ref[...])
        proj = jnp.dot(branch.astype(BF16), wbr_ref[n], preferred_element_type=F32)
        term = jax.nn.sigmoid(gates_ref[:, n * d:(n + 1) * d]) * proj
        mix = term if mix is None else mix + term
    y = jnp.dot(mix.astype(BF16), wout_ref[...], preferred_element_type=F32)
    out = _layer_norm(alpha * x_ref[...] + y, g_ref[...], beta_ref[...])
    o_ref[...] = out
    ow_ref[...] = _pack_rows(out)


def _merge(x, branches, gates, w_branch, w_out, g, beta, alpha, tm=256):
    n, d = x.shape
    prompt_tiles = branches[0][0].shape[0] // tm
    assert all(p.shape[0] == prompt_tiles * tm and (n - p.shape[0]) == s.shape[0] and s.shape[0] % tm == 0
               for p, s in branches)
    packed = jax.eval_shape(_pack_rows, jax.ShapeDtypeStruct((tm, d), F32))
    row = lambda width: pl.BlockSpec((tm, width), lambda i: (i, 0))
    full = lambda arr: pl.BlockSpec(arr.shape, lambda i: (0,) * arr.ndim)
    prompt_row = lambda width: pl.BlockSpec((tm, width), lambda i: (jnp.minimum(i, prompt_tiles - 1), 0))
    sample_row = lambda width: pl.BlockSpec((tm, width), lambda i: (jnp.maximum(i - prompt_tiles, 0), 0))
    branch_specs, branch_args = [], []
    for p, s in branches:
        branch_specs += [prompt_row(p.shape[1]), sample_row(s.shape[1])]
        branch_args += [p, s]
    return pl.pallas_call(
        functools.partial(_merge_body, alpha=alpha, prompt_tiles=prompt_tiles),
        out_shape=[jax.ShapeDtypeStruct((n, d), F32), jax.ShapeDtypeStruct((n, packed.shape[1]), packed.dtype)],
        grid=(n // tm,),
        in_specs=[row(d)] + branch_specs + [row(gates.shape[1]), full(w_branch), full(w_out), full(g), full(beta)],
        out_specs=[row(d), row(packed.shape[1])],
        compiler_params=_params(("parallel",)),
        name="merge",
    )(x, *branch_args, gates, w_branch, w_out, g, beta)


def _first_max(vals, idx, n):
    top = jnp.max(vals, axis=0, keepdims=True)
    arg = jnp.min(jnp.where(vals == top, idx, n), axis=0, keepdims=True)
    return top, arg


def _router_body(x_ref, wt_ref, bias_ref, eidx_ref, gw_ref, rank_ref, counts_ref, running):
    tm = x_ref.shape[0]

    @pl.when(pl.program_id(0) == 0)
    def _():
        running[...] = jnp.zeros(running.shape, F32)

    per = N_EXPERTS // N_GROUPS
    logits = lax.dot_general(wt_ref[...], x_ref[...].astype(BF16), _NT_DIMS, preferred_element_type=F32)
    s = jax.nn.sigmoid(logits)
    sb = s + bias_ref[...]
    in_group = lax.broadcasted_iota(I32, (per, tm), 0).astype(F32)
    group_scores = []
    for g in range(N_GROUPS):
        blk = sb[g * per:(g + 1) * per, :]
        top1, arg1 = _first_max(blk, in_group, per)
        top2 = jnp.max(jnp.where(in_group == arg1, -jnp.inf, blk), axis=0, keepdims=True)
        group_scores.append(top1 + top2)
    gs = jnp.concatenate(group_scores, axis=0)
    gidx = lax.broadcasted_iota(I32, (N_GROUPS, tm), 0).astype(F32)
    chosen = jnp.zeros((N_GROUPS, tm), F32)
    for _ in range(TOPK_GROUPS):
        _, arg = _first_max(gs, gidx, N_GROUPS)
        hit = gidx == arg
        chosen = jnp.where(hit, 1.0, chosen)
        gs = jnp.where(hit, -jnp.inf, gs)
    cand = jnp.concatenate(
        [jnp.where(chosen[g:g + 1, :] > 0.0, sb[g * per:(g + 1) * per, :], -jnp.inf) for g in range(N_GROUPS)],
        axis=0)
    eidx = lax.broadcasted_iota(I32, (N_EXPERTS, tm), 0).astype(F32)
    picks, weights, hits = [], [], []
    for _ in range(EXPERT_TOPK):
        _, arg = _first_max(cand, eidx, N_EXPERTS)
        hit = eidx == arg
        weights.append(jnp.sum(jnp.where(hit, s, 0.0), axis=0, keepdims=True))
        picks.append(arg)
        hits.append(hit)
        cand = jnp.where(hit, -jnp.inf, cand)
    total = weights[0]
    for w in weights[1:]:
        total = total + w
    taken = jnp.zeros((N_EXPERTS, tm), F32)
    for hit in hits:
        taken = jnp.where(hit, 1.0, taken)
    earlier = (lax.broadcasted_iota(I32, (tm, tm), 0) < lax.broadcasted_iota(I32, (tm, tm), 1))
    before = jnp.dot(taken.astype(BF16), jnp.where(earlier, 1.0, 0.0).astype(BF16),
                     preferred_element_type=F32) + running[...]
    for r in range(EXPERT_TOPK):
        eidx_ref[r:r + 1, :] = picks[r].astype(I32)
        gw_ref[r:r + 1, :] = weights[r] / total * ROUTE_SCALE
        rank_ref[r:r + 1, :] = jnp.sum(jnp.where(hits[r], before, 0.0), axis=0, keepdims=True).astype(I32)
    running[...] = running[...] + jnp.sum(taken, axis=1, keepdims=True)
    counts_ref[...] = running[...]


def _router(x, w_router_t, b_router_col, tm=256):
    n, d = x.shape
    slot = pl.BlockSpec((EXPERT_TOPK, tm), lambda i: (0, i))
    return pl.pallas_call(
        _router_body,
        out_shape=[jax.ShapeDtypeStruct((EXPERT_TOPK, n), I32), jax.ShapeDtypeStruct((EXPERT_TOPK, n), F32),
                   jax.ShapeDtypeStruct((EXPERT_TOPK, n), I32), jax.ShapeDtypeStruct((N_EXPERTS, 1), F32)],
        grid=(n // tm,),
        in_specs=[pl.BlockSpec((tm, d), lambda i: (i, 0)),
                  pl.BlockSpec(w_router_t.shape, lambda i: (0, 0)),
                  pl.BlockSpec(b_router_col.shape, lambda i: (0, 0))],
        out_specs=[slot, slot, slot, pl.BlockSpec((N_EXPERTS, 1), lambda i: (0, 0))],
        scratch_shapes=[pltpu.VMEM((N_EXPERTS, 1), F32)],
        compiler_params=_params(("arbitrary",)),
        name="router",
    )(x, w_router_t, b_router_col)


def _dest_body(eidx_ref, rank_ref, start_ref, dest_ref):
    tm = eidx_ref.shape[1]
    experts = lax.broadcasted_iota(I32, (N_EXPERTS, tm), 0)
    for r in range(EXPERT_TOPK):
        base = jnp.sum(jnp.where(experts == eidx_ref[r:r + 1, :], start_ref[...], 0.0), axis=0, keepdims=True)
        dest_ref[r:r + 1, :] = base.astype(I32) + rank_ref[r:r + 1, :]


def _dest_rows(eidx_t, rank_t, start_col, tm=512):
    n = eidx_t.shape[1]
    slot = pl.BlockSpec((EXPERT_TOPK, tm), lambda i: (0, i))
    return pl.pallas_call(
        _dest_body,
        out_shape=jax.ShapeDtypeStruct((EXPERT_TOPK, n), I32),
        grid=(n // tm,),
        in_specs=[slot, slot, pl.BlockSpec(start_col.shape, lambda i: (0, 0))],
        out_specs=slot,
        compiler_params=_params(("parallel",)),
        name="dest_rows",
    )(eidx_t, rank_t, start_col)


def _scatter_body(dest_ref, x_ref, _, o_ref, sem):
    tm = x_ref.shape[0]

    def row_copy(t, k):
        return pltpu.make_async_copy(x_ref.at[pl.ds(t, 1)], o_ref.at[pl.ds(dest_ref[k, t], 1)], sem)

    def issue(t, carry):
        for k in range(EXPERT_TOPK):
            row_copy(t, k).start(priority=k % 2)
        return carry

    lax.fori_loop(0, tm, issue, 0)
    for k in range(EXPERT_TOPK):
        pltpu.make_async_copy(x_ref, o_ref.at[pl.ds(0, tm)], sem).wait()


def _scatter_rows(xw, dest_t, rows, tm=256):
    n, width = xw.shape
    return pl.pallas_call(
        _scatter_body,
        out_shape=jax.ShapeDtypeStruct((rows, width), xw.dtype),
        grid=(n // tm,),
        in_specs=[pl.BlockSpec((EXPERT_TOPK, tm), lambda i: (0, i), memory_space=pltpu.SMEM),
                  pl.BlockSpec((tm, width), lambda i: (i, 0)),
                  pl.BlockSpec(memory_space=pl.ANY)],
        out_specs=pl.BlockSpec(memory_space=pl.ANY),
        scratch_shapes=[pltpu.SemaphoreType.DMA],
        input_output_aliases={2: 0},
        compiler_params=_params(("arbitrary",)),
        name="scatter_rows",
    )(dest_t, xw, jnp.zeros((rows, width), xw.dtype))


def _experts_body(block_exp_ref, n_used_ref, x_ref, wgu_ref, wdn_ref, o_ref, wgu_bf, wdn_bf):
    i = pl.program_id(0)
    used = i < n_used_ref[0]

    @pl.when(used)
    def _():
        prev = block_exp_ref[jnp.maximum(i - 1, 0)]

        @pl.when((i == 0) | (block_exp_ref[i] != prev))
        def _():
            wgu_bf[...] = wgu_ref[...].astype(BF16)
            wdn_bf[...] = wdn_ref[...].astype(BF16)

        x_lo, x_hi = _unpack_rows(x_ref[...])
        half = x_lo.shape[1]
        h = (jnp.dot(x_lo, wgu_bf[0:half, :], preferred_element_type=F32)
             + jnp.dot(x_hi, wgu_bf[half:, :], preferred_element_type=F32))
        ff = h.shape[1] // 2
        act = jax.nn.silu(h[:, :ff]) * h[:, ff:]
        o_ref[...] = jnp.dot(act.astype(BF16), wdn_bf[...], preferred_element_type=F32)

    @pl.when(jnp.logical_not(used))
    def _():
        o_ref[...] = jnp.zeros(o_ref.shape, F32)


def _experts(x_rows, block_exp, n_used, w_gu, w_dn, layer):
    rows, width = x_rows.shape
    n_blocks = rows // EXPERT_BLOCK
    d, ff2 = w_gu.shape[-2:]
    grid_spec = pltpu.PrefetchScalarGridSpec(
        num_scalar_prefetch=2,
        grid=(n_blocks,),
        in_specs=[pl.BlockSpec((EXPERT_BLOCK, width), lambda i, be, nu: (i, 0)),
                  pl.BlockSpec((None, None, d, ff2), lambda i, be, nu: (layer, be[i], 0, 0)),
                  pl.BlockSpec((None, None, ff2 // 2, d), lambda i, be, nu: (layer, be[i], 0, 0))],
        out_specs=pl.BlockSpec((EXPERT_BLOCK, d), lambda i, be, nu: (i, 0)),
        scratch_shapes=[pltpu.VMEM((d, ff2), BF16), pltpu.VMEM((ff2 // 2, d), BF16)],
    )
    return pl.pallas_call(
        _experts_body,
        out_shape=jax.ShapeDtypeStruct((rows, d), F32),
        grid_spec=grid_spec,
        compiler_params=_params(("arbitrary",)),
        name="experts",
    )(block_exp, n_used, x_rows, w_gu, w_dn)


def _block_plan(counts, n_blocks):
    counts = counts.reshape(-1).astype(I32)
    padded = (counts + EXPERT_BLOCK - 1) // EXPERT_BLOCK * EXPERT_BLOCK
    pad_end = jnp.cumsum(padded)
    start_col = (pad_end - padded).astype(F32)[:, None]
    end_blocks = pad_end // EXPERT_BLOCK
    block_exp = jnp.minimum(jnp.sum(end_blocks[None, :] <= jnp.arange(n_blocks)[:, None], axis=1),
                            N_EXPERTS - 1).astype(I32)
    return start_col, block_exp, end_blocks[-1:].astype(I32)


def _ffn_out_body(dest_ref, dest_next_ref, x_ref, gw_ref, yb_ref, wgu_ref, wdn_ref, g_ref, beta_ref, o_ref,
                  ybuf, sems, *, alpha):
    i = pl.program_id(0)
    n_steps = pl.num_programs(0)
    tm = x_ref.shape[0]
    slot = i % 2

    def gather(dests, to_slot):
        def issue(t, carry):
            for k in range(EXPERT_TOPK):
                pltpu.make_async_copy(yb_ref.at[pl.ds(dests[k, t], 1)], ybuf.at[to_slot, k, pl.ds(t, 1)],
                                      sems.at[to_slot]).start(priority=k % 2)
            return carry
        lax.fori_loop(0, tm, issue, 0)

    @pl.when(i == 0)
    def _():
        gather(dest_ref, 0)

    @pl.when(i + 1 < n_steps)
    def _():
        gather(dest_next_ref, 1 - slot)

    x = x_ref[...]
    h = jnp.dot(x.astype(BF16), wgu_ref[...], preferred_element_type=F32)
    ff = h.shape[1] // 2
    act = jax.nn.silu(h[:, :ff]) * h[:, ff:]
    shared = jnp.dot(act.astype(BF16), wdn_ref[...], preferred_element_type=F32)

    for k in range(EXPERT_TOPK):
        pltpu.make_async_copy(yb_ref.at[pl.ds(0, tm)], ybuf.at[slot, k], sems.at[slot]).wait()
    routed = None
    for k in range(EXPERT_TOPK):
        term = ybuf[slot, k] * gw_ref[:, k:k + 1]
        routed = term if routed is None else routed + term
    o_ref[...] = _layer_norm(alpha * x + (routed + shared), g_ref[...], beta_ref[...])


def _ffn_out(x, yb, dest_t, gw, ws_gu, ws_dn, g, beta, alpha, tm=256):
    n, d = x.shape
    n_tiles = n // tm
    row = pl.BlockSpec((tm, d), lambda i: (i, 0))
    full = lambda arr: pl.BlockSpec(arr.shape, lambda i: (0,) * arr.ndim)
    dest_spec = lambda index: pl.BlockSpec((EXPERT_TOPK, tm), index, memory_space=pltpu.SMEM)
    return pl.pallas_call(
        functools.partial(_ffn_out_body, alpha=alpha),
        out_shape=jax.ShapeDtypeStruct((n, d), F32),
        grid=(n_tiles,),
        in_specs=[dest_spec(lambda i: (0, i)), dest_spec(lambda i: (0, jnp.minimum(i + 1, n_tiles - 1))),
                  row, pl.BlockSpec((tm, EXPERT_TOPK), lambda i: (i, 0)),
                  pl.BlockSpec(memory_space=pl.ANY),
                  full(ws_gu), full(ws_dn), full(g), full(beta)],
        out_specs=row,
        scratch_shapes=[pltpu.VMEM((2, EXPERT_TOPK, tm, d), F32), pltpu.SemaphoreType.DMA((2,))],
        compiler_params=_params(("arbitrary",)),
        name="ffn_out",
    )(dest_t, dest_t, x, gw, yb, ws_gu, ws_dn, g, beta)


def _moe(x, xw, layer, w_router_t, b_router_col, w_exp_gate_up, w_exp_down, ws_gu, ws_dn, g, beta, alpha):
    n = x.shape[0]
    n_blocks = -(-n * EXPERT_TOPK // EXPERT_BLOCK) + N_EXPERTS
    eidx_t, gw_t, rank_t, counts = _router(x, w_router_t, b_router_col)
    start_col, block_exp, n_used = _block_plan(counts, n_blocks)
    dest_t = _dest_rows(eidx_t, rank_t, start_col)
    x_rows = _scatter_rows(xw, dest_t, n_blocks * EXPERT_BLOCK)
    yb = _experts(x_rows, block_exp, n_used, w_exp_gate_up, w_exp_down, layer)
    return _ffn_out(x, yb, dest_t, gw_t.T, ws_gu, ws_dn, g, beta, alpha)


def kernel(x_prompt, x_sample, mem_prompt, cache_a_k, cache_a_v, cache_b_k, cache_b_v, cache_b_idx_k,
           cache_mem_k, cache_mem_v, w_in, rel_bias, w_mem_kv, w_branch, w_out, ln_mix_g, ln_mix_b,
           w_router, b_router, w_exp_gate_up, w_exp_down, w_sh_gate_up, w_sh_down, ln_ffn_g, ln_ffn_b):
    n_batch, seq, d = x_prompt.shape
    s_batch, s_seq, _ = x_sample.shape
    depth = w_in.shape[0]
    past = cache_b_k.shape[2]
    assert s_seq == CHUNK and seq % CHUNK == 0 and cache_a_k.shape[2] == BAND and seq >= BAND
    alpha = (2 * depth) ** 0.25
    n_p, n_s = n_batch * seq, s_batch * s_seq
    n_tot = n_p + n_s
    tm = 512
    assert seq % tm == 0 and n_p % tm == 0 and n_s % tm == 0 and tm % CHUNK == 0

    x = jnp.concatenate([x_prompt.reshape(n_p, d), x_sample.reshape(n_s, d)])

    pos = jnp.concatenate([jnp.arange(seq), jnp.tile(past + jnp.arange(s_seq), tm // s_seq)])
    tab_heads = _rope_tables(pos, LANES)
    tab_kiw = _rope_tables(pos, HEAD_DIM)
    p_tiles, seq_tiles = n_p // tm, seq // tm
    table_index = lambda i: jnp.where(i < p_tiles, i % seq_tiles, seq_tiles)

    wd = WIDTH
    c_qi, c_ki, c_wi, c_qm, c_gates = 6 * wd, 7 * wd, 7 * wd + HEAD_DIM, 7 * wd + HEAD_DIM + N_HEADS, 8 * wd + HEAD_DIM + N_HEADS
    cache_a_k = cache_a_k.reshape(depth, s_batch, BAND, wd)
    cache_a_v = cache_a_v.reshape(depth, s_batch, BAND, wd)
    cache_b_k = cache_b_k.reshape(depth, s_batch, past, wd)
    cache_b_v = cache_b_v.reshape(depth, s_batch, past, wd)
    cache_mem_k = cache_mem_k.reshape(depth, s_batch, N_MEM, wd)
    cache_mem_v = cache_mem_v.reshape(depth, s_batch, N_MEM, wd)
    mem_rows = mem_prompt.reshape(n_batch * N_MEM, d)

    outs = {k: [] for k in ("p_ak", "p_av", "p_bk", "p_bv", "p_ik", "p_mk", "p_mv",
                            "s_ak", "s_av", "s_bk", "s_bv", "s_ik")}
    for l in range(depth):
        w = w_in[l]
        w_a = w[:, :3 * wd].astype(BF16)
        w_b = w[:, 3 * wd:7 * wd].astype(BF16)
        w_c = w[:, c_qm:].astype(BF16)
        w_d = jnp.concatenate([w[:, c_ki:c_qm], jnp.zeros((d, LANES - HEAD_DIM - N_HEADS), F32)], 1).astype(BF16)

        qa, ka, va = _project(x, w_a, (wd, wd, wd), split_flags=(False, True, True), n_prompt=n_p,
                              tm=tm, name="proj_a")
        qb, kb, vb, qi = _project(x, w_b, (wd, wd, wd, wd), (True, True, False, True), tab_heads, table_index,
                                  split_flags=(False, True, True, False), n_prompt=n_p, tm=tm, name="proj_b")
        qm, gates = _project(x, w_c, (wd, 3 * d), tm=tm, name="proj_c")
        (kiw,) = _project(x, w_d, (LANES,), (True,), tab_kiw, table_index, split_flags=(True,), n_prompt=n_p,
                          tm=tm, name="proj_d")
        mk, mv = _project(mem_rows, w_mem_kv[l].astype(BF16), (wd, wd), tm=N_MEM, name="proj_mem")

        a = (_band_prompt(qa, ka[0], va[0], rel_bias[l], n_batch, seq),
             _band_sample(qa, ka[1], va[1], cache_a_k, cache_a_v, l, _band_bias(rel_bias[l], CHUNK), s_batch, n_p))
        b = (_dsa_prompt(qb, kb[0], vb[0], qi, kiw[0], n_batch, seq),
             _dsa_sample(qb, kb[1], vb[1], qi, kiw[1], cache_b_k, cache_b_v, cache_b_idx_k, l, s_batch, n_p))
        m = (_mem_attn(qm, mk, mv, lambda bb: (bb, 0), n_batch, seq, 0, tm),
             _mem_attn(qm, cache_mem_k, cache_mem_v, lambda bb, l=l: (l, bb, 0, 0), s_batch, s_seq, n_p, s_seq))

        x1, x1w = _merge(x, (a, b, m), gates, w_branch[l].astype(BF16), w_out[l].astype(BF16),
                         ln_mix_g[l][None], ln_mix_b[l][None], alpha)
        x = _moe(x1, x1w, l, w_router[l].T.astype(BF16), b_router[l][:, None], w_exp_gate_up, w_exp_down,
                 w_sh_gate_up[l].astype(BF16), w_sh_down[l].astype(BF16),
                 ln_ffn_g[l][None], ln_ffn_b[l][None], alpha)

        outs["p_ak"].append(ka[0].reshape(n_batch, seq, wd)[:, -BAND:])
        outs["p_av"].append(va[0].reshape(n_batch, seq, wd)[:, -BAND:])
        outs["p_bk"].append(kb[0].reshape(n_batch, seq, wd))
        outs["p_bv"].append(vb[0].reshape(n_batch, seq, wd))
        outs["p_ik"].append(kiw[0][:, :HEAD_DIM].reshape(n_batch, seq, HEAD_DIM))
        outs["p_mk"].append(mk.reshape(n_batch, N_MEM, wd))
        outs["p_mv"].append(mv.reshape(n_batch, N_MEM, wd))
        ka_s = ka[1].reshape(s_batch, s_seq, wd)
        va_s = va[1].reshape(s_batch, s_seq, wd)
        outs["s_ak"].append(jnp.concatenate([cache_a_k[l], ka_s], 1)[:, -BAND:])
        outs["s_av"].append(jnp.concatenate([cache_a_v[l], va_s], 1)[:, -BAND:])
        outs["s_bk"].append(kb[1].reshape(s_batch, s_seq, wd))
        outs["s_bv"].append(vb[1].reshape(s_batch, s_seq, wd))
        outs["s_ik"].append(kiw[1][:, :HEAD_DIM].reshape(s_batch, s_seq, HEAD_DIM))

    heads = lambda t: t.reshape(t.shape[:-1] + (N_HEADS, HEAD_DIM))
    mheads = lambda t: t.reshape(t.shape[:-1] + (M_HEADS, M_HEAD_DIM))
    st = lambda key: jnp.stack(outs[key])
    return (x[:n_p].reshape(n_batch, seq, d), x[n_p:].reshape(s_batch, s_seq, d),
            heads(st("p_ak")), heads(st("p_av")), heads(st("p_bk")), heads(st("p_bv")), st("p_ik"),
            mheads(st("p_mk")), mheads(st("p_mv")),
            heads(st("s_ak")), heads(st("s_av")), heads(st("s_bk")), heads(st("s_bv")), st("s_ik"))
```

```python
import functools

import jax
import jax.numpy as jnp
from jax import lax
from jax.experimental import pallas as pl
from jax.experimental.pallas import tpu as pltpu

F32 = jnp.float32
BF16 = jnp.bfloat16
I32 = jnp.int32

CHUNK = 64
HEAD_DIM = 64
N_HEADS = 8
WIDTH = N_HEADS * HEAD_DIM
BAND = 8 * CHUNK
MAX_REL = 128
TOPK_KEYS = 256
M_HEADS = 4
M_HEAD_DIM = 128
N_MEM = 256
ROPE_THETA = 10000.0
N_EXPERTS = 256
EXPERT_TOPK = 8
N_GROUPS = 8
TOPK_GROUPS = 4
EXPERT_FF = 256
ROUTE_SCALE = 2.5
EXPERT_BLOCK = 256
LN_EPS = 1e-5

LANES = 128
SUBLANES = 8
NEG_BIG = -1e30
INT_MIN = -(2 ** 31)
VMEM_LIMIT = 48 * 1024 * 1024

_NT_DIMS = (((1,), (1,)), ((), ()))


def _params(semantics):
    return pltpu.CompilerParams(dimension_semantics=semantics, vmem_limit_bytes=VMEM_LIMIT)


def _proj_body(*refs, widths, rope_flags, split_flags, prompt_tiles):
    has_rope = any(rope_flags)
    x_ref, w_ref = refs[0], refs[1]
    if has_rope:
        cos_ref, sin_lo_ref, sin_hi_ref = refs[2:5]
        outs = list(refs[5:])
    else:
        outs = list(refs[2:])
    in_prompt = pl.program_id(0) < prompt_tiles
    xb = x_ref[...].astype(BF16)
    off = 0
    for g, width in enumerate(widths):
        y = jnp.dot(xb, w_ref[:, off:off + width], preferred_element_type=F32)
        if rope_flags[g]:
            cos, sin_lo, sin_hi = cos_ref[...], sin_lo_ref[...], sin_hi_ref[...]
            chunks = [yk * cos + pltpu.roll(yk, LANES - 32, 1) * sin_lo + pltpu.roll(yk, 32, 1) * sin_hi
                      for yk in (y[:, k * LANES:(k + 1) * LANES] for k in range(width // LANES))]
        else:
            chunks = [y]

        def store(ref, chunks=chunks):
            step = chunks[0].shape[1]
            for k, chunk in enumerate(chunks):
                ref[:, k * step:(k + 1) * step] = chunk

        if split_flags[g]:
            prompt_ref, sample_ref = outs.pop(0), outs.pop(0)
            pl.when(in_prompt)(functools.partial(store, prompt_ref))
            pl.when(jnp.logical_not(in_prompt))(functools.partial(store, sample_ref))
        else:
            store(outs.pop(0))
        off += width


def _project(x, w, widths, rope_flags=None, tables=None, table_index=None, split_flags=None, n_prompt=None,
             tm=512, name="proj"):
    n, d = x.shape
    rope_flags = rope_flags or (False,) * len(widths)
    split_flags = split_flags or (False,) * len(widths)
    assert n % tm == 0 and w.shape == (d, sum(widths))
    prompt_tiles = n // tm
    if any(split_flags):
        assert n_prompt % tm == 0 and 0 < n_prompt < n
        prompt_tiles = n_prompt // tm
    in_specs = [pl.BlockSpec((tm, d), lambda i: (i, 0)),
                pl.BlockSpec(w.shape, lambda i: (0, 0))]
    args = [x, w]
    if any(rope_flags):
        for t in tables:
            in_specs.append(pl.BlockSpec((tm, LANES), lambda i: (table_index(i), 0)))
            args.append(t)
    out_shape, out_specs = [], []
    for wd, split in zip(widths, split_flags):
        if split:
            out_shape += [jax.ShapeDtypeStruct((n_prompt, wd), F32), jax.ShapeDtypeStruct((n - n_prompt, wd), F32)]
            out_specs += [pl.BlockSpec((tm, wd), lambda i: (jnp.minimum(i, prompt_tiles - 1), 0)),
                          pl.BlockSpec((tm, wd), lambda i: (jnp.maximum(i - prompt_tiles, 0), 0))]
        else:
            out_shape.append(jax.ShapeDtypeStruct((n, wd), F32))
            out_specs.append(pl.BlockSpec((tm, wd), lambda i: (i, 0)))
    flat = pl.pallas_call(
        functools.partial(_proj_body, widths=tuple(widths), rope_flags=tuple(rope_flags),
                          split_flags=tuple(split_flags), prompt_tiles=prompt_tiles),
        out_shape=out_shape,
        grid=(n // tm,),
        in_specs=in_specs,
        out_specs=out_specs,
        compiler_params=_params(("arbitrary",)),
        name=name,
    )(*args)
    flat = list(flat)
    return [(flat.pop(0), flat.pop(0)) if split else flat.pop(0) for split in split_flags]


def _rope_tables(pos, rope_lanes):
    half = HEAD_DIM // 2
    inv_freq = ROPE_THETA ** (-jnp.arange(half, dtype=F32) / half)
    ang = pos.astype(F32)[:, None] * inv_freq[None, :]
    cos, sin = jnp.cos(ang), jnp.sin(ang)
    reps = LANES // HEAD_DIM
    zero = jnp.zeros_like(sin)
    cos_t = jnp.tile(jnp.concatenate([cos, cos], -1), (1, reps))
    sin_lo = jnp.tile(jnp.concatenate([-sin, zero], -1), (1, reps))
    sin_hi = jnp.tile(jnp.concatenate([zero, sin], -1), (1, reps))
    keep = (jnp.arange(LANES) < rope_lanes)[None, :]
    return (jnp.where(keep, cos_t, 1.0), jnp.where(keep, sin_lo, 0.0), jnp.where(keep, sin_hi, 0.0))


def _reduce_rows(x, op, keep_groups=False):
    rows, n = x.shape
    groups = next(g for g in (8, 4, 2, 1) if rows % (8 * g) == 0 and rows // (8 * g) >= 4 or g == 1)
    parts = op(x.reshape(groups, rows // (8 * groups), 8, n), axis=1)
    if keep_groups:
        return parts
    return op(op(parts, axis=0), axis=0, keepdims=True)


def _head_mask(rows, head_in_pair):
    lane = lax.broadcasted_iota(I32, (rows, LANES), 1)
    lo = head_in_pair * HEAD_DIM
    return (lane >= lo) & (lane < lo + HEAD_DIM)


def _pair_merge(o_even, o_odd):
    lane = lax.broadcasted_iota(I32, o_even.shape, 1)
    return jnp.where(lane < HEAD_DIM, o_even, o_odd)


def _band_sample_body(q_ref, kn_ref, vn_ref, kh_ref, vh_ref, bias_ref, o_ref, kwin, vwin):
    kwin[0:BAND, :] = kh_ref[...].astype(BF16)
    vwin[0:BAND, :] = vh_ref[...].astype(BF16)
    kwin[BAND:, :] = kn_ref[...].astype(BF16)
    vwin[BAND:, :] = vn_ref[...].astype(BF16)
    q = q_ref[...] * (HEAD_DIM ** -0.5)
    for hp in range(N_HEADS // 2):
        cols = slice(hp * LANES, (hp + 1) * LANES)
        outs = []
        for hh in range(2):
            qm = jnp.where(_head_mask(CHUNK, hh), q[:, cols], 0.0).astype(BF16)
            s = lax.dot_general(qm, kwin[:, cols], _NT_DIMS, preferred_element_type=F32) + bias_ref[2 * hp + hh]
            p = jnp.exp(s - jnp.max(s, axis=1, keepdims=True))
            denom = jnp.sum(p, axis=1, keepdims=True)
            outs.append(jnp.dot(p.astype(BF16), vwin[:, cols], preferred_element_type=F32) / denom)
        o_ref[:, cols] = _pair_merge(outs[0], outs[1])


def _band_prompt_body(q_ref, k_ref, v_ref, bias_ref, o_ref, k_even, k_odd, v_t, q_t, s_sc, p_sc, o_t):
    t = pl.program_id(1)
    nq = q_ref.shape[0]
    win = BAND + nq
    seq = k_ref.shape[0]
    stage_rows = 256

    @pl.when(t == 0)
    def _():
        k_even[0:BAND, :] = jnp.zeros((BAND, WIDTH), BF16)
        k_odd[0:BAND, :] = jnp.zeros((BAND, WIDTH), BF16)
        v_t[:, 0:BAND] = jnp.zeros((WIDTH, BAND), BF16)
        for r in range(0, seq, stage_rows):
            ke, ko = _split_pair_lanes(k_ref[r:r + stage_rows, :])
            k_even[BAND + r:BAND + r + stage_rows, :] = ke.astype(BF16)
            k_odd[BAND + r:BAND + r + stage_rows, :] = ko.astype(BF16)
            v_t[:, BAND + r:BAND + r + stage_rows] = v_ref[r:r + stage_rows, :].T.astype(BF16)

    start = pl.multiple_of(t * nq, nq)
    window = pl.ds(start, win)
    q_t[...] = (q_ref[...] * (HEAD_DIM ** -0.5)).T.astype(BF16)
    before_start = jnp.where(lax.broadcasted_iota(I32, (win, nq), 0) >= BAND - t * nq, 0.0, NEG_BIG)
    for h in range(N_HEADS):
        pair = slice((h // 2) * LANES, (h // 2 + 1) * LANES)
        k_ref_h = k_even if h % 2 == 0 else k_odd
        s_sc[h * win:(h + 1) * win, :] = (jnp.dot(k_ref_h[window, pair], q_t[pair, :], preferred_element_type=F32)
                                          + bias_ref[h] + before_start)
    for h in range(N_HEADS):
        s = s_sc[h * win:(h + 1) * win, :]
        p = jnp.exp(s - _reduce_rows(s, jnp.max))
        p_sc[h * win:(h + 1) * win, :] = (p / _reduce_rows(p, jnp.sum)).astype(BF16)
    for h in range(N_HEADS):
        dims = slice(h * HEAD_DIM, (h + 1) * HEAD_DIM)
        o_t[dims, :] = jnp.dot(v_t[dims, window], p_sc[h * win:(h + 1) * win, :], preferred_element_type=F32)
    o_ref[...] = o_t[...].T


def _band_bias(table, n_rows):
    win = BAND + n_rows
    assert n_rows <= MAX_REL <= BAND
    n_heads = table.shape[0]
    g = jnp.concatenate([jnp.broadcast_to(table[:, 2 * MAX_REL:], (n_heads, win - MAX_REL)),
                         table[:, MAX_REL - n_rows:2 * MAX_REL][:, ::-1],
                         jnp.zeros((n_heads, 1), table.dtype)], axis=1).astype(F32)
    period = g.shape[1]
    skew = jnp.tile(g, (1, n_rows))[:, :n_rows * (period - 1)].reshape(n_heads, n_rows, period - 1)
    return skew[:, :, n_rows - 1:n_rows - 1 + win]


def _band_prompt(q, k, v, table, n_batch, seq, q_rows=128):
    nt = seq // q_rows
    win = BAND + q_rows
    assert seq % q_rows == 0 and q_rows % CHUNK == 0
    key_chunk = jnp.arange(win)[:, None] // CHUNK - BAND // CHUNK
    query_chunk = jnp.arange(q_rows)[None, :] // CHUNK
    visible = (key_chunk <= query_chunk) & (key_chunk >= query_chunk - BAND // CHUNK)
    bias = jnp.where(visible[None], jnp.swapaxes(_band_bias(table, q_rows), 1, 2), NEG_BIG)
    tile_spec = pl.BlockSpec((q_rows, WIDTH), lambda b, t: (b * nt + t, 0))
    seq_spec = pl.BlockSpec((seq, WIDTH), lambda b, t: (b, 0))
    return pl.pallas_call(
        _band_prompt_body,
        out_shape=jax.ShapeDtypeStruct((n_batch * seq, WIDTH), F32),
        grid=(n_batch, nt),
        in_specs=[tile_spec, seq_spec, seq_spec, pl.BlockSpec(bias.shape, lambda b, t: (0, 0, 0))],
        out_specs=tile_spec,
        scratch_shapes=[pltpu.VMEM((BAND + seq, WIDTH), BF16), pltpu.VMEM((BAND + seq, WIDTH), BF16),
                        pltpu.VMEM((WIDTH, BAND + seq), BF16), pltpu.VMEM((WIDTH, q_rows), BF16),
                        pltpu.VMEM((N_HEADS * win, q_rows), F32), pltpu.VMEM((N_HEADS * win, q_rows), BF16),
                        pltpu.VMEM((WIDTH, q_rows), F32)],
        compiler_params=_params(("parallel", "arbitrary")),
        name="band_prompt",
    )(q, k, v, bias)


def _band_sample(q, k, v, k_hist, v_hist, layer, bias, n_batch, row0):
    blk0 = row0 // CHUNK
    hist_spec = pl.BlockSpec((None, None, BAND, WIDTH), lambda b: (layer, b, 0, 0))
    new_spec = pl.BlockSpec((CHUNK, WIDTH), lambda b: (b, 0))
    return pl.pallas_call(
        _band_sample_body,
        out_shape=jax.ShapeDtypeStruct((n_batch * CHUNK, WIDTH), F32),
        grid=(n_batch,),
        in_specs=[pl.BlockSpec((CHUNK, WIDTH), lambda b: (blk0 + b, 0)), new_spec, new_spec, hist_spec, hist_spec,
                  pl.BlockSpec(bias.shape, lambda b: (0, 0, 0))],
        out_specs=new_spec,
        scratch_shapes=[pltpu.VMEM((BAND + CHUNK, WIDTH), BF16), pltpu.VMEM((BAND + CHUNK, WIDTH), BF16)],
        compiler_params=_params(("parallel",)),
        name="band_sample",
    )(q, k, v, k_hist, v_hist, bias)


def _mem_body(q_ref, k_ref, v_ref, o_ref):
    scale = M_HEAD_DIM ** -0.5
    for h in range(M_HEADS):
        cols = slice(h * M_HEAD_DIM, (h + 1) * M_HEAD_DIM)
        s = lax.dot_general(q_ref[:, cols].astype(BF16), k_ref[:, cols].astype(BF16), _NT_DIMS,
                            preferred_element_type=F32) * scale
        p = jnp.exp(s - jnp.max(s, axis=1, keepdims=True))
        denom = jnp.sum(p, axis=1, keepdims=True)
        o = jnp.dot(p.astype(BF16), v_ref[:, cols].astype(BF16), preferred_element_type=F32)
        o_ref[:, cols] = o / denom


def _mem_attn(q, mk, mv, kv_index, n_batch, rows_per_batch, row0, tq):
    width = M_HEADS * M_HEAD_DIM
    per = rows_per_batch // tq
    blk0 = row0 // tq
    kv_block = (None,) * (mk.ndim - 2) + (N_MEM, width)
    kv_spec = pl.BlockSpec(kv_block, lambda b, t: kv_index(b))
    return pl.pallas_call(
        _mem_body,
        out_shape=jax.ShapeDtypeStruct((n_batch * rows_per_batch, width), F32),
        grid=(n_batch, per),
        in_specs=[pl.BlockSpec((tq, width), lambda b, t: (blk0 + b * per + t, 0)), kv_spec, kv_spec],
        out_specs=pl.BlockSpec((tq, width), lambda b, t: (b * per + t, 0)),
        compiler_params=_params(("parallel", "parallel")),
        name="mem_attn",
    )(q, mk, mv)


def _split_pair_lanes(x):
    lane = lax.broadcasted_iota(I32, x.shape, 1)
    even = (lane & (LANES - 1)) < HEAD_DIM
    return jnp.where(even, x, 0.0), jnp.where(even, 0.0, x)


def _dsa_body(*refs, sample, key_block, n_keys, n_sel):
    n_in = 8 if sample else 6
    if sample:
        q_ref, qi_ref, kiwq_ref, kn_ref, vn_ref, kc_ref, vc_ref, kic_ref = refs[:n_in]
    else:
        q_ref, qi_ref, kiwq_ref, k_ref, v_ref, kiw_ref = refs[:n_in]
    o_ref = refs[n_in]
    (k_even, k_odd, v_t, ki_lo, ki_hi, keybuf, maskbuf, q_t, qi_t, kiwq_t, m_sc, l_sc, a_sc, acc_t,
     s_sc, p_sc) = refs[n_in + 1:]
    kb = key_block
    n_pad = k_even.shape[0]
    nq = q_t.shape[1]
    widen = lambda rows: jnp.concatenate([rows] * (nq // rows.shape[0]), axis=0)
    stage_rows = 256

    def stage_kv(row0, k_new, v_new):
        n_new = k_new.shape[0]
        ke, ko = _split_pair_lanes(k_new)
        k_even[row0:row0 + n_new, :] = ke.astype(BF16)
        k_odd[row0:row0 + n_new, :] = ko.astype(BF16)
        v_t[:, row0:row0 + n_new] = v_new.T.astype(BF16)

    def stage_index_keys(row0, kiw):
        lane = lax.broadcasted_iota(I32, kiw.shape, 1)
        ki_lo[row0:row0 + kiw.shape[0], :] = jnp.where(lane < HEAD_DIM, kiw, 0.0).astype(BF16)
        ki_hi[row0:row0 + kiw.shape[0], :] = jnp.where(lane < HEAD_DIM, 0.0, pltpu.roll(kiw, HEAD_DIM, 1)).astype(BF16)

    def stage_keys():
        if sample:
            n_hist = kc_ref.shape[0]
            for r in range(0, n_hist, stage_rows):
                stage_kv(r, kc_ref[r:r + stage_rows, :], vc_ref[r:r + stage_rows, :])
            kic = kic_ref[...]
            zeros = jnp.zeros_like(kic)
            ki_lo[0:n_hist, :] = jnp.concatenate([kic, zeros], axis=1).astype(BF16)
            ki_hi[0:n_hist, :] = jnp.concatenate([zeros, kic], axis=1).astype(BF16)
            stage_kv(n_hist, kn_ref[...], vn_ref[...])
            stage_index_keys(n_hist, kiwq_ref[...])
            if n_pad > n_keys:
                for ref in (k_even, k_odd, ki_lo, ki_hi):
                    ref[n_keys:, :] = jnp.zeros((n_pad - n_keys, ref.shape[1]), BF16)
                v_t[:, n_keys:] = jnp.zeros((v_t.shape[0], n_pad - n_keys), BF16)
        else:
            for r in range(0, n_keys, stage_rows):
                stage_kv(r, k_ref[r:r + stage_rows, :], v_ref[r:r + stage_rows, :])
            stage_index_keys(0, kiw_ref[...])

    if sample:
        stage_keys()
        limit = n_keys
        nb = n_pad // kb
        search_bits = 32
    else:
        t = pl.program_id(1)
        pl.when(t == 0)(stage_keys)
        chunk = t * (nq // CHUNK) + lax.broadcasted_iota(I32, (1, nq), 1) // CHUNK
        limit = (chunk + 1) * CHUNK
        limit_max = (t + 1) * nq
        nb = (limit_max + kb - 1) // kb
        search_bits = jnp.where(limit_max <= n_sel, 0, 32)

    def block_rows(j):
        return pl.ds(pl.multiple_of(j * kb, kb), kb)

    def key_pos(j):
        return j * kb + lax.broadcasted_iota(I32, (kb, nq), 0)

    q_t[...] = widen(q_ref[...] * (HEAD_DIM ** -0.5)).T.astype(BF16)
    qi_t[...] = widen(qi_ref[...]).T.astype(BF16)
    kiwq_t[...] = widen(kiwq_ref[...]).T

    def score_block(j, carry):
        acc = jnp.zeros((kb, nq), F32)
        for h in range(N_HEADS):
            pair = slice((h // 2) * LANES, (h // 2 + 1) * LANES)
            ki_ref = ki_lo if h % 2 == 0 else ki_hi
            d = jnp.dot(ki_ref[block_rows(j), :], qi_t[pair, :], preferred_element_type=F32)
            acc = acc + kiwq_t[HEAD_DIM + h:HEAD_DIM + h + 1, :] * jnp.maximum(d, 0.0)
        acc = acc + 0.0
        acc = jnp.where(key_pos(j) < limit, acc, -jnp.inf)
        bits = lax.bitcast_convert_type(acc, I32)
        keybuf[block_rows(j), :] = bits ^ ((bits >> 31) & 0x7FFFFFFF)
        return carry

    lax.fori_loop(0, nb, score_block, 0)

    def count(pred_fn):
        def body(j, part):
            hit = jnp.where(pred_fn(keybuf[block_rows(j), :], j), 1.0, 0.0)
            return part + _reduce_rows(hit, jnp.sum, keep_groups=True)
        groups = jax.eval_shape(functools.partial(_reduce_rows, op=jnp.sum, keep_groups=True),
                                jax.ShapeDtypeStruct((kb, nq), F32)).shape[0]
        part = lax.fori_loop(0, nb, body, jnp.zeros((groups, 8, nq), F32))
        return jnp.sum(jnp.sum(part, axis=0), axis=0, keepdims=True)

    def bit_step(i, prefix):
        cand = prefix | lax.shift_left(jnp.int32(1), 31 - i)
        thr_i = cand ^ INT_MIN
        cnt = count(lambda kblk, j: kblk >= thr_i)
        return jnp.where(cnt >= n_sel, cand, prefix)

    prefix = lax.fori_loop(0, search_bits, bit_step, jnp.zeros((1, nq), I32))
    thr = prefix ^ INT_MIN
    above = count(lambda kblk, j: kblk > thr)
    at_least = count(lambda kblk, j: kblk >= thr)
    need = n_sel - above
    surplus = at_least > n_sel

    pos_bits = max(1, (n_pad - 1).bit_length() + 1)
    any_surplus = jnp.max(jnp.where(surplus, 1.0, 0.0)) > 0.0

    def pos_step(i, bound):
        cand = bound | lax.shift_left(jnp.int32(1), pos_bits - 1 - i)
        cnt = count(lambda kblk, j: (kblk == thr) & (key_pos(j) < cand))
        return jnp.where(cnt <= need, cand, bound)

    bound = lax.fori_loop(0, jnp.where(any_surplus, pos_bits, 0), pos_step, jnp.zeros((1, nq), I32))
    bound = jnp.where(surplus, bound, 2 ** pos_bits - 1)

    def mask_block(j, carry):
        kblk = keybuf[block_rows(j), :]
        pos = key_pos(j)
        sel = ((kblk > thr) | ((kblk == thr) & (pos < bound))) & (pos < limit)
        maskbuf[block_rows(j), :] = jnp.where(sel, 0.0, NEG_BIG)
        return carry

    lax.fori_loop(0, nb, mask_block, 0)

    m_sc[...] = jnp.full(m_sc.shape, NEG_BIG, F32)
    l_sc[...] = jnp.zeros(l_sc.shape, F32)
    acc_t[...] = jnp.zeros(acc_t.shape, F32)

    def attend(j, carry):
        mask = maskbuf[block_rows(j), :]
        for h in range(N_HEADS):
            pair = slice((h // 2) * LANES, (h // 2 + 1) * LANES)
            k_ref_h = k_even if h % 2 == 0 else k_odd
            s_sc[h * kb:(h + 1) * kb, :] = jnp.dot(k_ref_h[block_rows(j), pair], q_t[pair, :],
                                                   preferred_element_type=F32) + mask
        for h in range(N_HEADS):
            stat = slice(h * 8, (h + 1) * 8)
            s = s_sc[h * kb:(h + 1) * kb, :]
            m_prev = m_sc[stat, :]
            m_new = jnp.maximum(m_prev, _reduce_rows(s, jnp.max))
            alpha = jnp.exp(m_prev - m_new)
            p = jnp.exp(s - m_new[0:1, :])
            l_sc[stat, :] = alpha * l_sc[stat, :] + _reduce_rows(p, jnp.sum)
            m_sc[stat, :] = m_new
            a_sc[stat, :] = alpha
            p_sc[h * kb:(h + 1) * kb, :] = p.astype(BF16)
        for h in range(N_HEADS):
            dims = slice(h * HEAD_DIM, (h + 1) * HEAD_DIM)
            pv = jnp.dot(v_t[dims, block_rows(j)], p_sc[h * kb:(h + 1) * kb, :], preferred_element_type=F32)
            acc_t[dims, :] = a_sc[h * 8:h * 8 + 1, :] * acc_t[dims, :] + pv
        return carry

    lax.fori_loop(0, nb, attend, 0)
    for h in range(N_HEADS):
        dims = slice(h * HEAD_DIM, (h + 1) * HEAD_DIM)
        acc_t[dims, :] = acc_t[dims, :] / l_sc[h * 8:h * 8 + 1, :]
    o_ref[...] = acc_t[...].T[0:o_ref.shape[0], :]


def _dsa_scratch(nq, n_pad, key_block):
    nq = max(nq, LANES)
    return [pltpu.VMEM((n_pad, WIDTH), BF16), pltpu.VMEM((n_pad, WIDTH), BF16),
            pltpu.VMEM((WIDTH, n_pad), BF16),
            pltpu.VMEM((n_pad, LANES), BF16), pltpu.VMEM((n_pad, LANES), BF16),
            pltpu.VMEM((n_pad, nq), I32), pltpu.VMEM((n_pad, nq), F32),
            pltpu.VMEM((WIDTH, nq), BF16), pltpu.VMEM((WIDTH, nq), BF16),
            pltpu.VMEM((LANES, nq), F32),
            pltpu.VMEM((N_HEADS * 8, nq), F32), pltpu.VMEM((N_HEADS * 8, nq), F32),
            pltpu.VMEM((N_HEADS * 8, nq), F32),
            pltpu.VMEM((WIDTH, nq), F32),
            pltpu.VMEM((N_HEADS * key_block, nq), F32),
            pltpu.VMEM((N_HEADS * key_block, nq), BF16)]


def _dsa_prompt(q, k, v, qi, kiw, n_batch, seq, key_block=256, q_rows=256):
    nt = seq // q_rows
    key_block = min(key_block, seq)
    assert seq % key_block == 0 and seq % q_rows == 0 and q_rows % CHUNK == 0
    n_sel = min(TOPK_KEYS, seq // 4)
    tile_spec = lambda width: pl.BlockSpec((q_rows, width), lambda b, t: (b * nt + t, 0))
    seq_spec = lambda width: pl.BlockSpec((seq, width), lambda b, t: (b, 0))
    return pl.pallas_call(
        functools.partial(_dsa_body, sample=False, key_block=key_block, n_keys=seq, n_sel=n_sel),
        out_shape=jax.ShapeDtypeStruct((n_batch * seq, WIDTH), F32),
        grid=(n_batch, nt),
        in_specs=[tile_spec(WIDTH), tile_spec(WIDTH), tile_spec(LANES),
                  seq_spec(WIDTH), seq_spec(WIDTH), seq_spec(LANES)],
        out_specs=tile_spec(WIDTH),
        scratch_shapes=_dsa_scratch(q_rows, seq, key_block),
        compiler_params=_params(("parallel", "arbitrary")),
        name="dsa_prompt",
    )(q, qi, kiw, k, v, kiw)


def _dsa_sample(q, k, v, qi, kiw, k_cache, v_cache, ki_cache, layer, n_batch, row0, key_block=768):
    past = k_cache.shape[2]
    n_keys = past + CHUNK
    key_block = min(key_block, -(-n_keys // 256) * 256)
    n_pad = -(-n_keys // key_block) * key_block
    n_sel = min(TOPK_KEYS, n_keys // 4)
    blk0 = row0 // CHUNK
    query_spec = pl.BlockSpec((CHUNK, WIDTH), lambda b: (blk0 + b, 0))
    new_spec = lambda width: pl.BlockSpec((CHUNK, width), lambda b: (b, 0))
    cache_spec = lambda width: pl.BlockSpec((None, None, past, width), lambda b: (layer, b, 0, 0))
    return pl.pallas_call(
        functools.partial(_dsa_body, sample=True, key_block=key_block, n_keys=n_keys, n_sel=n_sel),
        out_shape=jax.ShapeDtypeStruct((n_batch * CHUNK, WIDTH), F32),
        grid=(n_batch,),
        in_specs=[query_spec, query_spec, new_spec(LANES), new_spec(WIDTH), new_spec(WIDTH),
                  cache_spec(WIDTH), cache_spec(WIDTH), cache_spec(HEAD_DIM)],
        out_specs=new_spec(WIDTH),
        scratch_shapes=_dsa_scratch(CHUNK, n_pad, key_block),
        compiler_params=_params(("parallel",)),
        name="dsa_sample",
    )(q, qi, kiw, k, v, k_cache, v_cache, ki_cache)


def _layer_norm(z, g, b):
    mu = jnp.mean(z, axis=-1, keepdims=True)
    zc = z - mu
    var = jnp.mean(zc * zc, axis=-1, keepdims=True)
    return zc * lax.rsqrt(var + LN_EPS) * g + b


HI_HALF = -65536


def _pack_rows(x):
    half = x.shape[1] // 2
    lo = lax.bitcast_convert_type(x[:, :half].astype(BF16).astype(F32), I32)
    hi = lax.bitcast_convert_type(x[:, half:].astype(BF16).astype(F32), I32)
    return (hi & HI_HALF) | lax.shift_right_logical(lo, 16)


def _unpack_rows(w):
    lo = lax.bitcast_convert_type(lax.shift_left(w, 16), F32).astype(BF16)
    hi = lax.bitcast_convert_type(w & HI_HALF, F32).astype(BF16)
    return lo, hi


def _merge_body(x_ref, ap_ref, as_ref, bp_ref, bs_ref, mp_ref, ms_ref, gates_ref, wbr_ref, wout_ref, g_ref, beta_ref,
                o_ref, ow_ref, *, alpha, prompt_tiles):
    d = x_ref.shape[1]
    in_prompt = pl.program_id(0) < prompt_tiles
    mix = None
    for n, (p_ref, s_ref) in enumerate(((ap_ref, as_ref), (bp_ref, bs_ref), (mp_ref, ms_ref))):
        branch = jnp.where(in_prompt, p_ref[...], s_ref[...])
        proj = jnp.dot(branch.astype(BF16), wbr_ref[n], preferred_element_type=F32)
        term = jax.nn.sigmoid(gates_ref[:, n * d:(n + 1) * d]) * proj
        mix = term if mix is None else mix + term
    y = jnp.dot(mix.astype(BF16), wout_ref[...], preferred_element_type=F32)
    out = _layer_norm(alpha * x_ref[...] + y, g_ref[...], beta_ref[...])
    o_ref[...] = out
    ow_ref[...] = _pack_rows(out)


def _merge(x, branches, gates, w_branch, w_out, g, beta, alpha, tm=256):
    n, d = x.shape
    prompt_tiles = branches[0][0].shape[0] // tm
    assert all(p.shape[0] == prompt_tiles * tm and (n - p.shape[0]) == s.shape[0] and s.shape[0] % tm == 0
               for p, s in branches)
    packed = jax.eval_shape(_pack_rows, jax.ShapeDtypeStruct((tm, d), F32))
    row = lambda width: pl.BlockSpec((tm, width), lambda i: (i, 0))
    full = lambda arr: pl.BlockSpec(arr.shape, lambda i: (0,) * arr.ndim)
    prompt_row = lambda width: pl.BlockSpec((tm, width), lambda i: (jnp.minimum(i, prompt_tiles - 1), 0))
    sample_row = lambda width: pl.BlockSpec((tm, width), lambda i: (jnp.maximum(i - prompt_tiles, 0), 0))
    branch_specs, branch_args = [], []
    for p, s in branches:
        branch_specs += [prompt_row(p.shape[1]), sample_row(s.shape[1])]
        branch_args += [p, s]
    return pl.pallas_call(
        functools.partial(_merge_body, alpha=alpha, prompt_tiles=prompt_tiles),
        out_shape=[jax.ShapeDtypeStruct((n, d), F32), jax.ShapeDtypeStruct((n, packed.shape[1]), packed.dtype)],
        grid=(n // tm,),
        in_specs=[row(d)] + branch_specs + [row(gates.shape[1]), full(w_branch), full(w_out), full(g), full(beta)],
        out_specs=[row(d), row(packed.shape[1])],
        compiler_params=_params(("parallel",)),
        name="merge",
    )(x, *branch_args, gates, w_branch, w_out, g, beta)


def _first_max(vals, idx, n):
    top = jnp.max(vals, axis=0, keepdims=True)
    arg = jnp.min(jnp.where(vals == top, idx, n), axis=0, keepdims=True)
    return top, arg


def _router_body(x_ref, wt_ref, bias_ref, eidx_ref, gw_ref, rank_ref, counts_ref, running):
    tm = x_ref.shape[0]

    @pl.when(pl.program_id(0) == 0)
    def _():
        running[...] = jnp.zeros(running.shape, F32)

    per = N_EXPERTS // N_GROUPS
    logits = lax.dot_general(wt_ref[...], x_ref[...].astype(BF16), _NT_DIMS, preferred_element_type=F32)
    s = jax.nn.sigmoid(logits)
    sb = s + bias_ref[...]
    in_group = lax.broadcasted_iota(I32, (per, tm), 0).astype(F32)
    group_scores = []
    for g in range(N_GROUPS):
        blk = sb[g * per:(g + 1) * per, :]
        top1, arg1 = _first_max(blk, in_group, per)
        top2 = jnp.max(jnp.where(in_group == arg1, -jnp.inf, blk), axis=0, keepdims=True)
        group_scores.append(top1 + top2)
    gs = jnp.concatenate(group_scores, axis=0)
    gidx = lax.broadcasted_iota(I32, (N_GROUPS, tm), 0).astype(F32)
    chosen = jnp.zeros((N_GROUPS, tm), F32)
    for _ in range(TOPK_GROUPS):
        _, arg = _first_max(gs, gidx, N_GROUPS)
        hit = gidx == arg
        chosen = jnp.where(hit, 1.0, chosen)
        gs = jnp.where(hit, -jnp.inf, gs)
    cand = jnp.concatenate(
        [jnp.where(chosen[g:g + 1, :] > 0.0, sb[g * per:(g + 1) * per, :], -jnp.inf) for g in range(N_GROUPS)],
        axis=0)
    eidx = lax.broadcasted_iota(I32, (N_EXPERTS, tm), 0).astype(F32)
    picks, weights, hits = [], [], []
    for _ in range(EXPERT_TOPK):
        _, arg = _first_max(cand, eidx, N_EXPERTS)
        hit = eidx == arg
        weights.append(jnp.sum(jnp.where(hit, s, 0.0), axis=0, keepdims=True))
        picks.append(arg)
        hits.append(hit)
        cand = jnp.where(hit, -jnp.inf, cand)
    total = weights[0]
    for w in weights[1:]:
        total = total + w
    taken = jnp.zeros((N_EXPERTS, tm), F32)
    for hit in hits:
        taken = jnp.where(hit, 1.0, taken)
    earlier = (lax.broadcasted_iota(I32, (tm, tm), 0) < lax.broadcasted_iota(I32, (tm, tm), 1))
    before = jnp.dot(taken.astype(BF16), jnp.where(earlier, 1.0, 0.0).astype(BF16),
                     preferred_element_type=F32) + running[...]
    for r in range(EXPERT_TOPK):
        eidx_ref[r:r + 1, :] = picks[r].astype(I32)
        gw_ref[r:r + 1, :] = weights[r] / total * ROUTE_SCALE
        rank_ref[r:r + 1, :] = jnp.sum(jnp.where(hits[r], before, 0.0), axis=0, keepdims=True).astype(I32)
    running[...] = running[...] + jnp.sum(taken, axis=1, keepdims=True)
    counts_ref[...] = running[...]


def _router(x, w_router_t, b_router_col, tm=256):
    n, d = x.shape
    slot = pl.BlockSpec((EXPERT_TOPK, tm), lambda i: (0, i))
    return pl.pallas_call(
        _router_body,
        out_shape=[jax.ShapeDtypeStruct((EXPERT_TOPK, n), I32), jax.ShapeDtypeStruct((EXPERT_TOPK, n), F32),
                   jax.ShapeDtypeStruct((EXPERT_TOPK, n), I32), jax.ShapeDtypeStruct((N_EXPERTS, 1), F32)],
        grid=(n // tm,),
        in_specs=[pl.BlockSpec((tm, d), lambda i: (i, 0)),
                  pl.BlockSpec(w_router_t.shape, lambda i: (0, 0)),
                  pl.BlockSpec(b_router_col.shape, lambda i: (0, 0))],
        out_specs=[slot, slot, slot, pl.BlockSpec((N_EXPERTS, 1), lambda i: (0, 0))],
        scratch_shapes=[pltpu.VMEM((N_EXPERTS, 1), F32)],
        compiler_params=_params(("arbitrary",)),
        name="router",
    )(x, w_router_t, b_router_col)


def _dest_body(eidx_ref, rank_ref, start_ref, dest_ref):
    tm = eidx_ref.shape[1]
    experts = lax.broadcasted_iota(I32, (N_EXPERTS, tm), 0)
    for r in range(EXPERT_TOPK):
        base = jnp.sum(jnp.where(experts == eidx_ref[r:r + 1, :], start_ref[...], 0.0), axis=0, keepdims=True)
        dest_ref[r:r + 1, :] = base.astype(I32) + rank_ref[r:r + 1, :]


def _dest_rows(eidx_t, rank_t, start_col, tm=512):
    n = eidx_t.shape[1]
    slot = pl.BlockSpec((EXPERT_TOPK, tm), lambda i: (0, i))
    return pl.pallas_call(
        _dest_body,
        out_shape=jax.ShapeDtypeStruct((EXPERT_TOPK, n), I32),
        grid=(n // tm,),
        in_specs=[slot, slot, pl.BlockSpec(start_col.shape, lambda i: (0, 0))],
        out_specs=slot,
        compiler_params=_params(("parallel",)),
        name="dest_rows",
    )(eidx_t, rank_t, start_col)


def _scatter_body(dest_ref, x_ref, _, o_ref, sem):
    tm = x_ref.shape[0]

    def row_copy(t, k):
        return pltpu.make_async_copy(x_ref.at[pl.ds(t, 1)], o_ref.at[pl.ds(dest_ref[k, t], 1)], sem)

    def issue(group, carry):
        base = pl.multiple_of(group * SUBLANES, SUBLANES)
        for r in range(SUBLANES):
            for k in range(EXPERT_TOPK):
                row_copy(base + r, k).start(priority=k % 2)
        return carry

    lax.fori_loop(0, tm // SUBLANES, issue, 0)
    for k in range(EXPERT_TOPK):
        pltpu.make_async_copy(x_ref, o_ref.at[pl.ds(0, tm)], sem).wait()


def _scatter_rows(xw, dest_t, rows, tm=256):
    n, width = xw.shape
    return pl.pallas_call(
        _scatter_body,
        out_shape=jax.ShapeDtypeStruct((rows, width), xw.dtype),
        grid=(n // tm,),
        in_specs=[pl.BlockSpec((EXPERT_TOPK, tm), lambda i: (0, i), memory_space=pltpu.SMEM),
                  pl.BlockSpec((tm, width), lambda i: (i, 0)),
                  pl.BlockSpec(memory_space=pl.ANY)],
        out_specs=pl.BlockSpec(memory_space=pl.ANY),
        scratch_shapes=[pltpu.SemaphoreType.DMA],
        input_output_aliases={2: 0},
        compiler_params=_params(("arbitrary",)),
        name="scatter_rows",
    )(dest_t, xw, jnp.zeros((rows, width), xw.dtype))


def _experts_body(block_exp_ref, n_used_ref, x_ref, wgu_ref, wdn_ref, o_ref, wgu_bf, wdn_bf):
    i = pl.program_id(0)
    used = i < n_used_ref[0]

    @pl.when(used)
    def _():
        prev = block_exp_ref[jnp.maximum(i - 1, 0)]

        @pl.when((i == 0) | (block_exp_ref[i] != prev))
        def _():
            wgu_bf[...] = wgu_ref[...].astype(BF16)
            wdn_bf[...] = wdn_ref[...].astype(BF16)

        x_lo, x_hi = _unpack_rows(x_ref[...])
        half = x_lo.shape[1]
        h = (jnp.dot(x_lo, wgu_bf[0:half, :], preferred_element_type=F32)
             + jnp.dot(x_hi, wgu_bf[half:, :], preferred_element_type=F32))
        ff = h.shape[1] // 2
        act = jax.nn.silu(h[:, :ff]) * h[:, ff:]
        o_ref[...] = jnp.dot(act.astype(BF16), wdn_bf[...], preferred_element_type=F32)

    @pl.when(jnp.logical_not(used))
    def _():
        o_ref[...] = jnp.zeros(o_ref.shape, F32)


def _experts(x_rows, block_exp, n_used, w_gu, w_dn, layer):
    rows, width = x_rows.shape
    n_blocks = rows // EXPERT_BLOCK
    d, ff2 = w_gu.shape[-2:]
    grid_spec = pltpu.PrefetchScalarGridSpec(
        num_scalar_prefetch=2,
        grid=(n_blocks,),
        in_specs=[pl.BlockSpec((EXPERT_BLOCK, width), lambda i, be, nu: (i, 0)),
                  pl.BlockSpec((None, None, d, ff2), lambda i, be, nu: (layer, be[i], 0, 0)),
                  pl.BlockSpec((None, None, ff2 // 2, d), lambda i, be, nu: (layer, be[i], 0, 0))],
        out_specs=pl.BlockSpec((EXPERT_BLOCK, d), lambda i, be, nu: (i, 0)),
        scratch_shapes=[pltpu.VMEM((d, ff2), BF16), pltpu.VMEM((ff2 // 2, d), BF16)],
    )
    return pl.pallas_call(
        _experts_body,
        out_shape=jax.ShapeDtypeStruct((rows, d), F32),
        grid_spec=grid_spec,
        compiler_params=_params(("arbitrary",)),
        name="experts",
    )(block_exp, n_used, x_rows, w_gu, w_dn)


def _block_plan(counts, n_blocks):
    counts = counts.reshape(-1).astype(I32)
    padded = (counts + EXPERT_BLOCK - 1) // EXPERT_BLOCK * EXPERT_BLOCK
    pad_end = jnp.cumsum(padded)
    start_col = (pad_end - padded).astype(F32)[:, None]
    end_blocks = pad_end // EXPERT_BLOCK
    block_exp = jnp.minimum(jnp.sum(end_blocks[None, :] <= jnp.arange(n_blocks)[:, None], axis=1),
                            N_EXPERTS - 1).astype(I32)
    return start_col, block_exp, end_blocks[-1:].astype(I32)


def _ffn_out_body(dest_ref, dest_next_ref, x_ref, gw_ref, yb_ref, wgu_ref, wdn_ref, g_ref, beta_ref, o_ref,
                  ybuf, sems, *, alpha):
    i = pl.program_id(0)
    n_steps = pl.num_programs(0)
    tm = x_ref.shape[0]
    slot = i % 2

    def gather(dests, to_slot):
        def issue(group, carry):
            base = pl.multiple_of(group * SUBLANES, SUBLANES)
            for r in range(SUBLANES):
                for k in range(EXPERT_TOPK):
                    pltpu.make_async_copy(yb_ref.at[pl.ds(dests[k, base + r], 1)],
                                          ybuf.at[to_slot, k, pl.ds(base + r, 1)],
                                          sems.at[to_slot]).start(priority=k % 2)
            return carry
        lax.fori_loop(0, tm // SUBLANES, issue, 0)

    @pl.when(i == 0)
    def _():
        gather(dest_ref, 0)

    for parity in range(2):
        @pl.when((i + 1 < n_steps) & (slot == parity))
        def _(parity=parity):
            gather(dest_next_ref, 1 - parity)

    x = x_ref[...]
    h = jnp.dot(x.astype(BF16), wgu_ref[...], preferred_element_type=F32)
    ff = h.shape[1] // 2
    act = jax.nn.silu(h[:, :ff]) * h[:, ff:]
    shared = jnp.dot(act.astype(BF16), wdn_ref[...], preferred_element_type=F32)

    for k in range(EXPERT_TOPK):
        pltpu.make_async_copy(yb_ref.at[pl.ds(0, tm)], ybuf.at[slot, k], sems.at[slot]).wait()
    routed = None
    for k in range(EXPERT_TOPK):
        term = ybuf[slot, k] * gw_ref[:, k:k + 1]
        routed = term if routed is None else routed + term
    o_ref[...] = _layer_norm(alpha * x + (routed + shared), g_ref[...], beta_ref[...])


def _ffn_out(x, yb, dest_t, gw, ws_gu, ws_dn, g, beta, alpha, tm=256):
    n, d = x.shape
    n_tiles = n // tm
    row = pl.BlockSpec((tm, d), lambda i: (i, 0))
    full = lambda arr: pl.BlockSpec(arr.shape, lambda i: (0,) * arr.ndim)
    dest_spec = lambda index: pl.BlockSpec((EXPERT_TOPK, tm), index, memory_space=pltpu.SMEM)
    return pl.pallas_call(
        functools.partial(_ffn_out_body, alpha=alpha),
        out_shape=jax.ShapeDtypeStruct((n, d), F32),
        grid=(n_tiles,),
        in_specs=[dest_spec(lambda i: (0, i)), dest_spec(lambda i: (0, jnp.minimum(i + 1, n_tiles - 1))),
                  row, pl.BlockSpec((tm, EXPERT_TOPK), lambda i: (i, 0)),
                  pl.BlockSpec(memory_space=pl.ANY),
                  full(ws_gu), full(ws_dn), full(g), full(beta)],
        out_specs=row,
        scratch_shapes=[pltpu.VMEM((2, EXPERT_TOPK, tm, d), F32), pltpu.SemaphoreType.DMA((2,))],
        compiler_params=_params(("arbitrary",)),
        name="ffn_out",
    )(dest_t, dest_t, x, gw, yb, ws_gu, ws_dn, g, beta)


def _moe(x, xw, layer, w_router_t, b_router_col, w_exp_gate_up, w_exp_down, ws_gu, ws_dn, g, beta, alpha):
    n = x.shape[0]
    n_blocks = -(-n * EXPERT_TOPK // EXPERT_BLOCK) + N_EXPERTS
    eidx_t, gw_t, rank_t, counts = _router(x, w_router_t, b_router_col)
    start_col, block_exp, n_used = _block_plan(counts, n_blocks)
    dest_t = _dest_rows(eidx_t, rank_t, start_col)
    x_rows = _scatter_rows(xw, dest_t, n_blocks * EXPERT_BLOCK)
    yb = _experts(x_rows, block_exp, n_used, w_exp_gate_up, w_exp_down, layer)
    return _ffn_out(x, yb, dest_t, gw_t.T, ws_gu, ws_dn, g, beta, alpha)


def kernel(x_prompt, x_sample, mem_prompt, cache_a_k, cache_a_v, cache_b_k, cache_b_v, cache_b_idx_k,
           cache_mem_k, cache_mem_v, w_in, rel_bias, w_mem_kv, w_branch, w_out, ln_mix_g, ln_mix_b,
           w_router, b_router, w_exp_gate_up, w_exp_down, w_sh_gate_up, w_sh_down, ln_ffn_g, ln_ffn_b):
    n_batch, seq, d = x_prompt.shape
    s_batch, s_seq, _ = x_sample.shape
    depth = w_in.shape[0]
    past = cache_b_k.shape[2]
    assert s_seq == CHUNK and seq % CHUNK == 0 and cache_a_k.shape[2] == BAND and seq >= BAND
    alpha = (2 * depth) ** 0.25
    n_p, n_s = n_batch * seq, s_batch * s_seq
    n_tot = n_p + n_s
    tm = 512
    assert seq % tm == 0 and n_p % tm == 0 and n_s % tm == 0 and tm % CHUNK == 0

    x = jnp.concatenate([x_prompt.reshape(n_p, d), x_sample.reshape(n_s, d)])

    pos = jnp.concatenate([jnp.arange(seq), jnp.tile(past + jnp.arange(s_seq), tm // s_seq)])
    tab_heads = _rope_tables(pos, LANES)
    tab_kiw = _rope_tables(pos, HEAD_DIM)
    p_tiles, seq_tiles = n_p // tm, seq // tm
    table_index = lambda i: jnp.where(i < p_tiles, i % seq_tiles, seq_tiles)

    wd = WIDTH
    c_qi, c_ki, c_wi, c_qm, c_gates = 6 * wd, 7 * wd, 7 * wd + HEAD_DIM, 7 * wd + HEAD_DIM + N_HEADS, 8 * wd + HEAD_DIM + N_HEADS
    cache_a_k = cache_a_k.reshape(depth, s_batch, BAND, wd)
    cache_a_v = cache_a_v.reshape(depth, s_batch, BAND, wd)
    cache_b_k = cache_b_k.reshape(depth, s_batch, past, wd)
    cache_b_v = cache_b_v.reshape(depth, s_batch, past, wd)
    cache_mem_k = cache_mem_k.reshape(depth, s_batch, N_MEM, wd)
    cache_mem_v = cache_mem_v.reshape(depth, s_batch, N_MEM, wd)
    mem_rows = mem_prompt.reshape(n_batch * N_MEM, d)

    outs = {k: [] for k in ("p_ak", "p_av", "p_bk", "p_bv", "p_ik", "p_mk", "p_mv",
                            "s_ak", "s_av", "s_bk", "s_bv", "s_ik")}
    for l in range(depth):
        w = w_in[l]
        w_a = w[:, :3 * wd].astype(BF16)
        w_b = w[:, 3 * wd:7 * wd].astype(BF16)
        w_c = w[:, c_qm:].astype(BF16)
        w_d = jnp.concatenate([w[:, c_ki:c_qm], jnp.zeros((d, LANES - HEAD_DIM - N_HEADS), F32)], 1).astype(BF16)

        qa, ka, va = _project(x, w_a, (wd, wd, wd), split_flags=(False, True, True), n_prompt=n_p,
                              tm=tm, name="proj_a")
        qb, kb, vb, qi = _project(x, w_b, (wd, wd, wd, wd), (True, True, False, True), tab_heads, table_index,
                                  split_flags=(False, True, True, False), n_prompt=n_p, tm=tm, name="proj_b")
        qm, gates = _project(x, w_c, (wd, 3 * d), tm=tm, name="proj_c")
        (kiw,) = _project(x, w_d, (LANES,), (True,), tab_kiw, table_index, split_flags=(True,), n_prompt=n_p,
                          tm=tm, name="proj_d")
        mk, mv = _project(mem_rows, w_mem_kv[l].astype(BF16), (wd, wd), tm=N_MEM, name="proj_mem")

        a = (_band_prompt(qa, ka[0], va[0], rel_bias[l], n_batch, seq),
             _band_sample(qa, ka[1], va[1], cache_a_k, cache_a_v, l, _band_bias(rel_bias[l], CHUNK), s_batch, n_p))
        b = (_dsa_prompt(qb, kb[0], vb[0], qi, kiw[0], n_batch, seq),
             _dsa_sample(qb, kb[1], vb[1], qi, kiw[1], cache_b_k, cache_b_v, cache_b_idx_k, l, s_batch, n_p))
        m = (_mem_attn(qm, mk, mv, lambda bb: (bb, 0), n_batch, seq, 0, tm),
             _mem_attn(qm, cache_mem_k, cache_mem_v, lambda bb, l=l: (l, bb, 0, 0), s_batch, s_seq, n_p, s_seq))

        x1, x1w = _merge(x, (a, b, m), gates, w_branch[l].astype(BF16), w_out[l].astype(BF16),
                         ln_mix_g[l][None], ln_mix_b[l][None], alpha)
        x = _moe(x1, x1w, l, w_router[l].T.astype(BF16), b_router[l][:, None], w_exp_gate_up, w_exp_down,
                 w_sh_gate_up[l].astype(BF16), w_sh_down[l].astype(BF16),
                 ln_ffn_g[l][None], ln_ffn_b[l][None], alpha)

        outs["p_ak"].append(ka[0].reshape(n_batch, seq, wd)[:, -BAND:])
        outs["p_av"].append(va[0].reshape(n_batch, seq, wd)[:, -BAND:])
        outs["p_bk"].append(kb[0].reshape(n_batch, seq, wd))
        outs["p_bv"].append(vb[0].reshape(n_batch, seq, wd))
        outs["p_ik"].append(kiw[0][:, :HEAD_DIM].reshape(n_batch, seq, HEAD_DIM))
        outs["p_mk"].append(mk.reshape(n_batch, N_MEM, wd))
        outs["p_mv"].append(mv.reshape(n_batch, N_MEM, wd))
        ka_s = ka[1].reshape(s_batch, s_seq, wd)
        va_s = va[1].reshape(s_batch, s_seq, wd)
        outs["s_ak"].append(jnp.concatenate([cache_a_k[l], ka_s], 1)[:, -BAND:])
        outs["s_av"].append(jnp.concatenate([cache_a_v[l], va_s], 1)[:, -BAND:])
        outs["s_bk"].append(kb[1].reshape(s_batch, s_seq, wd))
        outs["s_bv"].append(vb[1].reshape(s_batch, s_seq, wd))
        outs["s_ik"].append(kiw[1][:, :HEAD_DIM].reshape(s_batch, s_seq, HEAD_DIM))

    heads = lambda t: t.reshape(t.shape[:-1] + (N_HEADS, HEAD_DIM))
    mheads = lambda t: t.reshape(t.shape[:-1] + (M_HEADS, M_HEAD_DIM))
    st = lambda key: jnp.stack(outs[key])
    return (x[:n_p].reshape(n_batch, seq, d), x[n_p:].reshape(s_batch, s_seq, d),
            heads(st("p_ak")), heads(st("p_av")), heads(st("p_bk")), heads(st("p_bv")), st("p_ik"),
            mheads(st("p_mk")), mheads(st("p_mv")),
            heads(st("s_ak")), heads(st("s_av")), heads(st("s_bk")), heads(st("s_bv")), st("s_ik"))
```

```python
import functools

import jax
import jax.numpy as jnp
from jax import lax
from jax.experimental import pallas as pl
from jax.experimental.pallas import tpu as pltpu

F32 = jnp.float32
BF16 = jnp.bfloat16
I32 = jnp.int32

CHUNK = 64
HEAD_DIM = 64
N_HEADS = 8
WIDTH = N_HEADS * HEAD_DIM
BAND = 8 * CHUNK
MAX_REL = 128
TOPK_KEYS = 256
M_HEADS = 4
M_HEAD_DIM = 128
N_MEM = 256
ROPE_THETA = 10000.0
N_EXPERTS = 256
EXPERT_TOPK = 8
N_GROUPS = 8
TOPK_GROUPS = 4
EXPERT_FF = 256
ROUTE_SCALE = 2.5
EXPERT_BLOCK = 256
LN_EPS = 1e-5

LANES = 128
SUBLANES = 8
NEG_BIG = -1e30
INT_MIN = -(2 ** 31)
VMEM_LIMIT = 48 * 1024 * 1024

_NT_DIMS = (((1,), (1,)), ((), ()))


def _params(semantics):
    return pltpu.CompilerParams(dimension_semantics=semantics, vmem_limit_bytes=VMEM_LIMIT)


def _proj_body(*refs, widths, rope_flags, split_flags, prompt_tiles):
    has_rope = any(rope_flags)
    x_ref, w_ref = refs[0], refs[1]
    if has_rope:
        cos_ref, sin_lo_ref, sin_hi_ref = refs[2:5]
        outs = list(refs[5:])
    else:
        outs = list(refs[2:])
    in_prompt = pl.program_id(0) < prompt_tiles
    xb = x_ref[...].astype(BF16)
    off = 0
    for g, width in enumerate(widths):
        y = jnp.dot(xb, w_ref[:, off:off + width], preferred_element_type=F32)
        if rope_flags[g]:
            cos, sin_lo, sin_hi = cos_ref[...], sin_lo_ref[...], sin_hi_ref[...]
            chunks = [yk * cos + pltpu.roll(yk, LANES - 32, 1) * sin_lo + pltpu.roll(yk, 32, 1) * sin_hi
                      for yk in (y[:, k * LANES:(k + 1) * LANES] for k in range(width // LANES))]
        else:
            chunks = [y]

        def store(ref, chunks=chunks):
            step = chunks[0].shape[1]
            for k, chunk in enumerate(chunks):
                ref[:, k * step:(k + 1) * step] = chunk

        if split_flags[g]:
            prompt_ref, sample_ref = outs.pop(0), outs.pop(0)
            pl.when(in_prompt)(functools.partial(store, prompt_ref))
            pl.when(jnp.logical_not(in_prompt))(functools.partial(store, sample_ref))
        else:
            store(outs.pop(0))
        off += width


def _project(x, w, widths, rope_flags=None, tables=None, table_index=None, split_flags=None, n_prompt=None,
             tm=512, name="proj"):
    n, d = x.shape
    rope_flags = rope_flags or (False,) * len(widths)
    split_flags = split_flags or (False,) * len(widths)
    assert n % tm == 0 and w.shape == (d, sum(widths))
    prompt_tiles = n // tm
    if any(split_flags):
        assert n_prompt % tm == 0 and 0 < n_prompt < n
        prompt_tiles = n_prompt // tm
    in_specs = [pl.BlockSpec((tm, d), lambda i: (i, 0)),
                pl.BlockSpec(w.shape, lambda i: (0, 0))]
    args = [x, w]
    if any(rope_flags):
        for t in tables:
            in_specs.append(pl.BlockSpec((tm, LANES), lambda i: (table_index(i), 0)))
            args.append(t)
    out_shape, out_specs = [], []
    for wd, split in zip(widths, split_flags):
        if split:
            out_shape += [jax.ShapeDtypeStruct((n_prompt, wd), F32), jax.ShapeDtypeStruct((n - n_prompt, wd), F32)]
            out_specs += [pl.BlockSpec((tm, wd), lambda i: (jnp.minimum(i, prompt_tiles - 1), 0)),
                          pl.BlockSpec((tm, wd), lambda i: (jnp.maximum(i - prompt_tiles, 0), 0))]
        else:
            out_shape.append(jax.ShapeDtypeStruct((n, wd), F32))
            out_specs.append(pl.BlockSpec((tm, wd), lambda i: (i, 0)))
    flat = pl.pallas_call(
        functools.partial(_proj_body, widths=tuple(widths), rope_flags=tuple(rope_flags),
                          split_flags=tuple(split_flags), prompt_tiles=prompt_tiles),
        out_shape=out_shape,
        grid=(n // tm,),
        in_specs=in_specs,
        out_specs=out_specs,
        compiler_params=_params(("arbitrary",)),
        name=name,
    )(*args)
    flat = list(flat)
    return [(flat.pop(0), flat.pop(0)) if split else flat.pop(0) for split in split_flags]


def _rope_tables(pos, rope_lanes):
    half = HEAD_DIM // 2
    inv_freq = ROPE_THETA ** (-jnp.arange(half, dtype=F32) / half)
    ang = pos.astype(F32)[:, None] * inv_freq[None, :]
    cos, sin = jnp.cos(ang), jnp.sin(ang)
    reps = LANES // HEAD_DIM
    zero = jnp.zeros_like(sin)
    cos_t = jnp.tile(jnp.concatenate([cos, cos], -1), (1, reps))
    sin_lo = jnp.tile(jnp.concatenate([-sin, zero], -1), (1, reps))
    sin_hi = jnp.tile(jnp.concatenate([zero, sin], -1), (1, reps))
    keep = (jnp.arange(LANES) < rope_lanes)[None, :]
    return (jnp.where(keep, cos_t, 1.0), jnp.where(keep, sin_lo, 0.0), jnp.where(keep, sin_hi, 0.0))


def _reduce_rows(x, op, keep_groups=False):
    rows, n = x.shape
    groups = next(g for g in (8, 4, 2, 1) if rows % (8 * g) == 0 and rows // (8 * g) >= 4 or g == 1)
    parts = op(x.reshape(groups, rows // (8 * groups), 8, n), axis=1)
    if keep_groups:
        return parts
    return op(op(parts, axis=0), axis=0, keepdims=True)


def _head_mask(rows, head_in_pair):
    lane = lax.broadcasted_iota(I32, (rows, LANES), 1)
    lo = head_in_pair * HEAD_DIM
    return (lane >= lo) & (lane < lo + HEAD_DIM)


def _pair_merge(o_even, o_odd):
    lane = lax.broadcasted_iota(I32, o_even.shape, 1)
    return jnp.where(lane < HEAD_DIM, o_even, o_odd)


def _band_sample_body(q_ref, kn_ref, vn_ref, kh_ref, vh_ref, bias_ref, o_ref, kwin, vwin):
    kwin[0:BAND, :] = kh_ref[...].astype(BF16)
    vwin[0:BAND, :] = vh_ref[...].astype(BF16)
    kwin[BAND:, :] = kn_ref[...].astype(BF16)
    vwin[BAND:, :] = vn_ref[...].astype(BF16)
    q = q_ref[...] * (HEAD_DIM ** -0.5)
    for hp in range(N_HEADS // 2):
        cols = slice(hp * LANES, (hp + 1) * LANES)
        outs = []
        for hh in range(2):
            qm = jnp.where(_head_mask(CHUNK, hh), q[:, cols], 0.0).astype(BF16)
            s = lax.dot_general(qm, kwin[:, cols], _NT_DIMS, preferred_element_type=F32) + bias_ref[2 * hp + hh]
            p = jnp.exp(s - jnp.max(s, axis=1, keepdims=True))
            denom = jnp.sum(p, axis=1, keepdims=True)
            outs.append(jnp.dot(p.astype(BF16), vwin[:, cols], preferred_element_type=F32) / denom)
        o_ref[:, cols] = _pair_merge(outs[0], outs[1])


def _band_prompt_body(q_ref, k_ref, v_ref, bias_ref, o_ref, k_even, k_odd, v_t, q_t, s_sc, p_sc, o_t):
    t = pl.program_id(1)
    nq = q_ref.shape[0]
    win = BAND + nq
    seq = k_ref.shape[0]
    stage_rows = 256

    @pl.when(t == 0)
    def _():
        k_even[0:BAND, :] = jnp.zeros((BAND, WIDTH), BF16)
        k_odd[0:BAND, :] = jnp.zeros((BAND, WIDTH), BF16)
        v_t[:, 0:BAND] = jnp.zeros((WIDTH, BAND), BF16)
        for r in range(0, seq, stage_rows):
            ke, ko = _split_pair_lanes(k_ref[r:r + stage_rows, :])
            k_even[BAND + r:BAND + r + stage_rows, :] = ke.astype(BF16)
            k_odd[BAND + r:BAND + r + stage_rows, :] = ko.astype(BF16)
            v_t[:, BAND + r:BAND + r + stage_rows] = v_ref[r:r + stage_rows, :].T.astype(BF16)

    start = pl.multiple_of(t * nq, nq)
    window = pl.ds(start, win)
    q_t[...] = (q_ref[...] * (HEAD_DIM ** -0.5)).T.astype(BF16)
    before_start = jnp.where(lax.broadcasted_iota(I32, (win, nq), 0) >= BAND - t * nq, 0.0, NEG_BIG)
    for h in range(N_HEADS):
        pair = slice((h // 2) * LANES, (h // 2 + 1) * LANES)
        k_ref_h = k_even if h % 2 == 0 else k_odd
        s_sc[h * win:(h + 1) * win, :] = (jnp.dot(k_ref_h[window, pair], q_t[pair, :], preferred_element_type=F32)
                                          + bias_ref[h] + before_start)
    for h in range(N_HEADS):
        s = s_sc[h * win:(h + 1) * win, :]
        p = jnp.exp(s - _reduce_rows(s, jnp.max))
        p_sc[h * win:(h + 1) * win, :] = (p / _reduce_rows(p, jnp.sum)).astype(BF16)
    for h in range(N_HEADS):
        dims = slice(h * HEAD_DIM, (h + 1) * HEAD_DIM)
        o_t[dims, :] = jnp.dot(v_t[dims, window], p_sc[h * win:(h + 1) * win, :], preferred_element_type=F32)
    o_ref[...] = o_t[...].T


def _band_bias(table, n_rows):
    win = BAND + n_rows
    assert n_rows <= MAX_REL <= BAND
    n_heads = table.shape[0]
    g = jnp.concatenate([jnp.broadcast_to(table[:, 2 * MAX_REL:], (n_heads, win - MAX_REL)),
                         table[:, MAX_REL - n_rows:2 * MAX_REL][:, ::-1],
                         jnp.zeros((n_heads, 1), table.dtype)], axis=1).astype(F32)
    period = g.shape[1]
    skew = jnp.tile(g, (1, n_rows))[:, :n_rows * (period - 1)].reshape(n_heads, n_rows, period - 1)
    return skew[:, :, n_rows - 1:n_rows - 1 + win]


def _band_prompt(q, k, v, table, n_batch, seq, q_rows=128):
    nt = seq // q_rows
    win = BAND + q_rows
    assert seq % q_rows == 0 and q_rows % CHUNK == 0
    key_chunk = jnp.arange(win)[:, None] // CHUNK - BAND // CHUNK
    query_chunk = jnp.arange(q_rows)[None, :] // CHUNK
    visible = (key_chunk <= query_chunk) & (key_chunk >= query_chunk - BAND // CHUNK)
    bias = jnp.where(visible[None], jnp.swapaxes(_band_bias(table, q_rows), 1, 2), NEG_BIG)
    tile_spec = pl.BlockSpec((q_rows, WIDTH), lambda b, t: (b * nt + t, 0))
    seq_spec = pl.BlockSpec((seq, WIDTH), lambda b, t: (b, 0))
    return pl.pallas_call(
        _band_prompt_body,
        out_shape=jax.ShapeDtypeStruct((n_batch * seq, WIDTH), F32),
        grid=(n_batch, nt),
        in_specs=[tile_spec, seq_spec, seq_spec, pl.BlockSpec(bias.shape, lambda b, t: (0, 0, 0))],
        out_specs=tile_spec,
        scratch_shapes=[pltpu.VMEM((BAND + seq, WIDTH), BF16), pltpu.VMEM((BAND + seq, WIDTH), BF16),
                        pltpu.VMEM((WIDTH, BAND + seq), BF16), pltpu.VMEM((WIDTH, q_rows), BF16),
                        pltpu.VMEM((N_HEADS * win, q_rows), F32), pltpu.VMEM((N_HEADS * win, q_rows), BF16),
                        pltpu.VMEM((WIDTH, q_rows), F32)],
        compiler_params=_params(("parallel", "arbitrary")),
        name="band_prompt",
    )(q, k, v, bias)


def _band_sample(q, k, v, k_hist, v_hist, layer, bias, n_batch, row0):
    blk0 = row0 // CHUNK
    hist_spec = pl.BlockSpec((None, None, BAND, WIDTH), lambda b: (layer, b, 0, 0))
    new_spec = pl.BlockSpec((CHUNK, WIDTH), lambda b: (b, 0))
    return pl.pallas_call(
        _band_sample_body,
        out_shape=jax.ShapeDtypeStruct((n_batch * CHUNK, WIDTH), F32),
        grid=(n_batch,),
        in_specs=[pl.BlockSpec((CHUNK, WIDTH), lambda b: (blk0 + b, 0)), new_spec, new_spec, hist_spec, hist_spec,
                  pl.BlockSpec(bias.shape, lambda b: (0, 0, 0))],
        out_specs=new_spec,
        scratch_shapes=[pltpu.VMEM((BAND + CHUNK, WIDTH), BF16), pltpu.VMEM((BAND + CHUNK, WIDTH), BF16)],
        compiler_params=_params(("parallel",)),
        name="band_sample",
    )(q, k, v, k_hist, v_hist, bias)


def _mem_body(q_ref, k_ref, v_ref, o_ref):
    scale = M_HEAD_DIM ** -0.5
    for h in range(M_HEADS):
        cols = slice(h * M_HEAD_DIM, (h + 1) * M_HEAD_DIM)
        s = lax.dot_general(q_ref[:, cols].astype(BF16), k_ref[:, cols].astype(BF16), _NT_DIMS,
                            preferred_element_type=F32) * scale
        p = jnp.exp(s - jnp.max(s, axis=1, keepdims=True))
        denom = jnp.sum(p, axis=1, keepdims=True)
        o = jnp.dot(p.astype(BF16), v_ref[:, cols].astype(BF16), preferred_element_type=F32)
        o_ref[:, cols] = o / denom


def _mem_attn(q, mk, mv, kv_index, n_batch, rows_per_batch, row0, tq):
    width = M_HEADS * M_HEAD_DIM
    per = rows_per_batch // tq
    blk0 = row0 // tq
    kv_block = (None,) * (mk.ndim - 2) + (N_MEM, width)
    kv_spec = pl.BlockSpec(kv_block, lambda b, t: kv_index(b))
    return pl.pallas_call(
        _mem_body,
        out_shape=jax.ShapeDtypeStruct((n_batch * rows_per_batch, width), F32),
        grid=(n_batch, per),
        in_specs=[pl.BlockSpec((tq, width), lambda b, t: (blk0 + b * per + t, 0)), kv_spec, kv_spec],
        out_specs=pl.BlockSpec((tq, width), lambda b, t: (b * per + t, 0)),
        compiler_params=_params(("parallel", "parallel")),
        name="mem_attn",
    )(q, mk, mv)


def _split_pair_lanes(x):
    lane = lax.broadcasted_iota(I32, x.shape, 1)
    even = (lane & (LANES - 1)) < HEAD_DIM
    return jnp.where(even, x, 0.0), jnp.where(even, 0.0, x)


def _dsa_body(*refs, sample, key_block, n_keys, n_sel):
    n_in = 8 if sample else 6
    if sample:
        q_ref, qi_ref, kiwq_ref, kn_ref, vn_ref, kc_ref, vc_ref, kic_ref = refs[:n_in]
    else:
        q_ref, qi_ref, kiwq_ref, k_ref, v_ref, kiw_ref = refs[:n_in]
    o_ref = refs[n_in]
    (k_even, k_odd, v_t, ki_lo, ki_hi, keybuf, maskbuf, q_t, qi_t, kiwq_t, m_sc, l_sc, a_sc, acc_t,
     s_sc, p_sc) = refs[n_in + 1:]
    kb = key_block
    n_pad = k_even.shape[0]
    nq = q_t.shape[1]
    widen = lambda rows: jnp.concatenate([rows] * (nq // rows.shape[0]), axis=0)
    stage_rows = 256

    def stage_kv(row0, k_new, v_new):
        n_new = k_new.shape[0]
        ke, ko = _split_pair_lanes(k_new)
        k_even[row0:row0 + n_new, :] = ke.astype(BF16)
        k_odd[row0:row0 + n_new, :] = ko.astype(BF16)
        v_t[:, row0:row0 + n_new] = v_new.T.astype(BF16)

    def stage_index_keys(row0, kiw):
        lane = lax.broadcasted_iota(I32, kiw.shape, 1)
        ki_lo[row0:row0 + kiw.shape[0], :] = jnp.where(lane < HEAD_DIM, kiw, 0.0).astype(BF16)
        ki_hi[row0:row0 + kiw.shape[0], :] = jnp.where(lane < HEAD_DIM, 0.0, pltpu.roll(kiw, HEAD_DIM, 1)).astype(BF16)

    def stage_keys():
        if sample:
            n_hist = kc_ref.shape[0]
            for r in range(0, n_hist, stage_rows):
                stage_kv(r, kc_ref[r:r + stage_rows, :], vc_ref[r:r + stage_rows, :])
            kic = kic_ref[...]
            zeros = jnp.zeros_like(kic)
            ki_lo[0:n_hist, :] = jnp.concatenate([kic, zeros], axis=1).astype(BF16)
            ki_hi[0:n_hist, :] = jnp.concatenate([zeros, kic], axis=1).astype(BF16)
            stage_kv(n_hist, kn_ref[...], vn_ref[...])
            stage_index_keys(n_hist, kiwq_ref[...])
            if n_pad > n_keys:
                for ref in (k_even, k_odd, ki_lo, ki_hi):
                    ref[n_keys:, :] = jnp.zeros((n_pad - n_keys, ref.shape[1]), BF16)
                v_t[:, n_keys:] = jnp.zeros((v_t.shape[0], n_pad - n_keys), BF16)
        else:
            for r in range(0, n_keys, stage_rows):
                stage_kv(r, k_ref[r:r + stage_rows, :], v_ref[r:r + stage_rows, :])
            stage_index_keys(0, kiw_ref[...])

    if sample:
        stage_keys()
        limit = n_keys
        nb = n_pad // kb
        search_bits = 32
    else:
        t = pl.program_id(1)
        pl.when(t == 0)(stage_keys)
        chunk = t * (nq // CHUNK) + lax.broadcasted_iota(I32, (1, nq), 1) // CHUNK
        limit = (chunk + 1) * CHUNK
        limit_max = (t + 1) * nq
        nb = (limit_max + kb - 1) // kb
        search_bits = jnp.where(limit_max <= n_sel, 0, 32)

    def block_rows(j):
        return pl.ds(pl.multiple_of(j * kb, kb), kb)

    def key_pos(j):
        return j * kb + lax.broadcasted_iota(I32, (kb, nq), 0)

    q_t[...] = widen(q_ref[...] * (HEAD_DIM ** -0.5)).T.astype(BF16)
    qi_t[...] = widen(qi_ref[...]).T.astype(BF16)
    kiwq_t[...] = widen(kiwq_ref[...]).T

    def score_block(j, carry):
        acc = jnp.zeros((kb, nq), F32)
        for h in range(N_HEADS):
            pair = slice((h // 2) * LANES, (h // 2 + 1) * LANES)
            ki_ref = ki_lo if h % 2 == 0 else ki_hi
            d = jnp.dot(ki_ref[block_rows(j), :], qi_t[pair, :], preferred_element_type=F32)
            acc = acc + kiwq_t[HEAD_DIM + h:HEAD_DIM + h + 1, :] * jnp.maximum(d, 0.0)
        acc = acc + 0.0
        acc = jnp.where(key_pos(j) < limit, acc, -jnp.inf)
        bits = lax.bitcast_convert_type(acc, I32)
        keybuf[block_rows(j), :] = bits ^ ((bits >> 31) & 0x7FFFFFFF)
        return carry

    lax.fori_loop(0, nb, score_block, 0)

    def count(pred_fn):
        def body(j, part):
            hit = jnp.where(pred_fn(keybuf[block_rows(j), :], j), 1.0, 0.0)
            return part + _reduce_rows(hit, jnp.sum, keep_groups=True)
        groups = jax.eval_shape(functools.partial(_reduce_rows, op=jnp.sum, keep_groups=True),
                                jax.ShapeDtypeStruct((kb, nq), F32)).shape[0]
        part = lax.fori_loop(0, nb, body, jnp.zeros((groups, 8, nq), F32))
        return jnp.sum(jnp.sum(part, axis=0), axis=0, keepdims=True)

    def bit_step(i, prefix):
        cand = prefix | lax.shift_left(jnp.int32(1), 31 - i)
        thr_i = cand ^ INT_MIN
        cnt = count(lambda kblk, j: kblk >= thr_i)
        return jnp.where(cnt >= n_sel, cand, prefix)

    prefix = lax.fori_loop(0, search_bits, bit_step, jnp.zeros((1, nq), I32))
    thr = prefix ^ INT_MIN
    above = count(lambda kblk, j: kblk > thr)
    at_least = count(lambda kblk, j: kblk >= thr)
    need = n_sel - above
    surplus = at_least > n_sel

    pos_bits = max(1, (n_pad - 1).bit_length() + 1)
    any_surplus = jnp.max(jnp.where(surplus, 1.0, 0.0)) > 0.0

    def pos_step(i, bound):
        cand = bound | lax.shift_left(jnp.int32(1), pos_bits - 1 - i)
        cnt = count(lambda kblk, j: (kblk == thr) & (key_pos(j) < cand))
        return jnp.where(cnt <= need, cand, bound)

    bound = lax.fori_loop(0, jnp.where(any_surplus, pos_bits, 0), pos_step, jnp.zeros((1, nq), I32))
    bound = jnp.where(surplus, bound, 2 ** pos_bits - 1)

    def mask_block(j, carry):
        kblk = keybuf[block_rows(j), :]
        pos = key_pos(j)
        sel = ((kblk > thr) | ((kblk == thr) & (pos < bound))) & (pos < limit)
        maskbuf[block_rows(j), :] = jnp.where(sel, 0.0, NEG_BIG)
        return carry

    lax.fori_loop(0, nb, mask_block, 0)

    m_sc[...] = jnp.full(m_sc.shape, NEG_BIG, F32)
    l_sc[...] = jnp.zeros(l_sc.shape, F32)
    acc_t[...] = jnp.zeros(acc_t.shape, F32)

    def attend(j, carry):
        mask = maskbuf[block_rows(j), :]
        for h in range(N_HEADS):
            pair = slice((h // 2) * LANES, (h // 2 + 1) * LANES)
            k_ref_h = k_even if h % 2 == 0 else k_odd
            s_sc[h * kb:(h + 1) * kb, :] = jnp.dot(k_ref_h[block_rows(j), pair], q_t[pair, :],
                                                   preferred_element_type=F32) + mask
        for h in range(N_HEADS):
            stat = slice(h * 8, (h + 1) * 8)
            s = s_sc[h * kb:(h + 1) * kb, :]
            m_prev = m_sc[stat, :]
            m_new = jnp.maximum(m_prev, _reduce_rows(s, jnp.max))
            alpha = jnp.exp(m_prev - m_new)
            p = jnp.exp(s - m_new[0:1, :])
            l_sc[stat, :] = alpha * l_sc[stat, :] + _reduce_rows(p, jnp.sum)
            m_sc[stat, :] = m_new
            a_sc[stat, :] = alpha
            p_sc[h * kb:(h + 1) * kb, :] = p.astype(BF16)
        for h in range(N_HEADS):
            dims = slice(h * HEAD_DIM, (h + 1) * HEAD_DIM)
            pv = jnp.dot(v_t[dims, block_rows(j)], p_sc[h * kb:(h + 1) * kb, :], preferred_element_type=F32)
            acc_t[dims, :] = a_sc[h * 8:h * 8 + 1, :] * acc_t[dims, :] + pv
        return carry

    lax.fori_loop(0, nb, attend, 0)
    for h in range(N_HEADS):
        dims = slice(h * HEAD_DIM, (h + 1) * HEAD_DIM)
        acc_t[dims, :] = acc_t[dims, :] / l_sc[h * 8:h * 8 + 1, :]
    o_ref[...] = acc_t[...].T[0:o_ref.shape[0], :]


def _dsa_scratch(nq, n_pad, key_block):
    nq = max(nq, LANES)
    return [pltpu.VMEM((n_pad, WIDTH), BF16), pltpu.VMEM((n_pad, WIDTH), BF16),
            pltpu.VMEM((WIDTH, n_pad), BF16),
            pltpu.VMEM((n_pad, LANES), BF16), pltpu.VMEM((n_pad, LANES), BF16),
            pltpu.VMEM((n_pad, nq), I32), pltpu.VMEM((n_pad, nq), F32),
            pltpu.VMEM((WIDTH, nq), BF16), pltpu.VMEM((WIDTH, nq), BF16),
            pltpu.VMEM((LANES, nq), F32),
            pltpu.VMEM((N_HEADS * 8, nq), F32), pltpu.VMEM((N_HEADS * 8, nq), F32),
            pltpu.VMEM((N_HEADS * 8, nq), F32),
            pltpu.VMEM((WIDTH, nq), F32),
            pltpu.VMEM((N_HEADS * key_block, nq), F32),
            pltpu.VMEM((N_HEADS * key_block, nq), BF16)]


def _dsa_prompt(q, k, v, qi, kiw, n_batch, seq, key_block=256, q_rows=256):
    nt = seq // q_rows
    key_block = min(key_block, seq)
    assert seq % key_block == 0 and seq % q_rows == 0 and q_rows % CHUNK == 0
    n_sel = min(TOPK_KEYS, seq // 4)
    tile_spec = lambda width: pl.BlockSpec((q_rows, width), lambda b, t: (b * nt + t, 0))
    seq_spec = lambda width: pl.BlockSpec((seq, width), lambda b, t: (b, 0))
    return pl.pallas_call(
        functools.partial(_dsa_body, sample=False, key_block=key_block, n_keys=seq, n_sel=n_sel),
        out_shape=jax.ShapeDtypeStruct((n_batch * seq, WIDTH), F32),
        grid=(n_batch, nt),
        in_specs=[tile_spec(WIDTH), tile_spec(WIDTH), tile_spec(LANES),
                  seq_spec(WIDTH), seq_spec(WIDTH), seq_spec(LANES)],
        out_specs=tile_spec(WIDTH),
        scratch_shapes=_dsa_scratch(q_rows, seq, key_block),
        compiler_params=_params(("parallel", "arbitrary")),
        name="dsa_prompt",
    )(q, qi, kiw, k, v, kiw)


def _dsa_sample(q, k, v, qi, kiw, k_cache, v_cache, ki_cache, layer, n_batch, row0, key_block=768):
    past = k_cache.shape[2]
    n_keys = past + CHUNK
    key_block = min(key_block, -(-n_keys // 256) * 256)
    n_pad = -(-n_keys // key_block) * key_block
    n_sel = min(TOPK_KEYS, n_keys // 4)
    blk0 = row0 // CHUNK
    query_spec = pl.BlockSpec((CHUNK, WIDTH), lambda b: (blk0 + b, 0))
    new_spec = lambda width: pl.BlockSpec((CHUNK, width), lambda b: (b, 0))
    cache_spec = lambda width: pl.BlockSpec((None, None, past, width), lambda b: (layer, b, 0, 0))
    return pl.pallas_call(
        functools.partial(_dsa_body, sample=True, key_block=key_block, n_keys=n_keys, n_sel=n_sel),
        out_shape=jax.ShapeDtypeStruct((n_batch * CHUNK, WIDTH), F32),
        grid=(n_batch,),
        in_specs=[query_spec, query_spec, new_spec(LANES), new_spec(WIDTH), new_spec(WIDTH),
                  cache_spec(WIDTH), cache_spec(WIDTH), cache_spec(HEAD_DIM)],
        out_specs=new_spec(WIDTH),
        scratch_shapes=_dsa_scratch(CHUNK, n_pad, key_block),
        compiler_params=_params(("parallel",)),
        name="dsa_sample",
    )(q, qi, kiw, k, v, k_cache, v_cache, ki_cache)


def _layer_norm(z, g, b):
    mu = jnp.mean(z, axis=-1, keepdims=True)
    zc = z - mu
    var = jnp.mean(zc * zc, axis=-1, keepdims=True)
    return zc * lax.rsqrt(var + LN_EPS) * g + b


HI_HALF = -65536


def _pack_rows(x):
    half = x.shape[1] // 2
    lo = lax.bitcast_convert_type(x[:, :half].astype(BF16).astype(F32), I32)
    hi = lax.bitcast_convert_type(x[:, half:].astype(BF16).astype(F32), I32)
    return (hi & HI_HALF) | lax.shift_right_logical(lo, 16)


def _unpack_rows(w):
    lo = lax.bitcast_convert_type(lax.shift_left(w, 16), F32).astype(BF16)
    hi = lax.bitcast_convert_type(w & HI_HALF, F32).astype(BF16)
    return lo, hi


def _merge_body(x_ref, ap_ref, as_ref, bp_ref, bs_ref, mp_ref, ms_ref, gates_ref, wbr_ref, wout_ref, g_ref, beta_ref,
                o_ref, ow_ref, *, alpha, prompt_tiles):
    d = x_ref.shape[1]
    in_prompt = pl.program_id(0) < prompt_tiles
    mix = None
    for n, (p_ref, s_ref) in enumerate(((ap_ref, as_ref), (bp_ref, bs_ref), (mp_ref, ms_ref))):
        branch = jnp.where(in_prompt, p_ref[...], s_ref[...])
        proj = jnp.dot(branch.astype(BF16), wbr_ref[n], preferred_element_type=F32)
        term = jax.nn.sigmoid(gates_ref[:, n * d:(n + 1) * d]) * proj
        mix = term if mix is None else mix + term
    y = jnp.dot(mix.astype(BF16), wout_ref[...], preferred_element_type=F32)
    out = _layer_norm(alpha * x_ref[...] + y, g_ref[...], beta_ref[...])
    o_ref[...] = out
    ow_ref[...] = _pack_rows(out)


def _merge(x, branches, gates, w_branch, w_out, g, beta, alpha, tm=256):
    n, d = x.shape
    prompt_tiles = branches[0][0].shape[0] // tm
    assert all(p.shape[0] == prompt_tiles * tm and (n - p.shape[0]) == s.shape[0] and s.shape[0] % tm == 0
               for p, s in branches)
    packed = jax.eval_shape(_pack_rows, jax.ShapeDtypeStruct((tm, d), F32))
    row = lambda width: pl.BlockSpec((tm, width), lambda i: (i, 0))
    full = lambda arr: pl.BlockSpec(arr.shape, lambda i: (0,) * arr.ndim)
    prompt_row = lambda width: pl.BlockSpec((tm, width), lambda i: (jnp.minimum(i, prompt_tiles - 1), 0))
    sample_row = lambda width: pl.BlockSpec((tm, width), lambda i: (jnp.maximum(i - prompt_tiles, 0), 0))
    branch_specs, branch_args = [], []
    for p, s in branches:
        branch_specs += [prompt_row(p.shape[1]), sample_row(s.shape[1])]
        branch_args += [p, s]
    return pl.pallas_call(
        functools.partial(_merge_body, alpha=alpha, prompt_tiles=prompt_tiles),
        out_shape=[jax.ShapeDtypeStruct((n, d), F32), jax.ShapeDtypeStruct((n, packed.shape[1]), packed.dtype)],
        grid=(n // tm,),
        in_specs=[row(d)] + branch_specs + [row(gates.shape[1]), full(w_branch), full(w_out), full(g), full(beta)],
        out_specs=[row(d), row(packed.shape[1])],
        compiler_params=_params(("parallel",)),
        name="merge",
    )(x, *branch_args, gates, w_branch, w_out, g, beta)


def _first_max(vals, idx, n):
    top = jnp.max(vals, axis=0, keepdims=True)
    arg = jnp.min(jnp.where(vals == top, idx, n), axis=0, keepdims=True)
    return top, arg


def _router_body(x_ref, wt_ref, bias_ref, eidx_ref, gw_ref, rank_ref, counts_ref, running):
    tm = x_ref.shape[0]

    @pl.when(pl.program_id(0) == 0)
    def _():
        running[...] = jnp.zeros(running.shape, F32)

    per = N_EXPERTS // N_GROUPS
    logits = lax.dot_general(wt_ref[...], x_ref[...].astype(BF16), _NT_DIMS, preferred_element_type=F32)
    s = jax.nn.sigmoid(logits)
    sb = s + bias_ref[...]
    in_group = lax.broadcasted_iota(I32, (per, tm), 0).astype(F32)
    group_scores = []
    for g in range(N_GROUPS):
        blk = sb[g * per:(g + 1) * per, :]
        top1, arg1 = _first_max(blk, in_group, per)
        top2 = jnp.max(jnp.where(in_group == arg1, -jnp.inf, blk), axis=0, keepdims=True)
        group_scores.append(top1 + top2)
    gs = jnp.concatenate(group_scores, axis=0)
    gidx = lax.broadcasted_iota(I32, (N_GROUPS, tm), 0).astype(F32)
    chosen = jnp.zeros((N_GROUPS, tm), F32)
    for _ in range(TOPK_GROUPS):
        _, arg = _first_max(gs, gidx, N_GROUPS)
        hit = gidx == arg
        chosen = jnp.where(hit, 1.0, chosen)
        gs = jnp.where(hit, -jnp.inf, gs)
    cand = jnp.concatenate(
        [jnp.where(chosen[g:g + 1, :] > 0.0, sb[g * per:(g + 1) * per, :], -jnp.inf) for g in range(N_GROUPS)],
        axis=0)
    eidx = lax.broadcasted_iota(I32, (N_EXPERTS, tm), 0).astype(F32)
    picks, weights, hits = [], [], []
    for _ in range(EXPERT_TOPK):
        _, arg = _first_max(cand, eidx, N_EXPERTS)
        hit = eidx == arg
        weights.append(jnp.sum(jnp.where(hit, s, 0.0), axis=0, keepdims=True))
        picks.append(arg)
        hits.append(hit)
        cand = jnp.where(hit, -jnp.inf, cand)
    total = weights[0]
    for w in weights[1:]:
        total = total + w
    taken = jnp.zeros((N_EXPERTS, tm), F32)
    for hit in hits:
        taken = jnp.where(hit, 1.0, taken)
    earlier = (lax.broadcasted_iota(I32, (tm, tm), 0) < lax.broadcasted_iota(I32, (tm, tm), 1))
    before = jnp.dot(taken.astype(BF16), jnp.where(earlier, 1.0, 0.0).astype(BF16),
                     preferred_element_type=F32) + running[...]
    for r in range(EXPERT_TOPK):
        eidx_ref[r:r + 1, :] = picks[r].astype(I32)
        gw_ref[r:r + 1, :] = weights[r] / total * ROUTE_SCALE
        rank_ref[r:r + 1, :] = jnp.sum(jnp.where(hits[r], before, 0.0), axis=0, keepdims=True).astype(I32)
    running[...] = running[...] + jnp.sum(taken, axis=1, keepdims=True)
    counts_ref[...] = running[...]


def _router(x, w_router_t, b_router_col, tm=256):
    n, d = x.shape
    slot = pl.BlockSpec((EXPERT_TOPK, tm), lambda i: (0, i))
    return pl.pallas_call(
        _router_body,
        out_shape=[jax.ShapeDtypeStruct((EXPERT_TOPK, n), I32), jax.ShapeDtypeStruct((EXPERT_TOPK, n), F32),
                   jax.ShapeDtypeStruct((EXPERT_TOPK, n), I32), jax.ShapeDtypeStruct((N_EXPERTS, 1), F32)],
        grid=(n // tm,),
        in_specs=[pl.BlockSpec((tm, d), lambda i: (i, 0)),
                  pl.BlockSpec(w_router_t.shape, lambda i: (0, 0)),
                  pl.BlockSpec(b_router_col.shape, lambda i: (0, 0))],
        out_specs=[slot, slot, slot, pl.BlockSpec((N_EXPERTS, 1), lambda i: (0, 0))],
        scratch_shapes=[pltpu.VMEM((N_EXPERTS, 1), F32)],
        compiler_params=_params(("arbitrary",)),
        name="router",
    )(x, w_router_t, b_router_col)


def _dest_body(eidx_ref, rank_ref, start_ref, dest_ref):
    tm = eidx_ref.shape[1]
    experts = lax.broadcasted_iota(I32, (N_EXPERTS, tm), 0)
    for r in range(EXPERT_TOPK):
        base = jnp.sum(jnp.where(experts == eidx_ref[r:r + 1, :], start_ref[...], 0.0), axis=0, keepdims=True)
        dest_ref[r:r + 1, :] = base.astype(I32) + rank_ref[r:r + 1, :]


def _dest_rows(eidx_t, rank_t, start_col, tm=512):
    n = eidx_t.shape[1]
    slot = pl.BlockSpec((EXPERT_TOPK, tm), lambda i: (0, i))
    return pl.pallas_call(
        _dest_body,
        out_shape=jax.ShapeDtypeStruct((EXPERT_TOPK, n), I32),
        grid=(n // tm,),
        in_specs=[slot, slot, pl.BlockSpec(start_col.shape, lambda i: (0, 0))],
        out_specs=slot,
        compiler_params=_params(("parallel",)),
        name="dest_rows",
    )(eidx_t, rank_t, start_col)


def _tile_major(dest_t, tm):
    slots, n = dest_t.shape
    return dest_t.reshape(slots, n // tm, tm).transpose(1, 0, 2).reshape(n // tm, 1, slots * tm)


def _scatter_body(dest_ref, x_ref, _, o_ref, sem):
    tm = x_ref.shape[0]

    def row_copy(t, k):
        return pltpu.make_async_copy(x_ref.at[pl.ds(t, 1)], o_ref.at[pl.ds(dest_ref[0, k * tm + t], 1)], sem)

    def issue(group, carry):
        base = pl.multiple_of(group * SUBLANES, SUBLANES)
        for r in range(SUBLANES):
            for k in range(EXPERT_TOPK):
                row_copy(base + r, k).start(priority=k % 2)
        return carry

    lax.fori_loop(0, tm // SUBLANES, issue, 0)
    for k in range(EXPERT_TOPK):
        pltpu.make_async_copy(x_ref, o_ref.at[pl.ds(0, tm)], sem).wait()


def _scatter_rows(xw, dest_t, rows, tm=256):
    n, width = xw.shape
    return pl.pallas_call(
        _scatter_body,
        out_shape=jax.ShapeDtypeStruct((rows, width), xw.dtype),
        grid=(n // tm,),
        in_specs=[pl.BlockSpec((None, 1, EXPERT_TOPK * tm), lambda i: (i, 0, 0), memory_space=pltpu.SMEM),
                  pl.BlockSpec((tm, width), lambda i: (i, 0)),
                  pl.BlockSpec(memory_space=pl.ANY)],
        out_specs=pl.BlockSpec(memory_space=pl.ANY),
        scratch_shapes=[pltpu.SemaphoreType.DMA],
        input_output_aliases={2: 0},
        compiler_params=_params(("arbitrary",)),
        name="scatter_rows",
    )(_tile_major(dest_t, tm), xw, jnp.zeros((rows, width), xw.dtype))


def _experts_body(block_exp_ref, n_used_ref, x_ref, wgu_ref, wdn_ref, o_ref, wgu_bf, wdn_bf):
    i = pl.program_id(0)
    used = i < n_used_ref[0]

    @pl.when(used)
    def _():
        prev = block_exp_ref[jnp.maximum(i - 1, 0)]

        @pl.when((i == 0) | (block_exp_ref[i] != prev))
        def _():
            wgu_bf[...] = wgu_ref[...].astype(BF16)
            wdn_bf[...] = wdn_ref[...].astype(BF16)

        x_lo, x_hi = _unpack_rows(x_ref[...])
        half = x_lo.shape[1]
        h = (jnp.dot(x_lo, wgu_bf[0:half, :], preferred_element_type=F32)
             + jnp.dot(x_hi, wgu_bf[half:, :], preferred_element_type=F32))
        ff = h.shape[1] // 2
        act = jax.nn.silu(h[:, :ff]) * h[:, ff:]
        o_ref[...] = jnp.dot(act.astype(BF16), wdn_bf[...], preferred_element_type=F32)

    @pl.when(jnp.logical_not(used))
    def _():
        o_ref[...] = jnp.zeros(o_ref.shape, F32)


def _experts(x_rows, block_exp, n_used, w_gu, w_dn, layer):
    rows, width = x_rows.shape
    n_blocks = rows // EXPERT_BLOCK
    d, ff2 = w_gu.shape[-2:]
    grid_spec = pltpu.PrefetchScalarGridSpec(
        num_scalar_prefetch=2,
        grid=(n_blocks,),
        in_specs=[pl.BlockSpec((EXPERT_BLOCK, width), lambda i, be, nu: (i, 0)),
                  pl.BlockSpec((None, None, d, ff2), lambda i, be, nu: (layer, be[i], 0, 0)),
                  pl.BlockSpec((None, None, ff2 // 2, d), lambda i, be, nu: (layer, be[i], 0, 0))],
        out_specs=pl.BlockSpec((EXPERT_BLOCK, d), lambda i, be, nu: (i, 0)),
        scratch_shapes=[pltpu.VMEM((d, ff2), BF16), pltpu.VMEM((ff2 // 2, d), BF16)],
    )
    return pl.pallas_call(
        _experts_body,
        out_shape=jax.ShapeDtypeStruct((rows, d), F32),
        grid_spec=grid_spec,
        compiler_params=_params(("arbitrary",)),
        name="experts",
    )(block_exp, n_used, x_rows, w_gu, w_dn)


def _block_plan(counts, n_blocks):
    counts = counts.reshape(-1).astype(I32)
    padded = (counts + EXPERT_BLOCK - 1) // EXPERT_BLOCK * EXPERT_BLOCK
    pad_end = jnp.cumsum(padded)
    start_col = (pad_end - padded).astype(F32)[:, None]
    end_blocks = pad_end // EXPERT_BLOCK
    block_exp = jnp.minimum(jnp.sum(end_blocks[None, :] <= jnp.arange(n_blocks)[:, None], axis=1),
                            N_EXPERTS - 1).astype(I32)
    return start_col, block_exp, end_blocks[-1:].astype(I32)


def _ffn_out_body(dest_ref, dest_next_ref, x_ref, gw_ref, yb_ref, wgu_ref, wdn_ref, g_ref, beta_ref, o_ref,
                  ybuf, sems, *, alpha):
    i = pl.program_id(0)
    n_steps = pl.num_programs(0)
    tm = x_ref.shape[0]
    slot = i % 2

    def gather(dests, to_slot):
        def issue(group, carry):
            base = pl.multiple_of(group * SUBLANES, SUBLANES)
            for r in range(SUBLANES):
                for k in range(EXPERT_TOPK):
                    pltpu.make_async_copy(yb_ref.at[pl.ds(dests[0, k * tm + base + r], 1)],
                                          ybuf.at[to_slot, k, pl.ds(base + r, 1)],
                                          sems.at[to_slot]).start(priority=k % 2)
            return carry
        lax.fori_loop(0, tm // SUBLANES, issue, 0)

    @pl.when(i == 0)
    def _():
        gather(dest_ref, 0)

    for parity in range(2):
        @pl.when((i + 1 < n_steps) & (slot == parity))
        def _(parity=parity):
            gather(dest_next_ref, 1 - parity)

    x = x_ref[...]
    h = jnp.dot(x.astype(BF16), wgu_ref[...], preferred_element_type=F32)
    ff = h.shape[1] // 2
    act = jax.nn.silu(h[:, :ff]) * h[:, ff:]
    shared = jnp.dot(act.astype(BF16), wdn_ref[...], preferred_element_type=F32)

    for k in range(EXPERT_TOPK):
        pltpu.make_async_copy(yb_ref.at[pl.ds(0, tm)], ybuf.at[slot, k], sems.at[slot]).wait()
    routed = None
    for k in range(EXPERT_TOPK):
        term = ybuf[slot, k] * gw_ref[:, k:k + 1]
        routed = term if routed is None else routed + term
    o_ref[...] = _layer_norm(alpha * x + (routed + shared), g_ref[...], beta_ref[...])


def _ffn_out(x, yb, dest_t, gw, ws_gu, ws_dn, g, beta, alpha, tm=256):
    n, d = x.shape
    n_tiles = n // tm
    row = pl.BlockSpec((tm, d), lambda i: (i, 0))
    full = lambda arr: pl.BlockSpec(arr.shape, lambda i: (0,) * arr.ndim)
    dest_spec = lambda index: pl.BlockSpec((None, 1, EXPERT_TOPK * tm), index, memory_space=pltpu.SMEM)
    dest_tiles = _tile_major(dest_t, tm)
    return pl.pallas_call(
        functools.partial(_ffn_out_body, alpha=alpha),
        out_shape=jax.ShapeDtypeStruct((n, d), F32),
        grid=(n_tiles,),
        in_specs=[dest_spec(lambda i: (i, 0, 0)), dest_spec(lambda i: (jnp.minimum(i + 1, n_tiles - 1), 0, 0)),
                  row, pl.BlockSpec((tm, EXPERT_TOPK), lambda i: (i, 0)),
                  pl.BlockSpec(memory_space=pl.ANY),
                  full(ws_gu), full(ws_dn), full(g), full(beta)],
        out_specs=row,
        scratch_shapes=[pltpu.VMEM((2, EXPERT_TOPK, tm, d), F32), pltpu.SemaphoreType.DMA((2,))],
        compiler_params=_params(("arbitrary",)),
        name="ffn_out",
    )(dest_tiles, dest_tiles, x, gw, yb, ws_gu, ws_dn, g, beta)


def _moe(x, xw, layer, w_router_t, b_router_col, w_exp_gate_up, w_exp_down, ws_gu, ws_dn, g, beta, alpha):
    n = x.shape[0]
    n_blocks = -(-n * EXPERT_TOPK // EXPERT_BLOCK) + N_EXPERTS
    eidx_t, gw_t, rank_t, counts = _router(x, w_router_t, b_router_col)
    start_col, block_exp, n_used = _block_plan(counts, n_blocks)
    dest_t = _dest_rows(eidx_t, rank_t, start_col)
    x_rows = _scatter_rows(xw, dest_t, n_blocks * EXPERT_BLOCK)
    yb = _experts(x_rows, block_exp, n_used, w_exp_gate_up, w_exp_down, layer)
    return _ffn_out(x, yb, dest_t, gw_t.T, ws_gu, ws_dn, g, beta, alpha)


def kernel(x_prompt, x_sample, mem_prompt, cache_a_k, cache_a_v, cache_b_k, cache_b_v, cache_b_idx_k,
           cache_mem_k, cache_mem_v, w_in, rel_bias, w_mem_kv, w_branch, w_out, ln_mix_g, ln_mix_b,
           w_router, b_router, w_exp_gate_up, w_exp_down, w_sh_gate_up, w_sh_down, ln_ffn_g, ln_ffn_b):
    n_batch, seq, d = x_prompt.shape
    s_batch, s_seq, _ = x_sample.shape
    depth = w_in.shape[0]
    past = cache_b_k.shape[2]
    assert s_seq == CHUNK and seq % CHUNK == 0 and cache_a_k.shape[2] == BAND and seq >= BAND
    alpha = (2 * depth) ** 0.25
    n_p, n_s = n_batch * seq, s_batch * s_seq
    n_tot = n_p + n_s
    tm = 512
    assert seq % tm == 0 and n_p % tm == 0 and n_s % tm == 0 and tm % CHUNK == 0

    x = jnp.concatenate([x_prompt.reshape(n_p, d), x_sample.reshape(n_s, d)])

    pos = jnp.concatenate([jnp.arange(seq), jnp.tile(past + jnp.arange(s_seq), tm // s_seq)])
    tab_heads = _rope_tables(pos, LANES)
    tab_kiw = _rope_tables(pos, HEAD_DIM)
    p_tiles, seq_tiles = n_p // tm, seq // tm
    table_index = lambda i: jnp.where(i < p_tiles, i % seq_tiles, seq_tiles)

    wd = WIDTH
    c_qi, c_ki, c_wi, c_qm, c_gates = 6 * wd, 7 * wd, 7 * wd + HEAD_DIM, 7 * wd + HEAD_DIM + N_HEADS, 8 * wd + HEAD_DIM + N_HEADS
    cache_a_k = cache_a_k.reshape(depth, s_batch, BAND, wd)
    cache_a_v = cache_a_v.reshape(depth, s_batch, BAND, wd)
    cache_b_k = cache_b_k.reshape(depth, s_batch, past, wd)
    cache_b_v = cache_b_v.reshape(depth, s_batch, past, wd)
    cache_mem_k = cache_mem_k.reshape(depth, s_batch, N_MEM, wd)
    cache_mem_v = cache_mem_v.reshape(depth, s_batch, N_MEM, wd)
    mem_rows = mem_prompt.reshape(n_batch * N_MEM, d)

    outs = {k: [] for k in ("p_ak", "p_av", "p_bk", "p_bv", "p_ik", "p_mk", "p_mv",
                            "s_ak", "s_av", "s_bk", "s_bv", "s_ik")}
    for l in range(depth):
        w = w_in[l]
        w_a = w[:, :3 * wd].astype(BF16)
        w_b = w[:, 3 * wd:7 * wd].astype(BF16)
        w_c = w[:, c_qm:].astype(BF16)
        w_d = jnp.concatenate([w[:, c_ki:c_qm], jnp.zeros((d, LANES - HEAD_DIM - N_HEADS), F32)], 1).astype(BF16)

        qa, ka, va = _project(x, w_a, (wd, wd, wd), split_flags=(False, True, True), n_prompt=n_p,
                              tm=tm, name="proj_a")
        qb, kb, vb, qi = _project(x, w_b, (wd, wd, wd, wd), (True, True, False, True), tab_heads, table_index,
                                  split_flags=(False, True, True, False), n_prompt=n_p, tm=tm, name="proj_b")
        qm, gates = _project(x, w_c, (wd, 3 * d), tm=tm, name="proj_c")
        (kiw,) = _project(x, w_d, (LANES,), (True,), tab_kiw, table_index, split_flags=(True,), n_prompt=n_p,
                          tm=tm, name="proj_d")
        mk, mv = _project(mem_rows, w_mem_kv[l].astype(BF16), (wd, wd), tm=N_MEM, name="proj_mem")

        a = (_band_prompt(qa, ka[0], va[0], rel_bias[l], n_batch, seq),
             _band_sample(qa, ka[1], va[1], cache_a_k, cache_a_v, l, _band_bias(rel_bias[l], CHUNK), s_batch, n_p))
        b = (_dsa_prompt(qb, kb[0], vb[0], qi, kiw[0], n_batch, seq),
             _dsa_sample(qb, kb[1], vb[1], qi, kiw[1], cache_b_k, cache_b_v, cache_b_idx_k, l, s_batch, n_p))
        m = (_mem_attn(qm, mk, mv, lambda bb: (bb, 0), n_batch, seq, 0, tm),
             _mem_attn(qm, cache_mem_k, cache_mem_v, lambda bb, l=l: (l, bb, 0, 0), s_batch, s_seq, n_p, s_seq))

        x1, x1w = _merge(x, (a, b, m), gates, w_branch[l].astype(BF16), w_out[l].astype(BF16),
                         ln_mix_g[l][None], ln_mix_b[l][None], alpha)
        x = _moe(x1, x1w, l, w_router[l].T.astype(BF16), b_router[l][:, None], w_exp_gate_up, w_exp_down,
                 w_sh_gate_up[l].astype(BF16), w_sh_down[l].astype(BF16),
                 ln_ffn_g[l][None], ln_ffn_b[l][None], alpha)

        outs["p_ak"].append(ka[0].reshape(n_batch, seq, wd)[:, -BAND:])
        outs["p_av"].append(va[0].reshape(n_batch, seq, wd)[:, -BAND:])
        outs["p_bk"].append(kb[0].reshape(n_batch, seq, wd))
        outs["p_bv"].append(vb[0].reshape(n_batch, seq, wd))
        outs["p_ik"].append(kiw[0][:, :HEAD_DIM].reshape(n_batch, seq, HEAD_DIM))
        outs["p_mk"].append(mk.reshape(n_batch, N_MEM, wd))
        outs["p_mv"].append(mv.reshape(n_batch, N_MEM, wd))
        ka_s = ka[1].reshape(s_batch, s_seq, wd)
        va_s = va[1].reshape(s_batch, s_seq, wd)
        outs["s_ak"].append(jnp.concatenate([cache_a_k[l], ka_s], 1)[:, -BAND:])
        outs["s_av"].append(jnp.concatenate([cache_a_v[l], va_s], 1)[:, -BAND:])
        outs["s_bk"].append(kb[1].reshape(s_batch, s_seq, wd))
        outs["s_bv"].append(vb[1].reshape(s_batch, s_seq, wd))
        outs["s_ik"].append(kiw[1][:, :HEAD_DIM].reshape(s_batch, s_seq, HEAD_DIM))

    heads = lambda t: t.reshape(t.shape[:-1] + (N_HEADS, HEAD_DIM))
    mheads = lambda t: t.reshape(t.shape[:-1] + (M_HEADS, M_HEAD_DIM))
    st = lambda key: jnp.stack(outs[key])
    return (x[:n_p].reshape(n_batch, seq, d), x[n_p:].reshape(s_batch, s_seq, d),
            heads(st("p_ak")), heads(st("p_av")), heads(st("p_bk")), heads(st("p_bv")), st("p_ik"),
            mheads(st("p_mk")), mheads(st("p_mv")),
            heads(st("s_ak")), heads(st("s_av")), heads(st("s_bk")), heads(st("s_bv")), st("s_ik"))
```

```python
import functools

import jax
import jax.numpy as jnp
from jax import lax
from jax.experimental import pallas as pl
from jax.experimental.pallas import tpu as pltpu

F32 = jnp.float32
BF16 = jnp.bfloat16
I32 = jnp.int32

CHUNK = 64
HEAD_DIM = 64
N_HEADS = 8
WIDTH = N_HEADS * HEAD_DIM
BAND = 8 * CHUNK
MAX_REL = 128
TOPK_KEYS = 256
M_HEADS = 4
M_HEAD_DIM = 128
N_MEM = 256
ROPE_THETA = 10000.0
N_EXPERTS = 256
EXPERT_TOPK = 8
N_GROUPS = 8
TOPK_GROUPS = 4
ROUTE_SCALE = 2.5
EXPERT_BLOCK = 256
LN_EPS = 1e-5

LANES = 128
SUBLANES = 8
NEG_BIG = -1e30
INT_MIN = -(2 ** 31)
VMEM_LIMIT = 48 * 1024 * 1024

_NT_DIMS = (((1,), (1,)), ((), ()))


def _params(semantics):
    return pltpu.CompilerParams(dimension_semantics=semantics, vmem_limit_bytes=VMEM_LIMIT)


def _proj_body(*refs, widths, rope_flags, split_flags, prompt_tiles):
    has_rope = any(rope_flags)
    x_ref, w_ref = refs[0], refs[1]
    if has_rope:
        cos_ref, sin_lo_ref, sin_hi_ref = refs[2:5]
        outs = list(refs[5:])
    else:
        outs = list(refs[2:])
    in_prompt = pl.program_id(0) < prompt_tiles
    xb = x_ref[...].astype(BF16)
    off = 0
    for g, width in enumerate(widths):
        y = jnp.dot(xb, w_ref[:, off:off + width], preferred_element_type=F32)
        if rope_flags[g]:
            cos, sin_lo, sin_hi = cos_ref[...], sin_lo_ref[...], sin_hi_ref[...]
            chunks = [yk * cos + pltpu.roll(yk, LANES - 32, 1) * sin_lo + pltpu.roll(yk, 32, 1) * sin_hi
                      for yk in (y[:, k * LANES:(k + 1) * LANES] for k in range(width // LANES))]
        else:
            chunks = [y]

        def store(ref, chunks=chunks):
            step = chunks[0].shape[1]
            for k, chunk in enumerate(chunks):
                ref[:, k * step:(k + 1) * step] = chunk

        if split_flags[g]:
            prompt_ref, sample_ref = outs.pop(0), outs.pop(0)
            pl.when(in_prompt)(functools.partial(store, prompt_ref))
            pl.when(jnp.logical_not(in_prompt))(functools.partial(store, sample_ref))
        else:
            store(outs.pop(0))
        off += width


def _project(x, w, widths, rope_flags=None, tables=None, table_index=None, split_flags=None, n_prompt=None,
             tm=512, name="proj"):
    n, d = x.shape
    rope_flags = rope_flags or (False,) * len(widths)
    split_flags = split_flags or (False,) * len(widths)
    assert n % tm == 0 and w.shape == (d, sum(widths))
    prompt_tiles = n // tm
    if any(split_flags):
        assert n_prompt % tm == 0 and 0 < n_prompt < n
        prompt_tiles = n_prompt // tm
    in_specs = [pl.BlockSpec((tm, d), lambda i: (i, 0)),
                pl.BlockSpec(w.shape, lambda i: (0, 0))]
    args = [x, w]
    if any(rope_flags):
        for t in tables:
            in_specs.append(pl.BlockSpec((tm, LANES), lambda i: (table_index(i), 0)))
            args.append(t)
    out_shape, out_specs = [], []
    for wd, split in zip(widths, split_flags):
        if split:
            out_shape += [jax.ShapeDtypeStruct((n_prompt, wd), F32), jax.ShapeDtypeStruct((n - n_prompt, wd), F32)]
            out_specs += [pl.BlockSpec((tm, wd), lambda i: (jnp.minimum(i, prompt_tiles - 1), 0)),
                          pl.BlockSpec((tm, wd), lambda i: (jnp.maximum(i - prompt_tiles, 0), 0))]
        else:
            out_shape.append(jax.ShapeDtypeStruct((n, wd), F32))
            out_specs.append(pl.BlockSpec((tm, wd), lambda i: (i, 0)))
    flat = pl.pallas_call(
        functools.partial(_proj_body, widths=tuple(widths), rope_flags=tuple(rope_flags),
                          split_flags=tuple(split_flags), prompt_tiles=prompt_tiles),
        out_shape=out_shape,
        grid=(n // tm,),
        in_specs=in_specs,
        out_specs=out_specs,
        compiler_params=_params(("arbitrary",)),
        name=name,
    )(*args)
    flat = list(flat)
    return [(flat.pop(0), flat.pop(0)) if split else flat.pop(0) for split in split_flags]


def _rope_tables(pos, rope_lanes):
    half = HEAD_DIM // 2
    inv_freq = ROPE_THETA ** (-jnp.arange(half, dtype=F32) / half)
    ang = pos.astype(F32)[:, None] * inv_freq[None, :]
    cos, sin = jnp.cos(ang), jnp.sin(ang)
    reps = LANES // HEAD_DIM
    zero = jnp.zeros_like(sin)
    cos_t = jnp.tile(jnp.concatenate([cos, cos], -1), (1, reps))
    sin_lo = jnp.tile(jnp.concatenate([-sin, zero], -1), (1, reps))
    sin_hi = jnp.tile(jnp.concatenate([zero, sin], -1), (1, reps))
    keep = (jnp.arange(LANES) < rope_lanes)[None, :]
    return (jnp.where(keep, cos_t, 1.0), jnp.where(keep, sin_lo, 0.0), jnp.where(keep, sin_hi, 0.0))


def _reduce_rows(x, op, keep_groups=False):
    rows, n = x.shape
    groups = next(g for g in (8, 4, 2, 1) if rows % (8 * g) == 0 and rows // (8 * g) >= 4 or g == 1)
    parts = op(x.reshape(groups, rows // (8 * groups), 8, n), axis=1)
    if keep_groups:
        return parts
    return op(op(parts, axis=0), axis=0, keepdims=True)


def _head_mask(rows, head_in_pair):
    lane = lax.broadcasted_iota(I32, (rows, LANES), 1)
    lo = head_in_pair * HEAD_DIM
    return (lane >= lo) & (lane < lo + HEAD_DIM)


def _pair_merge(o_even, o_odd):
    lane = lax.broadcasted_iota(I32, o_even.shape, 1)
    return jnp.where(lane < HEAD_DIM, o_even, o_odd)


def _band_sample_body(q_ref, kn_ref, vn_ref, kh_ref, vh_ref, bias_ref, o_ref, kwin, vwin):
    kwin[0:BAND, :] = kh_ref[...].astype(BF16)
    vwin[0:BAND, :] = vh_ref[...].astype(BF16)
    kwin[BAND:, :] = kn_ref[...].astype(BF16)
    vwin[BAND:, :] = vn_ref[...].astype(BF16)
    q = q_ref[...] * (HEAD_DIM ** -0.5)
    for hp in range(N_HEADS // 2):
        cols = slice(hp * LANES, (hp + 1) * LANES)
        outs = []
        for hh in range(2):
            qm = jnp.where(_head_mask(CHUNK, hh), q[:, cols], 0.0).astype(BF16)
            s = lax.dot_general(qm, kwin[:, cols], _NT_DIMS, preferred_element_type=F32) + bias_ref[2 * hp + hh]
            p = jnp.exp(s - jnp.max(s, axis=1, keepdims=True))
            denom = jnp.sum(p, axis=1, keepdims=True)
            outs.append(jnp.dot(p.astype(BF16), vwin[:, cols], preferred_element_type=F32) / denom)
        o_ref[:, cols] = _pair_merge(outs[0], outs[1])


def _band_prompt_body(q_ref, k_ref, v_ref, bias_ref, o_ref, k_even, k_odd, v_t, q_t, s_sc, p_sc, o_t):
    t = pl.program_id(1)
    nq = q_ref.shape[0]
    win = BAND + nq
    seq = k_ref.shape[0]
    stage_rows = 256

    @pl.when(t == 0)
    def _():
        k_even[0:BAND, :] = jnp.zeros((BAND, WIDTH), BF16)
        k_odd[0:BAND, :] = jnp.zeros((BAND, WIDTH), BF16)
        v_t[:, 0:BAND] = jnp.zeros((WIDTH, BAND), BF16)
        for r in range(0, seq, stage_rows):
            ke, ko = _split_pair_lanes(k_ref[r:r + stage_rows, :])
            k_even[BAND + r:BAND + r + stage_rows, :] = ke.astype(BF16)
            k_odd[BAND + r:BAND + r + stage_rows, :] = ko.astype(BF16)
            v_t[:, BAND + r:BAND + r + stage_rows] = v_ref[r:r + stage_rows, :].T.astype(BF16)

    start = pl.multiple_of(t * nq, nq)
    window = pl.ds(start, win)
    q_t[...] = (q_ref[...] * (HEAD_DIM ** -0.5)).T.astype(BF16)
    before_start = jnp.where(lax.broadcasted_iota(I32, (win, nq), 0) >= BAND - t * nq, 0.0, NEG_BIG)
    for h in range(N_HEADS):
        pair = slice((h // 2) * LANES, (h // 2 + 1) * LANES)
        k_ref_h = k_even if h % 2 == 0 else k_odd
        s_sc[h * win:(h + 1) * win, :] = (jnp.dot(k_ref_h[window, pair], q_t[pair, :], preferred_element_type=F32)
                                          + bias_ref[h] + before_start)
    for h in range(N_HEADS):
        s = s_sc[h * win:(h + 1) * win, :]
        p = jnp.exp(s - _reduce_rows(s, jnp.max))
        p_sc[h * win:(h + 1) * win, :] = (p / _reduce_rows(p, jnp.sum)).astype(BF16)
    for h in range(N_HEADS):
        dims = slice(h * HEAD_DIM, (h + 1) * HEAD_DIM)
        o_t[dims, :] = jnp.dot(v_t[dims, window], p_sc[h * win:(h + 1) * win, :], preferred_element_type=F32)
    o_ref[...] = o_t[...].T


def _band_bias(table, n_rows):
    win = BAND + n_rows
    assert n_rows <= MAX_REL <= BAND
    n_heads = table.shape[0]
    g = jnp.concatenate([jnp.broadcast_to(table[:, 2 * MAX_REL:], (n_heads, win - MAX_REL)),
                         table[:, MAX_REL - n_rows:2 * MAX_REL][:, ::-1],
                         jnp.zeros((n_heads, 1), table.dtype)], axis=1).astype(F32)
    period = g.shape[1]
    skew = jnp.tile(g, (1, n_rows))[:, :n_rows * (period - 1)].reshape(n_heads, n_rows, period - 1)
    return skew[:, :, n_rows - 1:n_rows - 1 + win]


def _band_prompt(q, k, v, table, n_batch, seq, q_rows=128):
    nt = seq // q_rows
    win = BAND + q_rows
    assert seq % q_rows == 0 and q_rows % CHUNK == 0
    key_chunk = jnp.arange(win)[:, None] // CHUNK - BAND // CHUNK
    query_chunk = jnp.arange(q_rows)[None, :] // CHUNK
    visible = (key_chunk <= query_chunk) & (key_chunk >= query_chunk - BAND // CHUNK)
    bias = jnp.where(visible[None], jnp.swapaxes(_band_bias(table, q_rows), 1, 2), NEG_BIG)
    tile_spec = pl.BlockSpec((q_rows, WIDTH), lambda b, t: (b * nt + t, 0))
    seq_spec = pl.BlockSpec((seq, WIDTH), lambda b, t: (b, 0))
    return pl.pallas_call(
        _band_prompt_body,
        out_shape=jax.ShapeDtypeStruct((n_batch * seq, WIDTH), F32),
        grid=(n_batch, nt),
        in_specs=[tile_spec, seq_spec, seq_spec, pl.BlockSpec(bias.shape, lambda b, t: (0, 0, 0))],
        out_specs=tile_spec,
        scratch_shapes=[pltpu.VMEM((BAND + seq, WIDTH), BF16), pltpu.VMEM((BAND + seq, WIDTH), BF16),
                        pltpu.VMEM((WIDTH, BAND + seq), BF16), pltpu.VMEM((WIDTH, q_rows), BF16),
                        pltpu.VMEM((N_HEADS * win, q_rows), F32), pltpu.VMEM((N_HEADS * win, q_rows), BF16),
                        pltpu.VMEM((WIDTH, q_rows), F32)],
        compiler_params=_params(("parallel", "arbitrary")),
        name="band_prompt",
    )(q, k, v, bias)


def _band_sample(q, k, v, k_hist, v_hist, layer, bias, n_batch, row0):
    blk0 = row0 // CHUNK
    hist_spec = pl.BlockSpec((None, None, BAND, WIDTH), lambda b: (layer, b, 0, 0))
    new_spec = pl.BlockSpec((CHUNK, WIDTH), lambda b: (b, 0))
    return pl.pallas_call(
        _band_sample_body,
        out_shape=jax.ShapeDtypeStruct((n_batch * CHUNK, WIDTH), F32),
        grid=(n_batch,),
        in_specs=[pl.BlockSpec((CHUNK, WIDTH), lambda b: (blk0 + b, 0)), new_spec, new_spec, hist_spec, hist_spec,
                  pl.BlockSpec(bias.shape, lambda b: (0, 0, 0))],
        out_specs=new_spec,
        scratch_shapes=[pltpu.VMEM((BAND + CHUNK, WIDTH), BF16), pltpu.VMEM((BAND + CHUNK, WIDTH), BF16)],
        compiler_params=_params(("parallel",)),
        name="band_sample",
    )(q, k, v, k_hist, v_hist, bias)


def _mem_body(q_ref, k_ref, v_ref, o_ref):
    scale = M_HEAD_DIM ** -0.5
    for h in range(M_HEADS):
        cols = slice(h * M_HEAD_DIM, (h + 1) * M_HEAD_DIM)
        s = lax.dot_general(q_ref[:, cols].astype(BF16), k_ref[:, cols].astype(BF16), _NT_DIMS,
                            preferred_element_type=F32) * scale
        p = jnp.exp(s - jnp.max(s, axis=1, keepdims=True))
        denom = jnp.sum(p, axis=1, keepdims=True)
        o = jnp.dot(p.astype(BF16), v_ref[:, cols].astype(BF16), preferred_element_type=F32)
        o_ref[:, cols] = o / denom


def _mem_attn(q, mk, mv, kv_index, n_batch, rows_per_batch, row0, tq):
    width = M_HEADS * M_HEAD_DIM
    per = rows_per_batch // tq
    blk0 = row0 // tq
    kv_block = (None,) * (mk.ndim - 2) + (N_MEM, width)
    kv_spec = pl.BlockSpec(kv_block, lambda b, t: kv_index(b))
    return pl.pallas_call(
        _mem_body,
        out_shape=jax.ShapeDtypeStruct((n_batch * rows_per_batch, width), F32),
        grid=(n_batch, per),
        in_specs=[pl.BlockSpec((tq, width), lambda b, t: (blk0 + b * per + t, 0)), kv_spec, kv_spec],
        out_specs=pl.BlockSpec((tq, width), lambda b, t: (b * per + t, 0)),
        compiler_params=_params(("parallel", "parallel")),
        name="mem_attn",
    )(q, mk, mv)


def _split_pair_lanes(x):
    lane = lax.broadcasted_iota(I32, x.shape, 1)
    even = (lane & (LANES - 1)) < HEAD_DIM
    return jnp.where(even, x, 0.0), jnp.where(even, 0.0, x)


def _dsa_body(*refs, sample, key_block, n_keys, n_sel):
    n_in = 8 if sample else 6
    if sample:
        q_ref, qi_ref, kiwq_ref, kn_ref, vn_ref, kc_ref, vc_ref, kic_ref = refs[:n_in]
    else:
        q_ref, qi_ref, kiwq_ref, k_ref, v_ref, kiw_ref = refs[:n_in]
    o_ref = refs[n_in]
    (k_even, k_odd, v_t, ki_lo, ki_hi, keybuf, maskbuf, q_t, qi_t, kiwq_t, m_sc, l_sc, a_sc, acc_t,
     s_sc, p_sc) = refs[n_in + 1:]
    kb = key_block
    n_pad = k_even.shape[0]
    nq = q_t.shape[1]
    widen = lambda rows: jnp.concatenate([rows] * (nq // rows.shape[0]), axis=0)
    stage_rows = 256

    def stage_kv(row0, k_new, v_new):
        n_new = k_new.shape[0]
        ke, ko = _split_pair_lanes(k_new)
        k_even[row0:row0 + n_new, :] = ke.astype(BF16)
        k_odd[row0:row0 + n_new, :] = ko.astype(BF16)
        v_t[:, row0:row0 + n_new] = v_new.T.astype(BF16)

    def stage_index_keys(row0, kiw):
        lane = lax.broadcasted_iota(I32, kiw.shape, 1)
        ki_lo[row0:row0 + kiw.shape[0], :] = jnp.where(lane < HEAD_DIM, kiw, 0.0).astype(BF16)
        ki_hi[row0:row0 + kiw.shape[0], :] = jnp.where(lane < HEAD_DIM, 0.0, pltpu.roll(kiw, HEAD_DIM, 1)).astype(BF16)

    def stage_keys():
        if sample:
            n_hist = kc_ref.shape[0]
            for r in range(0, n_hist, stage_rows):
                stage_kv(r, kc_ref[r:r + stage_rows, :], vc_ref[r:r + stage_rows, :])
            kic = kic_ref[...]
            zeros = jnp.zeros_like(kic)
            ki_lo[0:n_hist, :] = jnp.concatenate([kic, zeros], axis=1).astype(BF16)
            ki_hi[0:n_hist, :] = jnp.concatenate([zeros, kic], axis=1).astype(BF16)
            stage_kv(n_hist, kn_ref[...], vn_ref[...])
            stage_index_keys(n_hist, kiwq_ref[...])
            if n_pad > n_keys:
                for ref in (k_even, k_odd, ki_lo, ki_hi):
                    ref[n_keys:, :] = jnp.zeros((n_pad - n_keys, ref.shape[1]), BF16)
                v_t[:, n_keys:] = jnp.zeros((v_t.shape[0], n_pad - n_keys), BF16)
        else:
            for r in range(0, n_keys, stage_rows):
                stage_kv(r, k_ref[r:r + stage_rows, :], v_ref[r:r + stage_rows, :])
            stage_index_keys(0, kiw_ref[...])

    if sample:
        stage_keys()
        limit = n_keys
        nb = n_pad // kb
        search_bits = 32
    else:
        t = pl.program_id(1)
        pl.when(t == 0)(stage_keys)
        chunk = t * (nq // CHUNK) + lax.broadcasted_iota(I32, (1, nq), 1) // CHUNK
        limit = (chunk + 1) * CHUNK
        limit_max = (t + 1) * nq
        nb = (limit_max + kb - 1) // kb
        search_bits = jnp.where(limit_max <= n_sel, 0, 32)

    def block_rows(j):
        return pl.ds(pl.multiple_of(j * kb, kb), kb)

    def key_pos(j):
        return j * kb + lax.broadcasted_iota(I32, (kb, nq), 0)

    q_t[...] = widen(q_ref[...] * (HEAD_DIM ** -0.5)).T.astype(BF16)
    qi_t[...] = widen(qi_ref[...]).T.astype(BF16)
    kiwq_t[...] = widen(kiwq_ref[...]).T

    def score_block(j, carry):
        acc = jnp.zeros((kb, nq), F32)
        for h in range(N_HEADS):
            pair = slice((h // 2) * LANES, (h // 2 + 1) * LANES)
            ki_ref = ki_lo if h % 2 == 0 else ki_hi
            d = jnp.dot(ki_ref[block_rows(j), :], qi_t[pair, :], preferred_element_type=F32)
            acc = acc + kiwq_t[HEAD_DIM + h:HEAD_DIM + h + 1, :] * jnp.maximum(d, 0.0)
        acc = acc + 0.0
        acc = jnp.where(key_pos(j) < limit, acc, -jnp.inf)
        bits = lax.bitcast_convert_type(acc, I32)
        keybuf[block_rows(j), :] = bits ^ ((bits >> 31) & 0x7FFFFFFF)
        return carry

    lax.fori_loop(0, nb, score_block, 0)

    def count(pred_fn):
        def body(j, part):
            hit = jnp.where(pred_fn(keybuf[block_rows(j), :], j), 1.0, 0.0)
            return part + _reduce_rows(hit, jnp.sum, keep_groups=True)
        groups = jax.eval_shape(functools.partial(_reduce_rows, op=jnp.sum, keep_groups=True),
                                jax.ShapeDtypeStruct((kb, nq), F32)).shape[0]
        part = lax.fori_loop(0, nb, body, jnp.zeros((groups, 8, nq), F32))
        return jnp.sum(jnp.sum(part, axis=0), axis=0, keepdims=True)

    def bit_step(i, prefix):
        cand = prefix | lax.shift_left(jnp.int32(1), 31 - i)
        thr_i = cand ^ INT_MIN
        cnt = count(lambda kblk, j: kblk >= thr_i)
        return jnp.where(cnt >= n_sel, cand, prefix)

    prefix = lax.fori_loop(0, search_bits, bit_step, jnp.zeros((1, nq), I32))
    thr = prefix ^ INT_MIN
    above = count(lambda kblk, j: kblk > thr)
    at_least = count(lambda kblk, j: kblk >= thr)
    need = n_sel - above
    surplus = at_least > n_sel

    pos_bits = max(1, (n_pad - 1).bit_length() + 1)
    any_surplus = jnp.max(jnp.where(surplus, 1.0, 0.0)) > 0.0

    def pos_step(i, bound):
        cand = bound | lax.shift_left(jnp.int32(1), pos_bits - 1 - i)
        cnt = count(lambda kblk, j: (kblk == thr) & (key_pos(j) < cand))
        return jnp.where(cnt <= need, cand, bound)

    bound = lax.fori_loop(0, jnp.where(any_surplus, pos_bits, 0), pos_step, jnp.zeros((1, nq), I32))
    bound = jnp.where(surplus, bound, 2 ** pos_bits - 1)

    def mask_block(j, carry):
        kblk = keybuf[block_rows(j), :]
        pos = key_pos(j)
        sel = ((kblk > thr) | ((kblk == thr) & (pos < bound))) & (pos < limit)
        maskbuf[block_rows(j), :] = jnp.where(sel, 0.0, NEG_BIG)
        return carry

    lax.fori_loop(0, nb, mask_block, 0)

    m_sc[...] = jnp.full(m_sc.shape, NEG_BIG, F32)
    l_sc[...] = jnp.zeros(l_sc.shape, F32)
    acc_t[...] = jnp.zeros(acc_t.shape, F32)

    def attend(j, carry):
        mask = maskbuf[block_rows(j), :]
        for h in range(N_HEADS):
            pair = slice((h // 2) * LANES, (h // 2 + 1) * LANES)
            k_ref_h = k_even if h % 2 == 0 else k_odd
            s_sc[h * kb:(h + 1) * kb, :] = jnp.dot(k_ref_h[block_rows(j), pair], q_t[pair, :],
                                                   preferred_element_type=F32) + mask
        for h in range(N_HEADS):
            stat = slice(h * 8, (h + 1) * 8)
            s = s_sc[h * kb:(h + 1) * kb, :]
            m_prev = m_sc[stat, :]
            m_new = jnp.maximum(m_prev, _reduce_rows(s, jnp.max))
            alpha = jnp.exp(m_prev - m_new)
            p = jnp.exp(s - m_new[0:1, :])
            l_sc[stat, :] = alpha * l_sc[stat, :] + _reduce_rows(p, jnp.sum)
            m_sc[stat, :] = m_new
            a_sc[stat, :] = alpha
            p_sc[h * kb:(h + 1) * kb, :] = p.astype(BF16)
        for h in range(N_HEADS):
            dims = slice(h * HEAD_DIM, (h + 1) * HEAD_DIM)
            pv = jnp.dot(v_t[dims, block_rows(j)], p_sc[h * kb:(h + 1) * kb, :], preferred_element_type=F32)
            acc_t[dims, :] = a_sc[h * 8:h * 8 + 1, :] * acc_t[dims, :] + pv
        return carry

    lax.fori_loop(0, nb, attend, 0)
    for h in range(N_HEADS):
        dims = slice(h * HEAD_DIM, (h + 1) * HEAD_DIM)
        acc_t[dims, :] = acc_t[dims, :] / l_sc[h * 8:h * 8 + 1, :]
    o_ref[...] = acc_t[...].T[0:o_ref.shape[0], :]


def _dsa_scratch(nq, n_pad, key_block):
    nq = max(nq, LANES)
    return [pltpu.VMEM((n_pad, WIDTH), BF16), pltpu.VMEM((n_pad, WIDTH), BF16),
            pltpu.VMEM((WIDTH, n_pad), BF16),
            pltpu.VMEM((n_pad, LANES), BF16), pltpu.VMEM((n_pad, LANES), BF16),
            pltpu.VMEM((n_pad, nq), I32), pltpu.VMEM((n_pad, nq), F32),
            pltpu.VMEM((WIDTH, nq), BF16), pltpu.VMEM((WIDTH, nq), BF16),
            pltpu.VMEM((LANES, nq), F32),
            pltpu.VMEM((N_HEADS * 8, nq), F32), pltpu.VMEM((N_HEADS * 8, nq), F32),
            pltpu.VMEM((N_HEADS * 8, nq), F32),
            pltpu.VMEM((WIDTH, nq), F32),
            pltpu.VMEM((N_HEADS * key_block, nq), F32),
            pltpu.VMEM((N_HEADS * key_block, nq), BF16)]


def _dsa_prompt(q, k, v, qi, kiw, n_batch, seq, key_block=256, q_rows=256):
    nt = seq // q_rows
    key_block = min(key_block, seq)
    assert seq % key_block == 0 and seq % q_rows == 0 and q_rows % CHUNK == 0
    n_sel = min(TOPK_KEYS, seq // 4)
    tile_spec = lambda width: pl.BlockSpec((q_rows, width), lambda b, t: (b * nt + t, 0))
    seq_spec = lambda width: pl.BlockSpec((seq, width), lambda b, t: (b, 0))
    return pl.pallas_call(
        functools.partial(_dsa_body, sample=False, key_block=key_block, n_keys=seq, n_sel=n_sel),
        out_shape=jax.ShapeDtypeStruct((n_batch * seq, WIDTH), F32),
        grid=(n_batch, nt),
        in_specs=[tile_spec(WIDTH), tile_spec(WIDTH), tile_spec(LANES),
                  seq_spec(WIDTH), seq_spec(WIDTH), seq_spec(LANES)],
        out_specs=tile_spec(WIDTH),
        scratch_shapes=_dsa_scratch(q_rows, seq, key_block),
        compiler_params=_params(("parallel", "arbitrary")),
        name="dsa_prompt",
    )(q, qi, kiw, k, v, kiw)


def _dsa_sample(q, k, v, qi, kiw, k_cache, v_cache, ki_cache, layer, n_batch, row0, key_block=768):
    past = k_cache.shape[2]
    n_keys = past + CHUNK
    key_block = min(key_block, -(-n_keys // 256) * 256)
    n_pad = -(-n_keys // key_block) * key_block
    n_sel = min(TOPK_KEYS, n_keys // 4)
    blk0 = row0 // CHUNK
    query_spec = pl.BlockSpec((CHUNK, WIDTH), lambda b: (blk0 + b, 0))
    new_spec = lambda width: pl.BlockSpec((CHUNK, width), lambda b: (b, 0))
    cache_spec = lambda width: pl.BlockSpec((None, None, past, width), lambda b: (layer, b, 0, 0))
    return pl.pallas_call(
        functools.partial(_dsa_body, sample=True, key_block=key_block, n_keys=n_keys, n_sel=n_sel),
        out_shape=jax.ShapeDtypeStruct((n_batch * CHUNK, WIDTH), F32),
        grid=(n_batch,),
        in_specs=[query_spec, query_spec, new_spec(LANES), new_spec(WIDTH), new_spec(WIDTH),
                  cache_spec(WIDTH), cache_spec(WIDTH), cache_spec(HEAD_DIM)],
        out_specs=new_spec(WIDTH),
        scratch_shapes=_dsa_scratch(CHUNK, n_pad, key_block),
        compiler_params=_params(("parallel",)),
        name="dsa_sample",
    )(q, qi, kiw, k, v, k_cache, v_cache, ki_cache)


def _layer_norm(z, g, b):
    mu = jnp.mean(z, axis=-1, keepdims=True)
    zc = z - mu
    var = jnp.mean(zc * zc, axis=-1, keepdims=True)
    return zc * lax.rsqrt(var + LN_EPS) * g + b


HI_HALF = -65536


def _pack_rows(x):
    half = x.shape[1] // 2
    lo = lax.bitcast_convert_type(x[:, :half].astype(BF16).astype(F32), I32)
    hi = lax.bitcast_convert_type(x[:, half:].astype(BF16).astype(F32), I32)
    return (hi & HI_HALF) | lax.shift_right_logical(lo, 16)


def _unpack_rows(w):
    lo = lax.bitcast_convert_type(lax.shift_left(w, 16), F32).astype(BF16)
    hi = lax.bitcast_convert_type(w & HI_HALF, F32).astype(BF16)
    return lo, hi


def _merge_body(x_ref, ap_ref, as_ref, bp_ref, bs_ref, mp_ref, ms_ref, gates_ref, wbr_ref, wout_ref, g_ref, beta_ref,
                o_ref, ow_ref, *, alpha, prompt_tiles):
    d = x_ref.shape[1]
    in_prompt = pl.program_id(0) < prompt_tiles
    mix = None
    for n, (p_ref, s_ref) in enumerate(((ap_ref, as_ref), (bp_ref, bs_ref), (mp_ref, ms_ref))):
        branch = jnp.where(in_prompt, p_ref[...], s_ref[...])
        proj = jnp.dot(branch.astype(BF16), wbr_ref[n], preferred_element_type=F32)
        term = jax.nn.sigmoid(gates_ref[:, n * d:(n + 1) * d]) * proj
        mix = term if mix is None else mix + term
    y = jnp.dot(mix.astype(BF16), wout_ref[...], preferred_element_type=F32)
    out = _layer_norm(alpha * x_ref[...] + y, g_ref[...], beta_ref[...])
    o_ref[...] = out
    ow_ref[...] = _pack_rows(out)


def _merge(x, branches, gates, w_branch, w_out, g, beta, alpha, tm=256):
    n, d = x.shape
    prompt_tiles = branches[0][0].shape[0] // tm
    assert all(p.shape[0] == prompt_tiles * tm and (n - p.shape[0]) == s.shape[0] and s.shape[0] % tm == 0
               for p, s in branches)
    packed = jax.eval_shape(_pack_rows, jax.ShapeDtypeStruct((tm, d), F32))
    row = lambda width: pl.BlockSpec((tm, width), lambda i: (i, 0))
    full = lambda arr: pl.BlockSpec(arr.shape, lambda i: (0,) * arr.ndim)
    prompt_row = lambda width: pl.BlockSpec((tm, width), lambda i: (jnp.minimum(i, prompt_tiles - 1), 0))
    sample_row = lambda width: pl.BlockSpec((tm, width), lambda i: (jnp.maximum(i - prompt_tiles, 0), 0))
    branch_specs, branch_args = [], []
    for p, s in branches:
        branch_specs += [prompt_row(p.shape[1]), sample_row(s.shape[1])]
        branch_args += [p, s]
    return pl.pallas_call(
        functools.partial(_merge_body, alpha=alpha, prompt_tiles=prompt_tiles),
        out_shape=[jax.ShapeDtypeStruct((n, d), F32), jax.ShapeDtypeStruct((n, packed.shape[1]), packed.dtype)],
        grid=(n // tm,),
        in_specs=[row(d)] + branch_specs + [row(gates.shape[1]), full(w_branch), full(w_out), full(g), full(beta)],
        out_specs=[row(d), row(packed.shape[1])],
        compiler_params=_params(("parallel",)),
        name="merge",
    )(x, *branch_args, gates, w_branch, w_out, g, beta)


def _first_max(vals, idx, n):
    top = jnp.max(vals, axis=0, keepdims=True)
    arg = jnp.min(jnp.where(vals == top, idx, n), axis=0, keepdims=True)
    return top, arg


def _router_body(x_ref, wt_ref, bias_ref, eidx_ref, gw_ref, rank_ref, counts_ref, running):
    tm = x_ref.shape[0]

    @pl.when(pl.program_id(0) == 0)
    def _():
        running[...] = jnp.zeros(running.shape, F32)

    per = N_EXPERTS // N_GROUPS
    logits = lax.dot_general(wt_ref[...], x_ref[...].astype(BF16), _NT_DIMS, preferred_element_type=F32)
    s = jax.nn.sigmoid(logits)
    sb = s + bias_ref[...]
    in_group = lax.broadcasted_iota(I32, (per, tm), 0).astype(F32)
    group_scores = []
    for g in range(N_GROUPS):
        blk = sb[g * per:(g + 1) * per, :]
        top1, arg1 = _first_max(blk, in_group, per)
        top2 = jnp.max(jnp.where(in_group == arg1, -jnp.inf, blk), axis=0, keepdims=True)
        group_scores.append(top1 + top2)
    gs = jnp.concatenate(group_scores, axis=0)
    gidx = lax.broadcasted_iota(I32, (N_GROUPS, tm), 0).astype(F32)
    chosen = jnp.zeros((N_GROUPS, tm), F32)
    for _ in range(TOPK_GROUPS):
        _, arg = _first_max(gs, gidx, N_GROUPS)
        hit = gidx == arg
        chosen = jnp.where(hit, 1.0, chosen)
        gs = jnp.where(hit, -jnp.inf, gs)
    cand = jnp.concatenate(
        [jnp.where(chosen[g:g + 1, :] > 0.0, sb[g * per:(g + 1) * per, :], -jnp.inf) for g in range(N_GROUPS)],
        axis=0)
    eidx = lax.broadcasted_iota(I32, (N_EXPERTS, tm), 0).astype(F32)
    picks, weights, hits = [], [], []
    for _ in range(EXPERT_TOPK):
        _, arg = _first_max(cand, eidx, N_EXPERTS)
        hit = eidx == arg
        weights.append(jnp.sum(jnp.where(hit, s, 0.0), axis=0, keepdims=True))
        picks.append(arg)
        hits.append(hit)
        cand = jnp.where(hit, -jnp.inf, cand)
    total = weights[0]
    for w in weights[1:]:
        total = total + w
    taken = jnp.zeros((N_EXPERTS, tm), F32)
    for hit in hits:
        taken = jnp.where(hit, 1.0, taken)
    earlier = (lax.broadcasted_iota(I32, (tm, tm), 0) < lax.broadcasted_iota(I32, (tm, tm), 1))
    before = jnp.dot(taken.astype(BF16), jnp.where(earlier, 1.0, 0.0).astype(BF16),
                     preferred_element_type=F32) + running[...]
    for r in range(EXPERT_TOPK):
        eidx_ref[r:r + 1, :] = picks[r].astype(I32)
        gw_ref[r:r + 1, :] = weights[r] / total * ROUTE_SCALE
        rank_ref[r:r + 1, :] = jnp.sum(jnp.where(hits[r], before, 0.0), axis=0, keepdims=True).astype(I32)
    running[...] = running[...] + jnp.sum(taken, axis=1, keepdims=True)
    counts_ref[...] = running[...]


def _router(x, w_router_t, b_router_col, tm=256):
    n, d = x.shape
    slot = pl.BlockSpec((EXPERT_TOPK, tm), lambda i: (0, i))
    return pl.pallas_call(
        _router_body,
        out_shape=[jax.ShapeDtypeStruct((EXPERT_TOPK, n), I32), jax.ShapeDtypeStruct((EXPERT_TOPK, n), F32),
                   jax.ShapeDtypeStruct((EXPERT_TOPK, n), I32), jax.ShapeDtypeStruct((N_EXPERTS, 1), F32)],
        grid=(n // tm,),
        in_specs=[pl.BlockSpec((tm, d), lambda i: (i, 0)),
                  pl.BlockSpec(w_router_t.shape, lambda i: (0, 0)),
                  pl.BlockSpec(b_router_col.shape, lambda i: (0, 0))],
        out_specs=[slot, slot, slot, pl.BlockSpec((N_EXPERTS, 1), lambda i: (0, 0))],
        scratch_shapes=[pltpu.VMEM((N_EXPERTS, 1), F32)],
        compiler_params=_params(("arbitrary",)),
        name="router",
    )(x, w_router_t, b_router_col)


def _dest_body(eidx_ref, rank_ref, start_ref, dest_ref):
    tm = eidx_ref.shape[1]
    experts = lax.broadcasted_iota(I32, (N_EXPERTS, tm), 0)
    for r in range(EXPERT_TOPK):
        base = jnp.sum(jnp.where(experts == eidx_ref[r:r + 1, :], start_ref[...], 0.0), axis=0, keepdims=True)
        dest_ref[r:r + 1, :] = base.astype(I32) + rank_ref[r:r + 1, :]


def _dest_rows(eidx_t, rank_t, start_col, tm=512):
    n = eidx_t.shape[1]
    slot = pl.BlockSpec((EXPERT_TOPK, tm), lambda i: (0, i))
    return pl.pallas_call(
        _dest_body,
        out_shape=jax.ShapeDtypeStruct((EXPERT_TOPK, n), I32),
        grid=(n // tm,),
        in_specs=[slot, slot, pl.BlockSpec(start_col.shape, lambda i: (0, 0))],
        out_specs=slot,
        compiler_params=_params(("parallel",)),
        name="dest_rows",
    )(eidx_t, rank_t, start_col)


def _tile_major(dest_t, tm):
    slots, n = dest_t.shape
    return dest_t.reshape(slots, n // tm, tm).transpose(1, 0, 2).reshape(n // tm, 1, slots * tm)


def _scatter_body(dest_ref, x_ref, _, o_ref, sem):
    tm = x_ref.shape[0]

    def row_copy(t, k):
        return pltpu.make_async_copy(x_ref.at[pl.ds(t, 1)], o_ref.at[pl.ds(dest_ref[0, k * tm + t], 1)], sem)

    def issue(group, carry):
        base = pl.multiple_of(group * SUBLANES, SUBLANES)
        for r in range(SUBLANES):
            for k in range(EXPERT_TOPK):
                row_copy(base + r, k).start(priority=k % 2)
        return carry

    lax.fori_loop(0, tm // SUBLANES, issue, 0)
    for k in range(EXPERT_TOPK):
        pltpu.make_async_copy(x_ref, o_ref.at[pl.ds(0, tm)], sem).wait()


def _scatter_rows(xw, dest_t, rows, tm=512):
    n, width = xw.shape
    return pl.pallas_call(
        _scatter_body,
        out_shape=jax.ShapeDtypeStruct((rows, width), xw.dtype),
        grid=(n // tm,),
        in_specs=[pl.BlockSpec((None, 1, EXPERT_TOPK * tm), lambda i: (i, 0, 0), memory_space=pltpu.SMEM),
                  pl.BlockSpec((tm, width), lambda i: (i, 0)),
                  pl.BlockSpec(memory_space=pl.ANY)],
        out_specs=pl.BlockSpec(memory_space=pl.ANY),
        scratch_shapes=[pltpu.SemaphoreType.DMA],
        input_output_aliases={2: 0},
        compiler_params=_params(("arbitrary",)),
        name="scatter_rows",
    )(_tile_major(dest_t, tm), xw, jnp.zeros((rows, width), xw.dtype))


def _experts_body(block_exp_ref, n_used_ref, x_ref, wgu_ref, wdn_ref, o_ref, wgu_bf, wdn_bf):
    i = pl.program_id(0)
    used = i < n_used_ref[0]

    @pl.when(used)
    def _():
        prev = block_exp_ref[jnp.maximum(i - 1, 0)]

        @pl.when((i == 0) | (block_exp_ref[i] != prev))
        def _():
            wgu_bf[...] = wgu_ref[...].astype(BF16)
            wdn_bf[...] = wdn_ref[...].astype(BF16)

        x_lo, x_hi = _unpack_rows(x_ref[...])
        half = x_lo.shape[1]
        h = (jnp.dot(x_lo, wgu_bf[0:half, :], preferred_element_type=F32)
             + jnp.dot(x_hi, wgu_bf[half:, :], preferred_element_type=F32))
        ff = h.shape[1] // 2
        act = jax.nn.silu(h[:, :ff]) * h[:, ff:]
        o_ref[...] = jnp.dot(act.astype(BF16), wdn_bf[...], preferred_element_type=F32)

    @pl.when(jnp.logical_not(used))
    def _():
        o_ref[...] = jnp.zeros(o_ref.shape, F32)


def _experts(x_rows, block_exp, n_used, w_gu, w_dn, layer):
    rows, width = x_rows.shape
    n_blocks = rows // EXPERT_BLOCK
    d, ff2 = w_gu.shape[-2:]
    grid_spec = pltpu.PrefetchScalarGridSpec(
        num_scalar_prefetch=2,
        grid=(n_blocks,),
        in_specs=[pl.BlockSpec((EXPERT_BLOCK, width), lambda i, be, nu: (jnp.clip(i, 0, jnp.maximum(nu[0] - 1, 0)), 0)),
                  pl.BlockSpec((None, None, d, ff2), lambda i, be, nu: (layer, be[i], 0, 0)),
                  pl.BlockSpec((None, None, ff2 // 2, d), lambda i, be, nu: (layer, be[i], 0, 0))],
        out_specs=pl.BlockSpec((EXPERT_BLOCK, d), lambda i, be, nu: (i, 0)),
        scratch_shapes=[pltpu.VMEM((d, ff2), BF16), pltpu.VMEM((ff2 // 2, d), BF16)],
    )
    return pl.pallas_call(
        _experts_body,
        out_shape=jax.ShapeDtypeStruct((rows, d), F32),
        grid_spec=grid_spec,
        compiler_params=_params(("arbitrary",)),
        name="experts",
    )(block_exp, n_used, x_rows, w_gu, w_dn)


def _block_plan(counts, n_blocks):
    counts = counts.reshape(-1).astype(I32)
    padded = (counts + EXPERT_BLOCK - 1) // EXPERT_BLOCK * EXPERT_BLOCK
    pad_end = jnp.cumsum(padded)
    start_col = (pad_end - padded).astype(F32)[:, None]
    end_blocks = pad_end // EXPERT_BLOCK
    block_exp = jnp.minimum(jnp.sum(end_blocks[None, :] <= jnp.arange(n_blocks)[:, None], axis=1),
                            N_EXPERTS - 1).astype(I32)
    return start_col, block_exp, end_blocks[-1:].astype(I32)


def _ffn_out_body(dest_ref, dest_next_ref, x_ref, gw_ref, yb_ref, wgu_ref, wdn_ref, g_ref, beta_ref, o_ref,
                  ybuf, sems, *, alpha):
    i = pl.program_id(0)
    n_steps = pl.num_programs(0)
    tm = x_ref.shape[0]
    slot = i % 2

    def gather(dests, to_slot):
        def issue(group, carry):
            base = pl.multiple_of(group * SUBLANES, SUBLANES)
            for r in range(SUBLANES):
                for k in range(EXPERT_TOPK):
                    pltpu.make_async_copy(yb_ref.at[pl.ds(dests[0, k * tm + base + r], 1)],
                                          ybuf.at[to_slot, k, pl.ds(base + r, 1)],
                                          sems.at[to_slot]).start(priority=k % 2)
            return carry
        lax.fori_loop(0, tm // SUBLANES, issue, 0)

    @pl.when(i == 0)
    def _():
        gather(dest_ref, 0)

    for parity in range(2):
        @pl.when((i + 1 < n_steps) & (slot == parity))
        def _(parity=parity):
            gather(dest_next_ref, 1 - parity)

    x = x_ref[...]
    h = jnp.dot(x.astype(BF16), wgu_ref[...], preferred_element_type=F32)
    ff = h.shape[1] // 2
    act = jax.nn.silu(h[:, :ff]) * h[:, ff:]
    shared = jnp.dot(act.astype(BF16), wdn_ref[...], preferred_element_type=F32)

    for k in range(EXPERT_TOPK):
        pltpu.make_async_copy(yb_ref.at[pl.ds(0, tm)], ybuf.at[slot, k], sems.at[slot]).wait()
    routed = None
    for k in range(EXPERT_TOPK):
        term = ybuf[slot, k] * gw_ref[:, k:k + 1]
        routed = term if routed is None else routed + term
    o_ref[...] = _layer_norm(alpha * x + (routed + shared), g_ref[...], beta_ref[...])


def _ffn_out(x, yb, dest_t, gw, ws_gu, ws_dn, g, beta, alpha, tm=256):
    n, d = x.shape
    n_tiles = n // tm
    row = pl.BlockSpec((tm, d), lambda i: (i, 0))
    full = lambda arr: pl.BlockSpec(arr.shape, lambda i: (0,) * arr.ndim)
    dest_spec = lambda index: pl.BlockSpec((None, 1, EXPERT_TOPK * tm), index, memory_space=pltpu.SMEM)
    dest_tiles = _tile_major(dest_t, tm)
    return pl.pallas_call(
        functools.partial(_ffn_out_body, alpha=alpha),
        out_shape=jax.ShapeDtypeStruct((n, d), F32),
        grid=(n_tiles,),
        in_specs=[dest_spec(lambda i: (i, 0, 0)), dest_spec(lambda i: (jnp.minimum(i + 1, n_tiles - 1), 0, 0)),
                  row, pl.BlockSpec((tm, EXPERT_TOPK), lambda i: (i, 0)),
                  pl.BlockSpec(memory_space=pl.ANY),
                  full(ws_gu), full(ws_dn), full(g), full(beta)],
        out_specs=row,
        scratch_shapes=[pltpu.VMEM((2, EXPERT_TOPK, tm, d), F32), pltpu.SemaphoreType.DMA((2,))],
        compiler_params=_params(("arbitrary",)),
        name="ffn_out",
    )(dest_tiles, dest_tiles, x, gw, yb, ws_gu, ws_dn, g, beta)


def _moe(x, xw, layer, w_router_t, b_router_col, w_exp_gate_up, w_exp_down, ws_gu, ws_dn, g, beta, alpha):
    n = x.shape[0]
    n_blocks = -(-n * EXPERT_TOPK // EXPERT_BLOCK) + N_EXPERTS
    eidx_t, gw_t, rank_t, counts = _router(x, w_router_t, b_router_col)
    start_col, block_exp, n_used = _block_plan(counts, n_blocks)
    dest_t = _dest_rows(eidx_t, rank_t, start_col)
    x_rows = _scatter_rows(xw, dest_t, n_blocks * EXPERT_BLOCK)
    yb = _experts(x_rows, block_exp, n_used, w_exp_gate_up, w_exp_down, layer)
    return _ffn_out(x, yb, dest_t, gw_t.T, ws_gu, ws_dn, g, beta, alpha)


def kernel(x_prompt, x_sample, mem_prompt, cache_a_k, cache_a_v, cache_b_k, cache_b_v, cache_b_idx_k,
           cache_mem_k, cache_mem_v, w_in, rel_bias, w_mem_kv, w_branch, w_out, ln_mix_g, ln_mix_b,
           w_router, b_router, w_exp_gate_up, w_exp_down, w_sh_gate_up, w_sh_down, ln_ffn_g, ln_ffn_b):
    n_batch, seq, d = x_prompt.shape
    s_batch, s_seq, _ = x_sample.shape
    depth = w_in.shape[0]
    past = cache_b_k.shape[2]
    assert s_seq == CHUNK and seq % CHUNK == 0 and cache_a_k.shape[2] == BAND and seq >= BAND
    alpha = (2 * depth) ** 0.25
    n_p, n_s = n_batch * seq, s_batch * s_seq
    n_tot = n_p + n_s
    tm = 512
    assert seq % tm == 0 and n_p % tm == 0 and n_s % tm == 0 and tm % CHUNK == 0

    x = jnp.concatenate([x_prompt.reshape(n_p, d), x_sample.reshape(n_s, d)])

    pos = jnp.concatenate([jnp.arange(seq), jnp.tile(past + jnp.arange(s_seq), tm // s_seq)])
    tab_heads = _rope_tables(pos, LANES)
    tab_kiw = _rope_tables(pos, HEAD_DIM)
    p_tiles, seq_tiles = n_p // tm, seq // tm
    table_index = lambda i: jnp.where(i < p_tiles, i % seq_tiles, seq_tiles)

    wd = WIDTH
    c_qi, c_ki, c_wi, c_qm, c_gates = 6 * wd, 7 * wd, 7 * wd + HEAD_DIM, 7 * wd + HEAD_DIM + N_HEADS, 8 * wd + HEAD_DIM + N_HEADS
    cache_a_k = cache_a_k.reshape(depth, s_batch, BAND, wd)
    cache_a_v = cache_a_v.reshape(depth, s_batch, BAND, wd)
    cache_b_k = cache_b_k.reshape(depth, s_batch, past, wd)
    cache_b_v = cache_b_v.reshape(depth, s_batch, past, wd)
    cache_mem_k = cache_mem_k.reshape(depth, s_batch, N_MEM, wd)
    cache_mem_v = cache_mem_v.reshape(depth, s_batch, N_MEM, wd)
    mem_rows = mem_prompt.reshape(n_batch * N_MEM, d)

    outs = {k: [] for k in ("p_ak", "p_av", "p_bk", "p_bv", "p_ik", "p_mk", "p_mv",
                            "s_ak", "s_av", "s_bk", "s_bv", "s_ik")}
    for l in range(depth):
        w = w_in[l]
        w_a = w[:, :3 * wd].astype(BF16)
        w_b = w[:, 3 * wd:7 * wd].astype(BF16)
        w_c = w[:, c_qm:].astype(BF16)
        w_d = jnp.concatenate([w[:, c_ki:c_qm], jnp.zeros((d, LANES - HEAD_DIM - N_HEADS), F32)], 1).astype(BF16)

        qa, ka, va = _project(x, w_a, (wd, wd, wd), split_flags=(False, True, True), n_prompt=n_p,
                              tm=tm, name="proj_a")
        qb, kb, vb, qi = _project(x, w_b, (wd, wd, wd, wd), (True, True, False, True), tab_heads, table_index,
                                  split_flags=(False, True, True, False), n_prompt=n_p, tm=tm, name="proj_b")
        qm, gates = _project(x, w_c, (wd, 3 * d), tm=tm, name="proj_c")
        (kiw,) = _project(x, w_d, (LANES,), (True,), tab_kiw, table_index, split_flags=(True,), n_prompt=n_p,
                          tm=tm, name="proj_d")
        mk, mv = _project(mem_rows, w_mem_kv[l].astype(BF16), (wd, wd), tm=N_MEM, name="proj_mem")

        a = (_band_prompt(qa, ka[0], va[0], rel_bias[l], n_batch, seq),
             _band_sample(qa, ka[1], va[1], cache_a_k, cache_a_v, l, _band_bias(rel_bias[l], CHUNK), s_batch, n_p))
        b = (_dsa_prompt(qb, kb[0], vb[0], qi, kiw[0], n_batch, seq),
             _dsa_sample(qb, kb[1], vb[1], qi, kiw[1], cache_b_k, cache_b_v, cache_b_idx_k, l, s_batch, n_p))
        m = (_mem_attn(qm, mk, mv, lambda bb: (bb, 0), n_batch, seq, 0, tm),
             _mem_attn(qm, cache_mem_k, cache_mem_v, lambda bb, l=l: (l, bb, 0, 0), s_batch, s_seq, n_p, s_seq))

        x1, x1w = _merge(x, (a, b, m), gates, w_branch[l].astype(BF16), w_out[l].astype(BF16),
                         ln_mix_g[l][None], ln_mix_b[l][None], alpha)
        x = _moe(x1, x1w, l, w_router[l].T.astype(BF16), b_router[l][:, None], w_exp_gate_up, w_exp_down,
                 w_sh_gate_up[l].astype(BF16), w_sh_down[l].astype(BF16),
                 ln_ffn_g[l][None], ln_ffn_b[l][None], alpha)

        outs["p_ak"].append(ka[0].reshape(n_batch, seq, wd)[:, -BAND:])
        outs["p_av"].append(va[0].reshape(n_batch, seq, wd)[:, -BAND:])
        outs["p_bk"].append(kb[0].reshape(n_batch, seq, wd))
        outs["p_bv"].append(vb[0].reshape(n_batch, seq, wd))
        outs["p_ik"].append(kiw[0][:, :HEAD_DIM].reshape(n_batch, seq, HEAD_DIM))
        outs["p_mk"].append(mk.reshape(n_batch, N_MEM, wd))
        outs["p_mv"].append(mv.reshape(n_batch, N_MEM, wd))
        ka_s = ka[1].reshape(s_batch, s_seq, wd)
        va_s = va[1].reshape(s_batch, s_seq, wd)
        outs["s_ak"].append(jnp.concatenate([cache_a_k[l], ka_s], 1)[:, -BAND:])
        outs["s_av"].append(jnp.concatenate([cache_a_v[l], va_s], 1)[:, -BAND:])
        outs["s_bk"].append(kb[1].reshape(s_batch, s_seq, wd))
        outs["s_bv"].append(vb[1].reshape(s_batch, s_seq, wd))
        outs["s_ik"].append(kiw[1][:, :HEAD_DIM].reshape(s_batch, s_seq, HEAD_DIM))

    heads = lambda t: t.reshape(t.shape[:-1] + (N_HEADS, HEAD_DIM))
    mheads = lambda t: t.reshape(t.shape[:-1] + (M_HEADS, M_HEAD_DIM))
    st = lambda key: jnp.stack(outs[key])
    return (x[:n_p].reshape(n_batch, seq, d), x[n_p:].reshape(s_batch, s_seq, d),
            heads(st("p_ak")), heads(st("p_av")), heads(st("p_bk")), heads(st("p_bv")), st("p_ik"),
            mheads(st("p_mk")), mheads(st("p_mv")),
            heads(st("s_ak")), heads(st("s_av")), heads(st("s_bk")), heads(st("s_bv")), st("s_ik"))
```

```python
import functools

import jax
import jax.numpy as jnp
from jax import lax
from jax.experimental import pallas as pl
from jax.experimental.pallas import tpu as pltpu

F32 = jnp.float32
BF16 = jnp.bfloat16
I32 = jnp.int32

CHUNK = 64
HEAD_DIM = 64
N_HEADS = 8
WIDTH = N_HEADS * HEAD_DIM
BAND = 8 * CHUNK
MAX_REL = 128
TOPK_KEYS = 256
M_HEADS = 4
M_HEAD_DIM = 128
N_MEM = 256
ROPE_THETA = 10000.0
N_EXPERTS = 256
EXPERT_TOPK = 8
N_GROUPS = 8
TOPK_GROUPS = 4
ROUTE_SCALE = 2.5
EXPERT_BLOCK = 256
LN_EPS = 1e-5

LANES = 128
SUBLANES = 8
NEG_BIG = -1e30
INT_MIN = -(2 ** 31)
VMEM_LIMIT = 48 * 1024 * 1024

_NT_DIMS = (((1,), (1,)), ((), ()))


def _params(semantics):
    return pltpu.CompilerParams(dimension_semantics=semantics, vmem_limit_bytes=VMEM_LIMIT)


def _proj_body(*refs, widths, rope_flags, split_flags, prompt_tiles):
    has_rope = any(rope_flags)
    x_ref, w_ref = refs[0], refs[1]
    if has_rope:
        cos_ref, sin_lo_ref, sin_hi_ref = refs[2:5]
        outs = list(refs[5:])
    else:
        outs = list(refs[2:])
    in_prompt = pl.program_id(0) < prompt_tiles
    xb = x_ref[...].astype(BF16)
    off = 0
    for g, width in enumerate(widths):
        y = jnp.dot(xb, w_ref[:, off:off + width], preferred_element_type=F32)
        if rope_flags[g]:
            cos, sin_lo, sin_hi = cos_ref[...], sin_lo_ref[...], sin_hi_ref[...]
            chunks = [yk * cos + pltpu.roll(yk, LANES - 32, 1) * sin_lo + pltpu.roll(yk, 32, 1) * sin_hi
                      for yk in (y[:, k * LANES:(k + 1) * LANES] for k in range(width // LANES))]
        else:
            chunks = [y]

        def store(ref, chunks=chunks):
            step = chunks[0].shape[1]
            for k, chunk in enumerate(chunks):
                ref[:, k * step:(k + 1) * step] = chunk

        if split_flags[g]:
            prompt_ref, sample_ref = outs.pop(0), outs.pop(0)
            pl.when(in_prompt)(functools.partial(store, prompt_ref))
            pl.when(jnp.logical_not(in_prompt))(functools.partial(store, sample_ref))
        else:
            store(outs.pop(0))
        off += width


def _project(x, w, widths, rope_flags=None, tables=None, table_index=None, split_flags=None, n_prompt=None,
             tm=512, name="proj"):
    n, d = x.shape
    rope_flags = rope_flags or (False,) * len(widths)
    split_flags = split_flags or (False,) * len(widths)
    assert n % tm == 0 and w.shape == (d, sum(widths))
    prompt_tiles = n // tm
    if any(split_flags):
        assert n_prompt % tm == 0 and 0 < n_prompt < n
        prompt_tiles = n_prompt // tm
    in_specs = [pl.BlockSpec((tm, d), lambda i: (i, 0)),
                pl.BlockSpec(w.shape, lambda i: (0, 0))]
    args = [x, w]
    if any(rope_flags):
        for t in tables:
            in_specs.append(pl.BlockSpec((tm, LANES), lambda i: (table_index(i), 0)))
            args.append(t)
    out_shape, out_specs = [], []
    for wd, split in zip(widths, split_flags):
        if split:
            out_shape += [jax.ShapeDtypeStruct((n_prompt, wd), F32), jax.ShapeDtypeStruct((n - n_prompt, wd), F32)]
            out_specs += [pl.BlockSpec((tm, wd), lambda i: (jnp.minimum(i, prompt_tiles - 1), 0)),
                          pl.BlockSpec((tm, wd), lambda i: (jnp.maximum(i - prompt_tiles, 0), 0))]
        else:
            out_shape.append(jax.ShapeDtypeStruct((n, wd), F32))
            out_specs.append(pl.BlockSpec((tm, wd), lambda i: (i, 0)))
    flat = pl.pallas_call(
        functools.partial(_proj_body, widths=tuple(widths), rope_flags=tuple(rope_flags),
                          split_flags=tuple(split_flags), prompt_tiles=prompt_tiles),
        out_shape=out_shape,
        grid=(n // tm,),
        in_specs=in_specs,
        out_specs=out_specs,
        compiler_params=_params(("arbitrary",)),
        name=name,
    )(*args)
    flat = list(flat)
    return [(flat.pop(0), flat.pop(0)) if split else flat.pop(0) for split in split_flags]


def _rope_tables(pos, rope_lanes):
    half = HEAD_DIM // 2
    inv_freq = ROPE_THETA ** (-jnp.arange(half, dtype=F32) / half)
    ang = pos.astype(F32)[:, None] * inv_freq[None, :]
    cos, sin = jnp.cos(ang), jnp.sin(ang)
    reps = LANES // HEAD_DIM
    zero = jnp.zeros_like(sin)
    cos_t = jnp.tile(jnp.concatenate([cos, cos], -1), (1, reps))
    sin_lo = jnp.tile(jnp.concatenate([-sin, zero], -1), (1, reps))
    sin_hi = jnp.tile(jnp.concatenate([zero, sin], -1), (1, reps))
    keep = (jnp.arange(LANES) < rope_lanes)[None, :]
    return (jnp.where(keep, cos_t, 1.0), jnp.where(keep, sin_lo, 0.0), jnp.where(keep, sin_hi, 0.0))


def _reduce_rows(x, op, keep_groups=False):
    rows, n = x.shape
    groups = next(g for g in (8, 4, 2, 1) if rows % (8 * g) == 0 and rows // (8 * g) >= 4 or g == 1)
    parts = op(x.reshape(groups, rows // (8 * groups), 8, n), axis=1)
    if keep_groups:
        return parts
    return op(op(parts, axis=0), axis=0, keepdims=True)


def _head_mask(rows, head_in_pair):
    lane = lax.broadcasted_iota(I32, (rows, LANES), 1)
    lo = head_in_pair * HEAD_DIM
    return (lane >= lo) & (lane < lo + HEAD_DIM)


def _pair_merge(o_even, o_odd):
    lane = lax.broadcasted_iota(I32, o_even.shape, 1)
    return jnp.where(lane < HEAD_DIM, o_even, o_odd)


def _band_sample_body(q_ref, kn_ref, vn_ref, kh_ref, vh_ref, bias_ref, o_ref, kwin, vwin):
    kwin[0:BAND, :] = kh_ref[...].astype(BF16)
    vwin[0:BAND, :] = vh_ref[...].astype(BF16)
    kwin[BAND:, :] = kn_ref[...].astype(BF16)
    vwin[BAND:, :] = vn_ref[...].astype(BF16)
    q = q_ref[...] * (HEAD_DIM ** -0.5)
    for hp in range(N_HEADS // 2):
        cols = slice(hp * LANES, (hp + 1) * LANES)
        outs = []
        for hh in range(2):
            qm = jnp.where(_head_mask(CHUNK, hh), q[:, cols], 0.0).astype(BF16)
            s = lax.dot_general(qm, kwin[:, cols], _NT_DIMS, preferred_element_type=F32) + bias_ref[2 * hp + hh]
            p = jnp.exp(s - jnp.max(s, axis=1, keepdims=True))
            denom = jnp.sum(p, axis=1, keepdims=True)
            outs.append(jnp.dot(p.astype(BF16), vwin[:, cols], preferred_element_type=F32) / denom)
        o_ref[:, cols] = _pair_merge(outs[0], outs[1])


def _band_prompt_body(q_ref, k_ref, v_ref, bias_ref, o_ref, k_even, k_odd, v_t, q_t, s_sc, p_sc, o_t):
    t = pl.program_id(1)
    nq = q_ref.shape[0]
    win = BAND + nq
    seq = k_ref.shape[0]
    stage_rows = 256

    @pl.when(t == 0)
    def _():
        k_even[0:BAND, :] = jnp.zeros((BAND, WIDTH), BF16)
        k_odd[0:BAND, :] = jnp.zeros((BAND, WIDTH), BF16)
        v_t[:, 0:BAND] = jnp.zeros((WIDTH, BAND), BF16)
        for r in range(0, seq, stage_rows):
            ke, ko = _split_pair_lanes(k_ref[r:r + stage_rows, :])
            k_even[BAND + r:BAND + r + stage_rows, :] = ke.astype(BF16)
            k_odd[BAND + r:BAND + r + stage_rows, :] = ko.astype(BF16)
            v_t[:, BAND + r:BAND + r + stage_rows] = v_ref[r:r + stage_rows, :].T.astype(BF16)

    start = pl.multiple_of(t * nq, nq)
    window = pl.ds(start, win)
    q_t[...] = (q_ref[...] * (HEAD_DIM ** -0.5)).T.astype(BF16)
    before_start = jnp.where(lax.broadcasted_iota(I32, (win, nq), 0) >= BAND - t * nq, 0.0, NEG_BIG)
    for h in range(N_HEADS):
        pair = slice((h // 2) * LANES, (h // 2 + 1) * LANES)
        k_ref_h = k_even if h % 2 == 0 else k_odd
        s_sc[h * win:(h + 1) * win, :] = (jnp.dot(k_ref_h[window, pair], q_t[pair, :], preferred_element_type=F32)
                                          + bias_ref[h] + before_start)
    for h in range(N_HEADS):
        s = s_sc[h * win:(h + 1) * win, :]
        p = jnp.exp(s - _reduce_rows(s, jnp.max))
        p_sc[h * win:(h + 1) * win, :] = (p / _reduce_rows(p, jnp.sum)).astype(BF16)
    for h in range(N_HEADS):
        dims = slice(h * HEAD_DIM, (h + 1) * HEAD_DIM)
        o_t[dims, :] = jnp.dot(v_t[dims, window], p_sc[h * win:(h + 1) * win, :], preferred_element_type=F32)
    o_ref[...] = o_t[...].T


def _band_bias(table, n_rows):
    win = BAND + n_rows
    assert n_rows <= MAX_REL <= BAND
    n_heads = table.shape[0]
    g = jnp.concatenate([jnp.broadcast_to(table[:, 2 * MAX_REL:], (n_heads, win - MAX_REL)),
                         table[:, MAX_REL - n_rows:2 * MAX_REL][:, ::-1],
                         jnp.zeros((n_heads, 1), table.dtype)], axis=1).astype(F32)
    period = g.shape[1]
    skew = jnp.tile(g, (1, n_rows))[:, :n_rows * (period - 1)].reshape(n_heads, n_rows, period - 1)
    return skew[:, :, n_rows - 1:n_rows - 1 + win]


def _band_prompt(q, k, v, table, n_batch, seq, q_rows=128):
    nt = seq // q_rows
    win = BAND + q_rows
    assert seq % q_rows == 0 and q_rows % CHUNK == 0
    key_chunk = jnp.arange(win)[:, None] // CHUNK - BAND // CHUNK
    query_chunk = jnp.arange(q_rows)[None, :] // CHUNK
    visible = (key_chunk <= query_chunk) & (key_chunk >= query_chunk - BAND // CHUNK)
    bias = jnp.where(visible[None], jnp.swapaxes(_band_bias(table, q_rows), 1, 2), NEG_BIG)
    tile_spec = pl.BlockSpec((q_rows, WIDTH), lambda b, t: (b * nt + t, 0))
    seq_spec = pl.BlockSpec((seq, WIDTH), lambda b, t: (b, 0))
    return pl.pallas_call(
        _band_prompt_body,
        out_shape=jax.ShapeDtypeStruct((n_batch * seq, WIDTH), F32),
        grid=(n_batch, nt),
        in_specs=[tile_spec, seq_spec, seq_spec, pl.BlockSpec(bias.shape, lambda b, t: (0, 0, 0))],
        out_specs=tile_spec,
        scratch_shapes=[pltpu.VMEM((BAND + seq, WIDTH), BF16), pltpu.VMEM((BAND + seq, WIDTH), BF16),
                        pltpu.VMEM((WIDTH, BAND + seq), BF16), pltpu.VMEM((WIDTH, q_rows), BF16),
                        pltpu.VMEM((N_HEADS * win, q_rows), F32), pltpu.VMEM((N_HEADS * win, q_rows), BF16),
                        pltpu.VMEM((WIDTH, q_rows), F32)],
        compiler_params=_params(("parallel", "arbitrary")),
        name="band_prompt",
    )(q, k, v, bias)


def _band_sample(q, k, v, k_hist, v_hist, layer, bias, n_batch, row0):
    blk0 = row0 // CHUNK
    hist_spec = pl.BlockSpec((None, None, BAND, WIDTH), lambda b: (layer, b, 0, 0))
    new_spec = pl.BlockSpec((CHUNK, WIDTH), lambda b: (b, 0))
    return pl.pallas_call(
        _band_sample_body,
        out_shape=jax.ShapeDtypeStruct((n_batch * CHUNK, WIDTH), F32),
        grid=(n_batch,),
        in_specs=[pl.BlockSpec((CHUNK, WIDTH), lambda b: (blk0 + b, 0)), new_spec, new_spec, hist_spec, hist_spec,
                  pl.BlockSpec(bias.shape, lambda b: (0, 0, 0))],
        out_specs=new_spec,
        scratch_shapes=[pltpu.VMEM((BAND + CHUNK, WIDTH), BF16), pltpu.VMEM((BAND + CHUNK, WIDTH), BF16)],
        compiler_params=_params(("parallel",)),
        name="band_sample",
    )(q, k, v, k_hist, v_hist, bias)


def _mem_body(q_ref, k_ref, v_ref, o_ref):
    scale = M_HEAD_DIM ** -0.5
    for h in range(M_HEADS):
        cols = slice(h * M_HEAD_DIM, (h + 1) * M_HEAD_DIM)
        s = lax.dot_general(q_ref[:, cols].astype(BF16), k_ref[:, cols].astype(BF16), _NT_DIMS,
                            preferred_element_type=F32) * scale
        p = jnp.exp(s - jnp.max(s, axis=1, keepdims=True))
        denom = jnp.sum(p, axis=1, keepdims=True)
        o = jnp.dot(p.astype(BF16), v_ref[:, cols].astype(BF16), preferred_element_type=F32)
        o_ref[:, cols] = o / denom


def _mem_attn(q, mk, mv, kv_index, n_batch, rows_per_batch, row0, tq):
    width = M_HEADS * M_HEAD_DIM
    per = rows_per_batch // tq
    blk0 = row0 // tq
    kv_block = (None,) * (mk.ndim - 2) + (N_MEM, width)
    kv_spec = pl.BlockSpec(kv_block, lambda b, t: kv_index(b))
    return pl.pallas_call(
        _mem_body,
        out_shape=jax.ShapeDtypeStruct((n_batch * rows_per_batch, width), F32),
        grid=(n_batch, per),
        in_specs=[pl.BlockSpec((tq, width), lambda b, t: (blk0 + b * per + t, 0)), kv_spec, kv_spec],
        out_specs=pl.BlockSpec((tq, width), lambda b, t: (b * per + t, 0)),
        compiler_params=_params(("parallel", "parallel")),
        name="mem_attn",
    )(q, mk, mv)


def _split_pair_lanes(x):
    lane = lax.broadcasted_iota(I32, x.shape, 1)
    even = (lane & (LANES - 1)) < HEAD_DIM
    return jnp.where(even, x, 0.0), jnp.where(even, 0.0, x)


def _dsa_body(*refs, sample, key_block, n_keys, n_sel):
    n_in = 8 if sample else 6
    if sample:
        q_ref, qi_ref, kiwq_ref, kn_ref, vn_ref, kc_ref, vc_ref, kic_ref = refs[:n_in]
    else:
        q_ref, qi_ref, kiwq_ref, k_ref, v_ref, kiw_ref = refs[:n_in]
    o_ref = refs[n_in]
    (k_even, k_odd, v_t, ki_lo, ki_hi, keybuf, maskbuf, q_t, qi_t, kiwq_t, m_sc, l_sc, a_sc, acc_t,
     s_sc, p_sc) = refs[n_in + 1:]
    kb = key_block
    n_pad = k_even.shape[0]
    nq = q_t.shape[1]
    widen = lambda rows: jnp.concatenate([rows] * (nq // rows.shape[0]), axis=0)
    stage_rows = 256

    def stage_kv(row0, k_new, v_new):
        n_new = k_new.shape[0]
        ke, ko = _split_pair_lanes(k_new)
        k_even[row0:row0 + n_new, :] = ke.astype(BF16)
        k_odd[row0:row0 + n_new, :] = ko.astype(BF16)
        v_t[:, row0:row0 + n_new] = v_new.T.astype(BF16)

    def stage_index_keys(row0, kiw):
        lane = lax.broadcasted_iota(I32, kiw.shape, 1)
        ki_lo[row0:row0 + kiw.shape[0], :] = jnp.where(lane < HEAD_DIM, kiw, 0.0).astype(BF16)
        ki_hi[row0:row0 + kiw.shape[0], :] = jnp.where(lane < HEAD_DIM, 0.0, pltpu.roll(kiw, HEAD_DIM, 1)).astype(BF16)

    def stage_keys():
        if sample:
            n_hist = kc_ref.shape[0]
            for r in range(0, n_hist, stage_rows):
                stage_kv(r, kc_ref[r:r + stage_rows, :], vc_ref[r:r + stage_rows, :])
            kic = kic_ref[...]
            zeros = jnp.zeros_like(kic)
            ki_lo[0:n_hist, :] = jnp.concatenate([kic, zeros], axis=1).astype(BF16)
            ki_hi[0:n_hist, :] = jnp.concatenate([zeros, kic], axis=1).astype(BF16)
            stage_kv(n_hist, kn_ref[...], vn_ref[...])
            stage_index_keys(n_hist, kiwq_ref[...])
            if n_pad > n_keys:
                for ref in (k_even, k_odd, ki_lo, ki_hi):
                    ref[n_keys:, :] = jnp.zeros((n_pad - n_keys, ref.shape[1]), BF16)
                v_t[:, n_keys:] = jnp.zeros((v_t.shape[0], n_pad - n_keys), BF16)
        else:
            for r in range(0, n_keys, stage_rows):
                stage_kv(r, k_ref[r:r + stage_rows, :], v_ref[r:r + stage_rows, :])
            stage_index_keys(0, kiw_ref[...])

    if sample:
        stage_keys()
        limit = n_keys
        nb = n_pad // kb
        search_bits = 32
    else:
        t = pl.program_id(1)
        pl.when(t == 0)(stage_keys)
        chunk = t * (nq // CHUNK) + lax.broadcasted_iota(I32, (1, nq), 1) // CHUNK
        limit = (chunk + 1) * CHUNK
        limit_max = (t + 1) * nq
        nb = (limit_max + kb - 1) // kb
        search_bits = jnp.where(limit_max <= n_sel, 0, 32)

    def block_rows(j):
        return pl.ds(pl.multiple_of(j * kb, kb), kb)

    def key_pos(j):
        return j * kb + lax.broadcasted_iota(I32, (kb, nq), 0)

    q_t[...] = widen(q_ref[...] * (HEAD_DIM ** -0.5)).T.astype(BF16)
    qi_t[...] = widen(qi_ref[...]).T.astype(BF16)
    kiwq_t[...] = widen(kiwq_ref[...]).T

    def score_block(j, carry):
        acc = jnp.zeros((kb, nq), F32)
        for h in range(N_HEADS):
            pair = slice((h // 2) * LANES, (h // 2 + 1) * LANES)
            ki_ref = ki_lo if h % 2 == 0 else ki_hi
            d = jnp.dot(ki_ref[block_rows(j), :], qi_t[pair, :], preferred_element_type=F32)
            acc = acc + kiwq_t[HEAD_DIM + h:HEAD_DIM + h + 1, :] * jnp.maximum(d, 0.0)
        acc = acc + 0.0
        acc = jnp.where(key_pos(j) < limit, acc, -jnp.inf)
        bits = lax.bitcast_convert_type(acc, I32)
        keybuf[block_rows(j), :] = bits ^ ((bits >> 31) & 0x7FFFFFFF)
        return carry

    lax.fori_loop(0, nb, score_block, 0)

    def count(pred_fn):
        def body(j, part):
            hit = jnp.where(pred_fn(keybuf[block_rows(j), :], j), 1.0, 0.0)
            return part + _reduce_rows(hit, jnp.sum, keep_groups=True)
        groups = jax.eval_shape(functools.partial(_reduce_rows, op=jnp.sum, keep_groups=True),
                                jax.ShapeDtypeStruct((kb, nq), F32)).shape[0]
        part = lax.fori_loop(0, nb, body, jnp.zeros((groups, 8, nq), F32))
        return jnp.sum(jnp.sum(part, axis=0), axis=0, keepdims=True)

    def bit_step(i, prefix):
        cand = prefix | lax.shift_left(jnp.int32(1), 31 - i)
        thr_i = cand ^ INT_MIN
        cnt = count(lambda kblk, j: kblk >= thr_i)
        return jnp.where(cnt >= n_sel, cand, prefix)

    prefix = lax.fori_loop(0, search_bits, bit_step, jnp.zeros((1, nq), I32))
    thr = prefix ^ INT_MIN
    above = count(lambda kblk, j: kblk > thr)
    at_least = count(lambda kblk, j: kblk >= thr)
    need = n_sel - above
    surplus = at_least > n_sel

    pos_bits = max(1, (n_pad - 1).bit_length() + 1)
    any_surplus = jnp.max(jnp.where(surplus, 1.0, 0.0)) > 0.0

    def pos_step(i, bound):
        cand = bound | lax.shift_left(jnp.int32(1), pos_bits - 1 - i)
        cnt = count(lambda kblk, j: (kblk == thr) & (key_pos(j) < cand))
        return jnp.where(cnt <= need, cand, bound)

    bound = lax.fori_loop(0, jnp.where(any_surplus, pos_bits, 0), pos_step, jnp.zeros((1, nq), I32))
    bound = jnp.where(surplus, bound, 2 ** pos_bits - 1)

    def mask_block(j, carry):
        kblk = keybuf[block_rows(j), :]
        pos = key_pos(j)
        sel = ((kblk > thr) | ((kblk == thr) & (pos < bound))) & (pos < limit)
        maskbuf[block_rows(j), :] = jnp.where(sel, 0.0, NEG_BIG)
        return carry

    lax.fori_loop(0, nb, mask_block, 0)

    m_sc[...] = jnp.full(m_sc.shape, NEG_BIG, F32)
    l_sc[...] = jnp.zeros(l_sc.shape, F32)
    acc_t[...] = jnp.zeros(acc_t.shape, F32)

    def attend(j, carry):
        mask = maskbuf[block_rows(j), :]
        for h in range(N_HEADS):
            pair = slice((h // 2) * LANES, (h // 2 + 1) * LANES)
            k_ref_h = k_even if h % 2 == 0 else k_odd
            s_sc[h * kb:(h + 1) * kb, :] = jnp.dot(k_ref_h[block_rows(j), pair], q_t[pair, :],
                                                   preferred_element_type=F32) + mask
        for h in range(N_HEADS):
            stat = slice(h * 8, (h + 1) * 8)
            s = s_sc[h * kb:(h + 1) * kb, :]
            m_prev = m_sc[stat, :]
            m_new = jnp.maximum(m_prev, _reduce_rows(s, jnp.max))
            alpha = jnp.exp(m_prev - m_new)
            p = jnp.exp(s - m_new[0:1, :])
            l_sc[stat, :] = alpha * l_sc[stat, :] + _reduce_rows(p, jnp.sum)
            m_sc[stat, :] = m_new
            a_sc[stat, :] = alpha
            p_sc[h * kb:(h + 1) * kb, :] = p.astype(BF16)
        for h in range(N_HEADS):
            dims = slice(h * HEAD_DIM, (h + 1) * HEAD_DIM)
            pv = jnp.dot(v_t[dims, block_rows(j)], p_sc[h * kb:(h + 1) * kb, :], preferred_element_type=F32)
            acc_t[dims, :] = a_sc[h * 8:h * 8 + 1, :] * acc_t[dims, :] + pv
        return carry

    lax.fori_loop(0, nb, attend, 0)
    for h in range(N_HEADS):
        dims = slice(h * HEAD_DIM, (h + 1) * HEAD_DIM)
        acc_t[dims, :] = acc_t[dims, :] / l_sc[h * 8:h * 8 + 1, :]
    o_ref[...] = acc_t[...].T[0:o_ref.shape[0], :]


def _dsa_scratch(nq, n_pad, key_block):
    nq = max(nq, LANES)
    return [pltpu.VMEM((n_pad, WIDTH), BF16), pltpu.VMEM((n_pad, WIDTH), BF16),
            pltpu.VMEM((WIDTH, n_pad), BF16),
            pltpu.VMEM((n_pad, LANES), BF16), pltpu.VMEM((n_pad, LANES), BF16),
            pltpu.VMEM((n_pad, nq), I32), pltpu.VMEM((n_pad, nq), F32),
            pltpu.VMEM((WIDTH, nq), BF16), pltpu.VMEM((WIDTH, nq), BF16),
            pltpu.VMEM((LANES, nq), F32),
            pltpu.VMEM((N_HEADS * 8, nq), F32), pltpu.VMEM((N_HEADS * 8, nq), F32),
            pltpu.VMEM((N_HEADS * 8, nq), F32),
            pltpu.VMEM((WIDTH, nq), F32),
            pltpu.VMEM((N_HEADS * key_block, nq), F32),
            pltpu.VMEM((N_HEADS * key_block, nq), BF16)]


def _dsa_prompt(q, k, v, qi, kiw, n_batch, seq, key_block=256, q_rows=256):
    nt = seq // q_rows
    key_block = min(key_block, seq)
    assert seq % key_block == 0 and seq % q_rows == 0 and q_rows % CHUNK == 0
    n_sel = min(TOPK_KEYS, seq // 4)
    tile_spec = lambda width: pl.BlockSpec((q_rows, width), lambda b, t: (b * nt + t, 0))
    seq_spec = lambda width: pl.BlockSpec((seq, width), lambda b, t: (b, 0))
    return pl.pallas_call(
        functools.partial(_dsa_body, sample=False, key_block=key_block, n_keys=seq, n_sel=n_sel),
        out_shape=jax.ShapeDtypeStruct((n_batch * seq, WIDTH), F32),
        grid=(n_batch, nt),
        in_specs=[tile_spec(WIDTH), tile_spec(WIDTH), tile_spec(LANES),
                  seq_spec(WIDTH), seq_spec(WIDTH), seq_spec(LANES)],
        out_specs=tile_spec(WIDTH),
        scratch_shapes=_dsa_scratch(q_rows, seq, key_block),
        compiler_params=_params(("parallel", "arbitrary")),
        name="dsa_prompt",
    )(q, qi, kiw, k, v, kiw)


def _dsa_sample(q, k, v, qi, kiw, k_cache, v_cache, ki_cache, layer, n_batch, row0, key_block=768):
    past = k_cache.shape[2]
    n_keys = past + CHUNK
    key_block = min(key_block, -(-n_keys // 256) * 256)
    n_pad = -(-n_keys // key_block) * key_block
    n_sel = min(TOPK_KEYS, n_keys // 4)
    blk0 = row0 // CHUNK
    query_spec = pl.BlockSpec((CHUNK, WIDTH), lambda b: (blk0 + b, 0))
    new_spec = lambda width: pl.BlockSpec((CHUNK, width), lambda b: (b, 0))
    cache_spec = lambda width: pl.BlockSpec((None, None, past, width), lambda b: (layer, b, 0, 0))
    return pl.pallas_call(
        functools.partial(_dsa_body, sample=True, key_block=key_block, n_keys=n_keys, n_sel=n_sel),
        out_shape=jax.ShapeDtypeStruct((n_batch * CHUNK, WIDTH), F32),
        grid=(n_batch,),
        in_specs=[query_spec, query_spec, new_spec(LANES), new_spec(WIDTH), new_spec(WIDTH),
                  cache_spec(WIDTH), cache_spec(WIDTH), cache_spec(HEAD_DIM)],
        out_specs=new_spec(WIDTH),
        scratch_shapes=_dsa_scratch(CHUNK, n_pad, key_block),
        compiler_params=_params(("parallel",)),
        name="dsa_sample",
    )(q, qi, kiw, k, v, k_cache, v_cache, ki_cache)


def _layer_norm(z, g, b):
    mu = jnp.mean(z, axis=-1, keepdims=True)
    zc = z - mu
    var = jnp.mean(zc * zc, axis=-1, keepdims=True)
    return zc * lax.rsqrt(var + LN_EPS) * g + b


HI_HALF = -65536


def _pack_rows(x):
    half = x.shape[1] // 2
    lo = lax.bitcast_convert_type(x[:, :half].astype(BF16).astype(F32), I32)
    hi = lax.bitcast_convert_type(x[:, half:].astype(BF16).astype(F32), I32)
    return (hi & HI_HALF) | lax.shift_right_logical(lo, 16)


def _unpack_rows(w):
    lo = lax.bitcast_convert_type(lax.shift_left(w, 16), F32).astype(BF16)
    hi = lax.bitcast_convert_type(w & HI_HALF, F32).astype(BF16)
    return lo, hi


def _merge_body(x_ref, ap_ref, as_ref, bp_ref, bs_ref, mp_ref, ms_ref, gates_ref, wbr_ref, wout_ref, g_ref, beta_ref,
                o_ref, ow_ref, *, alpha, prompt_tiles):
    d = x_ref.shape[1]
    in_prompt = pl.program_id(0) < prompt_tiles
    mix = None
    for n, (p_ref, s_ref) in enumerate(((ap_ref, as_ref), (bp_ref, bs_ref), (mp_ref, ms_ref))):
        branch = jnp.where(in_prompt, p_ref[...], s_ref[...])
        proj = jnp.dot(branch.astype(BF16), wbr_ref[n], preferred_element_type=F32)
        term = jax.nn.sigmoid(gates_ref[:, n * d:(n + 1) * d]) * proj
        mix = term if mix is None else mix + term
    y = jnp.dot(mix.astype(BF16), wout_ref[...], preferred_element_type=F32)
    out = _layer_norm(alpha * x_ref[...] + y, g_ref[...], beta_ref[...])
    o_ref[...] = out
    ow_ref[...] = _pack_rows(out)


def _merge(x, branches, gates, w_branch, w_out, g, beta, alpha, tm=512):
    n, d = x.shape
    prompt_tiles = branches[0][0].shape[0] // tm
    assert all(p.shape[0] == prompt_tiles * tm and (n - p.shape[0]) == s.shape[0] and s.shape[0] % tm == 0
               for p, s in branches)
    packed = jax.eval_shape(_pack_rows, jax.ShapeDtypeStruct((tm, d), F32))
    row = lambda width: pl.BlockSpec((tm, width), lambda i: (i, 0))
    full = lambda arr: pl.BlockSpec(arr.shape, lambda i: (0,) * arr.ndim)
    prompt_row = lambda width: pl.BlockSpec((tm, width), lambda i: (jnp.minimum(i, prompt_tiles - 1), 0))
    sample_row = lambda width: pl.BlockSpec((tm, width), lambda i: (jnp.maximum(i - prompt_tiles, 0), 0))
    branch_specs, branch_args = [], []
    for p, s in branches:
        branch_specs += [prompt_row(p.shape[1]), sample_row(s.shape[1])]
        branch_args += [p, s]
    return pl.pallas_call(
        functools.partial(_merge_body, alpha=alpha, prompt_tiles=prompt_tiles),
        out_shape=[jax.ShapeDtypeStruct((n, d), F32), jax.ShapeDtypeStruct((n, packed.shape[1]), packed.dtype)],
        grid=(n // tm,),
        in_specs=[row(d)] + branch_specs + [row(gates.shape[1]), full(w_branch), full(w_out), full(g), full(beta)],
        out_specs=[row(d), row(packed.shape[1])],
        compiler_params=_params(("parallel",)),
        name="merge",
    )(x, *branch_args, gates, w_branch, w_out, g, beta)


def _first_max(vals, idx, n):
    top = jnp.max(vals, axis=0, keepdims=True)
    arg = jnp.min(jnp.where(vals == top, idx, n), axis=0, keepdims=True)
    return top, arg


def _router_body(x_ref, wt_ref, bias_ref, eidx_ref, gw_ref, rank_ref, counts_ref, running):
    tm = x_ref.shape[0]

    @pl.when(pl.program_id(0) == 0)
    def _():
        running[...] = jnp.zeros(running.shape, F32)

    per = N_EXPERTS // N_GROUPS
    logits = lax.dot_general(wt_ref[...], x_ref[...].astype(BF16), _NT_DIMS, preferred_element_type=F32)
    s = jax.nn.sigmoid(logits)
    sb = s + bias_ref[...]
    in_group = lax.broadcasted_iota(I32, (per, tm), 0).astype(F32)
    group_scores = []
    for g in range(N_GROUPS):
        blk = sb[g * per:(g + 1) * per, :]
        top1, arg1 = _first_max(blk, in_group, per)
        top2 = jnp.max(jnp.where(in_group == arg1, -jnp.inf, blk), axis=0, keepdims=True)
        group_scores.append(top1 + top2)
    gs = jnp.concatenate(group_scores, axis=0)
    gidx = lax.broadcasted_iota(I32, (N_GROUPS, tm), 0).astype(F32)
    chosen = jnp.zeros((N_GROUPS, tm), F32)
    for _ in range(TOPK_GROUPS):
        _, arg = _first_max(gs, gidx, N_GROUPS)
        hit = gidx == arg
        chosen = jnp.where(hit, 1.0, chosen)
        gs = jnp.where(hit, -jnp.inf, gs)
    cand = jnp.concatenate(
        [jnp.where(chosen[g:g + 1, :] > 0.0, sb[g * per:(g + 1) * per, :], -jnp.inf) for g in range(N_GROUPS)],
        axis=0)
    eidx = lax.broadcasted_iota(I32, (N_EXPERTS, tm), 0).astype(F32)
    picks, weights, hits = [], [], []
    for _ in range(EXPERT_TOPK):
        _, arg = _first_max(cand, eidx, N_EXPERTS)
        hit = eidx == arg
        weights.append(jnp.sum(jnp.where(hit, s, 0.0), axis=0, keepdims=True))
        picks.append(arg)
        hits.append(hit)
        cand = jnp.where(hit, -jnp.inf, cand)
    total = weights[0]
    for w in weights[1:]:
        total = total + w
    taken = jnp.zeros((N_EXPERTS, tm), F32)
    for hit in hits:
        taken = jnp.where(hit, 1.0, taken)
    earlier = (lax.broadcasted_iota(I32, (tm, tm), 0) < lax.broadcasted_iota(I32, (tm, tm), 1))
    before = jnp.dot(taken.astype(BF16), jnp.where(earlier, 1.0, 0.0).astype(BF16),
                     preferred_element_type=F32) + running[...]
    for r in range(EXPERT_TOPK):
        eidx_ref[r:r + 1, :] = picks[r].astype(I32)
        gw_ref[r:r + 1, :] = weights[r] / total * ROUTE_SCALE
        rank_ref[r:r + 1, :] = jnp.sum(jnp.where(hits[r], before, 0.0), axis=0, keepdims=True).astype(I32)
    running[...] = running[...] + jnp.sum(taken, axis=1, keepdims=True)
    counts_ref[...] = running[...]


def _router(x, w_router_t, b_router_col, tm=512):
    n, d = x.shape
    slot = pl.BlockSpec((EXPERT_TOPK, tm), lambda i: (0, i))
    return pl.pallas_call(
        _router_body,
        out_shape=[jax.ShapeDtypeStruct((EXPERT_TOPK, n), I32), jax.ShapeDtypeStruct((EXPERT_TOPK, n), F32),
                   jax.ShapeDtypeStruct((EXPERT_TOPK, n), I32), jax.ShapeDtypeStruct((N_EXPERTS, 1), F32)],
        grid=(n // tm,),
        in_specs=[pl.BlockSpec((tm, d), lambda i: (i, 0)),
                  pl.BlockSpec(w_router_t.shape, lambda i: (0, 0)),
                  pl.BlockSpec(b_router_col.shape, lambda i: (0, 0))],
        out_specs=[slot, slot, slot, pl.BlockSpec((N_EXPERTS, 1), lambda i: (0, 0))],
        scratch_shapes=[pltpu.VMEM((N_EXPERTS, 1), F32)],
        compiler_params=_params(("arbitrary",)),
        name="router",
    )(x, w_router_t, b_router_col)


def _dest_body(eidx_ref, rank_ref, start_ref, dest_ref):
    tm = eidx_ref.shape[1]
    experts = lax.broadcasted_iota(I32, (N_EXPERTS, tm), 0)
    for r in range(EXPERT_TOPK):
        base = jnp.sum(jnp.where(experts == eidx_ref[r:r + 1, :], start_ref[...], 0.0), axis=0, keepdims=True)
        dest_ref[r:r + 1, :] = base.astype(I32) + rank_ref[r:r + 1, :]


def _dest_rows(eidx_t, rank_t, start_col, tm=512):
    n = eidx_t.shape[1]
    slot = pl.BlockSpec((EXPERT_TOPK, tm), lambda i: (0, i))
    return pl.pallas_call(
        _dest_body,
        out_shape=jax.ShapeDtypeStruct((EXPERT_TOPK, n), I32),
        grid=(n // tm,),
        in_specs=[slot, slot, pl.BlockSpec(start_col.shape, lambda i: (0, 0))],
        out_specs=slot,
        compiler_params=_params(("parallel",)),
        name="dest_rows",
    )(eidx_t, rank_t, start_col)


def _tile_major(dest_t, tm):
    slots, n = dest_t.shape
    return dest_t.reshape(slots, n // tm, tm).transpose(1, 0, 2).reshape(n // tm, 1, slots * tm)


def _scatter_body(dest_ref, x_ref, _, o_ref, sem):
    tm = x_ref.shape[0]

    def row_copy(t, k):
        return pltpu.make_async_copy(x_ref.at[pl.ds(t, 1)], o_ref.at[pl.ds(dest_ref[0, k * tm + t], 1)], sem)

    def issue(group, carry):
        base = pl.multiple_of(group * SUBLANES, SUBLANES)
        for r in range(SUBLANES):
            for k in range(EXPERT_TOPK):
                row_copy(base + r, k).start(priority=k % 2)
        return carry

    lax.fori_loop(0, tm // SUBLANES, issue, 0)
    for k in range(EXPERT_TOPK):
        pltpu.make_async_copy(x_ref, o_ref.at[pl.ds(0, tm)], sem).wait()


def _scatter_rows(xw, dest_t, rows, tm=512):
    n, width = xw.shape
    return pl.pallas_call(
        _scatter_body,
        out_shape=jax.ShapeDtypeStruct((rows, width), xw.dtype),
        grid=(n // tm,),
        in_specs=[pl.BlockSpec((None, 1, EXPERT_TOPK * tm), lambda i: (i, 0, 0), memory_space=pltpu.SMEM),
                  pl.BlockSpec((tm, width), lambda i: (i, 0)),
                  pl.BlockSpec(memory_space=pl.ANY)],
        out_specs=pl.BlockSpec(memory_space=pl.ANY),
        scratch_shapes=[pltpu.SemaphoreType.DMA],
        input_output_aliases={2: 0},
        compiler_params=_params(("arbitrary",)),
        name="scatter_rows",
    )(_tile_major(dest_t, tm), xw, jnp.zeros((rows, width), xw.dtype))


def _experts_body(block_exp_ref, n_used_ref, x_ref, wgu_ref, wdn_ref, o_ref, wgu_bf, wdn_bf):
    i = pl.program_id(0)
    used = i < n_used_ref[0]

    @pl.when(used)
    def _():
        prev = block_exp_ref[jnp.maximum(i - 1, 0)]

        @pl.when((i == 0) | (block_exp_ref[i] != prev))
        def _():
            wgu_bf[...] = wgu_ref[...].astype(BF16)
            wdn_bf[...] = wdn_ref[...].astype(BF16)

        x_lo, x_hi = _unpack_rows(x_ref[...])
        half = x_lo.shape[1]
        h = (jnp.dot(x_lo, wgu_bf[0:half, :], preferred_element_type=F32)
             + jnp.dot(x_hi, wgu_bf[half:, :], preferred_element_type=F32))
        ff = h.shape[1] // 2
        act = jax.nn.silu(h[:, :ff]) * h[:, ff:]
        o_ref[...] = jnp.dot(act.astype(BF16), wdn_bf[...], preferred_element_type=F32)

    @pl.when(jnp.logical_not(used))
    def _():
        o_ref[...] = jnp.zeros(o_ref.shape, F32)


def _experts(x_rows, block_exp, n_used, w_gu, w_dn, layer):
    rows, width = x_rows.shape
    n_blocks = rows // EXPERT_BLOCK
    d, ff2 = w_gu.shape[-2:]
    grid_spec = pltpu.PrefetchScalarGridSpec(
        num_scalar_prefetch=2,
        grid=(n_blocks,),
        in_specs=[pl.BlockSpec((EXPERT_BLOCK, width), lambda i, be, nu: (jnp.clip(i, 0, jnp.maximum(nu[0] - 1, 0)), 0)),
                  pl.BlockSpec((None, None, d, ff2), lambda i, be, nu: (layer, be[i], 0, 0)),
                  pl.BlockSpec((None, None, ff2 // 2, d), lambda i, be, nu: (layer, be[i], 0, 0))],
        out_specs=pl.BlockSpec((EXPERT_BLOCK, d), lambda i, be, nu: (i, 0)),
        scratch_shapes=[pltpu.VMEM((d, ff2), BF16), pltpu.VMEM((ff2 // 2, d), BF16)],
    )
    return pl.pallas_call(
        _experts_body,
        out_shape=jax.ShapeDtypeStruct((rows, d), F32),
        grid_spec=grid_spec,
        compiler_params=_params(("arbitrary",)),
        name="experts",
    )(block_exp, n_used, x_rows, w_gu, w_dn)


def _block_plan(counts, n_blocks):
    counts = counts.reshape(-1).astype(I32)
    padded = (counts + EXPERT_BLOCK - 1) // EXPERT_BLOCK * EXPERT_BLOCK
    pad_end = jnp.cumsum(padded)
    start_col = (pad_end - padded).astype(F32)[:, None]
    end_blocks = pad_end // EXPERT_BLOCK
    block_exp = jnp.minimum(jnp.sum(end_blocks[None, :] <= jnp.arange(n_blocks)[:, None], axis=1),
                            N_EXPERTS - 1).astype(I32)
    return start_col, block_exp, end_blocks[-1:].astype(I32)


def _ffn_out_body(dest_ref, dest_next_ref, x_ref, gw_ref, yb_ref, wgu_ref, wdn_ref, g_ref, beta_ref, o_ref,
                  ybuf, sems, *, alpha):
    i = pl.program_id(0)
    n_steps = pl.num_programs(0)
    tm = x_ref.shape[0]
    slot = i % 2

    def gather(dests, to_slot):
        def issue(group, carry):
            base = pl.multiple_of(group * SUBLANES, SUBLANES)
            for r in range(SUBLANES):
                for k in range(EXPERT_TOPK):
                    pltpu.make_async_copy(yb_ref.at[pl.ds(dests[0, k * tm + base + r], 1)],
                                          ybuf.at[to_slot, k, pl.ds(base + r, 1)],
                                          sems.at[to_slot]).start(priority=k % 2)
            return carry
        lax.fori_loop(0, tm // SUBLANES, issue, 0)

    @pl.when(i == 0)
    def _():
        gather(dest_ref, 0)

    for parity in range(2):
        @pl.when((i + 1 < n_steps) & (slot == parity))
        def _(parity=parity):
            gather(dest_next_ref, 1 - parity)

    x = x_ref[...]
    h = jnp.dot(x.astype(BF16), wgu_ref[...], preferred_element_type=F32)
    ff = h.shape[1] // 2
    act = jax.nn.silu(h[:, :ff]) * h[:, ff:]
    shared = jnp.dot(act.astype(BF16), wdn_ref[...], preferred_element_type=F32)

    for k in range(EXPERT_TOPK):
        pltpu.make_async_copy(yb_ref.at[pl.ds(0, tm)], ybuf.at[slot, k], sems.at[slot]).wait()
    routed = None
    for k in range(EXPERT_TOPK):
        term = ybuf[slot, k] * gw_ref[:, k:k + 1]
        routed = term if routed is None else routed + term
    o_ref[...] = _layer_norm(alpha * x + (routed + shared), g_ref[...], beta_ref[...])


def _ffn_out(x, yb, dest_t, gw, ws_gu, ws_dn, g, beta, alpha, tm=256):
    n, d = x.shape
    n_tiles = n // tm
    row = pl.BlockSpec((tm, d), lambda i: (i, 0))
    full = lambda arr: pl.BlockSpec(arr.shape, lambda i: (0,) * arr.ndim)
    dest_spec = lambda index: pl.BlockSpec((None, 1, EXPERT_TOPK * tm), index, memory_space=pltpu.SMEM)
    dest_tiles = _tile_major(dest_t, tm)
    return pl.pallas_call(
        functools.partial(_ffn_out_body, alpha=alpha),
        out_shape=jax.ShapeDtypeStruct((n, d), F32),
        grid=(n_tiles,),
        in_specs=[dest_spec(lambda i: (i, 0, 0)), dest_spec(lambda i: (jnp.minimum(i + 1, n_tiles - 1), 0, 0)),
                  row, pl.BlockSpec((tm, EXPERT_TOPK), lambda i: (i, 0)),
                  pl.BlockSpec(memory_space=pl.ANY),
                  full(ws_gu), full(ws_dn), full(g), full(beta)],
        out_specs=row,
        scratch_shapes=[pltpu.VMEM((2, EXPERT_TOPK, tm, d), F32), pltpu.SemaphoreType.DMA((2,))],
        compiler_params=_params(("arbitrary",)),
        name="ffn_out",
    )(dest_tiles, dest_tiles, x, gw, yb, ws_gu, ws_dn, g, beta)


def _moe(x, xw, layer, w_router_t, b_router_col, w_exp_gate_up, w_exp_down, ws_gu, ws_dn, g, beta, alpha):
    n = x.shape[0]
    n_blocks = -(-n * EXPERT_TOPK // EXPERT_BLOCK) + N_EXPERTS
    eidx_t, gw_t, rank_t, counts = _router(x, w_router_t, b_router_col)
    start_col, block_exp, n_used = _block_plan(counts, n_blocks)
    dest_t = _dest_rows(eidx_t, rank_t, start_col)
    x_rows = _scatter_rows(xw, dest_t, n_blocks * EXPERT_BLOCK)
    yb = _experts(x_rows, block_exp, n_used, w_exp_gate_up, w_exp_down, layer)
    return _ffn_out(x, yb, dest_t, gw_t.T, ws_gu, ws_dn, g, beta, alpha)


def kernel(x_prompt, x_sample, mem_prompt, cache_a_k, cache_a_v, cache_b_k, cache_b_v, cache_b_idx_k,
           cache_mem_k, cache_mem_v, w_in, rel_bias, w_mem_kv, w_branch, w_out, ln_mix_g, ln_mix_b,
           w_router, b_router, w_exp_gate_up, w_exp_down, w_sh_gate_up, w_sh_down, ln_ffn_g, ln_ffn_b):
    n_batch, seq, d = x_prompt.shape
    s_batch, s_seq, _ = x_sample.shape
    depth = w_in.shape[0]
    past = cache_b_k.shape[2]
    assert s_seq == CHUNK and seq % CHUNK == 0 and cache_a_k.shape[2] == BAND and seq >= BAND
    alpha = (2 * depth) ** 0.25
    n_p, n_s = n_batch * seq, s_batch * s_seq
    n_tot = n_p + n_s
    tm = 512
    assert seq % tm == 0 and n_p % tm == 0 and n_s % tm == 0 and tm % CHUNK == 0

    x = jnp.concatenate([x_prompt.reshape(n_p, d), x_sample.reshape(n_s, d)])

    pos = jnp.concatenate([jnp.arange(seq), jnp.tile(past + jnp.arange(s_seq), tm // s_seq)])
    tab_heads = _rope_tables(pos, LANES)
    tab_kiw = _rope_tables(pos, HEAD_DIM)
    p_tiles, seq_tiles = n_p // tm, seq // tm
    table_index = lambda i: jnp.where(i < p_tiles, i % seq_tiles, seq_tiles)

    wd = WIDTH
    c_qi, c_ki, c_wi, c_qm, c_gates = 6 * wd, 7 * wd, 7 * wd + HEAD_DIM, 7 * wd + HEAD_DIM + N_HEADS, 8 * wd + HEAD_DIM + N_HEADS
    cache_a_k = cache_a_k.reshape(depth, s_batch, BAND, wd)
    cache_a_v = cache_a_v.reshape(depth, s_batch, BAND, wd)
    cache_b_k = cache_b_k.reshape(depth, s_batch, past, wd)
    cache_b_v = cache_b_v.reshape(depth, s_batch, past, wd)
    cache_mem_k = cache_mem_k.reshape(depth, s_batch, N_MEM, wd)
    cache_mem_v = cache_mem_v.reshape(depth, s_batch, N_MEM, wd)
    mem_rows = mem_prompt.reshape(n_batch * N_MEM, d)

    outs = {k: [] for k in ("p_ak", "p_av", "p_bk", "p_bv", "p_ik", "p_mk", "p_mv",
                            "s_ak", "s_av", "s_bk", "s_bv", "s_ik")}
    for l in range(depth):
        w = w_in[l]
        w_a = w[:, :3 * wd].astype(BF16)
        w_b = w[:, 3 * wd:7 * wd].astype(BF16)
        w_c = w[:, c_qm:].astype(BF16)
        w_d = jnp.concatenate([w[:, c_ki:c_qm], jnp.zeros((d, LANES - HEAD_DIM - N_HEADS), F32)], 1).astype(BF16)

        qa, ka, va = _project(x, w_a, (wd, wd, wd), split_flags=(False, True, True), n_prompt=n_p,
                              tm=tm, name="proj_a")
        qb, kb, vb, qi = _project(x, w_b, (wd, wd, wd, wd), (True, True, False, True), tab_heads, table_index,
                                  split_flags=(False, True, True, False), n_prompt=n_p, tm=tm, name="proj_b")
        qm, gates = _project(x, w_c, (wd, 3 * d), tm=tm, name="proj_c")
        (kiw,) = _project(x, w_d, (LANES,), (True,), tab_kiw, table_index, split_flags=(True,), n_prompt=n_p,
                          tm=tm, name="proj_d")
        mk, mv = _project(mem_rows, w_mem_kv[l].astype(BF16), (wd, wd), tm=N_MEM, name="proj_mem")

        a = (_band_prompt(qa, ka[0], va[0], rel_bias[l], n_batch, seq),
             _band_sample(qa, ka[1], va[1], cache_a_k, cache_a_v, l, _band_bias(rel_bias[l], CHUNK), s_batch, n_p))
        b = (_dsa_prompt(qb, kb[0], vb[0], qi, kiw[0], n_batch, seq),
             _dsa_sample(qb, kb[1], vb[1], qi, kiw[1], cache_b_k, cache_b_v, cache_b_idx_k, l, s_batch, n_p))
        m = (_mem_attn(qm, mk, mv, lambda bb: (bb, 0), n_batch, seq, 0, tm),
             _mem_attn(qm, cache_mem_k, cache_mem_v, lambda bb, l=l: (l, bb, 0, 0), s_batch, s_seq, n_p, s_seq))

        x1, x1w = _merge(x, (a, b, m), gates, w_branch[l].astype(BF16), w_out[l].astype(BF16),
                         ln_mix_g[l][None], ln_mix_b[l][None], alpha)
        x = _moe(x1, x1w, l, w_router[l].T.astype(BF16), b_router[l][:, None], w_exp_gate_up, w_exp_down,
                 w_sh_gate_up[l].astype(BF16), w_sh_down[l].astype(BF16),
                 ln_ffn_g[l][None], ln_ffn_b[l][None], alpha)

        outs["p_ak"].append(ka[0].reshape(n_batch, seq, wd)[:, -BAND:])
        outs["p_av"].append(va[0].reshape(n_batch, seq, wd)[:, -BAND:])
        outs["p_bk"].append(kb[0].reshape(n_batch, seq, wd))
        outs["p_bv"].append(vb[0].reshape(n_batch, seq, wd))
        outs["p_ik"].append(kiw[0][:, :HEAD_DIM].reshape(n_batch, seq, HEAD_DIM))
        outs["p_mk"].append(mk.reshape(n_batch, N_MEM, wd))
        outs["p_mv"].append(mv.reshape(n_batch, N_MEM, wd))
        ka_s = ka[1].reshape(s_batch, s_seq, wd)
        va_s = va[1].reshape(s_batch, s_seq, wd)
        outs["s_ak"].append(jnp.concatenate([cache_a_k[l], ka_s], 1)[:, -BAND:])
        outs["s_av"].append(jnp.concatenate([cache_a_v[l], va_s], 1)[:, -BAND:])
        outs["s_bk"].append(kb[1].reshape(s_batch, s_seq, wd))
        outs["s_bv"].append(vb[1].reshape(s_batch, s_seq, wd))
        outs["s_ik"].append(kiw[1][:, :HEAD_DIM].reshape(s_batch, s_seq, HEAD_DIM))

    heads = lambda t: t.reshape(t.shape[:-1] + (N_HEADS, HEAD_DIM))
    mheads = lambda t: t.reshape(t.shape[:-1] + (M_HEADS, M_HEAD_DIM))
    st = lambda key: jnp.stack(outs[key])
    return (x[:n_p].reshape(n_batch, seq, d), x[n_p:].reshape(s_batch, s_seq, d),
            heads(st("p_ak")), heads(st("p_av")), heads(st("p_bk")), heads(st("p_bv")), st("p_ik"),
            mheads(st("p_mk")), mheads(st("p_mv")),
            heads(st("s_ak")), heads(st("s_av")), heads(st("s_bk")), heads(st("s_bv")), st("s_ik"))
```

```python
import functools

import jax
import jax.numpy as jnp
from jax import lax
from jax.experimental import pallas as pl
from jax.experimental.pallas import tpu as pltpu

F32 = jnp.float32
BF16 = jnp.bfloat16
I32 = jnp.int32

CHUNK = 64
HEAD_DIM = 64
N_HEADS = 8
WIDTH = N_HEADS * HEAD_DIM
BAND = 8 * CHUNK
MAX_REL = 128
TOPK_KEYS = 256
M_HEADS = 4
M_HEAD_DIM = 128
N_MEM = 256
ROPE_THETA = 10000.0
N_EXPERTS = 256
EXPERT_TOPK = 8
N_GROUPS = 8
TOPK_GROUPS = 4
ROUTE_SCALE = 2.5
EXPERT_BLOCK = 256
LN_EPS = 1e-5

LANES = 128
SUBLANES = 8
NEG_BIG = -1e30
INT_MIN = -(2 ** 31)
VMEM_LIMIT = 48 * 1024 * 1024

_NT_DIMS = (((1,), (1,)), ((), ()))


def _params(semantics):
    return pltpu.CompilerParams(dimension_semantics=semantics, vmem_limit_bytes=VMEM_LIMIT)


def _proj_body(*refs, widths, rope_flags, split_flags, prompt_tiles):
    has_rope = any(rope_flags)
    x_ref, w_ref = refs[0], refs[1]
    if has_rope:
        cos_ref, sin_lo_ref, sin_hi_ref = refs[2:5]
        outs = list(refs[5:])
    else:
        outs = list(refs[2:])
    in_prompt = pl.program_id(0) < prompt_tiles
    xb = x_ref[...].astype(BF16)
    off = 0
    for g, width in enumerate(widths):
        y = jnp.dot(xb, w_ref[:, off:off + width], preferred_element_type=F32)
        if rope_flags[g]:
            cos, sin_lo, sin_hi = cos_ref[...], sin_lo_ref[...], sin_hi_ref[...]
            chunks = [yk * cos + pltpu.roll(yk, LANES - 32, 1) * sin_lo + pltpu.roll(yk, 32, 1) * sin_hi
                      for yk in (y[:, k * LANES:(k + 1) * LANES] for k in range(width // LANES))]
        else:
            chunks = [y]

        def store(ref, chunks=chunks):
            step = chunks[0].shape[1]
            for k, chunk in enumerate(chunks):
                ref[:, k * step:(k + 1) * step] = chunk

        if split_flags[g]:
            prompt_ref, sample_ref = outs.pop(0), outs.pop(0)
            pl.when(in_prompt)(functools.partial(store, prompt_ref))
            pl.when(jnp.logical_not(in_prompt))(functools.partial(store, sample_ref))
        else:
            store(outs.pop(0))
        off += width


def _project(x, w, widths, rope_flags=None, tables=None, table_index=None, split_flags=None, n_prompt=None,
             tm=512, name="proj"):
    n, d = x.shape
    rope_flags = rope_flags or (False,) * len(widths)
    split_flags = split_flags or (False,) * len(widths)
    assert n % tm == 0 and w.shape == (d, sum(widths))
    prompt_tiles = n // tm
    if any(split_flags):
        assert n_prompt % tm == 0 and 0 < n_prompt < n
        prompt_tiles = n_prompt // tm
    in_specs = [pl.BlockSpec((tm, d), lambda i: (i, 0)),
                pl.BlockSpec(w.shape, lambda i: (0, 0))]
    args = [x, w]
    if any(rope_flags):
        for t in tables:
            in_specs.append(pl.BlockSpec((tm, LANES), lambda i: (table_index(i), 0)))
            args.append(t)
    out_shape, out_specs = [], []
    for wd, split in zip(widths, split_flags):
        if split:
            out_shape += [jax.ShapeDtypeStruct((n_prompt, wd), F32), jax.ShapeDtypeStruct((n - n_prompt, wd), F32)]
            out_specs += [pl.BlockSpec((tm, wd), lambda i: (jnp.minimum(i, prompt_tiles - 1), 0)),
                          pl.BlockSpec((tm, wd), lambda i: (jnp.maximum(i - prompt_tiles, 0), 0))]
        else:
            out_shape.append(jax.ShapeDtypeStruct((n, wd), F32))
            out_specs.append(pl.BlockSpec((tm, wd), lambda i: (i, 0)))
    flat = pl.pallas_call(
        functools.partial(_proj_body, widths=tuple(widths), rope_flags=tuple(rope_flags),
                          split_flags=tuple(split_flags), prompt_tiles=prompt_tiles),
        out_shape=out_shape,
        grid=(n // tm,),
        in_specs=in_specs,
        out_specs=out_specs,
        compiler_params=_params(("arbitrary",)),
        name=name,
    )(*args)
    flat = list(flat)
    return [(flat.pop(0), flat.pop(0)) if split else flat.pop(0) for split in split_flags]


def _rope_tables(pos, rope_lanes):
    half = HEAD_DIM // 2
    inv_freq = ROPE_THETA ** (-jnp.arange(half, dtype=F32) / half)
    ang = pos.astype(F32)[:, None] * inv_freq[None, :]
    cos, sin = jnp.cos(ang), jnp.sin(ang)
    reps = LANES // HEAD_DIM
    zero = jnp.zeros_like(sin)
    cos_t = jnp.tile(jnp.concatenate([cos, cos], -1), (1, reps))
    sin_lo = jnp.tile(jnp.concatenate([-sin, zero], -1), (1, reps))
    sin_hi = jnp.tile(jnp.concatenate([zero, sin], -1), (1, reps))
    keep = (jnp.arange(LANES) < rope_lanes)[None, :]
    return (jnp.where(keep, cos_t, 1.0), jnp.where(keep, sin_lo, 0.0), jnp.where(keep, sin_hi, 0.0))


def _reduce_rows(x, op, keep_groups=False):
    rows, n = x.shape
    groups = next(g for g in (8, 4, 2, 1) if rows % (8 * g) == 0 and rows // (8 * g) >= 4 or g == 1)
    parts = op(x.reshape(groups, rows // (8 * groups), 8, n), axis=1)
    if keep_groups:
        return parts
    return op(op(parts, axis=0), axis=0, keepdims=True)


def _head_mask(rows, head_in_pair):
    lane = lax.broadcasted_iota(I32, (rows, LANES), 1)
    lo = head_in_pair * HEAD_DIM
    return (lane >= lo) & (lane < lo + HEAD_DIM)


def _pair_merge(o_even, o_odd):
    lane = lax.broadcasted_iota(I32, o_even.shape, 1)
    return jnp.where(lane < HEAD_DIM, o_even, o_odd)


def _band_sample_body(q_ref, kn_ref, vn_ref, kh_ref, vh_ref, bias_ref, o_ref, kwin, vwin):
    kwin[0:BAND, :] = kh_ref[...].astype(BF16)
    vwin[0:BAND, :] = vh_ref[...].astype(BF16)
    kwin[BAND:, :] = kn_ref[...].astype(BF16)
    vwin[BAND:, :] = vn_ref[...].astype(BF16)
    q = q_ref[...] * (HEAD_DIM ** -0.5)
    for hp in range(N_HEADS // 2):
        cols = slice(hp * LANES, (hp + 1) * LANES)
        outs = []
        for hh in range(2):
            qm = jnp.where(_head_mask(CHUNK, hh), q[:, cols], 0.0).astype(BF16)
            s = lax.dot_general(qm, kwin[:, cols], _NT_DIMS, preferred_element_type=F32) + bias_ref[2 * hp + hh]
            p = jnp.exp(s - jnp.max(s, axis=1, keepdims=True))
            denom = jnp.sum(p, axis=1, keepdims=True)
            outs.append(jnp.dot(p.astype(BF16), vwin[:, cols], preferred_element_type=F32) / denom)
        o_ref[:, cols] = _pair_merge(outs[0], outs[1])


def _band_prompt_body(q_ref, k_ref, v_ref, bias_ref, o_ref, k_even, k_odd, v_t, q_t, s_sc, p_sc, o_t):
    t = pl.program_id(1)
    nq = q_ref.shape[0]
    win = BAND + nq
    seq = k_ref.shape[0]
    stage_rows = 256

    @pl.when(t == 0)
    def _():
        k_even[0:BAND, :] = jnp.zeros((BAND, WIDTH), BF16)
        k_odd[0:BAND, :] = jnp.zeros((BAND, WIDTH), BF16)
        v_t[:, 0:BAND] = jnp.zeros((WIDTH, BAND), BF16)
        for r in range(0, seq, stage_rows):
            ke, ko = _split_pair_lanes(k_ref[r:r + stage_rows, :])
            k_even[BAND + r:BAND + r + stage_rows, :] = ke.astype(BF16)
            k_odd[BAND + r:BAND + r + stage_rows, :] = ko.astype(BF16)
            v_t[:, BAND + r:BAND + r + stage_rows] = v_ref[r:r + stage_rows, :].T.astype(BF16)

    start = pl.multiple_of(t * nq, nq)
    window = pl.ds(start, win)
    q_t[...] = (q_ref[...] * (HEAD_DIM ** -0.5)).T.astype(BF16)
    before_start = jnp.where(lax.broadcasted_iota(I32, (win, nq), 0) >= BAND - t * nq, 0.0, NEG_BIG)
    for h in range(N_HEADS):
        pair = slice((h // 2) * LANES, (h // 2 + 1) * LANES)
        k_ref_h = k_even if h % 2 == 0 else k_odd
        s_sc[h * win:(h + 1) * win, :] = (jnp.dot(k_ref_h[window, pair], q_t[pair, :], preferred_element_type=F32)
                                          + bias_ref[h] + before_start)
    for h in range(N_HEADS):
        s = s_sc[h * win:(h + 1) * win, :]
        p = jnp.exp(s - _reduce_rows(s, jnp.max))
        p_sc[h * win:(h + 1) * win, :] = (p / _reduce_rows(p, jnp.sum)).astype(BF16)
    for h in range(N_HEADS):
        dims = slice(h * HEAD_DIM, (h + 1) * HEAD_DIM)
        o_t[dims, :] = jnp.dot(v_t[dims, window], p_sc[h * win:(h + 1) * win, :], preferred_element_type=F32)
    o_ref[...] = o_t[...].T


def _band_bias(table, n_rows):
    win = BAND + n_rows
    assert n_rows <= MAX_REL <= BAND
    n_heads = table.shape[0]
    g = jnp.concatenate([jnp.broadcast_to(table[:, 2 * MAX_REL:], (n_heads, win - MAX_REL)),
                         table[:, MAX_REL - n_rows:2 * MAX_REL][:, ::-1],
                         jnp.zeros((n_heads, 1), table.dtype)], axis=1).astype(F32)
    period = g.shape[1]
    skew = jnp.tile(g, (1, n_rows))[:, :n_rows * (period - 1)].reshape(n_heads, n_rows, period - 1)
    return skew[:, :, n_rows - 1:n_rows - 1 + win]


def _band_prompt(q, k, v, table, n_batch, seq, q_rows=128):
    nt = seq // q_rows
    win = BAND + q_rows
    assert seq % q_rows == 0 and q_rows % CHUNK == 0
    key_chunk = jnp.arange(win)[:, None] // CHUNK - BAND // CHUNK
    query_chunk = jnp.arange(q_rows)[None, :] // CHUNK
    visible = (key_chunk <= query_chunk) & (key_chunk >= query_chunk - BAND // CHUNK)
    bias = jnp.where(visible[None], jnp.swapaxes(_band_bias(table, q_rows), 1, 2), NEG_BIG)
    tile_spec = pl.BlockSpec((q_rows, WIDTH), lambda b, t: (b * nt + t, 0))
    seq_spec = pl.BlockSpec((seq, WIDTH), lambda b, t: (b, 0))
    return pl.pallas_call(
        _band_prompt_body,
        out_shape=jax.ShapeDtypeStruct((n_batch * seq, WIDTH), F32),
        grid=(n_batch, nt),
        in_specs=[tile_spec, seq_spec, seq_spec, pl.BlockSpec(bias.shape, lambda b, t: (0, 0, 0))],
        out_specs=tile_spec,
        scratch_shapes=[pltpu.VMEM((BAND + seq, WIDTH), BF16), pltpu.VMEM((BAND + seq, WIDTH), BF16),
                        pltpu.VMEM((WIDTH, BAND + seq), BF16), pltpu.VMEM((WIDTH, q_rows), BF16),
                        pltpu.VMEM((N_HEADS * win, q_rows), F32), pltpu.VMEM((N_HEADS * win, q_rows), BF16),
                        pltpu.VMEM((WIDTH, q_rows), F32)],
        compiler_params=_params(("parallel", "arbitrary")),
        name="band_prompt",
    )(q, k, v, bias)


def _band_sample(q, k, v, k_hist, v_hist, layer, bias, n_batch, row0):
    blk0 = row0 // CHUNK
    hist_spec = pl.BlockSpec((None, None, BAND, WIDTH), lambda b: (layer, b, 0, 0))
    new_spec = pl.BlockSpec((CHUNK, WIDTH), lambda b: (b, 0))
    return pl.pallas_call(
        _band_sample_body,
        out_shape=jax.ShapeDtypeStruct((n_batch * CHUNK, WIDTH), F32),
        grid=(n_batch,),
        in_specs=[pl.BlockSpec((CHUNK, WIDTH), lambda b: (blk0 + b, 0)), new_spec, new_spec, hist_spec, hist_spec,
                  pl.BlockSpec(bias.shape, lambda b: (0, 0, 0))],
        out_specs=new_spec,
        scratch_shapes=[pltpu.VMEM((BAND + CHUNK, WIDTH), BF16), pltpu.VMEM((BAND + CHUNK, WIDTH), BF16)],
        compiler_params=_params(("parallel",)),
        name="band_sample",
    )(q, k, v, k_hist, v_hist, bias)


def _mem_body(q_ref, k_ref, v_ref, o_ref):
    scale = M_HEAD_DIM ** -0.5
    for h in range(M_HEADS):
        cols = slice(h * M_HEAD_DIM, (h + 1) * M_HEAD_DIM)
        s = lax.dot_general(q_ref[:, cols].astype(BF16), k_ref[:, cols].astype(BF16), _NT_DIMS,
                            preferred_element_type=F32) * scale
        p = jnp.exp(s - jnp.max(s, axis=1, keepdims=True))
        denom = jnp.sum(p, axis=1, keepdims=True)
        o = jnp.dot(p.astype(BF16), v_ref[:, cols].astype(BF16), preferred_element_type=F32)
        o_ref[:, cols] = o / denom


def _mem_attn(q, mk, mv, kv_index, n_batch, rows_per_batch, row0, tq):
    width = M_HEADS * M_HEAD_DIM
    per = rows_per_batch // tq
    blk0 = row0 // tq
    kv_block = (None,) * (mk.ndim - 2) + (N_MEM, width)
    kv_spec = pl.BlockSpec(kv_block, lambda b, t: kv_index(b))
    return pl.pallas_call(
        _mem_body,
        out_shape=jax.ShapeDtypeStruct((n_batch * rows_per_batch, width), F32),
        grid=(n_batch, per),
        in_specs=[pl.BlockSpec((tq, width), lambda b, t: (blk0 + b * per + t, 0)), kv_spec, kv_spec],
        out_specs=pl.BlockSpec((tq, width), lambda b, t: (b * per + t, 0)),
        compiler_params=_params(("parallel", "parallel")),
        name="mem_attn",
    )(q, mk, mv)


def _split_pair_lanes(x):
    lane = lax.broadcasted_iota(I32, x.shape, 1)
    even = (lane & (LANES - 1)) < HEAD_DIM
    return jnp.where(even, x, 0.0), jnp.where(even, 0.0, x)


def _dsa_body(*refs, sample, key_block, n_keys, n_sel):
    n_in = 8 if sample else 6
    if sample:
        q_ref, qi_ref, kiwq_ref, kn_ref, vn_ref, kc_ref, vc_ref, kic_ref = refs[:n_in]
    else:
        q_ref, qi_ref, kiwq_ref, k_ref, v_ref, kiw_ref = refs[:n_in]
    o_ref = refs[n_in]
    (k_even, k_odd, v_t, ki_lo, ki_hi, keybuf, maskbuf, q_t, qi_t, kiwq_t, m_sc, l_sc, a_sc, acc_t,
     s_sc, p_sc) = refs[n_in + 1:]
    kb = key_block
    n_pad = k_even.shape[0]
    nq = q_t.shape[1]
    widen = lambda rows: jnp.concatenate([rows] * (nq // rows.shape[0]), axis=0)
    stage_rows = 256

    def stage_kv(row0, k_new, v_new):
        n_new = k_new.shape[0]
        ke, ko = _split_pair_lanes(k_new)
        k_even[row0:row0 + n_new, :] = ke.astype(BF16)
        k_odd[row0:row0 + n_new, :] = ko.astype(BF16)
        v_t[:, row0:row0 + n_new] = v_new.T.astype(BF16)

    def stage_index_keys(row0, kiw):
        lane = lax.broadcasted_iota(I32, kiw.shape, 1)
        ki_lo[row0:row0 + kiw.shape[0], :] = jnp.where(lane < HEAD_DIM, kiw, 0.0).astype(BF16)
        ki_hi[row0:row0 + kiw.shape[0], :] = jnp.where(lane < HEAD_DIM, 0.0, pltpu.roll(kiw, HEAD_DIM, 1)).astype(BF16)

    def stage_keys():
        if sample:
            n_hist = kc_ref.shape[0]
            for r in range(0, n_hist, stage_rows):
                stage_kv(r, kc_ref[r:r + stage_rows, :], vc_ref[r:r + stage_rows, :])
            kic = kic_ref[...]
            zeros = jnp.zeros_like(kic)
            ki_lo[0:n_hist, :] = jnp.concatenate([kic, zeros], axis=1).astype(BF16)
            ki_hi[0:n_hist, :] = jnp.concatenate([zeros, kic], axis=1).astype(BF16)
            stage_kv(n_hist, kn_ref[...], vn_ref[...])
            stage_index_keys(n_hist, kiwq_ref[...])
            if n_pad > n_keys:
                for ref in (k_even, k_odd, ki_lo, ki_hi):
                    ref[n_keys:, :] = jnp.zeros((n_pad - n_keys, ref.shape[1]), BF16)
                v_t[:, n_keys:] = jnp.zeros((v_t.shape[0], n_pad - n_keys), BF16)
        else:
            for r in range(0, n_keys, stage_rows):
                stage_kv(r, k_ref[r:r + stage_rows, :], v_ref[r:r + stage_rows, :])
            stage_index_keys(0, kiw_ref[...])

    if sample:
        stage_keys()
        limit = n_keys
        nb = n_pad // kb
        search_bits = 32
    else:
        t = pl.program_id(1)
        pl.when(t == 0)(stage_keys)
        chunk = t * (nq // CHUNK) + lax.broadcasted_iota(I32, (1, nq), 1) // CHUNK
        limit = (chunk + 1) * CHUNK
        limit_max = (t + 1) * nq
        nb = (limit_max + kb - 1) // kb
        search_bits = jnp.where(limit_max <= n_sel, 0, 32)

    def block_rows(j):
        return pl.ds(pl.multiple_of(j * kb, kb), kb)

    def key_pos(j):
        return j * kb + lax.broadcasted_iota(I32, (kb, nq), 0)

    q_t[...] = widen(q_ref[...] * (HEAD_DIM ** -0.5)).T.astype(BF16)
    qi_t[...] = widen(qi_ref[...]).T.astype(BF16)
    kiwq_t[...] = widen(kiwq_ref[...]).T

    def score_block(j, carry):
        acc = jnp.zeros((kb, nq), F32)
        for h in range(N_HEADS):
            pair = slice((h // 2) * LANES, (h // 2 + 1) * LANES)
            ki_ref = ki_lo if h % 2 == 0 else ki_hi
            d = jnp.dot(ki_ref[block_rows(j), :], qi_t[pair, :], preferred_element_type=F32)
            acc = acc + kiwq_t[HEAD_DIM + h:HEAD_DIM + h + 1, :] * jnp.maximum(d, 0.0)
        acc = acc + 0.0
        acc = jnp.where(key_pos(j) < limit, acc, -jnp.inf)
        bits = lax.bitcast_convert_type(acc, I32)
        keybuf[block_rows(j), :] = bits ^ ((bits >> 31) & 0x7FFFFFFF)
        return carry

    lax.fori_loop(0, nb, score_block, 0)

    def count(pred_fn):
        def body(j, part):
            hit = jnp.where(pred_fn(keybuf[block_rows(j), :], j), 1.0, 0.0)
            return part + _reduce_rows(hit, jnp.sum, keep_groups=True)
        groups = jax.eval_shape(functools.partial(_reduce_rows, op=jnp.sum, keep_groups=True),
                                jax.ShapeDtypeStruct((kb, nq), F32)).shape[0]
        part = lax.fori_loop(0, nb, body, jnp.zeros((groups, 8, nq), F32))
        return jnp.sum(jnp.sum(part, axis=0), axis=0, keepdims=True)

    def bit_step(i, carry):
        prefix, at_prefix = carry
        cand = prefix | lax.shift_left(jnp.int32(1), 31 - i)
        thr_i = cand ^ INT_MIN
        cnt = count(lambda kblk, j: kblk >= thr_i)
        keep = cnt >= n_sel
        return jnp.where(keep, cand, prefix), jnp.where(keep, cnt, at_prefix)

    staged = jnp.zeros((1, nq), F32) + jnp.asarray(nb * kb, F32)
    prefix, at_least = lax.fori_loop(0, search_bits, bit_step, (jnp.zeros((1, nq), I32), staged))
    thr = prefix ^ INT_MIN
    above = count(lambda kblk, j: kblk > thr)
    need = n_sel - above
    surplus = at_least > n_sel

    pos_bits = max(1, (n_pad - 1).bit_length() + 1)
    any_surplus = jnp.max(jnp.where(surplus, 1.0, 0.0)) > 0.0

    def pos_step(i, bound):
        cand = bound | lax.shift_left(jnp.int32(1), pos_bits - 1 - i)
        cnt = count(lambda kblk, j: (kblk == thr) & (key_pos(j) < cand))
        return jnp.where(cnt <= need, cand, bound)

    bound = lax.fori_loop(0, jnp.where(any_surplus, pos_bits, 0), pos_step, jnp.zeros((1, nq), I32))
    bound = jnp.where(surplus, bound, 2 ** pos_bits - 1)

    def mask_block(j, carry):
        kblk = keybuf[block_rows(j), :]
        pos = key_pos(j)
        sel = ((kblk > thr) | ((kblk == thr) & (pos < bound))) & (pos < limit)
        maskbuf[block_rows(j), :] = jnp.where(sel, 0.0, NEG_BIG)
        return carry

    lax.fori_loop(0, nb, mask_block, 0)

    m_sc[...] = jnp.full(m_sc.shape, NEG_BIG, F32)
    l_sc[...] = jnp.zeros(l_sc.shape, F32)
    acc_t[...] = jnp.zeros(acc_t.shape, F32)

    def attend(j, carry):
        mask = maskbuf[block_rows(j), :]
        for h in range(N_HEADS):
            pair = slice((h // 2) * LANES, (h // 2 + 1) * LANES)
            k_ref_h = k_even if h % 2 == 0 else k_odd
            s_sc[h * kb:(h + 1) * kb, :] = jnp.dot(k_ref_h[block_rows(j), pair], q_t[pair, :],
                                                   preferred_element_type=F32) + mask
        for h in range(N_HEADS):
            stat = slice(h * 8, (h + 1) * 8)
            s = s_sc[h * kb:(h + 1) * kb, :]
            m_prev = m_sc[stat, :]
            m_new = jnp.maximum(m_prev, _reduce_rows(s, jnp.max))
            alpha = jnp.exp(m_prev - m_new)
            p = jnp.exp(s - m_new[0:1, :])
            l_sc[stat, :] = alpha * l_sc[stat, :] + _reduce_rows(p, jnp.sum)
            m_sc[stat, :] = m_new
            a_sc[stat, :] = alpha
            p_sc[h * kb:(h + 1) * kb, :] = p.astype(BF16)
        for h in range(N_HEADS):
            dims = slice(h * HEAD_DIM, (h + 1) * HEAD_DIM)
            pv = jnp.dot(v_t[dims, block_rows(j)], p_sc[h * kb:(h + 1) * kb, :], preferred_element_type=F32)
            acc_t[dims, :] = a_sc[h * 8:h * 8 + 1, :] * acc_t[dims, :] + pv
        return carry

    lax.fori_loop(0, nb, attend, 0)
    for h in range(N_HEADS):
        dims = slice(h * HEAD_DIM, (h + 1) * HEAD_DIM)
        acc_t[dims, :] = acc_t[dims, :] / l_sc[h * 8:h * 8 + 1, :]
    o_ref[...] = acc_t[...].T[0:o_ref.shape[0], :]


def _dsa_scratch(nq, n_pad, key_block):
    nq = max(nq, LANES)
    return [pltpu.VMEM((n_pad, WIDTH), BF16), pltpu.VMEM((n_pad, WIDTH), BF16),
            pltpu.VMEM((WIDTH, n_pad), BF16),
            pltpu.VMEM((n_pad, LANES), BF16), pltpu.VMEM((n_pad, LANES), BF16),
            pltpu.VMEM((n_pad, nq), I32), pltpu.VMEM((n_pad, nq), F32),
            pltpu.VMEM((WIDTH, nq), BF16), pltpu.VMEM((WIDTH, nq), BF16),
            pltpu.VMEM((LANES, nq), F32),
            pltpu.VMEM((N_HEADS * 8, nq), F32), pltpu.VMEM((N_HEADS * 8, nq), F32),
            pltpu.VMEM((N_HEADS * 8, nq), F32),
            pltpu.VMEM((WIDTH, nq), F32),
            pltpu.VMEM((N_HEADS * key_block, nq), F32),
            pltpu.VMEM((N_HEADS * key_block, nq), BF16)]


def _dsa_prompt(q, k, v, qi, kiw, n_batch, seq, key_block=256, q_rows=256):
    nt = seq // q_rows
    key_block = min(key_block, seq)
    assert seq % key_block == 0 and seq % q_rows == 0 and q_rows % CHUNK == 0
    n_sel = min(TOPK_KEYS, seq // 4)
    tile_spec = lambda width: pl.BlockSpec((q_rows, width), lambda b, t: (b * nt + t, 0))
    seq_spec = lambda width: pl.BlockSpec((seq, width), lambda b, t: (b, 0))
    return pl.pallas_call(
        functools.partial(_dsa_body, sample=False, key_block=key_block, n_keys=seq, n_sel=n_sel),
        out_shape=jax.ShapeDtypeStruct((n_batch * seq, WIDTH), F32),
        grid=(n_batch, nt),
        in_specs=[tile_spec(WIDTH), tile_spec(WIDTH), tile_spec(LANES),
                  seq_spec(WIDTH), seq_spec(WIDTH), seq_spec(LANES)],
        out_specs=tile_spec(WIDTH),
        scratch_shapes=_dsa_scratch(q_rows, seq, key_block),
        compiler_params=_params(("parallel", "arbitrary")),
        name="dsa_prompt",
    )(q, qi, kiw, k, v, kiw)


def _dsa_sample(q, k, v, qi, kiw, k_cache, v_cache, ki_cache, layer, n_batch, row0, key_block=768):
    past = k_cache.shape[2]
    n_keys = past + CHUNK
    key_block = min(key_block, -(-n_keys // 256) * 256)
    n_pad = -(-n_keys // key_block) * key_block
    n_sel = min(TOPK_KEYS, n_keys // 4)
    blk0 = row0 // CHUNK
    query_spec = pl.BlockSpec((CHUNK, WIDTH), lambda b: (blk0 + b, 0))
    new_spec = lambda width: pl.BlockSpec((CHUNK, width), lambda b: (b, 0))
    cache_spec = lambda width: pl.BlockSpec((None, None, past, width), lambda b: (layer, b, 0, 0))
    return pl.pallas_call(
        functools.partial(_dsa_body, sample=True, key_block=key_block, n_keys=n_keys, n_sel=n_sel),
        out_shape=jax.ShapeDtypeStruct((n_batch * CHUNK, WIDTH), F32),
        grid=(n_batch,),
        in_specs=[query_spec, query_spec, new_spec(LANES), new_spec(WIDTH), new_spec(WIDTH),
                  cache_spec(WIDTH), cache_spec(WIDTH), cache_spec(HEAD_DIM)],
        out_specs=new_spec(WIDTH),
        scratch_shapes=_dsa_scratch(CHUNK, n_pad, key_block),
        compiler_params=_params(("parallel",)),
        name="dsa_sample",
    )(q, qi, kiw, k, v, k_cache, v_cache, ki_cache)


def _layer_norm(z, g, b):
    mu = jnp.mean(z, axis=-1, keepdims=True)
    zc = z - mu
    var = jnp.mean(zc * zc, axis=-1, keepdims=True)
    return zc * lax.rsqrt(var + LN_EPS) * g + b


HI_HALF = -65536


def _pack_rows(x):
    half = x.shape[1] // 2
    lo = lax.bitcast_convert_type(x[:, :half].astype(BF16).astype(F32), I32)
    hi = lax.bitcast_convert_type(x[:, half:].astype(BF16).astype(F32), I32)
    return (hi & HI_HALF) | lax.shift_right_logical(lo, 16)


def _unpack_rows(w):
    lo = lax.bitcast_convert_type(lax.shift_left(w, 16), F32).astype(BF16)
    hi = lax.bitcast_convert_type(w & HI_HALF, F32).astype(BF16)
    return lo, hi


def _merge_body(x_ref, ap_ref, as_ref, bp_ref, bs_ref, mp_ref, ms_ref, gates_ref, wbr_ref, wout_ref, g_ref, beta_ref,
                o_ref, ow_ref, *, alpha, prompt_tiles):
    d = x_ref.shape[1]
    in_prompt = pl.program_id(0) < prompt_tiles
    mix = None
    for n, (p_ref, s_ref) in enumerate(((ap_ref, as_ref), (bp_ref, bs_ref), (mp_ref, ms_ref))):
        branch = jnp.where(in_prompt, p_ref[...], s_ref[...])
        proj = jnp.dot(branch.astype(BF16), wbr_ref[n], preferred_element_type=F32)
        term = jax.nn.sigmoid(gates_ref[:, n * d:(n + 1) * d]) * proj
        mix = term if mix is None else mix + term
    y = jnp.dot(mix.astype(BF16), wout_ref[...], preferred_element_type=F32)
    out = _layer_norm(alpha * x_ref[...] + y, g_ref[...], beta_ref[...])
    o_ref[...] = out
    ow_ref[...] = _pack_rows(out)


def _merge(x, branches, gates, w_branch, w_out, g, beta, alpha, tm=512):
    n, d = x.shape
    prompt_tiles = branches[0][0].shape[0] // tm
    assert all(p.shape[0] == prompt_tiles * tm and (n - p.shape[0]) == s.shape[0] and s.shape[0] % tm == 0
               for p, s in branches)
    packed = jax.eval_shape(_pack_rows, jax.ShapeDtypeStruct((tm, d), F32))
    row = lambda width: pl.BlockSpec((tm, width), lambda i: (i, 0))
    full = lambda arr: pl.BlockSpec(arr.shape, lambda i: (0,) * arr.ndim)
    prompt_row = lambda width: pl.BlockSpec((tm, width), lambda i: (jnp.minimum(i, prompt_tiles - 1), 0))
    sample_row = lambda width: pl.BlockSpec((tm, width), lambda i: (jnp.maximum(i - prompt_tiles, 0), 0))
    branch_specs, branch_args = [], []
    for p, s in branches:
        branch_specs += [prompt_row(p.shape[1]), sample_row(s.shape[1])]
        branch_args += [p, s]
    return pl.pallas_call(
        functools.partial(_merge_body, alpha=alpha, prompt_tiles=prompt_tiles),
        out_shape=[jax.ShapeDtypeStruct((n, d), F32), jax.ShapeDtypeStruct((n, packed.shape[1]), packed.dtype)],
        grid=(n // tm,),
        in_specs=[row(d)] + branch_specs + [row(gates.shape[1]), full(w_branch), full(w_out), full(g), full(beta)],
        out_specs=[row(d), row(packed.shape[1])],
        compiler_params=_params(("parallel",)),
        name="merge",
    )(x, *branch_args, gates, w_branch, w_out, g, beta)


def _first_max(vals, idx, n):
    top = jnp.max(vals, axis=0, keepdims=True)
    arg = jnp.min(jnp.where(vals == top, idx, n), axis=0, keepdims=True)
    return top, arg


def _router_body(x_ref, wt_ref, bias_ref, eidx_ref, gw_ref, rank_ref, counts_ref, running):
    tm = x_ref.shape[0]

    @pl.when(pl.program_id(0) == 0)
    def _():
        running[...] = jnp.zeros(running.shape, F32)

    per = N_EXPERTS // N_GROUPS
    logits = lax.dot_general(wt_ref[...], x_ref[...].astype(BF16), _NT_DIMS, preferred_element_type=F32)
    s = jax.nn.sigmoid(logits)
    sb = s + bias_ref[...]
    in_group = lax.broadcasted_iota(I32, (per, tm), 0).astype(F32)
    group_scores = []
    for g in range(N_GROUPS):
        blk = sb[g * per:(g + 1) * per, :]
        top1, arg1 = _first_max(blk, in_group, per)
        top2 = jnp.max(jnp.where(in_group == arg1, -jnp.inf, blk), axis=0, keepdims=True)
        group_scores.append(top1 + top2)
    gs = jnp.concatenate(group_scores, axis=0)
    gidx = lax.broadcasted_iota(I32, (N_GROUPS, tm), 0).astype(F32)
    chosen = jnp.zeros((N_GROUPS, tm), F32)
    for _ in range(TOPK_GROUPS):
        _, arg = _first_max(gs, gidx, N_GROUPS)
        hit = gidx == arg
        chosen = jnp.where(hit, 1.0, chosen)
        gs = jnp.where(hit, -jnp.inf, gs)
    cand = jnp.concatenate(
        [jnp.where(chosen[g:g + 1, :] > 0.0, sb[g * per:(g + 1) * per, :], -jnp.inf) for g in range(N_GROUPS)],
        axis=0)
    eidx = lax.broadcasted_iota(I32, (N_EXPERTS, tm), 0).astype(F32)
    picks, weights, hits = [], [], []
    for _ in range(EXPERT_TOPK):
        _, arg = _first_max(cand, eidx, N_EXPERTS)
        hit = eidx == arg
        weights.append(jnp.sum(jnp.where(hit, s, 0.0), axis=0, keepdims=True))
        picks.append(arg)
        hits.append(hit)
        cand = jnp.where(hit, -jnp.inf, cand)
    total = weights[0]
    for w in weights[1:]:
        total = total + w
    taken = jnp.zeros((N_EXPERTS, tm), F32)
    for hit in hits:
        taken = jnp.where(hit, 1.0, taken)
    earlier = (lax.broadcasted_iota(I32, (tm, tm), 0) < lax.broadcasted_iota(I32, (tm, tm), 1))
    before = jnp.dot(taken.astype(BF16), jnp.where(earlier, 1.0, 0.0).astype(BF16),
                     preferred_element_type=F32) + running[...]
    for r in range(EXPERT_TOPK):
        eidx_ref[r:r + 1, :] = picks[r].astype(I32)
        gw_ref[r:r + 1, :] = weights[r] / total * ROUTE_SCALE
        rank_ref[r:r + 1, :] = jnp.sum(jnp.where(hits[r], before, 0.0), axis=0, keepdims=True).astype(I32)
    running[...] = running[...] + jnp.sum(taken, axis=1, keepdims=True)
    counts_ref[...] = running[...]


def _router(x, w_router_t, b_router_col, tm=512):
    n, d = x.shape
    slot = pl.BlockSpec((EXPERT_TOPK, tm), lambda i: (0, i))
    return pl.pallas_call(
        _router_body,
        out_shape=[jax.ShapeDtypeStruct((EXPERT_TOPK, n), I32), jax.ShapeDtypeStruct((EXPERT_TOPK, n), F32),
                   jax.ShapeDtypeStruct((EXPERT_TOPK, n), I32), jax.ShapeDtypeStruct((N_EXPERTS, 1), F32)],
        grid=(n // tm,),
        in_specs=[pl.BlockSpec((tm, d), lambda i: (i, 0)),
                  pl.BlockSpec(w_router_t.shape, lambda i: (0, 0)),
                  pl.BlockSpec(b_router_col.shape, lambda i: (0, 0))],
        out_specs=[slot, slot, slot, pl.BlockSpec((N_EXPERTS, 1), lambda i: (0, 0))],
        scratch_shapes=[pltpu.VMEM((N_EXPERTS, 1), F32)],
        compiler_params=_params(("arbitrary",)),
        name="router",
    )(x, w_router_t, b_router_col)


def _dest_body(eidx_ref, rank_ref, start_ref, dest_ref):
    tm = eidx_ref.shape[1]
    experts = lax.broadcasted_iota(I32, (N_EXPERTS, tm), 0)
    for r in range(EXPERT_TOPK):
        base = jnp.sum(jnp.where(experts == eidx_ref[r:r + 1, :], start_ref[...], 0.0), axis=0, keepdims=True)
        dest_ref[r:r + 1, :] = base.astype(I32) + rank_ref[r:r + 1, :]


def _dest_rows(eidx_t, rank_t, start_col, tm=512):
    n = eidx_t.shape[1]
    slot = pl.BlockSpec((EXPERT_TOPK, tm), lambda i: (0, i))
    return pl.pallas_call(
        _dest_body,
        out_shape=jax.ShapeDtypeStruct((EXPERT_TOPK, n), I32),
        grid=(n // tm,),
        in_specs=[slot, slot, pl.BlockSpec(start_col.shape, lambda i: (0, 0))],
        out_specs=slot,
        compiler_params=_params(("parallel",)),
        name="dest_rows",
    )(eidx_t, rank_t, start_col)


def _tile_major(dest_t, tm):
    slots, n = dest_t.shape
    return dest_t.reshape(slots, n // tm, tm).transpose(1, 0, 2).reshape(n // tm, 1, slots * tm)


def _scatter_body(dest_ref, x_ref, _, o_ref, sem):
    tm = x_ref.shape[0]

    def row_copy(t, k):
        return pltpu.make_async_copy(x_ref.at[pl.ds(t, 1)], o_ref.at[pl.ds(dest_ref[0, k * tm + t], 1)], sem)

    def issue(group, carry):
        base = pl.multiple_of(group * SUBLANES, SUBLANES)
        for r in range(SUBLANES):
            for k in range(EXPERT_TOPK):
                row_copy(base + r, k).start(priority=k % 2)
        return carry

    lax.fori_loop(0, tm // SUBLANES, issue, 0)
    for k in range(EXPERT_TOPK):
        pltpu.make_async_copy(x_ref, o_ref.at[pl.ds(0, tm)], sem).wait()


def _scatter_rows(xw, dest_t, rows, tm=512):
    n, width = xw.shape
    return pl.pallas_call(
        _scatter_body,
        out_shape=jax.ShapeDtypeStruct((rows, width), xw.dtype),
        grid=(n // tm,),
        in_specs=[pl.BlockSpec((None, 1, EXPERT_TOPK * tm), lambda i: (i, 0, 0), memory_space=pltpu.SMEM),
                  pl.BlockSpec((tm, width), lambda i: (i, 0)),
                  pl.BlockSpec(memory_space=pl.ANY)],
        out_specs=pl.BlockSpec(memory_space=pl.ANY),
        scratch_shapes=[pltpu.SemaphoreType.DMA],
        input_output_aliases={2: 0},
        compiler_params=_params(("arbitrary",)),
        name="scatter_rows",
    )(_tile_major(dest_t, tm), xw, jnp.zeros((rows, width), xw.dtype))


def _experts_body(block_exp_ref, n_used_ref, x_ref, wgu_ref, wdn_ref, o_ref, wgu_bf, wdn_bf):
    i = pl.program_id(0)
    used = i < n_used_ref[0]

    @pl.when(used)
    def _():
        prev = block_exp_ref[jnp.maximum(i - 1, 0)]

        @pl.when((i == 0) | (block_exp_ref[i] != prev))
        def _():
            wgu_bf[...] = wgu_ref[...].astype(BF16)
            wdn_bf[...] = wdn_ref[...].astype(BF16)

        x_lo, x_hi = _unpack_rows(x_ref[...])
        half = x_lo.shape[1]
        h = (jnp.dot(x_lo, wgu_bf[0:half, :], preferred_element_type=F32)
             + jnp.dot(x_hi, wgu_bf[half:, :], preferred_element_type=F32))
        ff = h.shape[1] // 2
        act = jax.nn.silu(h[:, :ff]) * h[:, ff:]
        o_ref[...] = jnp.dot(act.astype(BF16), wdn_bf[...], preferred_element_type=F32)

    @pl.when(jnp.logical_not(used))
    def _():
        o_ref[...] = jnp.zeros(o_ref.shape, F32)


def _experts(x_rows, block_exp, n_used, w_gu, w_dn, layer):
    rows, width = x_rows.shape
    n_blocks = rows // EXPERT_BLOCK
    d, ff2 = w_gu.shape[-2:]
    grid_spec = pltpu.PrefetchScalarGridSpec(
        num_scalar_prefetch=2,
        grid=(n_blocks,),
        in_specs=[pl.BlockSpec((EXPERT_BLOCK, width), lambda i, be, nu: (jnp.clip(i, 0, jnp.maximum(nu[0] - 1, 0)), 0)),
                  pl.BlockSpec((None, None, d, ff2), lambda i, be, nu: (layer, be[i], 0, 0)),
                  pl.BlockSpec((None, None, ff2 // 2, d), lambda i, be, nu: (layer, be[i], 0, 0))],
        out_specs=pl.BlockSpec((EXPERT_BLOCK, d), lambda i, be, nu: (i, 0)),
        scratch_shapes=[pltpu.VMEM((d, ff2), BF16), pltpu.VMEM((ff2 // 2, d), BF16)],
    )
    return pl.pallas_call(
        _experts_body,
        out_shape=jax.ShapeDtypeStruct((rows, d), F32),
        grid_spec=grid_spec,
        compiler_params=_params(("arbitrary",)),
        name="experts",
    )(block_exp, n_used, x_rows, w_gu, w_dn)


def _block_plan(counts, n_blocks):
    counts = counts.reshape(-1).astype(I32)
    padded = (counts + EXPERT_BLOCK - 1) // EXPERT_BLOCK * EXPERT_BLOCK
    pad_end = jnp.cumsum(padded)
    start_col = (pad_end - padded).astype(F32)[:, None]
    end_blocks = pad_end // EXPERT_BLOCK
    block_exp = jnp.minimum(jnp.sum(end_blocks[None, :] <= jnp.arange(n_blocks)[:, None], axis=1),
                            N_EXPERTS - 1).astype(I32)
    return start_col, block_exp, end_blocks[-1:].astype(I32)


def _ffn_out_body(dest_ref, dest_next_ref, x_ref, gw_ref, yb_ref, wgu_ref, wdn_ref, g_ref, beta_ref, o_ref,
                  ybuf, sems, *, alpha):
    i = pl.program_id(0)
    n_steps = pl.num_programs(0)
    tm = x_ref.shape[0]
    slot = i % 2

    def gather(dests, to_slot):
        def issue(group, carry):
            base = pl.multiple_of(group * SUBLANES, SUBLANES)
            for r in range(SUBLANES):
                for k in range(EXPERT_TOPK):
                    pltpu.make_async_copy(yb_ref.at[pl.ds(dests[0, k * tm + base + r], 1)],
                                          ybuf.at[to_slot, k, pl.ds(base + r, 1)],
                                          sems.at[to_slot]).start(priority=k % 2)
            return carry
        lax.fori_loop(0, tm // SUBLANES, issue, 0)

    @pl.when(i == 0)
    def _():
        gather(dest_ref, 0)

    for parity in range(2):
        @pl.when((i + 1 < n_steps) & (slot == parity))
        def _(parity=parity):
            gather(dest_next_ref, 1 - parity)

    x = x_ref[...]
    h = jnp.dot(x.astype(BF16), wgu_ref[...], preferred_element_type=F32)
    ff = h.shape[1] // 2
    act = jax.nn.silu(h[:, :ff]) * h[:, ff:]
    shared = jnp.dot(act.astype(BF16), wdn_ref[...], preferred_element_type=F32)

    for k in range(EXPERT_TOPK):
        pltpu.make_async_copy(yb_ref.at[pl.ds(0, tm)], ybuf.at[slot, k], sems.at[slot]).wait()
    routed = None
    for k in range(EXPERT_TOPK):
        term = ybuf[slot, k] * gw_ref[:, k:k + 1]
        routed = term if routed is None else routed + term
    o_ref[...] = _layer_norm(alpha * x + (routed + shared), g_ref[...], beta_ref[...])


def _ffn_out(x, yb, dest_t, gw, ws_gu, ws_dn, g, beta, alpha, tm=256):
    n, d = x.shape
    n_tiles = n // tm
    row = pl.BlockSpec((tm, d), lambda i: (i, 0))
    full = lambda arr: pl.BlockSpec(arr.shape, lambda i: (0,) * arr.ndim)
    dest_spec = lambda index: pl.BlockSpec((None, 1, EXPERT_TOPK * tm), index, memory_space=pltpu.SMEM)
    dest_tiles = _tile_major(dest_t, tm)
    return pl.pallas_call(
        functools.partial(_ffn_out_body, alpha=alpha),
        out_shape=jax.ShapeDtypeStruct((n, d), F32),
        grid=(n_tiles,),
        in_specs=[dest_spec(lambda i: (i, 0, 0)), dest_spec(lambda i: (jnp.minimum(i + 1, n_tiles - 1), 0, 0)),
                  row, pl.BlockSpec((tm, EXPERT_TOPK), lambda i: (i, 0)),
                  pl.BlockSpec(memory_space=pl.ANY),
                  full(ws_gu), full(ws_dn), full(g), full(beta)],
        out_specs=row,
        scratch_shapes=[pltpu.VMEM((2, EXPERT_TOPK, tm, d), F32), pltpu.SemaphoreType.DMA((2,))],
        compiler_params=_params(("arbitrary",)),
        name="ffn_out",
    )(dest_tiles, dest_tiles, x, gw, yb, ws_gu, ws_dn, g, beta)


def _moe(x, xw, layer, w_router_t, b_router_col, w_exp_gate_up, w_exp_down, ws_gu, ws_dn, g, beta, alpha):
    n = x.shape[0]
    n_blocks = -(-n * EXPERT_TOPK // EXPERT_BLOCK) + N_EXPERTS
    eidx_t, gw_t, rank_t, counts = _router(x, w_router_t, b_router_col)
    start_col, block_exp, n_used = _block_plan(counts, n_blocks)
    dest_t = _dest_rows(eidx_t, rank_t, start_col)
    x_rows = _scatter_rows(xw, dest_t, n_blocks * EXPERT_BLOCK)
    yb = _experts(x_rows, block_exp, n_used, w_exp_gate_up, w_exp_down, layer)
    return _ffn_out(x, yb, dest_t, gw_t.T, ws_gu, ws_dn, g, beta, alpha)


def kernel(x_prompt, x_sample, mem_prompt, cache_a_k, cache_a_v, cache_b_k, cache_b_v, cache_b_idx_k,
           cache_mem_k, cache_mem_v, w_in, rel_bias, w_mem_kv, w_branch, w_out, ln_mix_g, ln_mix_b,
           w_router, b_router, w_exp_gate_up, w_exp_down, w_sh_gate_up, w_sh_down, ln_ffn_g, ln_ffn_b):
    n_batch, seq, d = x_prompt.shape
    s_batch, s_seq, _ = x_sample.shape
    depth = w_in.shape[0]
    past = cache_b_k.shape[2]
    assert s_seq == CHUNK and seq % CHUNK == 0 and cache_a_k.shape[2] == BAND and seq >= BAND
    alpha = (2 * depth) ** 0.25
    n_p, n_s = n_batch * seq, s_batch * s_seq
    n_tot = n_p + n_s
    tm = 512
    assert seq % tm == 0 and n_p % tm == 0 and n_s % tm == 0 and tm % CHUNK == 0

    x = jnp.concatenate([x_prompt.reshape(n_p, d), x_sample.reshape(n_s, d)])

    pos = jnp.concatenate([jnp.arange(seq), jnp.tile(past + jnp.arange(s_seq), tm // s_seq)])
    tab_heads = _rope_tables(pos, LANES)
    tab_kiw = _rope_tables(pos, HEAD_DIM)
    p_tiles, seq_tiles = n_p // tm, seq // tm
    table_index = lambda i: jnp.where(i < p_tiles, i % seq_tiles, seq_tiles)

    wd = WIDTH
    c_qi, c_ki, c_wi, c_qm, c_gates = 6 * wd, 7 * wd, 7 * wd + HEAD_DIM, 7 * wd + HEAD_DIM + N_HEADS, 8 * wd + HEAD_DIM + N_HEADS
    cache_a_k = cache_a_k.reshape(depth, s_batch, BAND, wd)
    cache_a_v = cache_a_v.reshape(depth, s_batch, BAND, wd)
    cache_b_k = cache_b_k.reshape(depth, s_batch, past, wd)
    cache_b_v = cache_b_v.reshape(depth, s_batch, past, wd)
    cache_mem_k = cache_mem_k.reshape(depth, s_batch, N_MEM, wd)
    cache_mem_v = cache_mem_v.reshape(depth, s_batch, N_MEM, wd)
    mem_rows = mem_prompt.reshape(n_batch * N_MEM, d)

    outs = {k: [] for k in ("p_ak", "p_av", "p_bk", "p_bv", "p_ik", "p_mk", "p_mv",
                            "s_ak", "s_av", "s_bk", "s_bv", "s_ik")}
    for l in range(depth):
        w = w_in[l]
        w_a = w[:, :3 * wd].astype(BF16)
        w_b = w[:, 3 * wd:7 * wd].astype(BF16)
        w_c = w[:, c_qm:].astype(BF16)
        w_d = jnp.concatenate([w[:, c_ki:c_qm], jnp.zeros((d, LANES - HEAD_DIM - N_HEADS), F32)], 1).astype(BF16)

        qa, ka, va = _project(x, w_a, (wd, wd, wd), split_flags=(False, True, True), n_prompt=n_p,
                              tm=tm, name="proj_a")
        qb, kb, vb, qi = _project(x, w_b, (wd, wd, wd, wd), (True, True, False, True), tab_heads, table_index,
                                  split_flags=(False, True, True, False), n_prompt=n_p, tm=tm, name="proj_b")
        qm, gates = _project(x, w_c, (wd, 3 * d), tm=tm, name="proj_c")
        (kiw,) = _project(x, w_d, (LANES,), (True,), tab_kiw, table_index, split_flags=(True,), n_prompt=n_p,
                          tm=tm, name="proj_d")
        mk, mv = _project(mem_rows, w_mem_kv[l].astype(BF16), (wd, wd), tm=N_MEM, name="proj_mem")

        a = (_band_prompt(qa, ka[0], va[0], rel_bias[l], n_batch, seq),
             _band_sample(qa, ka[1], va[1], cache_a_k, cache_a_v, l, _band_bias(rel_bias[l], CHUNK), s_batch, n_p))
        b = (_dsa_prompt(qb, kb[0], vb[0], qi, kiw[0], n_batch, seq),
             _dsa_sample(qb, kb[1], vb[1], qi, kiw[1], cache_b_k, cache_b_v, cache_b_idx_k, l, s_batch, n_p))
        m = (_mem_attn(qm, mk, mv, lambda bb: (bb, 0), n_batch, seq, 0, tm),
             _mem_attn(qm, cache_mem_k, cache_mem_v, lambda bb, l=l: (l, bb, 0, 0), s_batch, s_seq, n_p, s_seq))

        x1, x1w = _merge(x, (a, b, m), gates, w_branch[l].astype(BF16), w_out[l].astype(BF16),
                         ln_mix_g[l][None], ln_mix_b[l][None], alpha)
        x = _moe(x1, x1w, l, w_router[l].T.astype(BF16), b_router[l][:, None], w_exp_gate_up, w_exp_down,
                 w_sh_gate_up[l].astype(BF16), w_sh_down[l].astype(BF16),
                 ln_ffn_g[l][None], ln_ffn_b[l][None], alpha)

        outs["p_ak"].append(ka[0].reshape(n_batch, seq, wd)[:, -BAND:])
        outs["p_av"].append(va[0].reshape(n_batch, seq, wd)[:, -BAND:])
        outs["p_bk"].append(kb[0].reshape(n_batch, seq, wd))
        outs["p_bv"].append(vb[0].reshape(n_batch, seq, wd))
        outs["p_ik"].append(kiw[0][:, :HEAD_DIM].reshape(n_batch, seq, HEAD_DIM))
        outs["p_mk"].append(mk.reshape(n_batch, N_MEM, wd))
        outs["p_mv"].append(mv.reshape(n_batch, N_MEM, wd))
        ka_s = ka[1].reshape(s_batch, s_seq, wd)
        va_s = va[1].reshape(s_batch, s_seq, wd)
        outs["s_ak"].append(jnp.concatenate([cache_a_k[l], ka_s], 1)[:, -BAND:])
        outs["s_av"].append(jnp.concatenate([cache_a_v[l], va_s], 1)[:, -BAND:])
        outs["s_bk"].append(kb[1].reshape(s_batch, s_seq, wd))
        outs["s_bv"].append(vb[1].reshape(s_batch, s_seq, wd))
        outs["s_ik"].append(kiw[1][:, :HEAD_DIM].reshape(s_batch, s_seq, HEAD_DIM))

    heads = lambda t: t.reshape(t.shape[:-1] + (N_HEADS, HEAD_DIM))
    mheads = lambda t: t.reshape(t.shape[:-1] + (M_HEADS, M_HEAD_DIM))
    st = lambda key: jnp.stack(outs[key])
    return (x[:n_p].reshape(n_batch, seq, d), x[n_p:].reshape(s_batch, s_seq, d),
            heads(st("p_ak")), heads(st("p_av")), heads(st("p_bk")), heads(st("p_bv")), st("p_ik"),
            mheads(st("p_mk")), mheads(st("p_mv")),
            heads(st("s_ak")), heads(st("s_av")), heads(st("s_bk")), heads(st("s_bv")), st("s_ik"))
```
